```python
import jax, jax.numpy as jnp
from jax import lax
import numpy as np

D_MODEL = 1024
BATCH = 8
SEQ = 2048
DEPTH = 4

GRID_W = 64
CTX_LEN = 256
A_HEADS = 8
A_KV_HEADS = 2
A_HEAD_DIM = 64
ROPE_THETA = 10000.0
Q_BLOCK = 128
B_HEADS = 4
B_HEAD_DIM = 128
B_CHUNK = 128
B_CONV = 3
POOL_WINDOWS = (2, 4, 8, 16)
POOL_GROUP = D_MODEL // 4
N_EXPERTS = 16
EC_FACTOR = 2
D_EXPERT = 1024
NORM_EPS = 1e-6

A_Q = A_HEADS * A_HEAD_DIM
A_KV = A_KV_HEADS * A_HEAD_DIM
B_W = B_HEADS * B_HEAD_DIM
N_GATES = 2 * 2 * B_HEADS
EVEN_IN = A_Q + 2 * A_KV + 4 * B_W + N_GATES
MIX_W = A_Q + B_W
EVEN_SPLITS = (A_Q, A_Q + A_KV, A_Q + 2 * A_KV, A_Q + 2 * A_KV + B_W, A_Q + 2 * A_KV + 2 * B_W,
               A_Q + 2 * A_KV + 3 * B_W, A_Q + 2 * A_KV + 4 * B_W)
N_EVEN = (DEPTH + 1) // 2
N_ODD = DEPTH // 2

kernel_name = "hybrid_diffusion_gqa_mlstm_pool_ecmoe"


def rmsnorm(x, g):
    xf = x.astype(jnp.float32)
    y = xf * lax.rsqrt(jnp.mean(xf * xf, axis=-1, keepdims=True) + NORM_EPS)
    return (y * g.astype(jnp.float32)).astype(x.dtype)


def adaln(cond, w, b):
    m = jax.nn.silu(cond) @ w + b
    return m.reshape(cond.shape[:-1] + (1, 6, D_MODEL))


def modulate(x, g, shift, scale):
    return rmsnorm(x, g) * (1 + scale) + shift


def axial_rope(n_tokens):
    rows = n_tokens // GRID_W
    row_ids = jnp.repeat(jnp.arange(rows, dtype=jnp.float32), GRID_W)
    col_ids = jnp.tile(jnp.arange(GRID_W, dtype=jnp.float32), rows)
    half = A_HEAD_DIM // 2
    inv = ROPE_THETA ** (-jnp.arange(0, half, 2, dtype=jnp.float32) / half)
    ang = jnp.stack([row_ids[:, None] * inv, col_ids[:, None] * inv], axis=1)
    return jnp.cos(ang), jnp.sin(ang)


def apply_rope(x, cos, sin):
    Bn, n, H, d = x.shape
    xr = x.reshape(Bn, n, H, 2, 2, d // 4)
    x1, x2 = xr[..., 0, :], xr[..., 1, :]
    c = cos[None, :, None].astype(x.dtype)
    s = sin[None, :, None].astype(x.dtype)
    return jnp.stack([x1 * c - x2 * s, x2 * c + x1 * s], axis=-2).reshape(x.shape)


def attend_blocks(q, k, v):
    Bn, Lq, _, d = q.shape
    G = A_HEADS // A_KV_HEADS
    nb = Lq // Q_BLOCK
    qb = q.reshape(Bn, nb, Q_BLOCK, A_KV_HEADS, G, d).transpose(1, 0, 3, 4, 2, 5)
    kt = jnp.swapaxes(k, 1, 2)
    vt = jnp.swapaxes(v, 1, 2)
    scale = d ** -0.5

    def one_block(qblk):
        s = jnp.einsum('bhgtd,bhsd->bhgts', qblk, kt).astype(jnp.float32) * scale
        p = jax.nn.softmax(s, axis=-1).astype(vt.dtype)
        return jnp.einsum('bhgts,bhsd->bhgtd', p, vt)

    o = lax.map(one_block, qb)
    return o.transpose(1, 0, 4, 2, 3, 5).reshape(Bn, Lq, A_HEADS * d)


def short_conv(x, w):
    n = x.shape[1]
    pad = B_CONV // 2
    xp = jnp.pad(x, ((0, 0), (pad, B_CONV - 1 - pad), (0, 0)))
    return sum(xp[:, j:j + n] * w[j] for j in range(B_CONV))


def mlstm_zero_state(Bn):
    return (jnp.zeros((Bn, B_HEADS, B_HEAD_DIM, B_HEAD_DIM), jnp.float32),
            jnp.zeros((Bn, B_HEADS, B_HEAD_DIM), jnp.float32),
            jnp.full((Bn, B_HEADS), -1e30, jnp.float32))


def mlstm_scan(q, k, v, ig, lf, state):
    Bn, H, L, d = q.shape
    nc = L // B_CHUNK
    k = k * (d ** -0.5)

    def to_chunks(a):
        a = a.reshape(a.shape[:2] + (nc, B_CHUNK) + a.shape[3:])
        return jnp.moveaxis(a, 2, 0)

    tril = jnp.tril(jnp.ones((B_CHUNK, B_CHUNK), dtype=bool))

    def step(carry, inp):
        C, n, m = carry
        qc, kc, vc, ic, fc = inp
        b = jnp.cumsum(fc, axis=-1)
        log_d = jnp.where(tril, b[..., :, None] - b[..., None, :] + ic[..., None, :], -jnp.inf)
        log_inter = m[..., None] + b
        m_t = jnp.maximum(log_inter, jnp.max(log_d, axis=-1))
        w_intra = jnp.exp(log_d - m_t[..., None]) * jnp.einsum('bhtd,bhsd->bhts', qc, kc)
        w_inter = jnp.exp(log_inter - m_t)
        num = (w_inter[..., None] * jnp.einsum('bhed,bhtd->bhte', C, qc)
               + jnp.einsum('bhts,bhse->bhte', w_intra, vc))
        den = w_inter * jnp.einsum('bhd,bhtd->bht', n, qc) + jnp.sum(w_intra, axis=-1)
        h = num / jnp.maximum(jnp.abs(den), jnp.exp(-m_t))[..., None]
        b_end = b[..., -1]
        log_w = b_end[..., None] - b + ic
        m_new = jnp.maximum(m + b_end, jnp.max(log_w, axis=-1))
        w_s = jnp.exp(log_w - m_new[..., None])
        decay = jnp.exp(m + b_end - m_new)
        C = decay[..., None, None] * C + jnp.einsum('bhs,bhse,bhsd->bhed', w_s, vc, kc)
        n = decay[..., None] * n + jnp.einsum('bhs,bhsd->bhd', w_s, kc)
        return (C, n, m_new), h

    state, h = lax.scan(step, state, tuple(to_chunks(a) for a in (q, k, v, ig, lf)))
    h = jnp.moveaxis(h, 0, 2).reshape(Bn, H, L, d)
    return h, state


def mlstm_bidir(q, k, v, ig, lf, states):
    h_f, s_f = mlstm_scan(q, k, v, ig[0], lf[0], states[0])
    flip = lambda a: jnp.flip(a, axis=2)
    h_b, s_b = mlstm_scan(flip(q), flip(k), flip(v), flip(ig[1]), flip(lf[1]), states[1])
    return h_f + flip(h_b), (s_f, s_b)


def even_mixer(h_lat, h_ctx, ctx_out, w_in, w_out, g_qn, g_kn, w_conv, b_gate, g_hn):
    Bn, L, _ = h_lat.shape

    def project(h):
        n = h.shape[1]
        qa, ka, va, qb, kb, vb, ob, gt = jnp.split(h @ w_in, EVEN_SPLITS, axis=-1)
        qa = rmsnorm(qa.reshape(Bn, n, A_HEADS, A_HEAD_DIM), g_qn)
        ka = rmsnorm(ka.reshape(Bn, n, A_KV_HEADS, A_HEAD_DIM), g_kn)
        va = va.reshape(Bn, n, A_KV_HEADS, A_HEAD_DIM)
        qk = jax.nn.silu(short_conv(jnp.concatenate([qb, kb], axis=-1), w_conv))
        to_heads = lambda a: a.reshape(Bn, n, B_HEADS, B_HEAD_DIM).transpose(0, 2, 1, 3).astype(jnp.float32)
        qm, km, vm = to_heads(qk[..., :B_W]), to_heads(qk[..., B_W:]), to_heads(vb)
        g = (gt + b_gate).astype(jnp.float32).reshape(Bn, n, 2, 2, B_HEADS).transpose(2, 3, 0, 4, 1)
        ig = g[:, 0]
        lf = jax.nn.log_sigmoid(g[:, 1])
        return (qa, ka, va), (qm, km, vm, ig, lf), ob

    def mlstm_out(hm, ob):
        n = hm.shape[2]
        hm = rmsnorm(jnp.swapaxes(hm, 1, 2), g_hn.reshape(B_HEADS, B_HEAD_DIM))
        return hm.reshape(Bn, n, B_W).astype(ob.dtype) * jax.nn.sigmoid(ob)

    (qa_c, ka_c, va_c), mc, ob_c = project(h_ctx)
    (qa_l, ka_l, va_l), ml, ob_l = project(h_lat)
    cos, sin = axial_rope(L)
    qa_l = apply_rope(qa_l, cos, sin)
    ka_l = apply_rope(ka_l, cos, sin)
    o_a = attend_blocks(qa_l, jnp.concatenate([ka_c, ka_l], axis=1), jnp.concatenate([va_c, va_l], axis=1))
    st0 = mlstm_zero_state(Bn)
    h_c, ctx_states = mlstm_bidir(*mc, (st0, st0))
    h_l, _ = mlstm_bidir(*ml, ctx_states)
    y = jnp.concatenate([o_a, mlstm_out(h_l, ob_l)], axis=-1) @ w_out
    yc = None
    if ctx_out:
        yc = jnp.concatenate([attend_blocks(qa_c, ka_c, va_c), mlstm_out(h_c, ob_c)], axis=-1) @ w_out
    return y, yc


def pool_mixer(h, w_in, w_grp, s_pool):
    Bn, n, _ = h.shape
    u = h @ w_in
    uf = u.astype(jnp.float32)
    S = jnp.pad(jnp.cumsum(uf, axis=1), ((0, 0), (1, 0), (0, 0)))
    t = jnp.arange(n)
    outs = []
    for gi, w in enumerate(POOL_WINDOWS):
        lo = w // 2
        hi = w - 1 - lo
        a = jnp.clip(t - lo, 0, n - 1)
        e = jnp.clip(t + hi, 0, n - 1)
        Sg = S[:, :, gi * POOL_GROUP:(gi + 1) * POOL_GROUP]
        cnt = (e - a + 1).astype(jnp.float32)[None, :, None]
        outs.append((Sg[:, e + 1] - Sg[:, a]) / cnt - uf[:, :, gi * POOL_GROUP:(gi + 1) * POOL_GROUP])
    p = jnp.stack(outs, axis=2).astype(u.dtype)
    y = jnp.einsum('bngc,gce->bnge', p, w_grp).reshape(Bn, n, D_MODEL)
    return y * s_pool


def ec_moe(h, w_router, w_gate, w_up, w_down):
    Bn, n, Dm = h.shape
    cap = max(1, EC_FACTOR * n // N_EXPERTS)
    aff = jax.nn.softmax(jnp.einsum('bnd,de->bne', h, w_router).astype(jnp.float32), axis=-1)
    gate, idx = lax.top_k(jnp.swapaxes(aff, 1, 2), cap)
    xs = jax.vmap(lambda hb, ib: hb[ib])(h, idx)
    a = jax.nn.silu(jnp.einsum('becd,edf->becf', xs, w_gate)) * jnp.einsum('becd,edf->becf', xs, w_up)
    out = jnp.einsum('becf,efd->becd', a, w_down) * gate[..., None].astype(h.dtype)
    return jax.vmap(lambda ob, ib: jnp.zeros((n, Dm), ob.dtype).at[ib.reshape(-1)].add(ob.reshape(-1, Dm)))(out, idx)


def setup_inputs(seed: int = 0) -> dict:
    key = jax.random.key(seed)
    ks = jax.random.split(key, 23)
    nrm = lambda k, shape, s: s * jax.random.normal(k, shape, jnp.float32)
    fb = jnp.linspace(3.0, 6.0, B_HEADS)
    gate_base = jnp.concatenate([jnp.zeros((B_HEADS,)), fb, jnp.zeros((B_HEADS,)), fb])
    return {
        "x": nrm(ks[0], (BATCH, SEQ, D_MODEL), 1.0),
        "c": nrm(ks[1], (BATCH, D_MODEL), 1.0),
        "ctx": nrm(ks[2], (BATCH, CTX_LEN, D_MODEL), 1.0),
        "c_ctx": nrm(ks[3], (D_MODEL,), 1.0),
        "w_mod": nrm(ks[4], (DEPTH, D_MODEL, 6 * D_MODEL), 0.5 * D_MODEL ** -0.5),
        "b_mod": nrm(ks[5], (DEPTH, 6 * D_MODEL), 0.02),
        "g_norm1": 1.0 + nrm(ks[6], (DEPTH, D_MODEL), 0.02),
        "g_norm2": 1.0 + nrm(ks[7], (DEPTH, D_MODEL), 0.02),
        "w_in_even": nrm(ks[8], (N_EVEN, D_MODEL, EVEN_IN), D_MODEL ** -0.5),
        "w_out_even": nrm(ks[9], (N_EVEN, MIX_W, D_MODEL), MIX_W ** -0.5),
        "g_qnorm": 1.0 + nrm(ks[10], (N_EVEN, A_HEAD_DIM), 0.02),
        "g_knorm": 1.0 + nrm(ks[11], (N_EVEN, A_HEAD_DIM), 0.02),
        "w_conv": nrm(ks[12], (N_EVEN, B_CONV, 2 * B_W), B_CONV ** -0.5),
        "b_gate": gate_base[None] + nrm(ks[13], (N_EVEN, N_GATES), 0.1),
        "g_hnorm": 1.0 + nrm(ks[14], (N_EVEN, B_W), 0.02),
        "w_in_odd": nrm(ks[15], (N_ODD, D_MODEL, D_MODEL), D_MODEL ** -0.5),
        "w_pool_grp": nrm(ks[16], (N_ODD, 4, POOL_GROUP, POOL_GROUP), POOL_GROUP ** -0.5),
        "s_pool": 1.0 + nrm(ks[17], (N_ODD, D_MODEL), 0.1),
        "w_router": nrm(ks[18], (DEPTH, D_MODEL, N_EXPERTS), D_MODEL ** -0.5),
        "w_exp_gate": nrm(ks[19], (DEPTH, N_EXPERTS, D_MODEL, D_EXPERT), D_MODEL ** -0.5),
        "w_exp_up": nrm(ks[20], (DEPTH, N_EXPERTS, D_MODEL, D_EXPERT), D_MODEL ** -0.5),
        "w_exp_down": nrm(ks[21], (DEPTH, N_EXPERTS, D_EXPERT, D_MODEL), D_EXPERT ** -0.5),
        "g_final": 1.0 + nrm(ks[22], (D_MODEL,), 0.02),
    }


def reference(x, c, ctx, c_ctx, w_mod, b_mod, g_norm1, g_norm2, w_in_even, w_out_even, g_qnorm, g_knorm,
              w_conv, b_gate, g_hnorm, w_in_odd, w_pool_grp, s_pool, w_router, w_exp_gate, w_exp_up,
              w_exp_down, g_final):
    xc = ctx
    for i in range(DEPTH):
        ctx_next = any(j % 2 == 0 for j in range(i + 1, DEPTH))
        ctx_used = (i % 2 == 0) or ctx_next
        m_lat = adaln(c, w_mod[i], b_mod[i])
        h = modulate(x, g_norm1[i], m_lat[..., 0, :], m_lat[..., 1, :])
        yc = None
        if ctx_used:
            m_ctx = adaln(c_ctx, w_mod[i], b_mod[i])
            hc = modulate(xc, g_norm1[i], m_ctx[..., 0, :], m_ctx[..., 1, :])
        if i % 2 == 0:
            e = i // 2
            y, yc = even_mixer(h, hc, ctx_next, w_in_even[e], w_out_even[e], g_qnorm[e], g_knorm[e],
                               w_conv[e], b_gate[e], g_hnorm[e])
        else:
            o = i // 2
            y = pool_mixer(h, w_in_odd[o], w_pool_grp[o], s_pool[o])
            if ctx_next:
                yc = pool_mixer(hc, w_in_odd[o], w_pool_grp[o], s_pool[o])
        x = x + m_lat[..., 2, :] * y
        h = modulate(x, g_norm2[i], m_lat[..., 3, :], m_lat[..., 4, :])
        x = x + m_lat[..., 5, :] * ec_moe(h, w_router[i], w_exp_gate[i], w_exp_up[i], w_exp_down[i])
        if ctx_next:
            xc = xc + m_ctx[..., 2, :] * yc
            hc = modulate(xc, g_norm2[i], m_ctx[..., 3, :], m_ctx[..., 4, :])
            xc = xc + m_ctx[..., 5, :] * ec_moe(hc, w_router[i], w_exp_gate[i], w_exp_up[i], w_exp_down[i])
    return rmsnorm(x, g_final)
```

```python
import functools

import jax
import jax.numpy as jnp
from jax import lax
from jax.experimental import pallas as pl
from jax.experimental.pallas import tpu as pltpu

D_MODEL = 1024
GRID_W = 64
A_HEADS = 8
A_KV_HEADS = 2
A_HEAD_DIM = 64
ROPE_THETA = 10000.0
B_HEADS = 4
B_HEAD_DIM = 128
B_CHUNK = 128
POOL_WINDOWS = (2, 4, 8, 16)
POOL_GROUP = D_MODEL // 4
N_EXPERTS = 16
EC_FACTOR = 2
NORM_EPS = 1e-6

A_Q = A_HEADS * A_HEAD_DIM
A_KV = A_KV_HEADS * A_HEAD_DIM
B_W = B_HEADS * B_HEAD_DIM
N_GATES = 2 * 2 * B_HEADS
GATE_PAD = 128
C_QA, C_KA, C_VA = 0, A_Q, A_Q + A_KV
C_QK = A_Q + 2 * A_KV
C_VB = C_QK + 2 * B_W
C_OB = C_VB + B_W
C_GT = C_OB + B_W
EVEN_COLS = C_GT + GATE_PAD
POOL_HALO = 16

F32 = jnp.float32
BF16 = jnp.bfloat16
HIGHEST = lax.Precision.HIGHEST
NT_DIMS = (((1,), (1,)), ((), ()))
TN_DIMS = (((0,), (0,)), ((), ()))
VMEM_LIMIT = 56 * 1024 * 1024


def _params(*sem):
    return pltpu.CompilerParams(dimension_semantics=sem, vmem_limit_bytes=VMEM_LIMIT)


def _modulate(x, g, shift, scale):
    y = x * lax.rsqrt(jnp.mean(x * x, axis=-1, keepdims=True) + NORM_EPS)
    return (y * g) * (1.0 + scale) + shift


def _silu(x):
    return x * jax.nn.sigmoid(x)


def _adaln_kernel(c_ref, w_ref, b_ref, o_ref):
    s = _silu(c_ref[...])
    o_ref[0] = jnp.dot(s, w_ref[0], precision=HIGHEST, preferred_element_type=F32) + b_ref[0]


def _adaln(cond, w_mod, b_mod):
    depth, dm, n6 = w_mod.shape
    rows = cond.shape[0]
    tn = 1536
    return pl.pallas_call(
        _adaln_kernel,
        grid=(depth, n6 // tn),
        in_specs=[pl.BlockSpec((rows, dm), lambda l, j: (0, 0)),
                  pl.BlockSpec((1, dm, tn), lambda l, j: (l, 0, j)),
                  pl.BlockSpec((1, 1, tn), lambda l, j: (l, 0, j))],
        out_specs=pl.BlockSpec((1, rows, tn), lambda l, j: (l, 0, j)),
        out_shape=jax.ShapeDtypeStruct((depth, rows, n6), F32),
        compiler_params=_params("arbitrary", "arbitrary"),
        name="adaln",
    )(cond, w_mod, b_mod.reshape(depth, 1, n6))


def _group_mean_sq(x, gmat, width):
    x2 = x * x
    hi = x2.astype(BF16)
    lo = (x2 - hi.astype(F32)).astype(BF16)
    s = jnp.dot(hi, gmat, preferred_element_type=F32) + jnp.dot(lo, gmat, preferred_element_type=F32)
    return s * (1.0 / width)


def _rope(x, cos, sin):
    w = x.shape[1]
    lane = lax.broadcasted_iota(jnp.int32, x.shape, 1)
    first = (lane % 32) < 16
    partner = jnp.where(first, pltpu.roll(x, w - 16, 1), pltpu.roll(x, 16, 1))
    return x * cos + partner * sin


def _inproj_kernel(x_ref, m_ref, g1_ref, w_ref, cos_ref, sin_ref, gq_ref, gk_ref, gm_ref, bg_ref,
                   q_ref, k_ref, v_ref, qk_ref, vb_ref, ob_ref, gt_ref):
    m = m_ref[0, 0]
    h = _modulate(x_ref[0], g1_ref[...], m[0:1], m[1:2]).astype(BF16)

    def mm(lo, hi):
        return jnp.dot(h, w_ref[:, lo:hi], preferred_element_type=F32)

    cos = cos_ref[...]
    sin = sin_ref[...]
    gm = gm_ref[...]
    qa = mm(C_QA, C_KA)
    qa = qa * lax.rsqrt(_group_mean_sq(qa, gm, A_HEAD_DIM) + NORM_EPS) * gq_ref[...]
    q_ref[0] = (_rope(qa, cos, sin) * (A_HEAD_DIM ** -0.5)).astype(BF16)
    ka = mm(C_KA, C_VA)
    ka = ka * lax.rsqrt(_group_mean_sq(ka, gm[:A_KV, :A_KV], A_HEAD_DIM) + NORM_EPS) * gk_ref[...]
    k_ref[0] = _rope(ka, cos[:, :A_KV], sin[:, :A_KV]).astype(BF16)
    v_ref[0] = mm(C_VA, C_QK).astype(BF16)
    qk_ref[0] = mm(C_QK, C_VB)
    vb_ref[0] = mm(C_VB, C_OB).astype(BF16)
    ob_ref[0] = mm(C_OB, C_GT)
    gt = mm(C_GT, EVEN_COLS) + bg_ref[...]
    lane = lax.broadcasted_iota(jnp.int32, gt.shape, 1)
    log_sig = jnp.minimum(gt, 0.0) - jnp.log1p(jnp.exp(-jnp.abs(gt)))
    gt_ref[0] = jnp.where((lane % (2 * B_HEADS)) >= B_HEADS, log_sig, gt)


def _inproj(xcat, modsel, g1, w, cos, sin, gq, gk, gmat, bgate, *, rt, ta, n_lat_tiles):
    b, _, dm = xcat.shape
    nt = ta // rt
    row = lambda bb, r: (bb, r, 0)
    full = lambda bb, r: (0, 0)
    out_widths = (A_Q, A_KV, A_KV, 2 * B_W, B_W, B_W, GATE_PAD)
    out_dtypes = (BF16, BF16, BF16, F32, BF16, F32, F32)
    return pl.pallas_call(
        _inproj_kernel,
        grid=(b, nt),
        in_specs=[pl.BlockSpec((1, rt, dm), row),
                  pl.BlockSpec((1, 1, 6, dm), lambda bb, r: (bb, (r >= n_lat_tiles).astype(jnp.int32), 0, 0)),
                  pl.BlockSpec((1, dm), full),
                  pl.BlockSpec((dm, EVEN_COLS), full),
                  pl.BlockSpec((rt, A_Q), lambda bb, r: (r, 0)),
                  pl.BlockSpec((rt, A_Q), lambda bb, r: (r, 0)),
                  pl.BlockSpec((1, A_Q), full),
                  pl.BlockSpec((1, A_KV), full),
                  pl.BlockSpec((A_Q, A_Q), full),
                  pl.BlockSpec((1, GATE_PAD), full)],
        out_specs=[pl.BlockSpec((1, rt, wd), row) for wd in out_widths],
        out_shape=[jax.ShapeDtypeStruct((b, ta, wd), dt) for wd, dt in zip(out_widths, out_dtypes)],
        compiler_params=_params("parallel", "arbitrary"),
        name="even_inproj",
    )(xcat, modsel, g1, w, cos, sin, gq, gk, gmat, bgate)


def _attn_kernel(q_ref, k_ref, v_ref, o_ref, *, n_lat_tiles, seq):
    r = pl.program_id(1)
    q = q_ref[0]
    k = k_ref[0]
    v = v_ref[0]
    ta = k.shape[0]
    col = lax.broadcasted_iota(jnp.int32, (1, ta), 1)
    hidden = jnp.logical_and(r >= n_lat_tiles, col < seq)
    bias = jnp.where(hidden, -jnp.inf, 0.0)
    group = A_HEADS // A_KV_HEADS
    for hd in range(A_HEADS):
        g = hd // group
        qh = q[:, hd * A_HEAD_DIM:(hd + 1) * A_HEAD_DIM]
        kg = k[:, g * A_HEAD_DIM:(g + 1) * A_HEAD_DIM]
        vg = v[:, g * A_HEAD_DIM:(g + 1) * A_HEAD_DIM]
        s = lax.dot_general(qh, kg, NT_DIMS, preferred_element_type=F32) + bias
        p = jnp.exp(s - jnp.max(s, axis=-1, keepdims=True))
        den = jnp.sum(p, axis=-1, keepdims=True)
        o = jnp.dot(p.astype(BF16), vg, preferred_element_type=F32) / den
        o_ref[0, :, hd * A_HEAD_DIM:(hd + 1) * A_HEAD_DIM] = o.astype(BF16)


def _attention(q, k, v, *, rt, n_tiles, n_lat_tiles, seq):
    b, ta, _ = q.shape
    return pl.pallas_call(
        functools.partial(_attn_kernel, n_lat_tiles=n_lat_tiles, seq=seq),
        grid=(b, n_tiles),
        in_specs=[pl.BlockSpec((1, rt, A_Q), lambda bb, r: (bb, r, 0)),
                  pl.BlockSpec((1, ta, A_KV), lambda bb, r: (bb, 0, 0)),
                  pl.BlockSpec((1, ta, A_KV), lambda bb, r: (bb, 0, 0))],
        out_specs=pl.BlockSpec((1, rt, A_Q), lambda bb, r: (bb, r, 0)),
        out_shape=jax.ShapeDtypeStruct((b, n_tiles * rt, A_Q), BF16),
        compiler_params=_params("parallel", "arbitrary"),
        name="gqa_attention",
    )(q, k, v)


def _conv_kernel(x_ref, w_ref, o_ref, *, seq):
    j = pl.program_id(1)
    x = x_ref[0]
    ta = x.shape[0]
    w = w_ref[...]
    row = lax.broadcasted_iota(jnp.int32, (ta, 1), 0)
    prev = jnp.where(jnp.logical_or(row == 0, row == seq), 0.0, pltpu.roll(x, 1, 0))
    nxt = jnp.where(jnp.logical_or(row == seq - 1, row == ta - 1), 0.0, pltpu.roll(x, ta - 1, 0))
    y = _silu(prev * w[0:1] + x * w[1:2] + nxt * w[2:3])
    n_q_tiles = B_W // x.shape[1]
    scale = jnp.where(j >= n_q_tiles, B_HEAD_DIM ** -0.5, 1.0)
    o_ref[0] = (y * scale).astype(BF16)


def _short_conv(qk, w_conv, *, seq):
    b, ta, wd = qk.shape
    ct = 256
    return pl.pallas_call(
        functools.partial(_conv_kernel, seq=seq),
        grid=(b, wd // ct),
        in_specs=[pl.BlockSpec((1, ta, ct), lambda bb, j: (bb, 0, j)),
                  pl.BlockSpec((w_conv.shape[0], ct), lambda bb, j: (0, j))],
        out_specs=pl.BlockSpec((1, ta, ct), lambda bb, j: (bb, 0, j)),
        out_shape=jax.ShapeDtypeStruct((b, ta, wd), BF16),
        compiler_params=_params("parallel", "arbitrary"),
        name="mlstm_conv",
    )(qk, w_conv)


def _mlstm_chunk(q, k, v, ig, lf, ct, n_row, m_old, tri, csm):
    t = q.shape[0]
    b_row = jnp.dot(jnp.broadcast_to(lf, (8, t)), csm, precision=HIGHEST, preferred_element_type=F32)[0:1]
    total = jnp.sum(lf, axis=-1, keepdims=True)
    b_t = jnp.broadcast_to(b_row, (t, t)).T
    b_col = b_t[:, 0:1]
    log_d = jnp.where(tri, b_t - b_row + ig, -jnp.inf)
    log_inter = m_old + b_col
    m_t = jnp.maximum(log_inter, jnp.max(log_d, axis=-1, keepdims=True))
    qk = lax.dot_general(q, k, NT_DIMS, preferred_element_type=F32)
    w_intra = jnp.exp(log_d - m_t) * qk
    w_inter = jnp.exp(log_inter - m_t)
    num = (w_inter * jnp.dot(q, ct.astype(BF16), preferred_element_type=F32)
           + jnp.dot(w_intra.astype(BF16), v, preferred_element_type=F32))
    qf = q.astype(F32)
    kf = k.astype(F32)
    den = w_inter * jnp.sum(qf * n_row, axis=-1, keepdims=True) + jnp.sum(w_intra, axis=-1, keepdims=True)
    h = num / jnp.maximum(jnp.abs(den), jnp.exp(-m_t))
    log_w = total - b_row + ig
    m_new = jnp.maximum(m_old + total, jnp.max(log_w, axis=-1, keepdims=True))
    w_s = jnp.exp(log_w - m_new)
    decay = jnp.exp(m_old + total - m_new)
    ktw = (kf.T * w_s).astype(BF16)
    ct_new = decay * ct + jnp.dot(ktw, v, preferred_element_type=F32)
    n_new = decay * n_row + jnp.dot(jnp.broadcast_to(w_s, (8, t)), kf, precision=HIGHEST,
                                    preferred_element_type=F32)[0:1]
    return h, ct_new, n_new, m_new


def _mlstm_kernel(qf_ref, kf_ref, vf_ref, gf_ref, qb_ref, kb_ref, vb_ref, gb_ref,
                  hf_ref, hb_ref, c_ref, n_ref, m_ref):
    @pl.when(pl.program_id(1) == 0)
    def _():
        c_ref[...] = jnp.zeros(c_ref.shape, F32)
        n_ref[...] = jnp.zeros(n_ref.shape, F32)
        m_ref[...] = jnp.full(m_ref.shape, -1e30, F32)

    t = B_CHUNK
    r = lax.broadcasted_iota(jnp.int32, (t, t), 0)
    c = lax.broadcasted_iota(jnp.int32, (t, t), 1)
    dirs = ((qf_ref, kf_ref, vf_ref, gf_ref, hf_ref, c <= r, r <= c),
            (qb_ref, kb_ref, vb_ref, gb_ref, hb_ref, c >= r, r >= c))
    for d, (q_ref, k_ref, v_ref, g_ref, h_ref, tri, cs_mask) in enumerate(dirs):
        csm = jnp.where(cs_mask, 1.0, 0.0)
        for hd in range(B_HEADS):
            sl = slice(hd * B_HEAD_DIM, (hd + 1) * B_HEAD_DIM)
            gi = d * 2 * B_HEADS + hd
            ig = g_ref[0, gi:gi + 1, :]
            lf = g_ref[0, gi + B_HEADS:gi + B_HEADS + 1, :]
            si = d * B_HEADS + hd
            h, ct_new, n_new, m_new = _mlstm_chunk(
                q_ref[0, :, sl], k_ref[0, :, sl], v_ref[0, :, sl], ig, lf,
                c_ref[si], n_ref[si], m_ref[si][:, 0:1], tri, csm)
            h_ref[0, :, sl] = h
            c_ref[si] = ct_new
            n_ref[si] = n_new
            m_ref[si] = jnp.broadcast_to(m_new, (1, 128))


def _mlstm(qkc, vb, gates_t, *, seq):
    b, ta, _ = qkc.shape
    nc = ta // B_CHUNK
    ncl = seq // B_CHUNK
    ncc = nc - ncl
    fwd = lambda j: jnp.where(j < ncc, ncl + j, j - ncc)
    bwd = lambda j: jnp.where(j < ncc, nc - 1 - j, ncl - 1 - (j - ncc))

    def specs(order):
        return [pl.BlockSpec((1, B_CHUNK, B_W), lambda bb, j: (bb, order(j), 0)),
                pl.BlockSpec((1, B_CHUNK, B_W), lambda bb, j: (bb, order(j), 1)),
                pl.BlockSpec((1, B_CHUNK, B_W), lambda bb, j: (bb, order(j), 0)),
                pl.BlockSpec((1, N_GATES, B_CHUNK), lambda bb, j: (bb, 0, order(j)))]

    n_state = 2 * B_HEADS
    return pl.pallas_call(
        _mlstm_kernel,
        grid=(b, nc),
        in_specs=specs(fwd) + specs(bwd),
        out_specs=[pl.BlockSpec((1, B_CHUNK, B_W), lambda bb, j: (bb, fwd(j), 0)),
                   pl.BlockSpec((1, B_CHUNK, B_W), lambda bb, j: (bb, bwd(j), 0))],
        out_shape=[jax.ShapeDtypeStruct((b, ta, B_W), F32)] * 2,
        scratch_shapes=[pltpu.VMEM((n_state, B_HEAD_DIM, B_HEAD_DIM), F32),
                        pltpu.VMEM((n_state, 1, B_HEAD_DIM), F32),
                        pltpu.VMEM((n_state, 1, 128), F32)],
        compiler_params=_params("parallel", "arbitrary"),
        name="mlstm_scan",
    )(qkc, qkc, vb, gates_t, qkc, qkc, vb, gates_t)


def _outproj_kernel(x_ref, m_ref, oa_ref, hf_ref, hb_ref, ob_ref, ghn_ref, w_ref, o_ref):
    hm = hf_ref[0] + hb_ref[0]
    ob = ob_ref[0]
    ghn = ghn_ref[...]
    y = jnp.dot(oa_ref[0], w_ref[0:A_Q, :], preferred_element_type=F32)
    for hd in range(B_HEADS):
        sl = slice(hd * B_HEAD_DIM, (hd + 1) * B_HEAD_DIM)
        z = hm[:, sl]
        z = z * lax.rsqrt(jnp.mean(z * z, axis=-1, keepdims=True) + NORM_EPS) * ghn[:, sl]
        z = (z * jax.nn.sigmoid(ob[:, sl])).astype(BF16)
        y = y + jnp.dot(z, w_ref[A_Q + hd * B_HEAD_DIM:A_Q + (hd + 1) * B_HEAD_DIM, :],
                        preferred_element_type=F32)
    o_ref[0] = x_ref[0] + m_ref[0, 0][2:3] * y


def _outproj(xcat, modsel, oa, hf, hb, ob, ghn, w, *, rt, n_tiles, n_lat_tiles):
    b, ta, dm = xcat.shape
    row = lambda bb, r: (bb, r, 0)
    full = lambda bb, r: (0, 0)
    return pl.pallas_call(
        _outproj_kernel,
        grid=(b, n_tiles),
        in_specs=[pl.BlockSpec((1, rt, dm), row),
                  pl.BlockSpec((1, 1, 6, dm), lambda bb, r: (bb, (r >= n_lat_tiles).astype(jnp.int32), 0, 0)),
                  pl.BlockSpec((1, rt, A_Q), row),
                  pl.BlockSpec((1, rt, B_W), row),
                  pl.BlockSpec((1, rt, B_W), row),
                  pl.BlockSpec((1, rt, B_W), row),
                  pl.BlockSpec((1, B_W), full),
                  pl.BlockSpec((A_Q + B_W, dm), full)],
        out_specs=pl.BlockSpec((1, rt, dm), row),
        out_shape=jax.ShapeDtypeStruct(xcat.shape, F32),
        input_output_aliases={0: 0},
        compiler_params=_params("parallel", "arbitrary"),
        name="even_outproj",
    )(xcat, modsel, oa, hf, hb, ob, ghn, w)


def _modmm_kernel(x_ref, m_ref, g_ref, w_ref, o_ref):
    m = m_ref[0, 0]
    h = _modulate(x_ref[0], g_ref[...], m[0:1], m[1:2]).astype(BF16)
    o_ref[0] = jnp.dot(h, w_ref[...], preferred_element_type=F32)


def _mod_matmul(xcat, modsel, g, w, *, rt, tile0, n_tiles, sel):
    b, _, dm = xcat.shape
    n_out = w.shape[1]
    return pl.pallas_call(
        _modmm_kernel,
        grid=(b, n_tiles),
        in_specs=[pl.BlockSpec((1, rt, dm), lambda bb, r: (bb, tile0 + r, 0)),
                  pl.BlockSpec((1, 1, 6, dm), lambda bb, r: (bb, sel, 0, 0)),
                  pl.BlockSpec((1, dm), lambda bb, r: (0, 0)),
                  pl.BlockSpec((dm, n_out), lambda bb, r: (0, 0))],
        out_specs=pl.BlockSpec((1, rt, n_out), lambda bb, r: (bb, r, 0)),
        out_shape=jax.ShapeDtypeStruct((b, n_tiles * rt, n_out), F32),
        compiler_params=_params("parallel", "arbitrary"),
        name="odd_inproj",
    )(xcat, modsel, g, w)


def _pool_kernel(u_ref, x_ref, m_ref, wg_ref, sp_ref, o_ref, pad_ref, *, n, rc):
    g = pl.program_id(1)
    zeros = jnp.zeros((POOL_HALO, pad_ref.shape[1]), F32)
    pad_ref[0:POOL_HALO, :] = zeros
    pad_ref[POOL_HALO + n:2 * POOL_HALO + n, :] = zeros
    pad_ref[POOL_HALO:POOL_HALO + n, :] = u_ref[0]
    gate = m_ref[0, 0][2:3]
    for gi, win in enumerate(POOL_WINDOWS):
        @pl.when(g == gi)
        def _(win=win):
            lo = win // 2
            hi = win - 1 - lo
            for r0 in range(0, n, rc):
                base = POOL_HALO + r0
                acc = pad_ref[base - lo:base - lo + rc, :]
                for j in range(-lo + 1, hi + 1):
                    acc = acc + pad_ref[base + j:base + j + rc, :]
                t = r0 + lax.broadcasted_iota(jnp.int32, (rc, 1), 0)
                cnt = jnp.minimum(t + hi, n - 1) - jnp.maximum(t - lo, 0) + 1
                p = acc / cnt.astype(F32) - pad_ref[base:base + rc, :]
                y = jnp.dot(p.astype(BF16), wg_ref[0], preferred_element_type=F32) * sp_ref[...]
                o_ref[0, r0:r0 + rc, :] = x_ref[0, r0:r0 + rc, :] + gate * y


def _pool(u, xcat, modsel, wgrp, spool, *, n, row_block, sel):
    b, _, dm = xcat.shape
    pg = POOL_GROUP
    rc = min(256, n)
    blk = lambda bb, g: (bb, row_block, g)
    return pl.pallas_call(
        functools.partial(_pool_kernel, n=n, rc=rc),
        grid=(b, dm // pg),
        in_specs=[pl.BlockSpec((1, n, pg), lambda bb, g: (bb, 0, g)),
                  pl.BlockSpec((1, n, pg), blk),
                  pl.BlockSpec((1, 1, 6, pg), lambda bb, g: (bb, sel, 0, g)),
                  pl.BlockSpec((1, pg, pg), lambda bb, g: (g, 0, 0)),
                  pl.BlockSpec((1, pg), lambda bb, g: (0, g))],
        out_specs=pl.BlockSpec((1, n, pg), blk),
        out_shape=jax.ShapeDtypeStruct(xcat.shape, F32),
        scratch_shapes=[pltpu.VMEM((n + 2 * POOL_HALO, pg), F32)],
        input_output_aliases={1: 0},
        compiler_params=_params("parallel", "arbitrary"),
        name="pool_mixer",
    )(u, xcat, modsel, wgrp, spool)


def _lane_cumsum(x01, upper, blk):
    n = x01.shape[1]
    out = []
    carry = jnp.zeros((x01.shape[0], 1), F32)
    for j in range(0, n, blk):
        cs = jnp.dot(x01[:, j:j + blk].astype(BF16), upper, preferred_element_type=F32) + carry
        carry = cs[:, blk - 1:blk]
        out.append(cs)
    return jnp.concatenate(out, axis=1) if len(out) > 1 else out[0]


def _router_kernel(x_ref, m_ref, g_ref, wr_ref, up_ref, h_ref, aff_ref, pos_ref, lg_ref, *, cap, n_tiles):
    r = pl.program_id(1)
    m = m_ref[0, 0]
    h = _modulate(x_ref[0], g_ref[...], m[3:4], m[4:5])
    h_ref[0] = h.astype(BF16)
    lg_ref[r] = lax.dot_general(wr_ref[...], h, NT_DIMS, precision=HIGHEST, preferred_element_type=F32)

    @pl.when(r == n_tiles - 1)
    def _():
        lg = jnp.concatenate([lg_ref[i] for i in range(n_tiles)], axis=1) if n_tiles > 1 else lg_ref[0]
        e = jnp.exp(lg - jnp.max(lg, axis=0, keepdims=True))
        aff = e / jnp.sum(e, axis=0, keepdims=True)

        def bit_step(i, thr):
            cand = thr | lax.shift_left(jnp.int32(1), 30 - i)
            cnt = jnp.sum(jnp.where(aff >= pltpu.bitcast(cand, F32), 1.0, 0.0), axis=-1, keepdims=True)
            return jnp.where(cnt >= cap, cand, thr)

        thr = lax.fori_loop(0, 31, bit_step, jnp.zeros((aff.shape[0], 1), jnp.int32))
        thr = pltpu.bitcast(thr, F32)
        above = aff > thr
        tied = aff == thr
        room = cap - jnp.sum(jnp.where(above, 1.0, 0.0), axis=-1, keepdims=True)
        upper = up_ref[...]
        blk = upper.shape[0]
        tie_rank = _lane_cumsum(jnp.where(tied, 1.0, 0.0), upper, blk)
        sel = jnp.logical_or(above, jnp.logical_and(tied, tie_rank <= room))
        slot = _lane_cumsum(jnp.where(sel, 1.0, 0.0), upper, blk) - 1.0
        pos_ref[0] = jnp.where(sel, slot, -1.0).astype(jnp.int32)
        aff_ref[0] = aff


def _router(xcat, modsel, g2, wr_t, upper, *, rt, n, row_tile0, sel):
    b, ta, dm = xcat.shape
    n_tiles = n // rt
    cap = max(1, EC_FACTOR * n // N_EXPERTS)
    ne = wr_t.shape[0]
    return pl.pallas_call(
        functools.partial(_router_kernel, cap=cap, n_tiles=n_tiles),
        grid=(b, n_tiles),
        in_specs=[pl.BlockSpec((1, rt, dm), lambda bb, r: (bb, row_tile0 + r, 0)),
                  pl.BlockSpec((1, 1, 6, dm), lambda bb, r: (bb, sel, 0, 0)),
                  pl.BlockSpec((1, dm), lambda bb, r: (0, 0)),
                  pl.BlockSpec((ne, dm), lambda bb, r: (0, 0)),
                  pl.BlockSpec(upper.shape, lambda bb, r: (0, 0))],
        out_specs=[pl.BlockSpec((1, rt, dm), lambda bb, r: (bb, r, 0)),
                   pl.BlockSpec((1, ne, n), lambda bb, r: (bb, 0, 0)),
                   pl.BlockSpec((1, ne, n), lambda bb, r: (bb, 0, 0))],
        out_shape=[jax.ShapeDtypeStruct((b, n, dm), BF16),
                   jax.ShapeDtypeStruct((b, ne, n), F32),
                   jax.ShapeDtypeStruct((b, ne, n), jnp.int32)],
        scratch_shapes=[pltpu.VMEM((n_tiles, ne, rt), F32)],
        compiler_params=_params("parallel", "arbitrary"),
        name="moe_router",
    )(xcat, modsel, g2, wr_t, upper)


def _moe_kernel(pos_ref, aff_ref, h_ref, m_ref, wg_ref, wu_ref, wd_ref, x_ref, o_ref, *, cap, n, rows, ts):
    @pl.when(pl.program_id(1) == 0)
    def _():
        o_ref[...] = x_ref[...]

    pos = pos_ref[0, 0]
    aff = aff_ref[0, 0]
    hit = lax.broadcasted_iota(jnp.int32, (cap, n), 0) == pos
    pick = jnp.where(hit, 1.0, 0.0).astype(BF16)
    gate = jnp.sum(jnp.where(hit, aff, 0.0), axis=-1, keepdims=True)
    xs = jnp.dot(pick, h_ref[0], preferred_element_type=F32).astype(BF16)
    a = jnp.dot(xs, wg_ref[0, 0], preferred_element_type=F32)
    u = jnp.dot(xs, wu_ref[0, 0], preferred_element_type=F32)
    act = (_silu(a) * u).astype(BF16)
    out = jnp.dot(act, wd_ref[0, 0], preferred_element_type=F32) * gate
    out = (out * m_ref[0, 0][5:6]).astype(BF16)
    for t0 in range(0, n, ts):
        upd = lax.dot_general(pick[:, t0:t0 + ts], out, TN_DIMS, preferred_element_type=F32)
        i, r0 = t0 // rows, t0 % rows
        o_ref[i, r0:r0 + ts, :] = o_ref[i, r0:r0 + ts, :] + upd


def _moe(pos, aff, h, modsel, wg, wu, wd, xcat, layer, *, cap, bb, row0, rows, sel):
    b_all, ta, dm = xcat.shape
    nb, n, _ = h.shape
    ne = wg.shape[1]
    df = wg.shape[3]
    ts = min(512, rows)
    out_block = (bb, rows, dm)
    out_index = (lambda b, e: (b, 0, 0)) if bb == 1 else (lambda b, e: (0, row0 // rows, 0))
    return pl.pallas_call(
        functools.partial(_moe_kernel, cap=cap, n=n, rows=rows, ts=ts),
        grid=(nb, ne),
        in_specs=[pl.BlockSpec((1, 1, 1, n), lambda b, e: (b, e, 0, 0)),
                  pl.BlockSpec((1, 1, 1, n), lambda b, e: (b, e, 0, 0)),
                  pl.BlockSpec((1, n, dm), lambda b, e: (b, 0, 0), pipeline_mode=pl.Buffered(1)),
                  pl.BlockSpec((1, 1, 6, dm), lambda b, e: (b, sel, 0, 0)),
                  pl.BlockSpec((1, 1, dm, df), lambda b, e: (layer, e, 0, 0)),
                  pl.BlockSpec((1, 1, dm, df), lambda b, e: (layer, e, 0, 0)),
                  pl.BlockSpec((1, 1, df, dm), lambda b, e: (layer, e, 0, 0)),
                  pl.BlockSpec(out_block, out_index, pipeline_mode=pl.Buffered(1))],
        out_specs=pl.BlockSpec(out_block, out_index),
        out_shape=jax.ShapeDtypeStruct(xcat.shape, F32),
        input_output_aliases={7: 0},
        compiler_params=_params("arbitrary", "arbitrary"),
        name="moe_experts",
    )(pos, aff, h, modsel, wg, wu, wd, xcat)


def _ec_moe(xcat, modsel, g2, wr_t, upper, wg, wu, wd, layer, *, rt, seq, ctx, with_ctx):
    b = xcat.shape[0]
    ne = wr_t.shape[0]
    h, aff, pos = _router(xcat, modsel, g2, wr_t, upper, rt=rt, n=seq, row_tile0=0, sel=0)
    cap = max(1, EC_FACTOR * seq // N_EXPERTS)
    xcat = _moe(pos.reshape(b, ne, 1, seq), aff.reshape(b, ne, 1, seq), h, modsel, wg, wu, wd, xcat, layer,
                cap=cap, bb=1, row0=0, rows=seq, sel=0)
    if with_ctx:
        hc, affc, posc = _router(xcat, modsel, g2, wr_t, upper, rt=rt, n=ctx, row_tile0=seq // rt, sel=1)
        capc = max(1, EC_FACTOR * ctx // N_EXPERTS)
        offs = (jnp.arange(b, dtype=jnp.int32) * capc)[:, None, None]
        posc = jnp.where(posc >= 0, posc + offs, -1)
        posc = jnp.transpose(posc, (1, 0, 2)).reshape(1, ne, 1, b * ctx)
        affc = jnp.transpose(affc, (1, 0, 2)).reshape(1, ne, 1, b * ctx)
        xcat = _moe(posc, affc, hc.reshape(1, b * ctx, -1), modsel, wg, wu, wd, xcat, layer,
                    cap=b * capc, bb=b, row0=seq, rows=ctx, sel=1)
    return xcat


def _final_kernel(x_ref, g_ref, o_ref):
    x = x_ref[0]
    o_ref[0] = x * lax.rsqrt(jnp.mean(x * x, axis=-1, keepdims=True) + NORM_EPS) * g_ref[...]


def _final_norm(xcat, g, *, rt, seq):
    b, ta, dm = xcat.shape
    return pl.pallas_call(
        _final_kernel,
        grid=(b, seq // rt),
        in_specs=[pl.BlockSpec((1, rt, dm), lambda bb, r: (bb, r, 0)),
                  pl.BlockSpec((1, dm), lambda bb, r: (0, 0))],
        out_specs=pl.BlockSpec((1, rt, dm), lambda bb, r: (bb, r, 0)),
        out_shape=jax.ShapeDtypeStruct((b, seq, dm), F32),
        compiler_params=_params("parallel", "arbitrary"),
        name="final_norm",
    )(xcat, g)


def _rope_tables(seq, ctx):
    rows = seq // GRID_W
    row_ids = jnp.repeat(jnp.arange(rows, dtype=F32), GRID_W)
    col_ids = jnp.tile(jnp.arange(GRID_W, dtype=F32), rows)
    half = A_HEAD_DIM // 2
    inv = ROPE_THETA ** (-jnp.arange(0, half, 2, dtype=F32) / half)
    ang_r = row_ids[:, None] * inv
    ang_c = col_ids[:, None] * inv
    cos = jnp.concatenate([jnp.cos(ang_r)] * 2 + [jnp.cos(ang_c)] * 2, axis=1)
    sin = jnp.concatenate([-jnp.sin(ang_r), jnp.sin(ang_r), -jnp.sin(ang_c), jnp.sin(ang_c)], axis=1)
    cos = jnp.tile(cos, (1, A_HEADS))
    sin = jnp.tile(sin, (1, A_HEADS))
    cos = jnp.concatenate([cos, jnp.ones((ctx, A_Q), F32)], axis=0)
    sin = jnp.concatenate([sin, jnp.zeros((ctx, A_Q), F32)], axis=0)
    return cos, sin


def kernel(x, c, ctx, c_ctx, w_mod, b_mod, g_norm1, g_norm2, w_in_even, w_out_even, g_qnorm, g_knorm, w_conv,
           b_gate, g_hnorm, w_in_odd, w_pool_grp, s_pool, w_router, w_exp_gate, w_exp_up, w_exp_down, g_final):
    b, seq, dm = x.shape
    n_ctx = ctx.shape[1]
    depth = w_mod.shape[0]
    rt = 256 if (seq % 256 == 0 and n_ctx % 256 == 0) else 128
    assert seq % rt == 0 and n_ctx % rt == 0 and seq % n_ctx == 0 and seq % GRID_W == 0
    nlt = seq // rt
    nt = (seq + n_ctx) // rt

    xcat = jnp.concatenate([x, ctx, jnp.zeros((b, seq - n_ctx, dm), F32)], axis=1)

    rows = -(-(b + 1) // 8) * 8
    cond = jnp.zeros((rows, dm), F32).at[:b].set(c).at[b].set(c_ctx)
    mods = _adaln(cond, w_mod, b_mod)
    lat = mods[:, :b].reshape(depth, b, 1, 6, dm)
    cx = jnp.broadcast_to(mods[:, b].reshape(depth, 1, 1, 6, dm), (depth, b, 1, 6, dm))
    modsel = jnp.concatenate([lat, cx], axis=2)

    cos, sin = _rope_tables(seq, n_ctx)
    lane = jnp.arange(A_Q)
    gmat = (lane[:, None] // A_HEAD_DIM == lane[None, :] // A_HEAD_DIM).astype(BF16)
    blk = min(256, n_ctx)
    tri = jnp.arange(blk)
    upper = (tri[:, None] <= tri[None, :]).astype(BF16)

    wg = w_exp_gate.astype(BF16)
    wu = w_exp_up.astype(BF16)
    wd = w_exp_down.astype(BF16)
    wr_t = jnp.swapaxes(w_router, 1, 2)

    for i in range(depth):
        ctx_next = any(j % 2 == 0 for j in range(i + 1, depth))
        n_upd = nt if ctx_next else nlt
        ms = modsel[i]
        if i % 2 == 0:
            e = i // 2
            w_in = w_in_even[e]
            pad = jnp.zeros((dm, GATE_PAD - N_GATES), F32)
            w_in = jnp.concatenate([w_in, pad], axis=1).astype(BF16)
            bg = jnp.concatenate([b_gate[e], jnp.zeros((GATE_PAD - N_GATES,), F32)])[None]
            q, k, v, qk, vb, ob, gt = _inproj(
                xcat, ms, g_norm1[i][None], w_in, cos, sin,
                jnp.tile(g_qnorm[e], A_HEADS)[None], jnp.tile(g_knorm[e], A_KV_HEADS)[None], gmat, bg,
                rt=rt, ta=seq + n_ctx, n_lat_tiles=nlt)
            oa = _attention(q, k, v, rt=rt, n_tiles=n_upd, n_lat_tiles=nlt, seq=seq)
            qkc = _short_conv(qk, w_conv[e], seq=seq)
            gates_t = jnp.swapaxes(gt[:, :, :N_GATES], 1, 2)
            hf, hb = _mlstm(qkc, vb, gates_t, seq=seq)
            xcat = _outproj(xcat, ms, oa, hf, hb, ob, g_hnorm[e][None], w_out_even[e].astype(BF16),
                            rt=rt, n_tiles=n_upd, n_lat_tiles=nlt)
        else:
            o = i // 2
            w_in = w_in_odd[o].astype(BF16)
            wgrp = w_pool_grp[o].astype(BF16)
            u = _mod_matmul(xcat, ms, g_norm1[i][None], w_in, rt=rt, tile0=0, n_tiles=nlt, sel=0)
            if ctx_next:
                uc = _mod_matmul(xcat, ms, g_norm1[i][None], w_in, rt=rt, tile0=nlt, n_tiles=nt - nlt, sel=1)
            xcat = _pool(u, xcat, ms, wgrp, s_pool[o][None], n=seq, row_block=0, sel=0)
            if ctx_next:
                xcat = _pool(uc, xcat, ms, wgrp, s_pool[o][None], n=n_ctx, row_block=seq // n_ctx, sel=1)
        xcat = _ec_moe(xcat, ms, g_norm2[i][None], wr_t[i], upper, wg, wu, wd, i,
                       rt=rt, seq=seq, ctx=n_ctx, with_ctx=ctx_next)
    return _final_norm(xcat, g_final[None], rt=rt, seq=seq)
```

```python
import functools

import jax
import jax.numpy as jnp
from jax import lax
from jax.experimental import pallas as pl
from jax.experimental.pallas import tpu as pltpu

D_MODEL = 1024
GRID_W = 64
A_HEADS = 8
A_KV_HEADS = 2
A_HEAD_DIM = 64
ROPE_THETA = 10000.0
B_HEADS = 4
B_HEAD_DIM = 128
B_CHUNK = 128
POOL_WINDOWS = (2, 4, 8, 16)
POOL_GROUP = D_MODEL // 4
N_EXPERTS = 16
EC_FACTOR = 2
NORM_EPS = 1e-6

A_Q = A_HEADS * A_HEAD_DIM
A_KV = A_KV_HEADS * A_HEAD_DIM
B_W = B_HEADS * B_HEAD_DIM
N_GATES = 2 * 2 * B_HEADS
GATE_PAD = 128
C_QA, C_KA, C_VA = 0, A_Q, A_Q + A_KV
C_QK = A_Q + 2 * A_KV
C_VB = C_QK + 2 * B_W
C_OB = C_VB + B_W
C_GT = C_OB + B_W
EVEN_COLS = C_GT + GATE_PAD
POOL_HALO = 16

F32 = jnp.float32
BF16 = jnp.bfloat16
HIGHEST = lax.Precision.HIGHEST
NT_DIMS = (((1,), (1,)), ((), ()))
TN_DIMS = (((0,), (0,)), ((), ()))
VMEM_LIMIT = 56 * 1024 * 1024


def _params(*sem):
    return pltpu.CompilerParams(dimension_semantics=sem, vmem_limit_bytes=VMEM_LIMIT)


def _modulate(x, g, shift, scale):
    y = x * lax.rsqrt(jnp.mean(x * x, axis=-1, keepdims=True) + NORM_EPS)
    return (y * g) * (1.0 + scale) + shift


def _silu(x):
    return x * jax.nn.sigmoid(x)


def _adaln_kernel(c_ref, w_ref, b_ref, o_ref):
    s = _silu(c_ref[...])
    o_ref[0] = jnp.dot(s, w_ref[0], precision=HIGHEST, preferred_element_type=F32) + b_ref[0]


def _adaln(cond, w_mod, b_mod):
    depth, dm, n6 = w_mod.shape
    rows = cond.shape[0]
    tn = 1536
    return pl.pallas_call(
        _adaln_kernel,
        grid=(depth, n6 // tn),
        in_specs=[pl.BlockSpec((rows, dm), lambda l, j: (0, 0)),
                  pl.BlockSpec((1, dm, tn), lambda l, j: (l, 0, j)),
                  pl.BlockSpec((1, 1, tn), lambda l, j: (l, 0, j))],
        out_specs=pl.BlockSpec((1, rows, tn), lambda l, j: (l, 0, j)),
        out_shape=jax.ShapeDtypeStruct((depth, rows, n6), F32),
        compiler_params=_params("arbitrary", "arbitrary"),
        name="adaln",
    )(cond, w_mod, b_mod.reshape(depth, 1, n6))


def _group_mean_sq(x, gmat, width):
    x2 = x * x
    hi = x2.astype(BF16)
    lo = (x2 - hi.astype(F32)).astype(BF16)
    s = jnp.dot(hi, gmat, preferred_element_type=F32) + jnp.dot(lo, gmat, preferred_element_type=F32)
    return s * (1.0 / width)


def _rope(x, cos, sin):
    w = x.shape[1]
    lane = lax.broadcasted_iota(jnp.int32, x.shape, 1)
    first = (lane % 32) < 16
    partner = jnp.where(first, pltpu.roll(x, w - 16, 1), pltpu.roll(x, 16, 1))
    return x * cos + partner * sin


def _inproj_kernel(x_ref, m_ref, g1_ref, w_ref, cos_ref, sin_ref, gq_ref, gk_ref, gm_ref, bg_ref, lp_ref, ls_ref,
                   q_ref, k_ref, v_ref, qk_ref, vb_ref, ob_ref, gt_ref, gc_ref):
    m = m_ref[0, 0]
    h = _modulate(x_ref[0], g1_ref[...], m[0:1], m[1:2]).astype(BF16)

    def mm(lo, hi):
        return jnp.dot(h, w_ref[:, lo:hi], preferred_element_type=F32)

    cos = cos_ref[...]
    sin = sin_ref[...]
    gm = gm_ref[...]
    qa = mm(C_QA, C_KA)
    qa = qa * lax.rsqrt(_group_mean_sq(qa, gm, A_HEAD_DIM) + NORM_EPS) * gq_ref[...]
    q_ref[0] = (_rope(qa, cos, sin) * (A_HEAD_DIM ** -0.5)).astype(BF16)
    ka = mm(C_KA, C_VA)
    ka = ka * lax.rsqrt(_group_mean_sq(ka, gm[:A_KV, :A_KV], A_HEAD_DIM) + NORM_EPS) * gk_ref[...]
    k_ref[0] = _rope(ka, cos[:, :A_KV], sin[:, :A_KV]).astype(BF16)
    va = mm(C_VA, C_QK).astype(BF16)
    ones = jnp.ones((va.shape[0], A_HEAD_DIM), BF16)
    v_ref[0] = jnp.concatenate(
        [piece for g in range(A_KV_HEADS) for piece in (va[:, g * A_HEAD_DIM:(g + 1) * A_HEAD_DIM], ones)], axis=1)
    qk_ref[0] = mm(C_QK, C_VB)
    vb_ref[0] = mm(C_VB, C_OB).astype(BF16)
    ob_ref[0] = mm(C_OB, C_GT)
    gt = mm(C_GT, EVEN_COLS) + bg_ref[...]
    lane = lax.broadcasted_iota(jnp.int32, gt.shape, 1)
    log_sig = jnp.minimum(gt, 0.0) - jnp.log1p(jnp.exp(-jnp.abs(gt)))
    gt = jnp.where((lane % (2 * B_HEADS)) >= B_HEADS, log_sig, gt)
    gt_ref[0] = gt
    prefix = jnp.dot(lp_ref[...], gt, precision=HIGHEST, preferred_element_type=F32)
    suffix = jnp.dot(ls_ref[...], gt, precision=HIGHEST, preferred_element_type=F32)
    gc_ref[0] = jnp.where((lane % (4 * B_HEADS)) < 2 * B_HEADS, prefix, suffix)


def _inproj(xcat, modsel, g1, w, cos, sin, gq, gk, gmat, bgate, *, rt, ta, n_lat_tiles):
    b, _, dm = xcat.shape
    nt = ta // rt
    row = lambda bb, r: (bb, r, 0)
    full = lambda bb, r: (0, 0)
    out_widths = (A_Q, A_KV, 2 * A_KV, 2 * B_W, B_W, B_W, GATE_PAD, GATE_PAD)
    out_dtypes = (BF16, BF16, BF16, F32, BF16, F32, F32, F32)
    idx = jnp.arange(rt)
    same_chunk = idx[:, None] // B_CHUNK == idx[None, :] // B_CHUNK
    lower = jnp.logical_and(same_chunk, idx[None, :] <= idx[:, None]).astype(F32)
    upper = jnp.logical_and(same_chunk, idx[None, :] >= idx[:, None]).astype(F32)
    return pl.pallas_call(
        _inproj_kernel,
        grid=(b, nt),
        in_specs=[pl.BlockSpec((1, rt, dm), row),
                  pl.BlockSpec((1, 1, 6, dm), lambda bb, r: (bb, (r >= n_lat_tiles).astype(jnp.int32), 0, 0)),
                  pl.BlockSpec((1, dm), full),
                  pl.BlockSpec((dm, EVEN_COLS), full),
                  pl.BlockSpec((rt, A_Q), lambda bb, r: (r, 0)),
                  pl.BlockSpec((rt, A_Q), lambda bb, r: (r, 0)),
                  pl.BlockSpec((1, A_Q), full),
                  pl.BlockSpec((1, A_KV), full),
                  pl.BlockSpec((A_Q, A_Q), full),
                  pl.BlockSpec((1, GATE_PAD), full),
                  pl.BlockSpec((rt, rt), full),
                  pl.BlockSpec((rt, rt), full)],
        out_specs=[pl.BlockSpec((1, rt, wd), row) for wd in out_widths],
        out_shape=[jax.ShapeDtypeStruct((b, ta, wd), dt) for wd, dt in zip(out_widths, out_dtypes)],
        compiler_params=_params("parallel", "arbitrary"),
        name="even_inproj",
    )(xcat, modsel, g1, w, cos, sin, gq, gk, gmat, bgate, lower, upper)


def _attn_kernel(q_ref, k_ref, v_ref, o_ref, *, n_lat_tiles, n_tiles, seq):
    group = A_HEADS // A_KV_HEADS

    def attend(key_lo):
        q = q_ref[0]
        for hd in range(A_HEADS):
            g = hd // group
            qh = q[:, hd * A_HEAD_DIM:(hd + 1) * A_HEAD_DIM]
            kg = k_ref[0, key_lo:, g * A_HEAD_DIM:(g + 1) * A_HEAD_DIM]
            vg = v_ref[0, key_lo:, 2 * g * A_HEAD_DIM:2 * (g + 1) * A_HEAD_DIM]
            s = lax.dot_general(qh, kg, NT_DIMS, preferred_element_type=F32)
            p = jnp.exp(s - jnp.max(s, axis=-1, keepdims=True))
            o = jnp.dot(p.astype(BF16), vg, preferred_element_type=F32)
            o = o[:, :A_HEAD_DIM] / o[:, A_HEAD_DIM:]
            o_ref[0, :, hd * A_HEAD_DIM:(hd + 1) * A_HEAD_DIM] = o.astype(BF16)

    if n_tiles == n_lat_tiles:
        attend(0)
    else:
        r = pl.program_id(1)
        pl.when(r < n_lat_tiles)(lambda: attend(0))
        pl.when(r >= n_lat_tiles)(lambda: attend(seq))


def _attention(q, k, v, *, rt, n_tiles, n_lat_tiles, seq):
    b, ta, _ = q.shape
    return pl.pallas_call(
        functools.partial(_attn_kernel, n_lat_tiles=n_lat_tiles, n_tiles=n_tiles, seq=seq),
        grid=(b, n_tiles),
        in_specs=[pl.BlockSpec((1, rt, A_Q), lambda bb, r: (bb, r, 0)),
                  pl.BlockSpec((1, ta, A_KV), lambda bb, r: (bb, 0, 0)),
                  pl.BlockSpec((1, ta, 2 * A_KV), lambda bb, r: (bb, 0, 0))],
        out_specs=pl.BlockSpec((1, rt, A_Q), lambda bb, r: (bb, r, 0)),
        out_shape=jax.ShapeDtypeStruct((b, n_tiles * rt, A_Q), BF16),
        compiler_params=_params("parallel", "arbitrary"),
        name="gqa_attention",
    )(q, k, v)


def _conv_kernel(x_ref, w_ref, o_ref, *, seq, keys):
    x = x_ref[0]
    ta = x.shape[0]
    w = w_ref[...]
    row = lax.broadcasted_iota(jnp.int32, (ta, 1), 0)
    prev = jnp.where(jnp.logical_or(row == 0, row == seq), 0.0, pltpu.roll(x, 1, 0))
    nxt = jnp.where(jnp.logical_or(row == seq - 1, row == ta - 1), 0.0, pltpu.roll(x, ta - 1, 0))
    y = _silu(prev * w[0:1] + x * w[1:2] + nxt * w[2:3])
    if keys:
        o_ref[0] = (y * (B_HEAD_DIM ** -0.5)).T.astype(BF16)
    else:
        o_ref[0] = y.astype(BF16)


def _short_conv(qk, w_conv, *, seq, keys):
    b, ta, _ = qk.shape
    ct = 256
    c0 = B_W // ct if keys else 0
    if keys:
        out_spec = pl.BlockSpec((1, ct, ta), lambda bb, j: (bb, j, 0))
        out_shape = jax.ShapeDtypeStruct((b, B_W, ta), BF16)
    else:
        out_spec = pl.BlockSpec((1, ta, ct), lambda bb, j: (bb, 0, j))
        out_shape = jax.ShapeDtypeStruct((b, ta, B_W), BF16)
    return pl.pallas_call(
        functools.partial(_conv_kernel, seq=seq, keys=keys),
        grid=(b, B_W // ct),
        in_specs=[pl.BlockSpec((1, ta, ct), lambda bb, j: (bb, 0, c0 + j)),
                  pl.BlockSpec((w_conv.shape[0], ct), lambda bb, j: (0, c0 + j))],
        out_specs=out_spec,
        out_shape=out_shape,
        compiler_params=_params("parallel", "arbitrary"),
        name="mlstm_conv_k" if keys else "mlstm_conv_q",
    )(qk, w_conv)


def _mlstm_chain(q, kt, va, ig, b_row, b_col, total, ca, m_old, tri):
    log_d = jnp.where(tri, b_col - b_row + ig, -jnp.inf)
    log_inter = m_old + b_col
    m_t = jnp.maximum(log_inter, jnp.max(log_d, axis=-1, keepdims=True))
    w_intra = jnp.exp(log_d - m_t) * jnp.dot(q, kt, preferred_element_type=F32)
    w_inter = jnp.exp(log_inter - m_t)
    nd = (w_inter * jnp.dot(q, ca.astype(BF16), preferred_element_type=F32)
          + jnp.dot(w_intra.astype(BF16), va, preferred_element_type=F32))
    dv = nd.shape[1] // 2
    h = nd[:, :dv] / jnp.maximum(jnp.abs(nd[:, dv:]), jnp.exp(-m_t))
    log_w = total - b_row + ig
    m_new = jnp.maximum(m_old + total, jnp.max(log_w, axis=-1, keepdims=True))
    w_s = jnp.exp(log_w - m_new)
    decay = jnp.exp(m_old + total - m_new)
    ca_new = decay * ca + jnp.dot((kt * w_s).astype(BF16), va, preferred_element_type=F32)
    return h, ca_new, m_new


def _mlstm_kernel(qf_ref, kf_ref, vf_ref, grf_ref, gcf_ref, qb_ref, kb_ref, vb_ref, grb_ref, gcb_ref,
                  hf_ref, hb_ref, c_ref, m_ref):
    @pl.when(pl.program_id(1) == 0)
    def _():
        c_ref[...] = jnp.zeros(c_ref.shape, F32)
        m_ref[...] = jnp.full(m_ref.shape, -1e30, F32)

    t = B_CHUNK
    r = lax.broadcasted_iota(jnp.int32, (t, t), 0)
    c = lax.broadcasted_iota(jnp.int32, (t, t), 1)
    ones = jnp.ones((t, B_HEAD_DIM), BF16)
    dirs = ((qf_ref, kf_ref, vf_ref, grf_ref, gcf_ref, hf_ref, c <= r, t - 1),
            (qb_ref, kb_ref, vb_ref, grb_ref, gcb_ref, hb_ref, c >= r, 0))
    results = []
    for d, (q_ref, k_ref, v_ref, gr_ref, gc_ref, h_ref, tri, last) in enumerate(dirs):
        for hd in range(B_HEADS):
            sl = slice(hd * B_HEAD_DIM, (hd + 1) * B_HEAD_DIM)
            gi = d * 2 * B_HEADS + hd
            fi = gi + B_HEADS
            ig = gr_ref[0, gi:gi + 1, :]
            b_row = gr_ref[0, N_GATES + fi:N_GATES + fi + 1, :]
            b_col = gc_ref[0, :, fi:fi + 1]
            va = jnp.concatenate([v_ref[0, :, sl], ones], axis=1)
            si = d * B_HEADS + hd
            out = _mlstm_chain(q_ref[0, :, sl], k_ref[0, sl, :], va, ig, b_row, b_col, b_row[:, last:last + 1],
                               c_ref[si], m_ref[si][:, 0:1], tri)
            results.append((h_ref, sl, si) + out)
    for h_ref, sl, si, h, ca_new, m_new in results:
        h_ref[0, :, sl] = h
        c_ref[si] = ca_new
        m_ref[si] = jnp.broadcast_to(m_new, (1, 128))


def _mlstm(qc, kct, vb, gates_row, gates_col, *, seq):
    b, ta, _ = qc.shape
    nc = ta // B_CHUNK
    ncl = seq // B_CHUNK
    ncc = nc - ncl
    fwd = lambda j: jnp.where(j < ncc, ncl + j, j - ncc)
    bwd = lambda j: jnp.where(j < ncc, nc - 1 - j, ncl - 1 - (j - ncc))

    def specs(order):
        return [pl.BlockSpec((1, B_CHUNK, B_W), lambda bb, j: (bb, order(j), 0)),
                pl.BlockSpec((1, B_W, B_CHUNK), lambda bb, j: (bb, 0, order(j))),
                pl.BlockSpec((1, B_CHUNK, B_W), lambda bb, j: (bb, order(j), 0)),
                pl.BlockSpec((1, 2 * N_GATES, B_CHUNK), lambda bb, j: (bb, 0, order(j))),
                pl.BlockSpec((1, B_CHUNK, GATE_PAD), lambda bb, j: (bb, order(j), 0))]

    n_state = 2 * B_HEADS
    ops = (qc, kct, vb, gates_row, gates_col)
    return pl.pallas_call(
        _mlstm_kernel,
        grid=(b, nc),
        in_specs=specs(fwd) + specs(bwd),
        out_specs=[pl.BlockSpec((1, B_CHUNK, B_W), lambda bb, j: (bb, fwd(j), 0)),
                   pl.BlockSpec((1, B_CHUNK, B_W), lambda bb, j: (bb, bwd(j), 0))],
        out_shape=[jax.ShapeDtypeStruct((b, ta, B_W), F32)] * 2,
        scratch_shapes=[pltpu.VMEM((n_state, B_HEAD_DIM, 2 * B_HEAD_DIM), F32),
                        pltpu.VMEM((n_state, 1, 128), F32)],
        compiler_params=_params("parallel", "arbitrary"),
        name="mlstm_scan",
    )(*ops, *ops)


def _outproj_kernel(x_ref, m_ref, oa_ref, hf_ref, hb_ref, ob_ref, ghn_ref, w_ref, o_ref):
    hm = hf_ref[0] + hb_ref[0]
    ob = ob_ref[0]
    ghn = ghn_ref[...]
    y = jnp.dot(oa_ref[0], w_ref[0:A_Q, :], preferred_element_type=F32)
    for hd in range(B_HEADS):
        sl = slice(hd * B_HEAD_DIM, (hd + 1) * B_HEAD_DIM)
        z = hm[:, sl]
        z = z * lax.rsqrt(jnp.mean(z * z, axis=-1, keepdims=True) + NORM_EPS) * ghn[:, sl]
        z = (z * jax.nn.sigmoid(ob[:, sl])).astype(BF16)
        y = y + jnp.dot(z, w_ref[A_Q + hd * B_HEAD_DIM:A_Q + (hd + 1) * B_HEAD_DIM, :],
                        preferred_element_type=F32)
    o_ref[0] = x_ref[0] + m_ref[0, 0][2:3] * y


def _outproj(xcat, modsel, oa, hf, hb, ob, ghn, w, *, rt, n_tiles, n_lat_tiles):
    b, ta, dm = xcat.shape
    row = lambda bb, r: (bb, r, 0)
    full = lambda bb, r: (0, 0)
    return pl.pallas_call(
        _outproj_kernel,
        grid=(b, n_tiles),
        in_specs=[pl.BlockSpec((1, rt, dm), row),
                  pl.BlockSpec((1, 1, 6, dm), lambda bb, r: (bb, (r >= n_lat_tiles).astype(jnp.int32), 0, 0)),
                  pl.BlockSpec((1, rt, A_Q), row),
                  pl.BlockSpec((1, rt, B_W), row),
                  pl.BlockSpec((1, rt, B_W), row),
                  pl.BlockSpec((1, rt, B_W), row),
                  pl.BlockSpec((1, B_W), full),
                  pl.BlockSpec((A_Q + B_W, dm), full)],
        out_specs=pl.BlockSpec((1, rt, dm), row),
        out_shape=jax.ShapeDtypeStruct(xcat.shape, F32),
        input_output_aliases={0: 0},
        compiler_params=_params("parallel", "arbitrary"),
        name="even_outproj",
    )(xcat, modsel, oa, hf, hb, ob, ghn, w)


def _modmm_kernel(x_ref, m_ref, g_ref, w_ref, o_ref):
    m = m_ref[0, 0]
    h = _modulate(x_ref[0], g_ref[...], m[0:1], m[1:2]).astype(BF16)
    o_ref[0] = jnp.dot(h, w_ref[...], preferred_element_type=F32)


def _mod_matmul(xcat, modsel, g, w, *, rt, tile0, n_tiles, sel):
    b, _, dm = xcat.shape
    n_out = w.shape[1]
    return pl.pallas_call(
        _modmm_kernel,
        grid=(b, n_tiles),
        in_specs=[pl.BlockSpec((1, rt, dm), lambda bb, r: (bb, tile0 + r, 0)),
                  pl.BlockSpec((1, 1, 6, dm), lambda bb, r: (bb, sel, 0, 0)),
                  pl.BlockSpec((1, dm), lambda bb, r: (0, 0)),
                  pl.BlockSpec((dm, n_out), lambda bb, r: (0, 0))],
        out_specs=pl.BlockSpec((1, rt, n_out), lambda bb, r: (bb, r, 0)),
        out_shape=jax.ShapeDtypeStruct((b, n_tiles * rt, n_out), F32),
        compiler_params=_params("parallel", "arbitrary"),
        name="odd_inproj",
    )(xcat, modsel, g, w)


def _pool_kernel(u_ref, x_ref, m_ref, wg_ref, sp_ref, o_ref, pad_ref, *, n, rc):
    g = pl.program_id(1)
    zeros = jnp.zeros((POOL_HALO, pad_ref.shape[1]), F32)
    pad_ref[0:POOL_HALO, :] = zeros
    pad_ref[POOL_HALO + n:2 * POOL_HALO + n, :] = zeros
    pad_ref[POOL_HALO:POOL_HALO + n, :] = u_ref[0]
    gate = m_ref[0, 0][2:3]
    for gi, win in enumerate(POOL_WINDOWS):
        @pl.when(g == gi)
        def _(win=win):
            lo = win // 2
            hi = win - 1 - lo
            for r0 in range(0, n, rc):
                base = POOL_HALO + r0
                acc = pad_ref[base - lo:base - lo + rc, :]
                for j in range(-lo + 1, hi + 1):
                    acc = acc + pad_ref[base + j:base + j + rc, :]
                t = r0 + lax.broadcasted_iota(jnp.int32, (rc, 1), 0)
                cnt = jnp.minimum(t + hi, n - 1) - jnp.maximum(t - lo, 0) + 1
                p = acc / cnt.astype(F32) - pad_ref[base:base + rc, :]
                y = jnp.dot(p.astype(BF16), wg_ref[0], preferred_element_type=F32) * sp_ref[...]
                o_ref[0, r0:r0 + rc, :] = x_ref[0, r0:r0 + rc, :] + gate * y


def _pool(u, xcat, modsel, wgrp, spool, *, n, row_block, sel):
    b, _, dm = xcat.shape
    pg = POOL_GROUP
    rc = min(256, n)
    blk = lambda bb, g: (bb, row_block, g)
    return pl.pallas_call(
        functools.partial(_pool_kernel, n=n, rc=rc),
        grid=(b, dm // pg),
        in_specs=[pl.BlockSpec((1, n, pg), lambda bb, g: (bb, 0, g)),
                  pl.BlockSpec((1, n, pg), blk),
                  pl.BlockSpec((1, 1, 6, pg), lambda bb, g: (bb, sel, 0, g)),
                  pl.BlockSpec((1, pg, pg), lambda bb, g: (g, 0, 0)),
                  pl.BlockSpec((1, pg), lambda bb, g: (0, g))],
        out_specs=pl.BlockSpec((1, n, pg), blk),
        out_shape=jax.ShapeDtypeStruct(xcat.shape, F32),
        scratch_shapes=[pltpu.VMEM((n + 2 * POOL_HALO, pg), F32)],
        input_output_aliases={1: 0},
        compiler_params=_params("parallel", "arbitrary"),
        name="pool_mixer",
    )(u, xcat, modsel, wgrp, spool)


def _lane_cumsum(x01, upper, blk):
    n = x01.shape[1]
    out = []
    carry = jnp.zeros((x01.shape[0], 1), F32)
    for j in range(0, n, blk):
        cs = jnp.dot(x01[:, j:j + blk].astype(BF16), upper, preferred_element_type=F32) + carry
        carry = cs[:, blk - 1:blk]
        out.append(cs)
    return jnp.concatenate(out, axis=1) if len(out) > 1 else out[0]


def _router_kernel(x_ref, m_ref, g_ref, wr_ref, up_ref, h_ref, aff_ref, pos_ref, lg_ref, *, cap, n_tiles):
    r = pl.program_id(1)
    m = m_ref[0, 0]
    h = _modulate(x_ref[0], g_ref[...], m[3:4], m[4:5])
    h_ref[0] = h.astype(BF16)
    lg = jnp.dot(h, wr_ref[...], precision=HIGHEST, preferred_element_type=F32)
    lg_ref[r] = lg.T[:lg_ref.shape[1]]

    @pl.when(r == n_tiles - 1)
    def _():
        lg = jnp.concatenate([lg_ref[i] for i in range(n_tiles)], axis=1) if n_tiles > 1 else lg_ref[0]
        e = jnp.exp(lg - jnp.max(lg, axis=0, keepdims=True))
        aff = e / jnp.sum(e, axis=0, keepdims=True)

        def bit_step(i, thr):
            cand = thr | lax.shift_left(jnp.int32(1), 30 - i)
            cnt = jnp.sum(jnp.where(aff >= pltpu.bitcast(cand, F32), 1.0, 0.0), axis=-1, keepdims=True)
            return jnp.where(cnt >= cap, cand, thr)

        thr = lax.fori_loop(0, 31, bit_step, jnp.zeros((aff.shape[0], 1), jnp.int32))
        thr = pltpu.bitcast(thr, F32)
        above = aff > thr
        tied = aff == thr
        room = cap - jnp.sum(jnp.where(above, 1.0, 0.0), axis=-1, keepdims=True)
        upper = up_ref[...]
        blk = upper.shape[0]
        tie_rank = _lane_cumsum(jnp.where(tied, 1.0, 0.0), upper, blk)
        sel = jnp.logical_or(above, jnp.logical_and(tied, tie_rank <= room))
        slot = _lane_cumsum(jnp.where(sel, 1.0, 0.0), upper, blk) - 1.0
        pos_ref[0] = jnp.where(sel, slot, -1.0).astype(jnp.int32)
        aff_ref[0] = aff


def _router(xcat, modsel, g2, wr, upper, *, rt, n, row_tile0, sel):
    b, ta, dm = xcat.shape
    n_tiles = n // rt
    cap = max(1, EC_FACTOR * n // N_EXPERTS)
    ne = N_EXPERTS
    return pl.pallas_call(
        functools.partial(_router_kernel, cap=cap, n_tiles=n_tiles),
        grid=(b, n_tiles),
        in_specs=[pl.BlockSpec((1, rt, dm), lambda bb, r: (bb, row_tile0 + r, 0)),
                  pl.BlockSpec((1, 1, 6, dm), lambda bb, r: (bb, sel, 0, 0)),
                  pl.BlockSpec((1, dm), lambda bb, r: (0, 0)),
                  pl.BlockSpec(wr.shape, lambda bb, r: (0, 0)),
                  pl.BlockSpec(upper.shape, lambda bb, r: (0, 0))],
        out_specs=[pl.BlockSpec((1, rt, dm), lambda bb, r: (bb, r, 0)),
                   pl.BlockSpec((1, ne, n), lambda bb, r: (bb, 0, 0)),
                   pl.BlockSpec((1, ne, n), lambda bb, r: (bb, 0, 0))],
        out_shape=[jax.ShapeDtypeStruct((b, n, dm), BF16),
                   jax.ShapeDtypeStruct((b, ne, n), F32),
                   jax.ShapeDtypeStruct((b, ne, n), jnp.int32)],
        scratch_shapes=[pltpu.VMEM((n_tiles, ne, rt), F32)],
        compiler_params=_params("parallel", "arbitrary"),
        name="moe_router",
    )(xcat, modsel, g2, wr, upper)


def _moe_kernel(pos_ref, aff_ref, h_ref, m_ref, wg_ref, wu_ref, wd_ref, x_ref, o_ref, *, cap, n, rows, ts):
    @pl.when(pl.program_id(1) == 0)
    def _():
        o_ref[...] = x_ref[...]

    pos = pos_ref[0, 0]
    aff = aff_ref[0, 0]
    hit = lax.broadcasted_iota(jnp.int32, (cap, n), 0) == pos
    pick = jnp.where(hit, 1.0, 0.0).astype(BF16)
    gate = jnp.sum(jnp.where(hit, aff, 0.0), axis=-1, keepdims=True)
    xs = jnp.dot(pick, h_ref[0], preferred_element_type=F32).astype(BF16)
    a = jnp.dot(xs, wg_ref[0, 0], preferred_element_type=F32)
    u = jnp.dot(xs, wu_ref[0, 0], preferred_element_type=F32)
    act = (_silu(a) * u).astype(BF16)
    out = jnp.dot(act, wd_ref[0, 0], preferred_element_type=F32) * gate
    out = (out * m_ref[0, 0][5:6]).astype(BF16)
    for t0 in range(0, n, ts):
        upd = lax.dot_general(pick[:, t0:t0 + ts], out, TN_DIMS, preferred_element_type=F32)
        i, r0 = t0 // rows, t0 % rows
        o_ref[i, r0:r0 + ts, :] = o_ref[i, r0:r0 + ts, :] + upd


def _moe(pos, aff, h, modsel, wg, wu, wd, xcat, layer, *, cap, bb, row0, rows, sel):
    b_all, ta, dm = xcat.shape
    nb, n, _ = h.shape
    ne = wg.shape[1]
    df = wg.shape[3]
    ts = min(512, rows)
    out_block = (bb, rows, dm)
    out_index = (lambda b, e: (b, 0, 0)) if bb == 1 else (lambda b, e: (0, row0 // rows, 0))
    return pl.pallas_call(
        functools.partial(_moe_kernel, cap=cap, n=n, rows=rows, ts=ts),
        grid=(nb, ne),
        in_specs=[pl.BlockSpec((1, 1, 1, n), lambda b, e: (b, e, 0, 0)),
                  pl.BlockSpec((1, 1, 1, n), lambda b, e: (b, e, 0, 0)),
                  pl.BlockSpec((1, n, dm), lambda b, e: (b, 0, 0), pipeline_mode=pl.Buffered(1)),
                  pl.BlockSpec((1, 1, 6, dm), lambda b, e: (b, sel, 0, 0)),
                  pl.BlockSpec((1, 1, dm, df), lambda b, e: (layer, e, 0, 0)),
                  pl.BlockSpec((1, 1, dm, df), lambda b, e: (layer, e, 0, 0)),
                  pl.BlockSpec((1, 1, df, dm), lambda b, e: (layer, e, 0, 0)),
                  pl.BlockSpec(out_block, out_index, pipeline_mode=pl.Buffered(1))],
        out_specs=pl.BlockSpec(out_block, out_index),
        out_shape=jax.ShapeDtypeStruct(xcat.shape, F32),
        input_output_aliases={7: 0},
        compiler_params=_params("arbitrary", "arbitrary"),
        name="moe_experts",
    )(pos, aff, h, modsel, wg, wu, wd, xcat)


def _ec_moe(xcat, modsel, g2, wr, upper, wg, wu, wd, layer, *, rt, seq, ctx, with_ctx):
    b = xcat.shape[0]
    ne = N_EXPERTS
    h, aff, pos = _router(xcat, modsel, g2, wr, upper, rt=rt, n=seq, row_tile0=0, sel=0)
    cap = max(1, EC_FACTOR * seq // N_EXPERTS)
    xcat = _moe(pos.reshape(b, ne, 1, seq), aff.reshape(b, ne, 1, seq), h, modsel, wg, wu, wd, xcat, layer,
                cap=cap, bb=1, row0=0, rows=seq, sel=0)
    if with_ctx:
        hc, affc, posc = _router(xcat, modsel, g2, wr, upper, rt=rt, n=ctx, row_tile0=seq // rt, sel=1)
        capc = max(1, EC_FACTOR * ctx // N_EXPERTS)
        offs = (jnp.arange(b, dtype=jnp.int32) * capc)[:, None, None]
        posc = jnp.where(posc >= 0, posc + offs, -1)
        posc = jnp.transpose(posc, (1, 0, 2)).reshape(1, ne, 1, b * ctx)
        affc = jnp.transpose(affc, (1, 0, 2)).reshape(1, ne, 1, b * ctx)
        xcat = _moe(posc, affc, hc.reshape(1, b * ctx, -1), modsel, wg, wu, wd, xcat, layer,
                    cap=b * capc, bb=b, row0=seq, rows=ctx, sel=1)
    return xcat


def _final_kernel(x_ref, g_ref, o_ref):
    x = x_ref[0]
    o_ref[0] = x * lax.rsqrt(jnp.mean(x * x, axis=-1, keepdims=True) + NORM_EPS) * g_ref[...]


def _final_norm(xcat, g, *, rt, seq):
    b, ta, dm = xcat.shape
    return pl.pallas_call(
        _final_kernel,
        grid=(b, seq // rt),
        in_specs=[pl.BlockSpec((1, rt, dm), lambda bb, r: (bb, r, 0)),
                  pl.BlockSpec((1, dm), lambda bb, r: (0, 0))],
        out_specs=pl.BlockSpec((1, rt, dm), lambda bb, r: (bb, r, 0)),
        out_shape=jax.ShapeDtypeStruct((b, seq, dm), F32),
        compiler_params=_params("parallel", "arbitrary"),
        name="final_norm",
    )(xcat, g)


def _rope_tables(seq, ctx):
    rows = seq // GRID_W
    row_ids = jnp.repeat(jnp.arange(rows, dtype=F32), GRID_W)
    col_ids = jnp.tile(jnp.arange(GRID_W, dtype=F32), rows)
    half = A_HEAD_DIM // 2
    inv = ROPE_THETA ** (-jnp.arange(0, half, 2, dtype=F32) / half)
    ang_r = row_ids[:, None] * inv
    ang_c = col_ids[:, None] * inv
    cos = jnp.concatenate([jnp.cos(ang_r)] * 2 + [jnp.cos(ang_c)] * 2, axis=1)
    sin = jnp.concatenate([-jnp.sin(ang_r), jnp.sin(ang_r), -jnp.sin(ang_c), jnp.sin(ang_c)], axis=1)
    cos = jnp.tile(cos, (1, A_HEADS))
    sin = jnp.tile(sin, (1, A_HEADS))
    cos = jnp.concatenate([cos, jnp.ones((ctx, A_Q), F32)], axis=0)
    sin = jnp.concatenate([sin, jnp.zeros((ctx, A_Q), F32)], axis=0)
    return cos, sin


def kernel(x, c, ctx, c_ctx, w_mod, b_mod, g_norm1, g_norm2, w_in_even, w_out_even, g_qnorm, g_knorm, w_conv,
           b_gate, g_hnorm, w_in_odd, w_pool_grp, s_pool, w_router, w_exp_gate, w_exp_up, w_exp_down, g_final):
    b, seq, dm = x.shape
    n_ctx = ctx.shape[1]
    depth = w_mod.shape[0]
    rt = 256 if (seq % 256 == 0 and n_ctx % 256 == 0) else 128
    assert seq % rt == 0 and n_ctx % rt == 0 and seq % n_ctx == 0 and seq % GRID_W == 0
    nlt = seq // rt
    nt = (seq + n_ctx) // rt

    xcat = jnp.concatenate([x, ctx, jnp.zeros((b, seq - n_ctx, dm), F32)], axis=1)

    rows = -(-(b + 1) // 8) * 8
    cond = jnp.zeros((rows, dm), F32).at[:b].set(c).at[b].set(c_ctx)
    mods = _adaln(cond, w_mod, b_mod)
    lat = mods[:, :b].reshape(depth, b, 1, 6, dm)
    cx = jnp.broadcast_to(mods[:, b].reshape(depth, 1, 1, 6, dm), (depth, b, 1, 6, dm))
    modsel = jnp.concatenate([lat, cx], axis=2)

    cos, sin = _rope_tables(seq, n_ctx)
    lane = jnp.arange(A_Q)
    gmat = (lane[:, None] // A_HEAD_DIM == lane[None, :] // A_HEAD_DIM).astype(BF16)
    blk = min(256, n_ctx)
    tri = jnp.arange(blk)
    upper = (tri[:, None] <= tri[None, :]).astype(BF16)

    wg = w_exp_gate.astype(BF16)
    wu = w_exp_up.astype(BF16)
    wd = w_exp_down.astype(BF16)
    wr = jnp.pad(w_router, ((0, 0), (0, 0), (0, GATE_PAD - N_EXPERTS)))

    for i in range(depth):
        ctx_next = any(j % 2 == 0 for j in range(i + 1, depth))
        n_upd = nt if ctx_next else nlt
        ms = modsel[i]
        if i % 2 == 0:
            e = i // 2
            w_in = w_in_even[e]
            pad = jnp.zeros((dm, GATE_PAD - N_GATES), F32)
            w_in = jnp.concatenate([w_in, pad], axis=1).astype(BF16)
            bg = jnp.concatenate([b_gate[e], jnp.zeros((GATE_PAD - N_GATES,), F32)])[None]
            q, k, v, qk, vb, ob, gt, gc = _inproj(
                xcat, ms, g_norm1[i][None], w_in, cos, sin,
                jnp.tile(g_qnorm[e], A_HEADS)[None], jnp.tile(g_knorm[e], A_KV_HEADS)[None], gmat, bg,
                rt=rt, ta=seq + n_ctx, n_lat_tiles=nlt)
            oa = _attention(q, k, v, rt=rt, n_tiles=n_upd, n_lat_tiles=nlt, seq=seq)
            qc = _short_conv(qk, w_conv[e], seq=seq, keys=False)
            kct = _short_conv(qk, w_conv[e], seq=seq, keys=True)
            gates_row = jnp.swapaxes(jnp.concatenate([gt[:, :, :N_GATES], gc[:, :, :N_GATES]], axis=2), 1, 2)
            hf, hb = _mlstm(qc, kct, vb, gates_row, gc, seq=seq)
            xcat = _outproj(xcat, ms, oa, hf, hb, ob, g_hnorm[e][None], w_out_even[e].astype(BF16),
                            rt=rt, n_tiles=n_upd, n_lat_tiles=nlt)
        else:
            o = i // 2
            w_in = w_in_odd[o].astype(BF16)
            wgrp = w_pool_grp[o].astype(BF16)
            u = _mod_matmul(xcat, ms, g_norm1[i][None], w_in, rt=rt, tile0=0, n_tiles=nlt, sel=0)
            if ctx_next:
                uc = _mod_matmul(xcat, ms, g_norm1[i][None], w_in, rt=rt, tile0=nlt, n_tiles=nt - nlt, sel=1)
            xcat = _pool(u, xcat, ms, wgrp, s_pool[o][None], n=seq, row_block=0, sel=0)
            if ctx_next:
                xcat = _pool(uc, xcat, ms, wgrp, s_pool[o][None], n=n_ctx, row_block=seq // n_ctx, sel=1)
        xcat = _ec_moe(xcat, ms, g_norm2[i][None], wr[i], upper, wg, wu, wd, i,
                       rt=rt, seq=seq, ctx=n_ctx, with_ctx=ctx_next)
    return _final_norm(xcat, g_final[None], rt=rt, seq=seq)
```

```python
import functools

import jax
import jax.numpy as jnp
from jax import lax
from jax.experimental import pallas as pl
from jax.experimental.pallas import tpu as pltpu

D_MODEL = 1024
GRID_W = 64
A_HEADS = 8
A_KV_HEADS = 2
A_HEAD_DIM = 64
ROPE_THETA = 10000.0
B_HEADS = 4
B_HEAD_DIM = 128
B_CHUNK = 128
POOL_WINDOWS = (2, 4, 8, 16)
POOL_GROUP = D_MODEL // 4
N_EXPERTS = 16
EC_FACTOR = 2
NORM_EPS = 1e-6

A_Q = A_HEADS * A_HEAD_DIM
A_KV = A_KV_HEADS * A_HEAD_DIM
B_W = B_HEADS * B_HEAD_DIM
N_GATES = 2 * 2 * B_HEADS
GATE_PAD = 128
C_QA, C_KA, C_VA = 0, A_Q, A_Q + A_KV
C_QK = A_Q + 2 * A_KV
C_VB = C_QK + 2 * B_W
C_OB = C_VB + B_W
C_GT = C_OB + B_W
EVEN_COLS = C_GT + GATE_PAD
POOL_HALO = 16

F32 = jnp.float32
BF16 = jnp.bfloat16
HIGHEST = lax.Precision.HIGHEST
NT_DIMS = (((1,), (1,)), ((), ()))
TN_DIMS = (((0,), (0,)), ((), ()))
VMEM_LIMIT = 56 * 1024 * 1024


def _params(*sem):
    return pltpu.CompilerParams(dimension_semantics=sem, vmem_limit_bytes=VMEM_LIMIT)


def _modulate(x, g, shift, scale):
    y = x * lax.rsqrt(jnp.mean(x * x, axis=-1, keepdims=True) + NORM_EPS)
    return (y * g) * (1.0 + scale) + shift


def _silu(x):
    return x * jax.nn.sigmoid(x)


def _adaln_kernel(c_ref, w_ref, b_ref, o_ref):
    s = _silu(c_ref[...])
    o_ref[0] = jnp.dot(s, w_ref[0], precision=HIGHEST, preferred_element_type=F32) + b_ref[0]


def _adaln(cond, w_mod, b_mod):
    depth, dm, n6 = w_mod.shape
    rows = cond.shape[0]
    tn = 1536
    return pl.pallas_call(
        _adaln_kernel,
        grid=(depth, n6 // tn),
        in_specs=[pl.BlockSpec((rows, dm), lambda l, j: (0, 0)),
                  pl.BlockSpec((1, dm, tn), lambda l, j: (l, 0, j)),
                  pl.BlockSpec((1, 1, tn), lambda l, j: (l, 0, j))],
        out_specs=pl.BlockSpec((1, rows, tn), lambda l, j: (l, 0, j)),
        out_shape=jax.ShapeDtypeStruct((depth, rows, n6), F32),
        compiler_params=_params("arbitrary", "arbitrary"),
        name="adaln",
    )(cond, w_mod, b_mod.reshape(depth, 1, n6))


def _group_mean_sq(x, gmat, width):
    x2 = x * x
    hi = x2.astype(BF16)
    lo = (x2 - hi.astype(F32)).astype(BF16)
    s = jnp.dot(hi, gmat, preferred_element_type=F32) + jnp.dot(lo, gmat, preferred_element_type=F32)
    return s * (1.0 / width)


def _rope(x, cos, sin):
    w = x.shape[1]
    lane = lax.broadcasted_iota(jnp.int32, x.shape, 1)
    first = (lane % 32) < 16
    partner = jnp.where(first, pltpu.roll(x, w - 16, 1), pltpu.roll(x, 16, 1))
    return x * cos + partner * sin


def _inproj_kernel(x_ref, m_ref, g1_ref, w_ref, cos_ref, sin_ref, gq_ref, gk_ref, gm_ref, bg_ref, lp_ref, ls_ref,
                   q_ref, k_ref, v_ref, qk_ref, vb_ref, ob_ref, gt_ref, gc_ref):
    m = m_ref[0, 0]
    h = _modulate(x_ref[0], g1_ref[...], m[0:1], m[1:2]).astype(BF16)

    def mm(lo, hi):
        return jnp.dot(h, w_ref[:, lo:hi], preferred_element_type=F32)

    cos = cos_ref[...]
    sin = sin_ref[...]
    gm = gm_ref[...]
    qa = mm(C_QA, C_KA)
    qa = qa * lax.rsqrt(_group_mean_sq(qa, gm, A_HEAD_DIM) + NORM_EPS) * gq_ref[...]
    q_ref[0] = (_rope(qa, cos, sin) * (A_HEAD_DIM ** -0.5)).astype(BF16)
    ka = mm(C_KA, C_VA)
    ka = ka * lax.rsqrt(_group_mean_sq(ka, gm[:A_KV, :A_KV], A_HEAD_DIM) + NORM_EPS) * gk_ref[...]
    k_ref[0] = _rope(ka, cos[:, :A_KV], sin[:, :A_KV]).astype(BF16)
    va = mm(C_VA, C_QK).astype(BF16)
    ones = jnp.ones((va.shape[0], A_HEAD_DIM), BF16)
    v_ref[0] = jnp.concatenate(
        [piece for g in range(A_KV_HEADS) for piece in (va[:, g * A_HEAD_DIM:(g + 1) * A_HEAD_DIM], ones)], axis=1)
    qk_ref[0] = mm(C_QK, C_VB)
    vb_ref[0] = mm(C_VB, C_OB).astype(BF16)
    ob_ref[0] = mm(C_OB, C_GT)
    gt = mm(C_GT, EVEN_COLS) + bg_ref[...]
    lane = lax.broadcasted_iota(jnp.int32, gt.shape, 1)
    log_sig = jnp.minimum(gt, 0.0) - jnp.log1p(jnp.exp(-jnp.abs(gt)))
    gt = jnp.where((lane % (2 * B_HEADS)) >= B_HEADS, log_sig, gt)
    gt_ref[0] = gt
    prefix = jnp.dot(lp_ref[...], gt, precision=HIGHEST, preferred_element_type=F32)
    suffix = jnp.dot(ls_ref[...], gt, precision=HIGHEST, preferred_element_type=F32)
    gc_ref[0] = jnp.where((lane % (4 * B_HEADS)) < 2 * B_HEADS, prefix, suffix)


def _inproj(xcat, modsel, g1, w, cos, sin, gq, gk, gmat, bgate, *, rt, ta, n_lat_tiles):
    b, _, dm = xcat.shape
    nt = ta // rt
    row = lambda bb, r: (bb, r, 0)
    full = lambda bb, r: (0, 0)
    out_widths = (A_Q, A_KV, 2 * A_KV, 2 * B_W, B_W, B_W, GATE_PAD, GATE_PAD)
    out_dtypes = (BF16, BF16, BF16, F32, BF16, F32, F32, F32)
    idx = jnp.arange(rt)
    same_chunk = idx[:, None] // B_CHUNK == idx[None, :] // B_CHUNK
    lower = jnp.logical_and(same_chunk, idx[None, :] <= idx[:, None]).astype(F32)
    upper = jnp.logical_and(same_chunk, idx[None, :] >= idx[:, None]).astype(F32)
    return pl.pallas_call(
        _inproj_kernel,
        grid=(b, nt),
        in_specs=[pl.BlockSpec((1, rt, dm), row),
                  pl.BlockSpec((1, 1, 6, dm), lambda bb, r: (bb, (r >= n_lat_tiles).astype(jnp.int32), 0, 0)),
                  pl.BlockSpec((1, dm), full),
                  pl.BlockSpec((dm, EVEN_COLS), full),
                  pl.BlockSpec((rt, A_Q), lambda bb, r: (r, 0)),
                  pl.BlockSpec((rt, A_Q), lambda bb, r: (r, 0)),
                  pl.BlockSpec((1, A_Q), full),
                  pl.BlockSpec((1, A_KV), full),
                  pl.BlockSpec((A_Q, A_Q), full),
                  pl.BlockSpec((1, GATE_PAD), full),
                  pl.BlockSpec((rt, rt), full),
                  pl.BlockSpec((rt, rt), full)],
        out_specs=[pl.BlockSpec((1, rt, wd), row) for wd in out_widths],
        out_shape=[jax.ShapeDtypeStruct((b, ta, wd), dt) for wd, dt in zip(out_widths, out_dtypes)],
        compiler_params=_params("parallel", "arbitrary"),
        name="even_inproj",
    )(xcat, modsel, g1, w, cos, sin, gq, gk, gmat, bgate, lower, upper)


def _attn_kernel(q_ref, k_ref, v_ref, o_ref, *, n_lat_tiles, n_tiles, seq):
    group = A_HEADS // A_KV_HEADS

    def attend(key_lo):
        q = q_ref[0]
        for hd in range(A_HEADS):
            g = hd // group
            qh = q[:, hd * A_HEAD_DIM:(hd + 1) * A_HEAD_DIM]
            kg = k_ref[0, key_lo:, g * A_HEAD_DIM:(g + 1) * A_HEAD_DIM]
            vg = v_ref[0, key_lo:, 2 * g * A_HEAD_DIM:2 * (g + 1) * A_HEAD_DIM]
            s = lax.dot_general(qh, kg, NT_DIMS, preferred_element_type=F32)
            p = jnp.exp(s - jnp.max(s, axis=-1, keepdims=True))
            o = jnp.dot(p.astype(BF16), vg, preferred_element_type=F32)
            o = o[:, :A_HEAD_DIM] / o[:, A_HEAD_DIM:]
            o_ref[0, :, hd * A_HEAD_DIM:(hd + 1) * A_HEAD_DIM] = o.astype(BF16)

    if n_tiles == n_lat_tiles:
        attend(0)
    else:
        r = pl.program_id(1)
        pl.when(r < n_lat_tiles)(lambda: attend(0))
        pl.when(r >= n_lat_tiles)(lambda: attend(seq))


def _attention(q, k, v, *, rt, n_tiles, n_lat_tiles, seq):
    b, ta, _ = q.shape
    return pl.pallas_call(
        functools.partial(_attn_kernel, n_lat_tiles=n_lat_tiles, n_tiles=n_tiles, seq=seq),
        grid=(b, n_tiles),
        in_specs=[pl.BlockSpec((1, rt, A_Q), lambda bb, r: (bb, r, 0)),
                  pl.BlockSpec((1, ta, A_KV), lambda bb, r: (bb, 0, 0)),
                  pl.BlockSpec((1, ta, 2 * A_KV), lambda bb, r: (bb, 0, 0))],
        out_specs=pl.BlockSpec((1, rt, A_Q), lambda bb, r: (bb, r, 0)),
        out_shape=jax.ShapeDtypeStruct((b, n_tiles * rt, A_Q), BF16),
        compiler_params=_params("parallel", "arbitrary"),
        name="gqa_attention",
    )(q, k, v)


def _conv_kernel(x_ref, w_ref, o_ref, *, seq, keys):
    x = x_ref[0]
    ta = x.shape[0]
    w = w_ref[...]
    row = lax.broadcasted_iota(jnp.int32, (ta, 1), 0)
    prev = jnp.where(jnp.logical_or(row == 0, row == seq), 0.0, pltpu.roll(x, 1, 0))
    nxt = jnp.where(jnp.logical_or(row == seq - 1, row == ta - 1), 0.0, pltpu.roll(x, ta - 1, 0))
    y = _silu(prev * w[0:1] + x * w[1:2] + nxt * w[2:3])
    if keys:
        o_ref[0] = (y * (B_HEAD_DIM ** -0.5)).T.astype(BF16)
    else:
        o_ref[0] = y.astype(BF16)


def _short_conv(qk, w_conv, *, seq, keys):
    b, ta, _ = qk.shape
    ct = 256
    c0 = B_W // ct if keys else 0
    if keys:
        out_spec = pl.BlockSpec((1, ct, ta), lambda bb, j: (bb, j, 0))
        out_shape = jax.ShapeDtypeStruct((b, B_W, ta), BF16)
    else:
        out_spec = pl.BlockSpec((1, ta, ct), lambda bb, j: (bb, 0, j))
        out_shape = jax.ShapeDtypeStruct((b, ta, B_W), BF16)
    return pl.pallas_call(
        functools.partial(_conv_kernel, seq=seq, keys=keys),
        grid=(b, B_W // ct),
        in_specs=[pl.BlockSpec((1, ta, ct), lambda bb, j: (bb, 0, c0 + j)),
                  pl.BlockSpec((w_conv.shape[0], ct), lambda bb, j: (0, c0 + j))],
        out_specs=out_spec,
        out_shape=out_shape,
        compiler_params=_params("parallel", "arbitrary"),
        name="mlstm_conv_k" if keys else "mlstm_conv_q",
    )(qk, w_conv)


def _mlstm_chain(q, kt, va, ig, b_row, b_col, total, ca, m_old, tri):
    log_d = jnp.where(tri, b_col - b_row + ig, -jnp.inf)
    log_inter = m_old + b_col
    m_t = jnp.maximum(log_inter, jnp.max(log_d, axis=-1, keepdims=True))
    w_intra = jnp.exp(log_d - m_t) * jnp.dot(q, kt, preferred_element_type=F32)
    w_inter = jnp.exp(log_inter - m_t)
    nd = (w_inter * jnp.dot(q, ca.astype(BF16), preferred_element_type=F32)
          + jnp.dot(w_intra.astype(BF16), va, preferred_element_type=F32))
    dv = nd.shape[1] // 2
    h = nd[:, :dv] / jnp.maximum(jnp.abs(nd[:, dv:]), jnp.exp(-m_t))
    log_w = total - b_row + ig
    m_new = jnp.maximum(m_old + total, jnp.max(log_w, axis=-1, keepdims=True))
    w_s = jnp.exp(log_w - m_new)
    decay = jnp.exp(m_old + total - m_new)
    ca_new = decay * ca + jnp.dot((kt * w_s).astype(BF16), va, preferred_element_type=F32)
    return h, ca_new, m_new


def _mlstm_kernel(qf_ref, kf_ref, vf_ref, grf_ref, gcf_ref, qb_ref, kb_ref, vb_ref, grb_ref, gcb_ref,
                  hf_ref, hb_ref, c_ref, m_ref):
    @pl.when(pl.program_id(1) == 0)
    def _():
        c_ref[...] = jnp.zeros(c_ref.shape, F32)
        m_ref[...] = jnp.full(m_ref.shape, -1e30, F32)

    t = B_CHUNK
    r = lax.broadcasted_iota(jnp.int32, (t, t), 0)
    c = lax.broadcasted_iota(jnp.int32, (t, t), 1)
    ones = jnp.ones((t, B_HEAD_DIM), BF16)
    dirs = ((qf_ref, kf_ref, vf_ref, grf_ref, gcf_ref, hf_ref, c <= r, t - 1),
            (qb_ref, kb_ref, vb_ref, grb_ref, gcb_ref, hb_ref, c >= r, 0))
    results = []
    for d, (q_ref, k_ref, v_ref, gr_ref, gc_ref, h_ref, tri, last) in enumerate(dirs):
        for hd in range(B_HEADS):
            sl = slice(hd * B_HEAD_DIM, (hd + 1) * B_HEAD_DIM)
            gi = d * 2 * B_HEADS + hd
            fi = gi + B_HEADS
            ig = gr_ref[0, gi:gi + 1, :]
            b_row = gr_ref[0, N_GATES + fi:N_GATES + fi + 1, :]
            b_col = gc_ref[0, :, fi:fi + 1]
            va = jnp.concatenate([v_ref[0, :, sl], ones], axis=1)
            si = d * B_HEADS + hd
            out = _mlstm_chain(q_ref[0, :, sl], k_ref[0, sl, :], va, ig, b_row, b_col, b_row[:, last:last + 1],
                               c_ref[si], m_ref[si][:, 0:1], tri)
            results.append((h_ref, sl, si) + out)
    for h_ref, sl, si, h, ca_new, m_new in results:
        h_ref[0, :, sl] = h
        c_ref[si] = ca_new
        m_ref[si] = jnp.broadcast_to(m_new, (1, 128))


def _mlstm(qc, kct, vb, gates_row, gates_col, *, seq):
    b, ta, _ = qc.shape
    nc = ta // B_CHUNK
    ncl = seq // B_CHUNK
    ncc = nc - ncl
    fwd = lambda j: jnp.where(j < ncc, ncl + j, j - ncc)
    bwd = lambda j: jnp.where(j < ncc, nc - 1 - j, ncl - 1 - (j - ncc))

    def specs(order):
        return [pl.BlockSpec((1, B_CHUNK, B_W), lambda bb, j: (bb, order(j), 0)),
                pl.BlockSpec((1, B_W, B_CHUNK), lambda bb, j: (bb, 0, order(j))),
                pl.BlockSpec((1, B_CHUNK, B_W), lambda bb, j: (bb, order(j), 0)),
                pl.BlockSpec((1, 2 * N_GATES, B_CHUNK), lambda bb, j: (bb, 0, order(j))),
                pl.BlockSpec((1, B_CHUNK, GATE_PAD), lambda bb, j: (bb, order(j), 0))]

    n_state = 2 * B_HEADS
    ops = (qc, kct, vb, gates_row, gates_col)
    return pl.pallas_call(
        _mlstm_kernel,
        grid=(b, nc),
        in_specs=specs(fwd) + specs(bwd),
        out_specs=[pl.BlockSpec((1, B_CHUNK, B_W), lambda bb, j: (bb, fwd(j), 0)),
                   pl.BlockSpec((1, B_CHUNK, B_W), lambda bb, j: (bb, bwd(j), 0))],
        out_shape=[jax.ShapeDtypeStruct((b, ta, B_W), F32)] * 2,
        scratch_shapes=[pltpu.VMEM((n_state, B_HEAD_DIM, 2 * B_HEAD_DIM), F32),
                        pltpu.VMEM((n_state, 1, 128), F32)],
        compiler_params=_params("parallel", "arbitrary"),
        name="mlstm_scan",
    )(*ops, *ops)


def _outproj_kernel(x_ref, m_ref, oa_ref, hf_ref, hb_ref, ob_ref, ghn_ref, w_ref, o_ref):
    hm = hf_ref[0] + hb_ref[0]
    ob = ob_ref[0]
    ghn = ghn_ref[...]
    y = jnp.dot(oa_ref[0], w_ref[0:A_Q, :], preferred_element_type=F32)
    for hd in range(B_HEADS):
        sl = slice(hd * B_HEAD_DIM, (hd + 1) * B_HEAD_DIM)
        z = hm[:, sl]
        z = z * lax.rsqrt(jnp.mean(z * z, axis=-1, keepdims=True) + NORM_EPS) * ghn[:, sl]
        z = (z * jax.nn.sigmoid(ob[:, sl])).astype(BF16)
        y = y + jnp.dot(z, w_ref[A_Q + hd * B_HEAD_DIM:A_Q + (hd + 1) * B_HEAD_DIM, :],
                        preferred_element_type=F32)
    o_ref[0] = x_ref[0] + m_ref[0, 0][2:3] * y


def _outproj(xcat, modsel, oa, hf, hb, ob, ghn, w, *, rt, n_tiles, n_lat_tiles):
    b, ta, dm = xcat.shape
    row = lambda bb, r: (bb, r, 0)
    full = lambda bb, r: (0, 0)
    return pl.pallas_call(
        _outproj_kernel,
        grid=(b, n_tiles),
        in_specs=[pl.BlockSpec((1, rt, dm), row),
                  pl.BlockSpec((1, 1, 6, dm), lambda bb, r: (bb, (r >= n_lat_tiles).astype(jnp.int32), 0, 0)),
                  pl.BlockSpec((1, rt, A_Q), row),
                  pl.BlockSpec((1, rt, B_W), row),
                  pl.BlockSpec((1, rt, B_W), row),
                  pl.BlockSpec((1, rt, B_W), row),
                  pl.BlockSpec((1, B_W), full),
                  pl.BlockSpec((A_Q + B_W, dm), full)],
        out_specs=pl.BlockSpec((1, rt, dm), row),
        out_shape=jax.ShapeDtypeStruct(xcat.shape, F32),
        input_output_aliases={0: 0},
        compiler_params=_params("parallel", "arbitrary"),
        name="even_outproj",
    )(xcat, modsel, oa, hf, hb, ob, ghn, w)


def _modmm_kernel(x_ref, m_ref, g_ref, w_ref, o_ref):
    m = m_ref[0, 0]
    h = _modulate(x_ref[0], g_ref[...], m[0:1], m[1:2]).astype(BF16)
    o_ref[0] = jnp.dot(h, w_ref[...], preferred_element_type=F32)


def _mod_matmul(xcat, modsel, g, w, *, rt, tile0, n_tiles, sel):
    b, _, dm = xcat.shape
    n_out = w.shape[1]
    return pl.pallas_call(
        _modmm_kernel,
        grid=(b, n_tiles),
        in_specs=[pl.BlockSpec((1, rt, dm), lambda bb, r: (bb, tile0 + r, 0)),
                  pl.BlockSpec((1, 1, 6, dm), lambda bb, r: (bb, sel, 0, 0)),
                  pl.BlockSpec((1, dm), lambda bb, r: (0, 0)),
                  pl.BlockSpec((dm, n_out), lambda bb, r: (0, 0))],
        out_specs=pl.BlockSpec((1, rt, n_out), lambda bb, r: (bb, r, 0)),
        out_shape=jax.ShapeDtypeStruct((b, n_tiles * rt, n_out), F32),
        compiler_params=_params("parallel", "arbitrary"),
        name="odd_inproj",
    )(xcat, modsel, g, w)


def _pool_kernel(u_ref, x_ref, m_ref, wg_ref, sp_ref, o_ref, pad_ref, *, n, rc):
    g = pl.program_id(1)
    zeros = jnp.zeros((POOL_HALO, pad_ref.shape[1]), F32)
    pad_ref[0:POOL_HALO, :] = zeros
    pad_ref[POOL_HALO + n:2 * POOL_HALO + n, :] = zeros
    pad_ref[POOL_HALO:POOL_HALO + n, :] = u_ref[0]
    gate = m_ref[0, 0][2:3]
    for gi, win in enumerate(POOL_WINDOWS):
        @pl.when(g == gi)
        def _(win=win):
            lo = win // 2
            hi = win - 1 - lo
            for r0 in range(0, n, rc):
                base = POOL_HALO + r0
                acc = pad_ref[base - lo:base - lo + rc, :]
                for j in range(-lo + 1, hi + 1):
                    acc = acc + pad_ref[base + j:base + j + rc, :]
                t = r0 + lax.broadcasted_iota(jnp.int32, (rc, 1), 0)
                cnt = jnp.minimum(t + hi, n - 1) - jnp.maximum(t - lo, 0) + 1
                p = acc / cnt.astype(F32) - pad_ref[base:base + rc, :]
                y = jnp.dot(p.astype(BF16), wg_ref[0], preferred_element_type=F32) * sp_ref[...]
                o_ref[0, r0:r0 + rc, :] = x_ref[0, r0:r0 + rc, :] + gate * y


def _pool(u, xcat, modsel, wgrp, spool, *, n, row_block, sel):
    b, _, dm = xcat.shape
    pg = POOL_GROUP
    rc = min(256, n)
    blk = lambda bb, g: (bb, row_block, g)
    return pl.pallas_call(
        functools.partial(_pool_kernel, n=n, rc=rc),
        grid=(b, dm // pg),
        in_specs=[pl.BlockSpec((1, n, pg), lambda bb, g: (bb, 0, g)),
                  pl.BlockSpec((1, n, pg), blk),
                  pl.BlockSpec((1, 1, 6, pg), lambda bb, g: (bb, sel, 0, g)),
                  pl.BlockSpec((1, pg, pg), lambda bb, g: (g, 0, 0)),
                  pl.BlockSpec((1, pg), lambda bb, g: (0, g))],
        out_specs=pl.BlockSpec((1, n, pg), blk),
        out_shape=jax.ShapeDtypeStruct(xcat.shape, F32),
        scratch_shapes=[pltpu.VMEM((n + 2 * POOL_HALO, pg), F32)],
        input_output_aliases={1: 0},
        compiler_params=_params("parallel", "arbitrary"),
        name="pool_mixer",
    )(u, xcat, modsel, wgrp, spool)


def _lane_cumsum(x01, upper, blk):
    n = x01.shape[1]
    out = []
    carry = jnp.zeros((x01.shape[0], 1), F32)
    for j in range(0, n, blk):
        cs = jnp.dot(x01[:, j:j + blk].astype(BF16), upper, preferred_element_type=F32) + carry
        carry = cs[:, blk - 1:blk]
        out.append(cs)
    return jnp.concatenate(out, axis=1) if len(out) > 1 else out[0]


def _router_kernel(x_ref, m_ref, g_ref, wr_ref, up_ref, h_ref, aff_ref, pos_ref, lg_ref, *, cap, n_tiles):
    r = pl.program_id(1)
    m = m_ref[0, 0]
    h = _modulate(x_ref[0], g_ref[...], m[3:4], m[4:5])
    h_ref[0] = h.astype(BF16)
    lg = jnp.dot(h, wr_ref[...], precision=HIGHEST, preferred_element_type=F32)
    lg_ref[r] = lg.T[:lg_ref.shape[1]]

    @pl.when(r == n_tiles - 1)
    def _():
        lg = jnp.concatenate([lg_ref[i] for i in range(n_tiles)], axis=1) if n_tiles > 1 else lg_ref[0]
        e = jnp.exp(lg - jnp.max(lg, axis=0, keepdims=True))
        aff = e / jnp.sum(e, axis=0, keepdims=True)

        def bit_step(i, thr):
            cand = thr | lax.shift_left(jnp.int32(1), 30 - i)
            cnt = jnp.sum(jnp.where(aff >= pltpu.bitcast(cand, F32), 1.0, 0.0), axis=-1, keepdims=True)
            return jnp.where(cnt >= cap, cand, thr)

        thr = lax.fori_loop(0, 31, bit_step, jnp.zeros((aff.shape[0], 1), jnp.int32))
        thr = pltpu.bitcast(thr, F32)
        above = aff > thr
        tied = aff == thr
        room = cap - jnp.sum(jnp.where(above, 1.0, 0.0), axis=-1, keepdims=True)
        upper = up_ref[...]
        blk = upper.shape[0]
        tie_rank = _lane_cumsum(jnp.where(tied, 1.0, 0.0), upper, blk)
        sel = jnp.logical_or(above, jnp.logical_and(tied, tie_rank <= room))
        slot = _lane_cumsum(jnp.where(sel, 1.0, 0.0), upper, blk) - 1.0
        pos_ref[0] = jnp.where(sel, slot, -1.0).astype(jnp.int32)
        aff_ref[0] = aff


def _router(xcat, modsel, g2, wr, upper, *, rt, n, row_tile0, sel):
    b, ta, dm = xcat.shape
    n_tiles = n // rt
    cap = max(1, EC_FACTOR * n // N_EXPERTS)
    ne = N_EXPERTS
    return pl.pallas_call(
        functools.partial(_router_kernel, cap=cap, n_tiles=n_tiles),
        grid=(b, n_tiles),
        in_specs=[pl.BlockSpec((1, rt, dm), lambda bb, r: (bb, row_tile0 + r, 0)),
                  pl.BlockSpec((1, 1, 6, dm), lambda bb, r: (bb, sel, 0, 0)),
                  pl.BlockSpec((1, dm), lambda bb, r: (0, 0)),
                  pl.BlockSpec(wr.shape, lambda bb, r: (0, 0)),
                  pl.BlockSpec(upper.shape, lambda bb, r: (0, 0))],
        out_specs=[pl.BlockSpec((1, rt, dm), lambda bb, r: (bb, r, 0)),
                   pl.BlockSpec((1, ne, n), lambda bb, r: (bb, 0, 0)),
                   pl.BlockSpec((1, ne, n), lambda bb, r: (bb, 0, 0))],
        out_shape=[jax.ShapeDtypeStruct((b, n, dm), BF16),
                   jax.ShapeDtypeStruct((b, ne, n), F32),
                   jax.ShapeDtypeStruct((b, ne, n), jnp.int32)],
        scratch_shapes=[pltpu.VMEM((n_tiles, ne, rt), F32)],
        compiler_params=_params("parallel", "arbitrary"),
        name="moe_router",
    )(xcat, modsel, g2, wr, upper)


MOE_EXPERT_BLOCK = 4


def _slot_onehot(pos_ref, cap):
    hits = []
    for i in range(pos_ref.shape[1]):
        pos = pos_ref[0, i]
        hits.append(lax.broadcasted_iota(jnp.int32, (cap, pos.shape[1]), 0) == pos)
    pick = jnp.concatenate([jnp.where(hit, 1.0, 0.0).astype(BF16) for hit in hits], axis=0)
    return pick, hits


def _moe_gather_kernel(pos_ref, aff_ref, h_ref, xs_ref, gate_ref, *, cap):
    pick, hits = _slot_onehot(pos_ref, cap)
    xs = jnp.dot(pick, h_ref[0], preferred_element_type=F32).astype(BF16)
    xs_ref[0] = xs.reshape(xs_ref.shape[1:])
    for i, hit in enumerate(hits):
        gate = jnp.sum(jnp.where(hit, aff_ref[0, i], 0.0), axis=-1, keepdims=True)
        gate_ref[0, i] = jnp.broadcast_to(gate, gate_ref.shape[2:])


def _moe_gather(pos, aff, h, *, cap):
    nb, ne, _, n = pos.shape
    dm = h.shape[2]
    eb = MOE_EXPERT_BLOCK
    return pl.pallas_call(
        functools.partial(_moe_gather_kernel, cap=cap),
        grid=(nb, ne // eb),
        in_specs=[pl.BlockSpec((1, eb, 1, n), lambda b, e: (b, e, 0, 0)),
                  pl.BlockSpec((1, eb, 1, n), lambda b, e: (b, e, 0, 0)),
                  pl.BlockSpec((1, n, dm), lambda b, e: (b, 0, 0), pipeline_mode=pl.Buffered(1))],
        out_specs=[pl.BlockSpec((1, eb, cap, dm), lambda b, e: (b, e, 0, 0)),
                   pl.BlockSpec((1, eb, cap, 128), lambda b, e: (b, e, 0, 0))],
        out_shape=[jax.ShapeDtypeStruct((nb, ne, cap, dm), BF16),
                   jax.ShapeDtypeStruct((nb, ne, cap, 128), F32)],
        compiler_params=_params("parallel", "arbitrary"),
        name="moe_gather",
    )(pos, aff, h)


def _moe_ffn_kernel(xs_ref, gate_ref, m_ref, wg_ref, wu_ref, wd_ref, y_ref, wgb_ref, wub_ref, wdb_ref):
    @pl.when(pl.program_id(1) == 0)
    def _():
        rows = 256
        for src, dst in ((wg_ref, wgb_ref), (wu_ref, wub_ref), (wd_ref, wdb_ref)):
            for r0 in range(0, dst.shape[0], rows):
                dst[r0:r0 + rows, :] = src[0, 0, r0:r0 + rows, :].astype(BF16)

    nb, _, cap, dm = xs_ref.shape
    xs = xs_ref[...].reshape(nb * cap, dm)
    a = jnp.dot(xs, wgb_ref[...], preferred_element_type=F32)
    u = jnp.dot(xs, wub_ref[...], preferred_element_type=F32)
    act = (_silu(a) * u).astype(BF16)
    out = jnp.dot(act, wdb_ref[...], preferred_element_type=F32)
    for i in range(nb):
        y = out[i * cap:(i + 1) * cap] * gate_ref[i, 0][:, 0:1]
        y_ref[i, 0] = (y * m_ref[i, 0][5:6]).astype(BF16)


def _moe_ffn(xs, gate, modsel, wg, wu, wd, layer, *, sel):
    nb_all, ne, cap, dm = xs.shape
    df = wg.shape[3]
    nb = 2 if nb_all % 2 == 0 else 1
    return pl.pallas_call(
        _moe_ffn_kernel,
        grid=(ne, nb_all // nb),
        in_specs=[pl.BlockSpec((nb, 1, cap, dm), lambda e, c: (c, e, 0, 0)),
                  pl.BlockSpec((nb, 1, cap, 128), lambda e, c: (c, e, 0, 0)),
                  pl.BlockSpec((nb, 1, 6, dm), lambda e, c: (c, sel, 0, 0)),
                  pl.BlockSpec((1, 1, dm, df), lambda e, c: (layer, e, 0, 0)),
                  pl.BlockSpec((1, 1, dm, df), lambda e, c: (layer, e, 0, 0)),
                  pl.BlockSpec((1, 1, df, dm), lambda e, c: (layer, e, 0, 0))],
        out_specs=pl.BlockSpec((nb, 1, cap, dm), lambda e, c: (c, e, 0, 0)),
        out_shape=jax.ShapeDtypeStruct(xs.shape, BF16),
        scratch_shapes=[pltpu.VMEM((dm, df), BF16), pltpu.VMEM((dm, df), BF16), pltpu.VMEM((df, dm), BF16)],
        compiler_params=_params("arbitrary", "arbitrary"),
        name="moe_ffn",
    )(xs, gate, modsel, wg, wu, wd)


def _moe_scatter_kernel(pos_ref, y_ref, x_ref, o_ref, *, cap, rows, ts):
    @pl.when(pl.program_id(1) == 0)
    def _():
        o_ref[...] = x_ref[...]

    pick, _ = _slot_onehot(pos_ref, cap)
    n = pick.shape[1]
    y = y_ref[0].reshape(pick.shape[0], y_ref.shape[3])
    for t0 in range(0, n, ts):
        upd = lax.dot_general(pick[:, t0:t0 + ts], y, TN_DIMS, preferred_element_type=F32)
        i, r0 = t0 // rows, t0 % rows
        o_ref[i, r0:r0 + ts, :] = o_ref[i, r0:r0 + ts, :] + upd


def _moe_scatter(pos, y, xcat, *, bb, row0, rows):
    nb, ne, cap, dm = y.shape
    n = pos.shape[3]
    eb = MOE_EXPERT_BLOCK
    ts = min(512, rows)
    out_block = (bb, rows, dm)
    out_index = (lambda b, e: (b, 0, 0)) if bb == 1 else (lambda b, e: (0, row0 // rows, 0))
    return pl.pallas_call(
        functools.partial(_moe_scatter_kernel, cap=cap, rows=rows, ts=ts),
        grid=(nb, ne // eb),
        in_specs=[pl.BlockSpec((1, eb, 1, n), lambda b, e: (b, e, 0, 0)),
                  pl.BlockSpec((1, eb, cap, dm), lambda b, e: (b, e, 0, 0)),
                  pl.BlockSpec(out_block, out_index, pipeline_mode=pl.Buffered(1))],
        out_specs=pl.BlockSpec(out_block, out_index),
        out_shape=jax.ShapeDtypeStruct(xcat.shape, F32),
        input_output_aliases={2: 0},
        compiler_params=_params("arbitrary", "arbitrary"),
        name="moe_scatter",
    )(pos, y, xcat)


def _ec_moe(xcat, modsel, g2, wr, upper, wg, wu, wd, layer, *, rt, seq, ctx, with_ctx):
    b = xcat.shape[0]
    ne = N_EXPERTS
    h, aff, pos = _router(xcat, modsel, g2, wr, upper, rt=rt, n=seq, row_tile0=0, sel=0)
    cap = max(1, EC_FACTOR * seq // N_EXPERTS)
    pos = pos.reshape(b, ne, 1, seq)
    xs, gate = _moe_gather(pos, aff.reshape(b, ne, 1, seq), h, cap=cap)
    y = _moe_ffn(xs, gate, modsel, wg, wu, wd, layer, sel=0)
    xcat = _moe_scatter(pos, y, xcat, bb=1, row0=0, rows=seq)
    if with_ctx:
        hc, affc, posc = _router(xcat, modsel, g2, wr, upper, rt=rt, n=ctx, row_tile0=seq // rt, sel=1)
        capc = max(1, EC_FACTOR * ctx // N_EXPERTS)
        offs = (jnp.arange(b, dtype=jnp.int32) * capc)[:, None, None]
        posc = jnp.where(posc >= 0, posc + offs, -1)
        posc = jnp.transpose(posc, (1, 0, 2)).reshape(1, ne, 1, b * ctx)
        affc = jnp.transpose(affc, (1, 0, 2)).reshape(1, ne, 1, b * ctx)
        xs, gate = _moe_gather(posc, affc, hc.reshape(1, b * ctx, -1), cap=b * capc)
        y = _moe_ffn(xs, gate, modsel, wg, wu, wd, layer, sel=1)
        xcat = _moe_scatter(posc, y, xcat, bb=b, row0=seq, rows=ctx)
    return xcat


def _final_kernel(x_ref, g_ref, o_ref):
    x = x_ref[0]
    o_ref[0] = x * lax.rsqrt(jnp.mean(x * x, axis=-1, keepdims=True) + NORM_EPS) * g_ref[...]


def _final_norm(xcat, g, *, rt, seq):
    b, ta, dm = xcat.shape
    return pl.pallas_call(
        _final_kernel,
        grid=(b, seq // rt),
        in_specs=[pl.BlockSpec((1, rt, dm), lambda bb, r: (bb, r, 0)),
                  pl.BlockSpec((1, dm), lambda bb, r: (0, 0))],
        out_specs=pl.BlockSpec((1, rt, dm), lambda bb, r: (bb, r, 0)),
        out_shape=jax.ShapeDtypeStruct((b, seq, dm), F32),
        compiler_params=_params("parallel", "arbitrary"),
        name="final_norm",
    )(xcat, g)


def _rope_tables(seq, ctx):
    rows = seq // GRID_W
    row_ids = jnp.repeat(jnp.arange(rows, dtype=F32), GRID_W)
    col_ids = jnp.tile(jnp.arange(GRID_W, dtype=F32), rows)
    half = A_HEAD_DIM // 2
    inv = ROPE_THETA ** (-jnp.arange(0, half, 2, dtype=F32) / half)
    ang_r = row_ids[:, None] * inv
    ang_c = col_ids[:, None] * inv
    cos = jnp.concatenate([jnp.cos(ang_r)] * 2 + [jnp.cos(ang_c)] * 2, axis=1)
    sin = jnp.concatenate([-jnp.sin(ang_r), jnp.sin(ang_r), -jnp.sin(ang_c), jnp.sin(ang_c)], axis=1)
    cos = jnp.tile(cos, (1, A_HEADS))
    sin = jnp.tile(sin, (1, A_HEADS))
    cos = jnp.concatenate([cos, jnp.ones((ctx, A_Q), F32)], axis=0)
    sin = jnp.concatenate([sin, jnp.zeros((ctx, A_Q), F32)], axis=0)
    return cos, sin


def kernel(x, c, ctx, c_ctx, w_mod, b_mod, g_norm1, g_norm2, w_in_even, w_out_even, g_qnorm, g_knorm, w_conv,
           b_gate, g_hnorm, w_in_odd, w_pool_grp, s_pool, w_router, w_exp_gate, w_exp_up, w_exp_down, g_final):
    b, seq, dm = x.shape
    n_ctx = ctx.shape[1]
    depth = w_mod.shape[0]
    rt = 256 if (seq % 256 == 0 and n_ctx % 256 == 0) else 128
    assert seq % rt == 0 and n_ctx % rt == 0 and seq % n_ctx == 0 and seq % GRID_W == 0
    nlt = seq // rt
    nt = (seq + n_ctx) // rt

    xcat = jnp.concatenate([x, ctx, jnp.zeros((b, seq - n_ctx, dm), F32)], axis=1)

    rows = -(-(b + 1) // 8) * 8
    cond = jnp.zeros((rows, dm), F32).at[:b].set(c).at[b].set(c_ctx)
    mods = _adaln(cond, w_mod, b_mod)
    lat = mods[:, :b].reshape(depth, b, 1, 6, dm)
    cx = jnp.broadcast_to(mods[:, b].reshape(depth, 1, 1, 6, dm), (depth, b, 1, 6, dm))
    modsel = jnp.concatenate([lat, cx], axis=2)

    cos, sin = _rope_tables(seq, n_ctx)
    lane = jnp.arange(A_Q)
    gmat = (lane[:, None] // A_HEAD_DIM == lane[None, :] // A_HEAD_DIM).astype(BF16)
    blk = min(256, n_ctx)
    tri = jnp.arange(blk)
    upper = (tri[:, None] <= tri[None, :]).astype(BF16)

    wg, wu, wd = w_exp_gate, w_exp_up, w_exp_down
    wr = jnp.pad(w_router, ((0, 0), (0, 0), (0, GATE_PAD - N_EXPERTS)))

    for i in range(depth):
        ctx_next = any(j % 2 == 0 for j in range(i + 1, depth))
        n_upd = nt if ctx_next else nlt
        ms = modsel[i]
        if i % 2 == 0:
            e = i // 2
            w_in = w_in_even[e]
            pad = jnp.zeros((dm, GATE_PAD - N_GATES), F32)
            w_in = jnp.concatenate([w_in, pad], axis=1).astype(BF16)
            bg = jnp.concatenate([b_gate[e], jnp.zeros((GATE_PAD - N_GATES,), F32)])[None]
            q, k, v, qk, vb, ob, gt, gc = _inproj(
                xcat, ms, g_norm1[i][None], w_in, cos, sin,
                jnp.tile(g_qnorm[e], A_HEADS)[None], jnp.tile(g_knorm[e], A_KV_HEADS)[None], gmat, bg,
                rt=rt, ta=seq + n_ctx, n_lat_tiles=nlt)
            oa = _attention(q, k, v, rt=rt, n_tiles=n_upd, n_lat_tiles=nlt, seq=seq)
            qc = _short_conv(qk, w_conv[e], seq=seq, keys=False)
            kct = _short_conv(qk, w_conv[e], seq=seq, keys=True)
            gates_row = jnp.swapaxes(jnp.concatenate([gt[:, :, :N_GATES], gc[:, :, :N_GATES]], axis=2), 1, 2)
            hf, hb = _mlstm(qc, kct, vb, gates_row, gc, seq=seq)
            xcat = _outproj(xcat, ms, oa, hf, hb, ob, g_hnorm[e][None], w_out_even[e].astype(BF16),
                            rt=rt, n_tiles=n_upd, n_lat_tiles=nlt)
        else:
            o = i // 2
            w_in = w_in_odd[o].astype(BF16)
            wgrp = w_pool_grp[o].astype(BF16)
            u = _mod_matmul(xcat, ms, g_norm1[i][None], w_in, rt=rt, tile0=0, n_tiles=nlt, sel=0)
            if ctx_next:
                uc = _mod_matmul(xcat, ms, g_norm1[i][None], w_in, rt=rt, tile0=nlt, n_tiles=nt - nlt, sel=1)
            xcat = _pool(u, xcat, ms, wgrp, s_pool[o][None], n=seq, row_block=0, sel=0)
            if ctx_next:
                xcat = _pool(uc, xcat, ms, wgrp, s_pool[o][None], n=n_ctx, row_block=seq // n_ctx, sel=1)
        xcat = _ec_moe(xcat, ms, g_norm2[i][None], wr[i], upper, wg, wu, wd, i,
                       rt=rt, seq=seq, ctx=n_ctx, with_ctx=ctx_next)
    return _final_norm(xcat, g_final[None], rt=rt, seq=seq)
```

```python
import functools

import jax
import jax.numpy as jnp
from jax import lax
from jax.experimental import pallas as pl
from jax.experimental.pallas import tpu as pltpu

D_MODEL = 1024
GRID_W = 64
A_HEADS = 8
A_KV_HEADS = 2
A_HEAD_DIM = 64
ROPE_THETA = 10000.0
B_HEADS = 4
B_HEAD_DIM = 128
B_CHUNK = 128
POOL_WINDOWS = (2, 4, 8, 16)
POOL_GROUP = D_MODEL // 4
N_EXPERTS = 16
EC_FACTOR = 2
NORM_EPS = 1e-6

A_Q = A_HEADS * A_HEAD_DIM
A_KV = A_KV_HEADS * A_HEAD_DIM
B_W = B_HEADS * B_HEAD_DIM
N_GATES = 2 * 2 * B_HEADS
GATE_PAD = 128
C_QA, C_KA, C_VA = 0, A_Q, A_Q + A_KV
C_QK = A_Q + 2 * A_KV
C_VB = C_QK + 2 * B_W
C_OB = C_VB + B_W
C_GT = C_OB + B_W
EVEN_COLS = C_GT + GATE_PAD
POOL_HALO = 16

F32 = jnp.float32
BF16 = jnp.bfloat16
HIGHEST = lax.Precision.HIGHEST
NT_DIMS = (((1,), (1,)), ((), ()))
TN_DIMS = (((0,), (0,)), ((), ()))
VMEM_LIMIT = 56 * 1024 * 1024


def _params(*sem):
    return pltpu.CompilerParams(dimension_semantics=sem, vmem_limit_bytes=VMEM_LIMIT)


def _modulate(x, g, shift, scale):
    y = x * lax.rsqrt(jnp.mean(x * x, axis=-1, keepdims=True) + NORM_EPS)
    return (y * g) * (1.0 + scale) + shift


def _silu(x):
    return x * jax.nn.sigmoid(x)


def _adaln_kernel(c_ref, w_ref, b_ref, o_ref):
    s = _silu(c_ref[...])
    o_ref[0] = jnp.dot(s, w_ref[0], precision=HIGHEST, preferred_element_type=F32) + b_ref[0]


def _adaln(cond, w_mod, b_mod):
    depth, dm, n6 = w_mod.shape
    rows = cond.shape[0]
    tn = 1536
    return pl.pallas_call(
        _adaln_kernel,
        grid=(depth, n6 // tn),
        in_specs=[pl.BlockSpec((rows, dm), lambda l, j: (0, 0)),
                  pl.BlockSpec((1, dm, tn), lambda l, j: (l, 0, j)),
                  pl.BlockSpec((1, 1, tn), lambda l, j: (l, 0, j))],
        out_specs=pl.BlockSpec((1, rows, tn), lambda l, j: (l, 0, j)),
        out_shape=jax.ShapeDtypeStruct((depth, rows, n6), F32),
        compiler_params=_params("arbitrary", "arbitrary"),
        name="adaln",
    )(cond, w_mod, b_mod.reshape(depth, 1, n6))


def _group_mean_sq(x, gmat, width):
    x2 = x * x
    hi = x2.astype(BF16)
    lo = (x2 - hi.astype(F32)).astype(BF16)
    s = jnp.dot(hi, gmat, preferred_element_type=F32) + jnp.dot(lo, gmat, preferred_element_type=F32)
    return s * (1.0 / width)


def _rope(x, cos, sin):
    w = x.shape[1]
    lane = lax.broadcasted_iota(jnp.int32, x.shape, 1)
    first = (lane % 32) < 16
    partner = jnp.where(first, pltpu.roll(x, w - 16, 1), pltpu.roll(x, 16, 1))
    return x * cos + partner * sin


def _inproj_kernel(x_ref, m_ref, g1_ref, w_ref, cos_ref, sin_ref, gq_ref, gk_ref, gm_ref, bg_ref, lp_ref, ls_ref,
                   q_ref, k_ref, v_ref, qk_ref, vb_ref, ob_ref, gt_ref, gc_ref, gr_ref):
    m = m_ref[0, 0]
    h = _modulate(x_ref[0], g1_ref[...], m[0:1], m[1:2]).astype(BF16)

    def mm(lo, hi):
        return jnp.dot(h, w_ref[:, lo:hi], preferred_element_type=F32)

    cos = cos_ref[...]
    sin = sin_ref[...]
    gm = gm_ref[...]
    qa = mm(C_QA, C_KA)
    qa = qa * lax.rsqrt(_group_mean_sq(qa, gm, A_HEAD_DIM) + NORM_EPS) * gq_ref[...]
    q_ref[0] = (_rope(qa, cos, sin) * (A_HEAD_DIM ** -0.5)).astype(BF16)
    ka = mm(C_KA, C_VA)
    ka = ka * lax.rsqrt(_group_mean_sq(ka, gm[:A_KV, :A_KV], A_HEAD_DIM) + NORM_EPS) * gk_ref[...]
    k_ref[0] = _rope(ka, cos[:, :A_KV], sin[:, :A_KV]).astype(BF16)
    va = mm(C_VA, C_QK).astype(BF16)
    ones = jnp.ones((va.shape[0], A_HEAD_DIM), BF16)
    v_ref[0] = jnp.concatenate(
        [piece for g in range(A_KV_HEADS) for piece in (va[:, g * A_HEAD_DIM:(g + 1) * A_HEAD_DIM], ones)], axis=1)
    qk_ref[0] = mm(C_QK, C_VB)
    vb_ref[0] = mm(C_VB, C_OB).T.astype(BF16)
    ob_ref[0] = mm(C_OB, C_GT)
    gt = mm(C_GT, EVEN_COLS) + bg_ref[...]
    lane = lax.broadcasted_iota(jnp.int32, gt.shape, 1)
    log_sig = jnp.minimum(gt, 0.0) - jnp.log1p(jnp.exp(-jnp.abs(gt)))
    gt = jnp.where((lane % (2 * B_HEADS)) >= B_HEADS, log_sig, gt)
    gt_ref[0] = gt
    prefix = jnp.dot(lp_ref[...], gt, precision=HIGHEST, preferred_element_type=F32)
    suffix = jnp.dot(ls_ref[...], gt, precision=HIGHEST, preferred_element_type=F32)
    cum = jnp.where((lane % (4 * B_HEADS)) < 2 * B_HEADS, prefix, suffix)
    gc_ref[0] = cum
    gr_ref[0] = gt - pltpu.roll(cum, GATE_PAD - B_HEADS, 1)


def _inproj(xcat, modsel, g1, w, cos, sin, gq, gk, gmat, bgate, *, rt, ta, n_lat_tiles):
    b, _, dm = xcat.shape
    nt = ta // rt
    row = lambda bb, r: (bb, r, 0)
    full = lambda bb, r: (0, 0)
    out_widths = (A_Q, A_KV, 2 * A_KV, 2 * B_W, B_W, B_W, GATE_PAD, GATE_PAD, GATE_PAD)
    out_dtypes = (BF16, BF16, BF16, F32, BF16, F32, F32, F32, F32)
    out_specs = [pl.BlockSpec((1, rt, wd), row) for wd in out_widths]
    out_shapes = [jax.ShapeDtypeStruct((b, ta, wd), dt) for wd, dt in zip(out_widths, out_dtypes)]
    vb_slot = 4
    out_specs[vb_slot] = pl.BlockSpec((1, B_W, rt), lambda bb, r: (bb, 0, r))
    out_shapes[vb_slot] = jax.ShapeDtypeStruct((b, B_W, ta), BF16)
    idx = jnp.arange(rt)
    same_chunk = idx[:, None] // B_CHUNK == idx[None, :] // B_CHUNK
    lower = jnp.logical_and(same_chunk, idx[None, :] <= idx[:, None]).astype(F32)
    upper = jnp.logical_and(same_chunk, idx[None, :] >= idx[:, None]).astype(F32)
    return pl.pallas_call(
        _inproj_kernel,
        grid=(b, nt),
        in_specs=[pl.BlockSpec((1, rt, dm), row),
                  pl.BlockSpec((1, 1, 6, dm), lambda bb, r: (bb, (r >= n_lat_tiles).astype(jnp.int32), 0, 0)),
                  pl.BlockSpec((1, dm), full),
                  pl.BlockSpec((dm, EVEN_COLS), full),
                  pl.BlockSpec((rt, A_Q), lambda bb, r: (r, 0)),
                  pl.BlockSpec((rt, A_Q), lambda bb, r: (r, 0)),
                  pl.BlockSpec((1, A_Q), full),
                  pl.BlockSpec((1, A_KV), full),
                  pl.BlockSpec((A_Q, A_Q), full),
                  pl.BlockSpec((1, GATE_PAD), full),
                  pl.BlockSpec((rt, rt), full),
                  pl.BlockSpec((rt, rt), full)],
        out_specs=out_specs,
        out_shape=out_shapes,
        compiler_params=_params("parallel", "arbitrary"),
        name="even_inproj",
    )(xcat, modsel, g1, w, cos, sin, gq, gk, gmat, bgate, lower, upper)


def _attn_kernel(q_ref, k_ref, v_ref, o_ref, *, n_lat_tiles, n_tiles, seq):
    group = A_HEADS // A_KV_HEADS

    def attend(key_lo):
        q = q_ref[0]
        for hd in range(A_HEADS):
            g = hd // group
            qh = q[:, hd * A_HEAD_DIM:(hd + 1) * A_HEAD_DIM]
            kg = k_ref[0, key_lo:, g * A_HEAD_DIM:(g + 1) * A_HEAD_DIM]
            vg = v_ref[0, key_lo:, 2 * g * A_HEAD_DIM:2 * (g + 1) * A_HEAD_DIM]
            s = lax.dot_general(qh, kg, NT_DIMS, preferred_element_type=F32)
            p = jnp.exp(s - jnp.max(s, axis=-1, keepdims=True))
            o = jnp.dot(p.astype(BF16), vg, preferred_element_type=F32)
            o = o[:, :A_HEAD_DIM] / o[:, A_HEAD_DIM:]
            o_ref[0, :, hd * A_HEAD_DIM:(hd + 1) * A_HEAD_DIM] = o.astype(BF16)

    if n_tiles == n_lat_tiles:
        attend(0)
    else:
        r = pl.program_id(1)
        pl.when(r < n_lat_tiles)(lambda: attend(0))
        pl.when(r >= n_lat_tiles)(lambda: attend(seq))


def _attention(q, k, v, *, rt, n_tiles, n_lat_tiles, seq):
    b, ta, _ = q.shape
    return pl.pallas_call(
        functools.partial(_attn_kernel, n_lat_tiles=n_lat_tiles, n_tiles=n_tiles, seq=seq),
        grid=(b, n_tiles),
        in_specs=[pl.BlockSpec((1, rt, A_Q), lambda bb, r: (bb, r, 0)),
                  pl.BlockSpec((1, ta, A_KV), lambda bb, r: (bb, 0, 0)),
                  pl.BlockSpec((1, ta, 2 * A_KV), lambda bb, r: (bb, 0, 0))],
        out_specs=pl.BlockSpec((1, rt, A_Q), lambda bb, r: (bb, r, 0)),
        out_shape=jax.ShapeDtypeStruct((b, n_tiles * rt, A_Q), BF16),
        compiler_params=_params("parallel", "arbitrary"),
        name="gqa_attention",
    )(q, k, v)


def _conv_kernel(x_ref, w_ref, o_ref, *, seq, keys):
    x = x_ref[0]
    ta = x.shape[0]
    w = w_ref[...]
    row = lax.broadcasted_iota(jnp.int32, (ta, 1), 0)
    prev = jnp.where(jnp.logical_or(row == 0, row == seq), 0.0, pltpu.roll(x, 1, 0))
    nxt = jnp.where(jnp.logical_or(row == seq - 1, row == ta - 1), 0.0, pltpu.roll(x, ta - 1, 0))
    y = _silu(prev * w[0:1] + x * w[1:2] + nxt * w[2:3])
    if keys:
        o_ref[0] = (y * (B_HEAD_DIM ** -0.5)).astype(BF16)
    else:
        o_ref[0] = y.T.astype(BF16)


def _short_conv(qk, w_conv, *, seq, keys):
    b, ta, _ = qk.shape
    ct = 256
    c0 = B_W // ct if keys else 0
    if keys:
        out_spec = pl.BlockSpec((1, ta, ct), lambda bb, j: (bb, 0, j))
        out_shape = jax.ShapeDtypeStruct((b, ta, B_W), BF16)
    else:
        out_spec = pl.BlockSpec((1, ct, ta), lambda bb, j: (bb, j, 0))
        out_shape = jax.ShapeDtypeStruct((b, B_W, ta), BF16)
    return pl.pallas_call(
        functools.partial(_conv_kernel, seq=seq, keys=keys),
        grid=(b, B_W // ct),
        in_specs=[pl.BlockSpec((1, ta, ct), lambda bb, j: (bb, 0, c0 + j)),
                  pl.BlockSpec((w_conv.shape[0], ct), lambda bb, j: (0, c0 + j))],
        out_specs=out_spec,
        out_shape=out_shape,
        compiler_params=_params("parallel", "arbitrary"),
        name="mlstm_conv_k" if keys else "mlstm_conv_q",
    )(qk, w_conv)


def _mlstm_chain(qt, k, vat, ig, b_row, r_col, total, ca, m_old, tri):
    log_d = jnp.where(tri, b_row + r_col, -jnp.inf)
    log_inter = m_old + b_row
    m_t = jnp.maximum(log_inter, jnp.max(log_d, axis=0, keepdims=True))
    w_intra = jnp.exp(log_d - m_t) * jnp.dot(k, qt, preferred_element_type=F32)
    w_inter = jnp.exp(log_inter - m_t)
    nd = (w_inter * jnp.dot(ca.astype(BF16), qt, preferred_element_type=F32)
          + jnp.dot(vat, w_intra.astype(BF16), preferred_element_type=F32))
    dv = nd.shape[0] // 2
    h = nd[:dv] / jnp.maximum(jnp.abs(nd[dv:]), jnp.exp(-m_t))
    log_w = total - b_row + ig
    m_new = jnp.maximum(m_old + total, jnp.max(log_w, axis=-1, keepdims=True))
    w_s = jnp.exp(log_w - m_new)
    decay = jnp.exp(m_old + total - m_new)
    ca_new = decay * ca + jnp.dot((vat * w_s).astype(BF16), k, preferred_element_type=F32)
    return h, ca_new, m_new


def _mlstm_kernel(qf_ref, kf_ref, vf_ref, grf_ref, gcf_ref, qb_ref, kb_ref, vb_ref, grb_ref, gcb_ref,
                  hf_ref, hb_ref, c_ref, m_ref):
    @pl.when(pl.program_id(1) == 0)
    def _():
        c_ref[...] = jnp.zeros(c_ref.shape, F32)
        m_ref[...] = jnp.full(m_ref.shape, -1e30, F32)

    t = B_CHUNK
    n_chunks = kf_ref.shape[1] // t
    src = lax.broadcasted_iota(jnp.int32, (t, t), 0)
    tgt = lax.broadcasted_iota(jnp.int32, (t, t), 1)
    ones = jnp.ones((B_HEAD_DIM, t), BF16)
    dirs = ((qf_ref, kf_ref, vf_ref, grf_ref, gcf_ref, hf_ref, src <= tgt, t - 1, range(n_chunks)),
            (qb_ref, kb_ref, vb_ref, grb_ref, gcb_ref, hb_ref, src >= tgt, 0, range(n_chunks - 1, -1, -1)))
    outputs = []
    states = []
    for d, (q_ref, k_ref, v_ref, gr_ref, gc_ref, h_ref, tri, last, order) in enumerate(dirs):
        for hd in range(B_HEADS):
            sl = slice(hd * B_HEAD_DIM, (hd + 1) * B_HEAD_DIM)
            gi = d * 2 * B_HEADS + hd
            fi = gi + B_HEADS
            si = d * B_HEADS + hd
            ca = c_ref[si]
            m = m_ref[si][:, 0:1]
            for ci in order:
                rows = slice(ci * t, (ci + 1) * t)
                ig = gr_ref[0, gi:gi + 1, rows]
                b_row = gr_ref[0, N_GATES + fi:N_GATES + fi + 1, rows]
                r_col = gc_ref[0, rows, gi:gi + 1]
                vat = jnp.concatenate([v_ref[0, sl, rows], ones], axis=0)
                h, ca, m = _mlstm_chain(q_ref[0, sl, rows], k_ref[0, rows, sl], vat, ig, b_row, r_col,
                                        b_row[:, last:last + 1], ca, m, tri)
                outputs.append((h_ref, rows, sl, h))
            states.append((si, ca, m))
    for h_ref, rows, sl, h in outputs:
        h_ref[0, sl, rows] = h
    for si, ca, m in states:
        c_ref[si] = ca
        m_ref[si] = jnp.broadcast_to(m, (1, 128))


def _mlstm(qct, kc, vbt, gates_row, gates_col, *, seq, blk):
    b, ta, _ = kc.shape
    nc = ta // blk
    ncl = seq // blk
    ncc = nc - ncl
    fwd = lambda j: jnp.where(j < ncc, ncl + j, j - ncc)
    bwd = lambda j: jnp.where(j < ncc, nc - 1 - j, ncl - 1 - (j - ncc))

    def specs(order):
        feature_major = pl.BlockSpec((1, B_W, blk), lambda bb, j: (bb, 0, order(j)))
        return [feature_major,
                pl.BlockSpec((1, blk, B_W), lambda bb, j: (bb, order(j), 0)),
                feature_major,
                pl.BlockSpec((1, 2 * N_GATES, blk), lambda bb, j: (bb, 0, order(j))),
                pl.BlockSpec((1, blk, GATE_PAD), lambda bb, j: (bb, order(j), 0))]

    n_state = 2 * B_HEADS
    ops = (qct, kc, vbt, gates_row, gates_col)
    return pl.pallas_call(
        _mlstm_kernel,
        grid=(b, nc),
        in_specs=specs(fwd) + specs(bwd),
        out_specs=[pl.BlockSpec((1, B_W, blk), lambda bb, j: (bb, 0, fwd(j))),
                   pl.BlockSpec((1, B_W, blk), lambda bb, j: (bb, 0, bwd(j)))],
        out_shape=[jax.ShapeDtypeStruct((b, B_W, ta), F32)] * 2,
        scratch_shapes=[pltpu.VMEM((n_state, 2 * B_HEAD_DIM, B_HEAD_DIM), F32),
                        pltpu.VMEM((n_state, 1, 128), F32)],
        compiler_params=_params("parallel", "arbitrary"),
        name="mlstm_scan",
    )(*ops, *ops)


def _outproj_kernel(x_ref, m_ref, oa_ref, hf_ref, hb_ref, ob_ref, ghn_ref, w_ref, o_ref):
    hm = (hf_ref[0] + hb_ref[0]).T
    ob = ob_ref[0]
    ghn = ghn_ref[...]
    y = jnp.dot(oa_ref[0], w_ref[0:A_Q, :], preferred_element_type=F32)
    for hd in range(B_HEADS):
        sl = slice(hd * B_HEAD_DIM, (hd + 1) * B_HEAD_DIM)
        z = hm[:, sl]
        z = z * lax.rsqrt(jnp.mean(z * z, axis=-1, keepdims=True) + NORM_EPS) * ghn[:, sl]
        z = (z * jax.nn.sigmoid(ob[:, sl])).astype(BF16)
        y = y + jnp.dot(z, w_ref[A_Q + hd * B_HEAD_DIM:A_Q + (hd + 1) * B_HEAD_DIM, :],
                        preferred_element_type=F32)
    o_ref[0] = x_ref[0] + m_ref[0, 0][2:3] * y


def _outproj(xcat, modsel, oa, hf, hb, ob, ghn, w, *, rt, n_tiles, n_lat_tiles):
    b, ta, dm = xcat.shape
    row = lambda bb, r: (bb, r, 0)
    full = lambda bb, r: (0, 0)
    return pl.pallas_call(
        _outproj_kernel,
        grid=(b, n_tiles),
        in_specs=[pl.BlockSpec((1, rt, dm), row),
                  pl.BlockSpec((1, 1, 6, dm), lambda bb, r: (bb, (r >= n_lat_tiles).astype(jnp.int32), 0, 0)),
                  pl.BlockSpec((1, rt, A_Q), row),
                  pl.BlockSpec((1, B_W, rt), lambda bb, r: (bb, 0, r)),
                  pl.BlockSpec((1, B_W, rt), lambda bb, r: (bb, 0, r)),
                  pl.BlockSpec((1, rt, B_W), row),
                  pl.BlockSpec((1, B_W), full),
                  pl.BlockSpec((A_Q + B_W, dm), full)],
        out_specs=pl.BlockSpec((1, rt, dm), row),
        out_shape=jax.ShapeDtypeStruct(xcat.shape, F32),
        input_output_aliases={0: 0},
        compiler_params=_params("parallel", "arbitrary"),
        name="even_outproj",
    )(xcat, modsel, oa, hf, hb, ob, ghn, w)


def _modmm_kernel(x_ref, m_ref, g_ref, w_ref, o_ref):
    m = m_ref[0, 0]
    h = _modulate(x_ref[0], g_ref[...], m[0:1], m[1:2]).astype(BF16)
    o_ref[0] = jnp.dot(h, w_ref[...], preferred_element_type=F32)


def _mod_matmul(xcat, modsel, g, w, *, rt, tile0, n_tiles, sel):
    b, _, dm = xcat.shape
    n_out = w.shape[1]
    return pl.pallas_call(
        _modmm_kernel,
        grid=(b, n_tiles),
        in_specs=[pl.BlockSpec((1, rt, dm), lambda bb, r: (bb, tile0 + r, 0)),
                  pl.BlockSpec((1, 1, 6, dm), lambda bb, r: (bb, sel, 0, 0)),
                  pl.BlockSpec((1, dm), lambda bb, r: (0, 0)),
                  pl.BlockSpec((dm, n_out), lambda bb, r: (0, 0))],
        out_specs=pl.BlockSpec((1, rt, n_out), lambda bb, r: (bb, r, 0)),
        out_shape=jax.ShapeDtypeStruct((b, n_tiles * rt, n_out), F32),
        compiler_params=_params("parallel", "arbitrary"),
        name="odd_inproj",
    )(xcat, modsel, g, w)


def _pool_kernel(u_ref, x_ref, m_ref, wg_ref, sp_ref, o_ref, pad_ref, *, n, rc):
    g = pl.program_id(1)
    zeros = jnp.zeros((POOL_HALO, pad_ref.shape[1]), F32)
    pad_ref[0:POOL_HALO, :] = zeros
    pad_ref[POOL_HALO + n:2 * POOL_HALO + n, :] = zeros
    pad_ref[POOL_HALO:POOL_HALO + n, :] = u_ref[0]
    gate = m_ref[0, 0][2:3]
    for gi, win in enumerate(POOL_WINDOWS):
        @pl.when(g == gi)
        def _(win=win):
            lo = win // 2
            hi = win - 1 - lo
            for r0 in range(0, n, rc):
                base = POOL_HALO + r0
                acc = pad_ref[base - lo:base - lo + rc, :]
                for j in range(-lo + 1, hi + 1):
                    acc = acc + pad_ref[base + j:base + j + rc, :]
                t = r0 + lax.broadcasted_iota(jnp.int32, (rc, 1), 0)
                cnt = jnp.minimum(t + hi, n - 1) - jnp.maximum(t - lo, 0) + 1
                p = acc / cnt.astype(F32) - pad_ref[base:base + rc, :]
                y = jnp.dot(p.astype(BF16), wg_ref[0], preferred_element_type=F32) * sp_ref[...]
                o_ref[0, r0:r0 + rc, :] = x_ref[0, r0:r0 + rc, :] + gate * y


def _pool(u, xcat, modsel, wgrp, spool, *, n, row_block, sel):
    b, _, dm = xcat.shape
    pg = POOL_GROUP
    rc = min(256, n)
    blk = lambda bb, g: (bb, row_block, g)
    return pl.pallas_call(
        functools.partial(_pool_kernel, n=n, rc=rc),
        grid=(b, dm // pg),
        in_specs=[pl.BlockSpec((1, n, pg), lambda bb, g: (bb, 0, g)),
                  pl.BlockSpec((1, n, pg), blk),
                  pl.BlockSpec((1, 1, 6, pg), lambda bb, g: (bb, sel, 0, g)),
                  pl.BlockSpec((1, pg, pg), lambda bb, g: (g, 0, 0)),
                  pl.BlockSpec((1, pg), lambda bb, g: (0, g))],
        out_specs=pl.BlockSpec((1, n, pg), blk),
        out_shape=jax.ShapeDtypeStruct(xcat.shape, F32),
        scratch_shapes=[pltpu.VMEM((n + 2 * POOL_HALO, pg), F32)],
        input_output_aliases={1: 0},
        compiler_params=_params("parallel", "arbitrary"),
        name="pool_mixer",
    )(u, xcat, modsel, wgrp, spool)


def _lane_cumsum(x01, upper, blk):
    n = x01.shape[1]
    out = []
    carry = jnp.zeros((x01.shape[0], 1), F32)
    for j in range(0, n, blk):
        cs = jnp.dot(x01[:, j:j + blk].astype(BF16), upper, preferred_element_type=F32) + carry
        carry = cs[:, blk - 1:blk]
        out.append(cs)
    return jnp.concatenate(out, axis=1) if len(out) > 1 else out[0]


def _router_kernel(x_ref, m_ref, g_ref, wr_ref, up_ref, h_ref, aff_ref, pos_ref, lg_ref, *, cap, n_tiles):
    r = pl.program_id(1)
    m = m_ref[0, 0]
    h = _modulate(x_ref[0], g_ref[...], m[3:4], m[4:5])
    h_hi = h.astype(BF16)
    h_ref[0] = h_hi
    h_lo = (h - h_hi.astype(F32)).astype(BF16)
    wr = wr_ref[...]
    w_hi = wr.astype(BF16)
    w_lo = (wr - w_hi.astype(F32)).astype(BF16)
    lg = (jnp.dot(h_hi, w_hi, preferred_element_type=F32)
          + (jnp.dot(h_hi, w_lo, preferred_element_type=F32) + jnp.dot(h_lo, w_hi, preferred_element_type=F32)))
    lg_ref[r] = lg.T[:lg_ref.shape[1]]

    @pl.when(r == n_tiles - 1)
    def _():
        lg = jnp.concatenate([lg_ref[i] for i in range(n_tiles)], axis=1) if n_tiles > 1 else lg_ref[0]
        e = jnp.exp(lg - jnp.max(lg, axis=0, keepdims=True))
        aff = e / jnp.sum(e, axis=0, keepdims=True)

        def bit_step(i, thr):
            cand = thr | lax.shift_left(jnp.int32(1), 30 - i)
            cnt = jnp.sum(jnp.where(aff >= pltpu.bitcast(cand, F32), 1.0, 0.0), axis=-1, keepdims=True)
            return jnp.where(cnt >= cap, cand, thr)

        thr = lax.fori_loop(0, 31, bit_step, jnp.zeros((aff.shape[0], 1), jnp.int32))
        thr = pltpu.bitcast(thr, F32)
        above = aff > thr
        tied = aff == thr
        room = cap - jnp.sum(jnp.where(above, 1.0, 0.0), axis=-1, keepdims=True)
        upper = up_ref[...]
        blk = upper.shape[0]
        tie_rank = _lane_cumsum(jnp.where(tied, 1.0, 0.0), upper, blk)
        sel = jnp.logical_or(above, jnp.logical_and(tied, tie_rank <= room))
        slot = _lane_cumsum(jnp.where(sel, 1.0, 0.0), upper, blk) - 1.0
        pos_ref[0] = jnp.where(sel, slot, -1.0).astype(jnp.int32)
        aff_ref[0] = aff


def _router(xcat, modsel, g2, wr, upper, *, rt, n, row_tile0, sel):
    b, ta, dm = xcat.shape
    n_tiles = n // rt
    cap = max(1, EC_FACTOR * n // N_EXPERTS)
    ne = N_EXPERTS
    return pl.pallas_call(
        functools.partial(_router_kernel, cap=cap, n_tiles=n_tiles),
        grid=(b, n_tiles),
        in_specs=[pl.BlockSpec((1, rt, dm), lambda bb, r: (bb, row_tile0 + r, 0)),
                  pl.BlockSpec((1, 1, 6, dm), lambda bb, r: (bb, sel, 0, 0)),
                  pl.BlockSpec((1, dm), lambda bb, r: (0, 0)),
                  pl.BlockSpec(wr.shape, lambda bb, r: (0, 0)),
                  pl.BlockSpec(upper.shape, lambda bb, r: (0, 0))],
        out_specs=[pl.BlockSpec((1, rt, dm), lambda bb, r: (bb, r, 0)),
                   pl.BlockSpec((1, ne, n), lambda bb, r: (bb, 0, 0)),
                   pl.BlockSpec((1, ne, n), lambda bb, r: (bb, 0, 0))],
        out_shape=[jax.ShapeDtypeStruct((b, n, dm), BF16),
                   jax.ShapeDtypeStruct((b, ne, n), F32),
                   jax.ShapeDtypeStruct((b, ne, n), jnp.int32)],
        scratch_shapes=[pltpu.VMEM((n_tiles, ne, rt), F32)],
        compiler_params=_params("parallel", "arbitrary"),
        name="moe_router",
    )(xcat, modsel, g2, wr, upper)


MOE_EXPERT_BLOCK = 4


def _slot_onehot(pos_ref, cap):
    hits = []
    for i in range(pos_ref.shape[1]):
        pos = pos_ref[0, i]
        hits.append(lax.broadcasted_iota(jnp.int32, (cap, pos.shape[1]), 0) == pos)
    pick = jnp.concatenate([jnp.where(hit, 1.0, 0.0).astype(BF16) for hit in hits], axis=0)
    return pick, hits


def _moe_gather_kernel(pos_ref, aff_ref, h_ref, xs_ref, gate_ref, *, cap):
    pick, hits = _slot_onehot(pos_ref, cap)
    xs = jnp.dot(pick, h_ref[0], preferred_element_type=F32).astype(BF16)
    xs_ref[0] = xs.reshape(xs_ref.shape[1:])
    for i, hit in enumerate(hits):
        gate = jnp.sum(jnp.where(hit, aff_ref[0, i], 0.0), axis=-1, keepdims=True)
        gate_ref[0, i] = jnp.broadcast_to(gate, gate_ref.shape[2:])


def _moe_gather(pos, aff, h, *, cap):
    nb, ne, _, n = pos.shape
    dm = h.shape[2]
    eb = MOE_EXPERT_BLOCK
    return pl.pallas_call(
        functools.partial(_moe_gather_kernel, cap=cap),
        grid=(nb, ne // eb),
        in_specs=[pl.BlockSpec((1, eb, 1, n), lambda b, e: (b, e, 0, 0)),
                  pl.BlockSpec((1, eb, 1, n), lambda b, e: (b, e, 0, 0)),
                  pl.BlockSpec((1, n, dm), lambda b, e: (b, 0, 0), pipeline_mode=pl.Buffered(1))],
        out_specs=[pl.BlockSpec((1, eb, cap, dm), lambda b, e: (b, e, 0, 0)),
                   pl.BlockSpec((1, eb, cap, 128), lambda b, e: (b, e, 0, 0))],
        out_shape=[jax.ShapeDtypeStruct((nb, ne, cap, dm), BF16),
                   jax.ShapeDtypeStruct((nb, ne, cap, 128), F32)],
        compiler_params=_params("parallel", "arbitrary"),
        name="moe_gather",
    )(pos, aff, h)


def _moe_ffn_kernel(xs_ref, gate_ref, m_ref, wg_ref, wu_ref, wd_ref, y_ref, wgb_ref, wub_ref, wdb_ref):
    @pl.when(pl.program_id(1) == 0)
    def _():
        rows = 256
        for src, dst in ((wg_ref, wgb_ref), (wu_ref, wub_ref), (wd_ref, wdb_ref)):
            for r0 in range(0, dst.shape[0], rows):
                dst[r0:r0 + rows, :] = src[0, 0, r0:r0 + rows, :].astype(BF16)

    nb, _, cap, dm = xs_ref.shape
    xs = xs_ref[...].reshape(nb * cap, dm)
    a = jnp.dot(xs, wgb_ref[...], preferred_element_type=F32)
    u = jnp.dot(xs, wub_ref[...], preferred_element_type=F32)
    act = (_silu(a) * u).astype(BF16)
    out = jnp.dot(act, wdb_ref[...], preferred_element_type=F32)
    for i in range(nb):
        y = out[i * cap:(i + 1) * cap] * gate_ref[i, 0][:, 0:1]
        y_ref[i, 0] = (y * m_ref[i, 0][5:6]).astype(BF16)


def _moe_ffn(xs, gate, modsel, wg, wu, wd, layer, *, sel):
    nb_all, ne, cap, dm = xs.shape
    df = wg.shape[3]
    nb = 2 if nb_all % 2 == 0 else 1
    return pl.pallas_call(
        _moe_ffn_kernel,
        grid=(ne, nb_all // nb),
        in_specs=[pl.BlockSpec((nb, 1, cap, dm), lambda e, c: (c, e, 0, 0)),
                  pl.BlockSpec((nb, 1, cap, 128), lambda e, c: (c, e, 0, 0)),
                  pl.BlockSpec((nb, 1, 6, dm), lambda e, c: (c, sel, 0, 0)),
                  pl.BlockSpec((1, 1, dm, df), lambda e, c: (layer, e, 0, 0)),
                  pl.BlockSpec((1, 1, dm, df), lambda e, c: (layer, e, 0, 0)),
                  pl.BlockSpec((1, 1, df, dm), lambda e, c: (layer, e, 0, 0))],
        out_specs=pl.BlockSpec((nb, 1, cap, dm), lambda e, c: (c, e, 0, 0)),
        out_shape=jax.ShapeDtypeStruct(xs.shape, BF16),
        scratch_shapes=[pltpu.VMEM((dm, df), BF16), pltpu.VMEM((dm, df), BF16), pltpu.VMEM((df, dm), BF16)],
        compiler_params=_params("arbitrary", "arbitrary"),
        name="moe_ffn",
    )(xs, gate, modsel, wg, wu, wd)


def _moe_scatter_kernel(pos_ref, y_ref, x_ref, o_ref, *, cap, rows, ts):
    @pl.when(pl.program_id(1) == 0)
    def _():
        o_ref[...] = x_ref[...]

    pick, _ = _slot_onehot(pos_ref, cap)
    n = pick.shape[1]
    y = y_ref[0].reshape(pick.shape[0], y_ref.shape[3])
    for t0 in range(0, n, ts):
        upd = lax.dot_general(pick[:, t0:t0 + ts], y, TN_DIMS, preferred_element_type=F32)
        i, r0 = t0 // rows, t0 % rows
        o_ref[i, r0:r0 + ts, :] = o_ref[i, r0:r0 + ts, :] + upd


def _moe_scatter(pos, y, xcat, *, bb, row0, rows):
    nb, ne, cap, dm = y.shape
    n = pos.shape[3]
    eb = MOE_EXPERT_BLOCK
    ts = min(512, rows)
    out_block = (bb, rows, dm)
    out_index = (lambda b, e: (b, 0, 0)) if bb == 1 else (lambda b, e: (0, row0 // rows, 0))
    return pl.pallas_call(
        functools.partial(_moe_scatter_kernel, cap=cap, rows=rows, ts=ts),
        grid=(nb, ne // eb),
        in_specs=[pl.BlockSpec((1, eb, 1, n), lambda b, e: (b, e, 0, 0)),
                  pl.BlockSpec((1, eb, cap, dm), lambda b, e: (b, e, 0, 0)),
                  pl.BlockSpec(out_block, out_index, pipeline_mode=pl.Buffered(1))],
        out_specs=pl.BlockSpec(out_block, out_index),
        out_shape=jax.ShapeDtypeStruct(xcat.shape, F32),
        input_output_aliases={2: 0},
        compiler_params=_params("arbitrary", "arbitrary"),
        name="moe_scatter",
    )(pos, y, xcat)


def _ec_moe(xcat, modsel, g2, wr, upper, wg, wu, wd, layer, *, rt, seq, ctx, with_ctx):
    b = xcat.shape[0]
    ne = N_EXPERTS
    h, aff, pos = _router(xcat, modsel, g2, wr, upper, rt=rt, n=seq, row_tile0=0, sel=0)
    cap = max(1, EC_FACTOR * seq // N_EXPERTS)
    pos = pos.reshape(b, ne, 1, seq)
    xs, gate = _moe_gather(pos, aff.reshape(b, ne, 1, seq), h, cap=cap)
    y = _moe_ffn(xs, gate, modsel, wg, wu, wd, layer, sel=0)
    xcat = _moe_scatter(pos, y, xcat, bb=1, row0=0, rows=seq)
    if with_ctx:
        hc, affc, posc = _router(xcat, modsel, g2, wr, upper, rt=rt, n=ctx, row_tile0=seq // rt, sel=1)
        capc = max(1, EC_FACTOR * ctx // N_EXPERTS)
        offs = (jnp.arange(b, dtype=jnp.int32) * capc)[:, None, None]
        posc = jnp.where(posc >= 0, posc + offs, -1)
        posc = jnp.transpose(posc, (1, 0, 2)).reshape(1, ne, 1, b * ctx)
        affc = jnp.transpose(affc, (1, 0, 2)).reshape(1, ne, 1, b * ctx)
        xs, gate = _moe_gather(posc, affc, hc.reshape(1, b * ctx, -1), cap=b * capc)
        y = _moe_ffn(xs, gate, modsel, wg, wu, wd, layer, sel=1)
        xcat = _moe_scatter(posc, y, xcat, bb=b, row0=seq, rows=ctx)
    return xcat


def _final_kernel(x_ref, g_ref, o_ref):
    x = x_ref[0]
    o_ref[0] = x * lax.rsqrt(jnp.mean(x * x, axis=-1, keepdims=True) + NORM_EPS) * g_ref[...]


def _final_norm(xcat, g, *, rt, seq):
    b, ta, dm = xcat.shape
    return pl.pallas_call(
        _final_kernel,
        grid=(b, seq // rt),
        in_specs=[pl.BlockSpec((1, rt, dm), lambda bb, r: (bb, r, 0)),
                  pl.BlockSpec((1, dm), lambda bb, r: (0, 0))],
        out_specs=pl.BlockSpec((1, rt, dm), lambda bb, r: (bb, r, 0)),
        out_shape=jax.ShapeDtypeStruct((b, seq, dm), F32),
        compiler_params=_params("parallel", "arbitrary"),
        name="final_norm",
    )(xcat, g)


def _rope_tables(seq, ctx):
    rows = seq // GRID_W
    row_ids = jnp.repeat(jnp.arange(rows, dtype=F32), GRID_W)
    col_ids = jnp.tile(jnp.arange(GRID_W, dtype=F32), rows)
    half = A_HEAD_DIM // 2
    inv = ROPE_THETA ** (-jnp.arange(0, half, 2, dtype=F32) / half)
    ang_r = row_ids[:, None] * inv
    ang_c = col_ids[:, None] * inv
    cos = jnp.concatenate([jnp.cos(ang_r)] * 2 + [jnp.cos(ang_c)] * 2, axis=1)
    sin = jnp.concatenate([-jnp.sin(ang_r), jnp.sin(ang_r), -jnp.sin(ang_c), jnp.sin(ang_c)], axis=1)
    cos = jnp.tile(cos, (1, A_HEADS))
    sin = jnp.tile(sin, (1, A_HEADS))
    cos = jnp.concatenate([cos, jnp.ones((ctx, A_Q), F32)], axis=0)
    sin = jnp.concatenate([sin, jnp.zeros((ctx, A_Q), F32)], axis=0)
    return cos, sin


def kernel(x, c, ctx, c_ctx, w_mod, b_mod, g_norm1, g_norm2, w_in_even, w_out_even, g_qnorm, g_knorm, w_conv,
           b_gate, g_hnorm, w_in_odd, w_pool_grp, s_pool, w_router, w_exp_gate, w_exp_up, w_exp_down, g_final):
    b, seq, dm = x.shape
    n_ctx = ctx.shape[1]
    depth = w_mod.shape[0]
    rt = 256 if (seq % 256 == 0 and n_ctx % 256 == 0) else 128
    assert seq % rt == 0 and n_ctx % rt == 0 and seq % n_ctx == 0 and seq % GRID_W == 0
    nlt = seq // rt
    nt = (seq + n_ctx) // rt

    xcat = jnp.concatenate([x, ctx, jnp.zeros((b, seq - n_ctx, dm), F32)], axis=1)

    rows = -(-(b + 1) // 8) * 8
    cond = jnp.zeros((rows, dm), F32).at[:b].set(c).at[b].set(c_ctx)
    mods = _adaln(cond, w_mod, b_mod)
    lat = mods[:, :b].reshape(depth, b, 1, 6, dm)
    cx = jnp.broadcast_to(mods[:, b].reshape(depth, 1, 1, 6, dm), (depth, b, 1, 6, dm))
    modsel = jnp.concatenate([lat, cx], axis=2)

    cos, sin = _rope_tables(seq, n_ctx)
    lane = jnp.arange(A_Q)
    gmat = (lane[:, None] // A_HEAD_DIM == lane[None, :] // A_HEAD_DIM).astype(BF16)
    blk = min(256, n_ctx)
    tri = jnp.arange(blk)
    upper = (tri[:, None] <= tri[None, :]).astype(BF16)

    wg, wu, wd = w_exp_gate, w_exp_up, w_exp_down
    wr = jnp.pad(w_router, ((0, 0), (0, 0), (0, GATE_PAD - N_EXPERTS)))

    for i in range(depth):
        ctx_next = any(j % 2 == 0 for j in range(i + 1, depth))
        n_upd = nt if ctx_next else nlt
        ms = modsel[i]
        if i % 2 == 0:
            e = i // 2
            w_in = w_in_even[e]
            pad = jnp.zeros((dm, GATE_PAD - N_GATES), F32)
            w_in = jnp.concatenate([w_in, pad], axis=1).astype(BF16)
            bg = jnp.concatenate([b_gate[e], jnp.zeros((GATE_PAD - N_GATES,), F32)])[None]
            q, k, v, qk, vbt, ob, gt, gc, gr = _inproj(
                xcat, ms, g_norm1[i][None], w_in, cos, sin,
                jnp.tile(g_qnorm[e], A_HEADS)[None], jnp.tile(g_knorm[e], A_KV_HEADS)[None], gmat, bg,
                rt=rt, ta=seq + n_ctx, n_lat_tiles=nlt)
            oa = _attention(q, k, v, rt=rt, n_tiles=n_upd, n_lat_tiles=nlt, seq=seq)
            qct = _short_conv(qk, w_conv[e], seq=seq, keys=False)
            kc = _short_conv(qk, w_conv[e], seq=seq, keys=True)
            gates_row = jnp.swapaxes(jnp.concatenate([gt[:, :, :N_GATES], gc[:, :, :N_GATES]], axis=2), 1, 2)
            hf, hb = _mlstm(qct, kc, vbt, gates_row, gr, seq=seq, blk=rt)
            xcat = _outproj(xcat, ms, oa, hf, hb, ob, g_hnorm[e][None], w_out_even[e].astype(BF16),
                            rt=rt, n_tiles=n_upd, n_lat_tiles=nlt)
        else:
            o = i // 2
            w_in = w_in_odd[o].astype(BF16)
            wgrp = w_pool_grp[o].astype(BF16)
            u = _mod_matmul(xcat, ms, g_norm1[i][None], w_in, rt=rt, tile0=0, n_tiles=nlt, sel=0)
            if ctx_next:
                uc = _mod_matmul(xcat, ms, g_norm1[i][None], w_in, rt=rt, tile0=nlt, n_tiles=nt - nlt, sel=1)
            xcat = _pool(u, xcat, ms, wgrp, s_pool[o][None], n=seq, row_block=0, sel=0)
            if ctx_next:
                xcat = _pool(uc, xcat, ms, wgrp, s_pool[o][None], n=n_ctx, row_block=seq // n_ctx, sel=1)
        xcat = _ec_moe(xcat, ms, g_norm2[i][None], wr[i], upper, wg, wu, wd, i,
                       rt=rt, seq=seq, ctx=n_ctx, with_ctx=ctx_next)
    return _final_norm(xcat, g_final[None], rt=rt, seq=seq)
```

```python
import functools

import jax
import jax.numpy as jnp
from jax import lax
from jax.experimental import pallas as pl
from jax.experimental.pallas import tpu as pltpu

D_MODEL = 1024
GRID_W = 64
A_HEADS = 8
A_KV_HEADS = 2
A_HEAD_DIM = 64
ROPE_THETA = 10000.0
B_HEADS = 4
B_HEAD_DIM = 128
B_CHUNK = 128
POOL_WINDOWS = (2, 4, 8, 16)
POOL_GROUP = D_MODEL // 4
N_EXPERTS = 16
EC_FACTOR = 2
NORM_EPS = 1e-6

A_Q = A_HEADS * A_HEAD_DIM
A_KV = A_KV_HEADS * A_HEAD_DIM
B_W = B_HEADS * B_HEAD_DIM
N_GATES = 2 * 2 * B_HEADS
GATE_PAD = 128
C_QA, C_KA, C_VA = 0, A_Q, A_Q + A_KV
C_QK = A_Q + 2 * A_KV
C_VB = C_QK + 2 * B_W
C_OB = C_VB + B_W
C_GT = C_OB + B_W
EVEN_COLS = C_GT + GATE_PAD
POOL_HALO = 16

F32 = jnp.float32
BF16 = jnp.bfloat16
HIGHEST = lax.Precision.HIGHEST
NT_DIMS = (((1,), (1,)), ((), ()))
TN_DIMS = (((0,), (0,)), ((), ()))
VMEM_LIMIT = 56 * 1024 * 1024


def _params(*sem):
    return pltpu.CompilerParams(dimension_semantics=sem, vmem_limit_bytes=VMEM_LIMIT)


def _modulate(x, g, shift, scale):
    y = x * lax.rsqrt(jnp.mean(x * x, axis=-1, keepdims=True) + NORM_EPS)
    return (y * g) * (1.0 + scale) + shift


def _silu(x):
    return x * jax.nn.sigmoid(x)


def _adaln_kernel(c_ref, w_ref, b_ref, o_ref):
    s = _silu(c_ref[...])
    o_ref[0] = jnp.dot(s, w_ref[0], precision=HIGHEST, preferred_element_type=F32) + b_ref[0]


def _adaln(cond, w_mod, b_mod):
    depth, dm, n6 = w_mod.shape
    rows = cond.shape[0]
    tn = 1536
    return pl.pallas_call(
        _adaln_kernel,
        grid=(depth, n6 // tn),
        in_specs=[pl.BlockSpec((rows, dm), lambda l, j: (0, 0)),
                  pl.BlockSpec((1, dm, tn), lambda l, j: (l, 0, j)),
                  pl.BlockSpec((1, 1, tn), lambda l, j: (l, 0, j))],
        out_specs=pl.BlockSpec((1, rows, tn), lambda l, j: (l, 0, j)),
        out_shape=jax.ShapeDtypeStruct((depth, rows, n6), F32),
        compiler_params=_params("arbitrary", "arbitrary"),
        name="adaln",
    )(cond, w_mod, b_mod.reshape(depth, 1, n6))


def _group_mean_sq(x, gmat, width):
    x2 = x * x
    hi = x2.astype(BF16)
    lo = (x2 - hi.astype(F32)).astype(BF16)
    s = jnp.dot(hi, gmat, preferred_element_type=F32) + jnp.dot(lo, gmat, preferred_element_type=F32)
    return s * (1.0 / width)


def _rope(x, cos, sin):
    w = x.shape[1]
    lane = lax.broadcasted_iota(jnp.int32, x.shape, 1)
    first = (lane % 32) < 16
    partner = jnp.where(first, pltpu.roll(x, w - 16, 1), pltpu.roll(x, 16, 1))
    return x * cos + partner * sin


def _inproj_kernel(x_ref, m_ref, g1_ref, w_ref, cos_ref, sin_ref, gq_ref, gk_ref, gm_ref, bg_ref, lp_ref, ls_ref,
                   q_ref, k_ref, v_ref, qk_ref, vb_ref, ob_ref, gt_ref, gc_ref, gr_ref):
    m = m_ref[0, 0]
    h = _modulate(x_ref[0], g1_ref[...], m[0:1], m[1:2]).astype(BF16)

    def mm(lo, hi):
        return jnp.dot(h, w_ref[:, lo:hi], preferred_element_type=F32)

    cos = cos_ref[...]
    sin = sin_ref[...]
    gm = gm_ref[...]
    qa = mm(C_QA, C_KA)
    qa = qa * lax.rsqrt(_group_mean_sq(qa, gm, A_HEAD_DIM) + NORM_EPS) * gq_ref[...]
    q_ref[0] = (_rope(qa, cos, sin) * (A_HEAD_DIM ** -0.5)).astype(BF16)
    ka = mm(C_KA, C_VA)
    ka = ka * lax.rsqrt(_group_mean_sq(ka, gm[:A_KV, :A_KV], A_HEAD_DIM) + NORM_EPS) * gk_ref[...]
    k_ref[0] = _rope(ka, cos[:, :A_KV], sin[:, :A_KV]).astype(BF16)
    va = mm(C_VA, C_QK).astype(BF16)
    ones = jnp.ones((va.shape[0], A_HEAD_DIM), BF16)
    v_ref[0] = jnp.concatenate(
        [piece for g in range(A_KV_HEADS) for piece in (va[:, g * A_HEAD_DIM:(g + 1) * A_HEAD_DIM], ones)], axis=1)
    qk_ref[0] = mm(C_QK, C_VB)
    vb_ref[0] = mm(C_VB, C_OB).T.astype(BF16)
    ob_ref[0] = mm(C_OB, C_GT)
    gt = mm(C_GT, EVEN_COLS) + bg_ref[...]
    lane = lax.broadcasted_iota(jnp.int32, gt.shape, 1)
    log_sig = jnp.minimum(gt, 0.0) - jnp.log1p(jnp.exp(-jnp.abs(gt)))
    gt = jnp.where((lane % (2 * B_HEADS)) >= B_HEADS, log_sig, gt)
    gt_ref[0] = gt
    prefix = jnp.dot(lp_ref[...], gt, precision=HIGHEST, preferred_element_type=F32)
    suffix = jnp.dot(ls_ref[...], gt, precision=HIGHEST, preferred_element_type=F32)
    cum = jnp.where((lane % (4 * B_HEADS)) < 2 * B_HEADS, prefix, suffix)
    gc_ref[0] = cum
    gr_ref[0] = gt - pltpu.roll(cum, GATE_PAD - B_HEADS, 1)


def _inproj(xcat, modsel, g1, w, cos, sin, gq, gk, gmat, bgate, *, rt, ta, n_lat_tiles):
    b, _, dm = xcat.shape
    nt = ta // rt
    row = lambda bb, r: (bb, r, 0)
    full = lambda bb, r: (0, 0)
    out_widths = (A_Q, A_KV, 2 * A_KV, 2 * B_W, B_W, B_W, GATE_PAD, GATE_PAD, GATE_PAD)
    out_dtypes = (BF16, BF16, BF16, F32, BF16, F32, F32, F32, F32)
    out_specs = [pl.BlockSpec((1, rt, wd), row) for wd in out_widths]
    out_shapes = [jax.ShapeDtypeStruct((b, ta, wd), dt) for wd, dt in zip(out_widths, out_dtypes)]
    vb_slot = 4
    out_specs[vb_slot] = pl.BlockSpec((1, B_W, rt), lambda bb, r: (bb, 0, r))
    out_shapes[vb_slot] = jax.ShapeDtypeStruct((b, B_W, ta), BF16)
    idx = jnp.arange(rt)
    same_chunk = idx[:, None] // B_CHUNK == idx[None, :] // B_CHUNK
    lower = jnp.logical_and(same_chunk, idx[None, :] <= idx[:, None]).astype(F32)
    upper = jnp.logical_and(same_chunk, idx[None, :] >= idx[:, None]).astype(F32)
    return pl.pallas_call(
        _inproj_kernel,
        grid=(b, nt),
        in_specs=[pl.BlockSpec((1, rt, dm), row),
                  pl.BlockSpec((1, 1, 6, dm), lambda bb, r: (bb, (r >= n_lat_tiles).astype(jnp.int32), 0, 0)),
                  pl.BlockSpec((1, dm), full),
                  pl.BlockSpec((dm, EVEN_COLS), full),
                  pl.BlockSpec((rt, A_Q), lambda bb, r: (r, 0)),
                  pl.BlockSpec((rt, A_Q), lambda bb, r: (r, 0)),
                  pl.BlockSpec((1, A_Q), full),
                  pl.BlockSpec((1, A_KV), full),
                  pl.BlockSpec((A_Q, A_Q), full),
                  pl.BlockSpec((1, GATE_PAD), full),
                  pl.BlockSpec((rt, rt), full),
                  pl.BlockSpec((rt, rt), full)],
        out_specs=out_specs,
        out_shape=out_shapes,
        compiler_params=_params("parallel", "arbitrary"),
        name="even_inproj",
    )(xcat, modsel, g1, w, cos, sin, gq, gk, gmat, bgate, lower, upper)


def _attn_kernel(q_ref, k_ref, v_ref, o_ref, *, n_lat_tiles, n_tiles, seq):
    group = A_HEADS // A_KV_HEADS

    def attend(key_lo):
        q = q_ref[0]
        for hd in range(A_HEADS):
            g = hd // group
            qh = q[:, hd * A_HEAD_DIM:(hd + 1) * A_HEAD_DIM]
            kg = k_ref[0, key_lo:, g * A_HEAD_DIM:(g + 1) * A_HEAD_DIM]
            vg = v_ref[0, key_lo:, 2 * g * A_HEAD_DIM:2 * (g + 1) * A_HEAD_DIM]
            s = lax.dot_general(qh, kg, NT_DIMS, preferred_element_type=F32)
            p = jnp.exp(s - jnp.max(s, axis=-1, keepdims=True))
            o = jnp.dot(p.astype(BF16), vg, preferred_element_type=F32)
            o = o[:, :A_HEAD_DIM] / o[:, A_HEAD_DIM:]
            o_ref[0, :, hd * A_HEAD_DIM:(hd + 1) * A_HEAD_DIM] = o.astype(BF16)

    if n_tiles == n_lat_tiles:
        attend(0)
    else:
        r = pl.program_id(1)
        pl.when(r < n_lat_tiles)(lambda: attend(0))
        pl.when(r >= n_lat_tiles)(lambda: attend(seq))


def _attention(q, k, v, *, rt, n_tiles, n_lat_tiles, seq):
    b, ta, _ = q.shape
    return pl.pallas_call(
        functools.partial(_attn_kernel, n_lat_tiles=n_lat_tiles, n_tiles=n_tiles, seq=seq),
        grid=(b, n_tiles),
        in_specs=[pl.BlockSpec((1, rt, A_Q), lambda bb, r: (bb, r, 0)),
                  pl.BlockSpec((1, ta, A_KV), lambda bb, r: (bb, 0, 0)),
                  pl.BlockSpec((1, ta, 2 * A_KV), lambda bb, r: (bb, 0, 0))],
        out_specs=pl.BlockSpec((1, rt, A_Q), lambda bb, r: (bb, r, 0)),
        out_shape=jax.ShapeDtypeStruct((b, n_tiles * rt, A_Q), BF16),
        compiler_params=_params("parallel", "arbitrary"),
        name="gqa_attention",
    )(q, k, v)


def _conv_kernel(x_ref, w_ref, o_ref, *, seq, keys):
    x = x_ref[0]
    ta = x.shape[0]
    w = w_ref[...]
    row = lax.broadcasted_iota(jnp.int32, (ta, 1), 0)
    prev = jnp.where(jnp.logical_or(row == 0, row == seq), 0.0, pltpu.roll(x, 1, 0))
    nxt = jnp.where(jnp.logical_or(row == seq - 1, row == ta - 1), 0.0, pltpu.roll(x, ta - 1, 0))
    y = _silu(prev * w[0:1] + x * w[1:2] + nxt * w[2:3])
    if keys:
        o_ref[0] = (y * (B_HEAD_DIM ** -0.5)).astype(BF16)
    else:
        o_ref[0] = y.T.astype(BF16)


def _short_conv(qk, w_conv, *, seq, keys):
    b, ta, _ = qk.shape
    ct = 256
    c0 = B_W // ct if keys else 0
    if keys:
        out_spec = pl.BlockSpec((1, ta, ct), lambda bb, j: (bb, 0, j))
        out_shape = jax.ShapeDtypeStruct((b, ta, B_W), BF16)
    else:
        out_spec = pl.BlockSpec((1, ct, ta), lambda bb, j: (bb, j, 0))
        out_shape = jax.ShapeDtypeStruct((b, B_W, ta), BF16)
    return pl.pallas_call(
        functools.partial(_conv_kernel, seq=seq, keys=keys),
        grid=(b, B_W // ct),
        in_specs=[pl.BlockSpec((1, ta, ct), lambda bb, j: (bb, 0, c0 + j)),
                  pl.BlockSpec((w_conv.shape[0], ct), lambda bb, j: (0, c0 + j))],
        out_specs=out_spec,
        out_shape=out_shape,
        compiler_params=_params("parallel", "arbitrary"),
        name="mlstm_conv_k" if keys else "mlstm_conv_q",
    )(qk, w_conv)


def _mlstm_chain(qt, k, vat, ig, b_row, r_col, total, ca, m_old, tri):
    log_d = jnp.where(tri, b_row + r_col, -jnp.inf)
    log_inter = m_old + b_row
    m_t = jnp.maximum(log_inter, jnp.max(log_d, axis=0, keepdims=True))
    w_intra = jnp.exp(log_d - m_t) * jnp.dot(k, qt, preferred_element_type=F32)
    w_inter = jnp.exp(log_inter - m_t)
    nd = (w_inter * jnp.dot(ca.astype(BF16), qt, preferred_element_type=F32)
          + jnp.dot(vat, w_intra.astype(BF16), preferred_element_type=F32))
    dv = nd.shape[0] // 2
    h = nd[:dv] / jnp.maximum(jnp.abs(nd[dv:]), jnp.exp(-m_t))
    log_w = total - b_row + ig
    m_new = jnp.maximum(m_old + total, jnp.max(log_w, axis=-1, keepdims=True))
    w_s = jnp.exp(log_w - m_new)
    decay = jnp.exp(m_old + total - m_new)
    ca_new = decay * ca + jnp.dot((vat * w_s).astype(BF16), k, preferred_element_type=F32)
    return h, ca_new, m_new


def _mlstm_kernel(qf_ref, kf_ref, vf_ref, grf_ref, gcf_ref, qb_ref, kb_ref, vb_ref, grb_ref, gcb_ref,
                  hf_ref, hb_ref, c_ref, m_ref):
    @pl.when(pl.program_id(1) == 0)
    def _():
        c_ref[...] = jnp.zeros(c_ref.shape, F32)
        m_ref[...] = jnp.full(m_ref.shape, -1e30, F32)

    t = B_CHUNK
    n_chunks = kf_ref.shape[1] // t
    src = lax.broadcasted_iota(jnp.int32, (t, t), 0)
    tgt = lax.broadcasted_iota(jnp.int32, (t, t), 1)
    ones = jnp.ones((B_HEAD_DIM, t), BF16)
    dirs = ((qf_ref, kf_ref, vf_ref, grf_ref, gcf_ref, hf_ref, src <= tgt, t - 1, range(n_chunks)),
            (qb_ref, kb_ref, vb_ref, grb_ref, gcb_ref, hb_ref, src >= tgt, 0, range(n_chunks - 1, -1, -1)))
    outputs = []
    states = []
    for d, (q_ref, k_ref, v_ref, gr_ref, gc_ref, h_ref, tri, last, order) in enumerate(dirs):
        for hd in range(B_HEADS):
            sl = slice(hd * B_HEAD_DIM, (hd + 1) * B_HEAD_DIM)
            gi = d * 2 * B_HEADS + hd
            fi = gi + B_HEADS
            si = d * B_HEADS + hd
            ca = c_ref[si]
            m = m_ref[si][:, 0:1]
            for ci in order:
                rows = slice(ci * t, (ci + 1) * t)
                ig = gr_ref[0, gi:gi + 1, rows]
                b_row = gr_ref[0, N_GATES + fi:N_GATES + fi + 1, rows]
                r_col = gc_ref[0, rows, gi:gi + 1]
                vat = jnp.concatenate([v_ref[0, sl, rows], ones], axis=0)
                h, ca, m = _mlstm_chain(q_ref[0, sl, rows], k_ref[0, rows, sl], vat, ig, b_row, r_col,
                                        b_row[:, last:last + 1], ca, m, tri)
                outputs.append((h_ref, rows, sl, h))
            states.append((si, ca, m))
    for h_ref, rows, sl, h in outputs:
        h_ref[0, sl, rows] = h
    for si, ca, m in states:
        c_ref[si] = ca
        m_ref[si] = jnp.broadcast_to(m, (1, 128))


def _mlstm(qct, kc, vbt, gates_row, gates_col, *, seq, blk):
    b, ta, _ = kc.shape
    nc = ta // blk
    ncl = seq // blk
    ncc = nc - ncl
    fwd = lambda j: jnp.where(j < ncc, ncl + j, j - ncc)
    bwd = lambda j: jnp.where(j < ncc, nc - 1 - j, ncl - 1 - (j - ncc))

    def specs(order):
        feature_major = pl.BlockSpec((1, B_W, blk), lambda bb, j: (bb, 0, order(j)))
        return [feature_major,
                pl.BlockSpec((1, blk, B_W), lambda bb, j: (bb, order(j), 0)),
                feature_major,
                pl.BlockSpec((1, 2 * N_GATES, blk), lambda bb, j: (bb, 0, order(j))),
                pl.BlockSpec((1, blk, GATE_PAD), lambda bb, j: (bb, order(j), 0))]

    n_state = 2 * B_HEADS
    ops = (qct, kc, vbt, gates_row, gates_col)
    return pl.pallas_call(
        _mlstm_kernel,
        grid=(b, nc),
        in_specs=specs(fwd) + specs(bwd),
        out_specs=[pl.BlockSpec((1, B_W, blk), lambda bb, j: (bb, 0, fwd(j))),
                   pl.BlockSpec((1, B_W, blk), lambda bb, j: (bb, 0, bwd(j)))],
        out_shape=[jax.ShapeDtypeStruct((b, B_W, ta), F32)] * 2,
        scratch_shapes=[pltpu.VMEM((n_state, 2 * B_HEAD_DIM, B_HEAD_DIM), F32),
                        pltpu.VMEM((n_state, 1, 128), F32)],
        compiler_params=_params("parallel", "arbitrary"),
        name="mlstm_scan",
    )(*ops, *ops)


def _outproj_kernel(x_ref, m_ref, oa_ref, hf_ref, hb_ref, ob_ref, ghn_ref, w_ref, o_ref):
    hm = (hf_ref[0] + hb_ref[0]).T
    ob = ob_ref[0]
    ghn = ghn_ref[...]
    y = jnp.dot(oa_ref[0], w_ref[0:A_Q, :], preferred_element_type=F32)
    for hd in range(B_HEADS):
        sl = slice(hd * B_HEAD_DIM, (hd + 1) * B_HEAD_DIM)
        z = hm[:, sl]
        z = z * lax.rsqrt(jnp.mean(z * z, axis=-1, keepdims=True) + NORM_EPS) * ghn[:, sl]
        z = (z * jax.nn.sigmoid(ob[:, sl])).astype(BF16)
        y = y + jnp.dot(z, w_ref[A_Q + hd * B_HEAD_DIM:A_Q + (hd + 1) * B_HEAD_DIM, :],
                        preferred_element_type=F32)
    o_ref[0] = x_ref[0] + m_ref[0, 0][2:3] * y


def _outproj(xcat, modsel, oa, hf, hb, ob, ghn, w, *, rt, n_tiles, n_lat_tiles):
    b, ta, dm = xcat.shape
    row = lambda bb, r: (bb, r, 0)
    full = lambda bb, r: (0, 0)
    return pl.pallas_call(
        _outproj_kernel,
        grid=(b, n_tiles),
        in_specs=[pl.BlockSpec((1, rt, dm), row),
                  pl.BlockSpec((1, 1, 6, dm), lambda bb, r: (bb, (r >= n_lat_tiles).astype(jnp.int32), 0, 0)),
                  pl.BlockSpec((1, rt, A_Q), row),
                  pl.BlockSpec((1, B_W, rt), lambda bb, r: (bb, 0, r)),
                  pl.BlockSpec((1, B_W, rt), lambda bb, r: (bb, 0, r)),
                  pl.BlockSpec((1, rt, B_W), row),
                  pl.BlockSpec((1, B_W), full),
                  pl.BlockSpec((A_Q + B_W, dm), full)],
        out_specs=pl.BlockSpec((1, rt, dm), row),
        out_shape=jax.ShapeDtypeStruct(xcat.shape, F32),
        input_output_aliases={0: 0},
        compiler_params=_params("parallel", "arbitrary"),
        name="even_outproj",
    )(xcat, modsel, oa, hf, hb, ob, ghn, w)


def _modmm_kernel(x_ref, m_ref, g_ref, w_ref, o_ref):
    m = m_ref[0, 0]
    h = _modulate(x_ref[0], g_ref[...], m[0:1], m[1:2]).astype(BF16)
    o_ref[0] = jnp.dot(h, w_ref[...], preferred_element_type=F32)


def _mod_matmul(xcat, modsel, g, w, *, rt, tile0, n_tiles, sel):
    b, _, dm = xcat.shape
    n_out = w.shape[1]
    return pl.pallas_call(
        _modmm_kernel,
        grid=(b, n_tiles),
        in_specs=[pl.BlockSpec((1, rt, dm), lambda bb, r: (bb, tile0 + r, 0)),
                  pl.BlockSpec((1, 1, 6, dm), lambda bb, r: (bb, sel, 0, 0)),
                  pl.BlockSpec((1, dm), lambda bb, r: (0, 0)),
                  pl.BlockSpec((dm, n_out), lambda bb, r: (0, 0))],
        out_specs=pl.BlockSpec((1, rt, n_out), lambda bb, r: (bb, r, 0)),
        out_shape=jax.ShapeDtypeStruct((b, n_tiles * rt, n_out), F32),
        compiler_params=_params("parallel", "arbitrary"),
        name="odd_inproj",
    )(xcat, modsel, g, w)


def _pool_kernel(u_ref, x_ref, m_ref, wg_ref, sp_ref, o_ref, pad_ref, *, n, rc):
    g = pl.program_id(1)
    zeros = jnp.zeros((POOL_HALO, pad_ref.shape[1]), F32)
    pad_ref[0:POOL_HALO, :] = zeros
    pad_ref[POOL_HALO + n:2 * POOL_HALO + n, :] = zeros
    pad_ref[POOL_HALO:POOL_HALO + n, :] = u_ref[0]
    gate = m_ref[0, 0][2:3]
    for gi, win in enumerate(POOL_WINDOWS):
        @pl.when(g == gi)
        def _(win=win):
            lo = win // 2
            hi = win - 1 - lo
            for r0 in range(0, n, rc):
                base = POOL_HALO + r0
                acc = pad_ref[base - lo:base - lo + rc, :]
                for j in range(-lo + 1, hi + 1):
                    acc = acc + pad_ref[base + j:base + j + rc, :]
                t = r0 + lax.broadcasted_iota(jnp.int32, (rc, 1), 0)
                cnt = jnp.minimum(t + hi, n - 1) - jnp.maximum(t - lo, 0) + 1
                p = acc / cnt.astype(F32) - pad_ref[base:base + rc, :]
                y = jnp.dot(p.astype(BF16), wg_ref[0], preferred_element_type=F32) * sp_ref[...]
                o_ref[0, r0:r0 + rc, :] = x_ref[0, r0:r0 + rc, :] + gate * y


def _pool(u, xcat, modsel, wgrp, spool, *, n, row_block, sel):
    b, _, dm = xcat.shape
    pg = POOL_GROUP
    rc = min(256, n)
    blk = lambda bb, g: (bb, row_block, g)
    return pl.pallas_call(
        functools.partial(_pool_kernel, n=n, rc=rc),
        grid=(b, dm // pg),
        in_specs=[pl.BlockSpec((1, n, pg), lambda bb, g: (bb, 0, g)),
                  pl.BlockSpec((1, n, pg), blk),
                  pl.BlockSpec((1, 1, 6, pg), lambda bb, g: (bb, sel, 0, g)),
                  pl.BlockSpec((1, pg, pg), lambda bb, g: (g, 0, 0)),
                  pl.BlockSpec((1, pg), lambda bb, g: (0, g))],
        out_specs=pl.BlockSpec((1, n, pg), blk),
        out_shape=jax.ShapeDtypeStruct(xcat.shape, F32),
        scratch_shapes=[pltpu.VMEM((n + 2 * POOL_HALO, pg), F32)],
        input_output_aliases={1: 0},
        compiler_params=_params("parallel", "arbitrary"),
        name="pool_mixer",
    )(u, xcat, modsel, wgrp, spool)


def _lane_cumsum(x01, upper, blk):
    n = x01.shape[1]
    out = []
    carry = jnp.zeros((x01.shape[0], 1), F32)
    for j in range(0, n, blk):
        cs = jnp.dot(x01[:, j:j + blk].astype(BF16), upper, preferred_element_type=F32) + carry
        carry = cs[:, blk - 1:blk]
        out.append(cs)
    return jnp.concatenate(out, axis=1) if len(out) > 1 else out[0]


def _router_kernel(x_ref, m_ref, g_ref, wr_ref, up_ref, h_ref, aff_ref, pos_ref, lg_ref, *, cap, n_tiles):
    r = pl.program_id(1)
    m = m_ref[0, 0]
    h = _modulate(x_ref[0], g_ref[...], m[3:4], m[4:5])
    h_hi = h.astype(BF16)
    h_ref[0] = h_hi
    h_lo = (h - h_hi.astype(F32)).astype(BF16)
    wr = wr_ref[...]
    w_hi = wr.astype(BF16)
    w_lo = (wr - w_hi.astype(F32)).astype(BF16)
    lg = (jnp.dot(h_hi, w_hi, preferred_element_type=F32)
          + (jnp.dot(h_hi, w_lo, preferred_element_type=F32) + jnp.dot(h_lo, w_hi, preferred_element_type=F32)))
    lg_ref[r] = lg.T[:lg_ref.shape[1]]

    @pl.when(r == n_tiles - 1)
    def _():
        lg = jnp.concatenate([lg_ref[i] for i in range(n_tiles)], axis=1) if n_tiles > 1 else lg_ref[0]
        e = jnp.exp(lg - jnp.max(lg, axis=0, keepdims=True))
        aff = e / jnp.sum(e, axis=0, keepdims=True)

        def bit_step(i, thr):
            cand = thr | lax.shift_left(jnp.int32(1), 30 - i)
            cnt = jnp.sum(jnp.where(aff >= pltpu.bitcast(cand, F32), 1.0, 0.0), axis=-1, keepdims=True)
            return jnp.where(cnt >= cap, cand, thr)

        thr = lax.fori_loop(0, 31, bit_step, jnp.zeros((aff.shape[0], 1), jnp.int32))
        thr = pltpu.bitcast(thr, F32)
        above = aff > thr
        tied = aff == thr
        room = cap - jnp.sum(jnp.where(above, 1.0, 0.0), axis=-1, keepdims=True)
        upper = up_ref[...]
        blk = upper.shape[0]
        tie_rank = _lane_cumsum(jnp.where(tied, 1.0, 0.0), upper, blk)
        sel = jnp.logical_or(above, jnp.logical_and(tied, tie_rank <= room))
        slot = _lane_cumsum(jnp.where(sel, 1.0, 0.0), upper, blk) - 1.0
        pos_ref[0] = jnp.where(sel, slot, -1.0).astype(jnp.int32)
        aff_ref[0] = aff


def _router(xcat, modsel, g2, wr, upper, *, rt, n, row_tile0, sel):
    b, ta, dm = xcat.shape
    n_tiles = n // rt
    cap = max(1, EC_FACTOR * n // N_EXPERTS)
    ne = N_EXPERTS
    return pl.pallas_call(
        functools.partial(_router_kernel, cap=cap, n_tiles=n_tiles),
        grid=(b, n_tiles),
        in_specs=[pl.BlockSpec((1, rt, dm), lambda bb, r: (bb, row_tile0 + r, 0)),
                  pl.BlockSpec((1, 1, 6, dm), lambda bb, r: (bb, sel, 0, 0)),
                  pl.BlockSpec((1, dm), lambda bb, r: (0, 0)),
                  pl.BlockSpec(wr.shape, lambda bb, r: (0, 0)),
                  pl.BlockSpec(upper.shape, lambda bb, r: (0, 0))],
        out_specs=[pl.BlockSpec((1, rt, dm), lambda bb, r: (bb, r, 0)),
                   pl.BlockSpec((1, ne, n), lambda bb, r: (bb, 0, 0)),
                   pl.BlockSpec((1, ne, n), lambda bb, r: (bb, 0, 0))],
        out_shape=[jax.ShapeDtypeStruct((b, n, dm), BF16),
                   jax.ShapeDtypeStruct((b, ne, n), F32),
                   jax.ShapeDtypeStruct((b, ne, n), jnp.int32)],
        scratch_shapes=[pltpu.VMEM((n_tiles, ne, rt), F32)],
        compiler_params=_params("parallel", "arbitrary"),
        name="moe_router",
    )(xcat, modsel, g2, wr, upper)


MOE_EXPERT_BLOCK = 4
MOE_TOKEN_BLOCK = 256
MOE_WINDOW = 64


def _slot_onehot(pos_ref, cap):
    hits = []
    for i in range(pos_ref.shape[1]):
        pos = pos_ref[0, i]
        hits.append(lax.broadcasted_iota(jnp.int32, (cap, pos.shape[1]), 0) == pos)
    pick = jnp.concatenate([jnp.where(hit, 1.0, 0.0).astype(BF16) for hit in hits], axis=0)
    return pick, hits


def _moe_plan(pos, *, cap, tb, w):
    nb, ne, _, n = pos.shape
    nk = n // tb
    cnt = jnp.sum((pos.reshape(nb, ne, nk, tb) >= 0).astype(jnp.int32), axis=-1)
    first = jnp.cumsum(cnt, axis=-1) - cnt
    start = jnp.minimum((first // 16) * 16, cap - w)
    fits = first - start + cnt <= w
    dense = jnp.logical_not(jnp.all(fits.reshape(nb, ne // MOE_EXPERT_BLOCK, -1), axis=-1))
    return start.reshape(-1).astype(jnp.int32), dense.reshape(-1).astype(jnp.int32)


def _moe_gather_kernel(start_ref, dense_ref, pos_ref, aff_ref, h_ref, xs_ref, gate_ref, *, cap, tb, w):
    b = pl.program_id(0)
    g = pl.program_id(1)
    eb, n = pos_ref.shape[1], pos_ref.shape[3]
    nk = n // tb
    dense = dense_ref[b * pl.num_programs(1) + g]

    @pl.when(dense != 0)
    def _():
        pick, hits = _slot_onehot(pos_ref, cap)
        xs = jnp.dot(pick, h_ref[0], preferred_element_type=F32).astype(BF16)
        xs_ref[0] = xs.reshape(xs_ref.shape[1:])
        for i, hit in enumerate(hits):
            gate = jnp.sum(jnp.where(hit, aff_ref[0, i], 0.0), axis=-1, keepdims=True)
            gate_ref[0, i] = jnp.broadcast_to(gate, gate_ref.shape[2:])

    @pl.when(dense == 0)
    def _():
        xs_ref[...] = jnp.zeros(xs_ref.shape, BF16)
        gate_ref[...] = jnp.zeros(gate_ref.shape, F32)
        slot = lax.broadcasted_iota(jnp.int32, (w, tb), 0)
        for i in range(eb):
            for k in range(nk):
                s = pl.multiple_of(start_ref[((b * pl.num_programs(1) + g) * eb + i) * nk + k], 16)
                cols = slice(k * tb, (k + 1) * tb)
                hit = (slot + s) == pos_ref[0, i, :, cols]
                pick = jnp.where(hit, 1.0, 0.0).astype(BF16)
                part = jnp.dot(pick, h_ref[0, cols, :], preferred_element_type=F32).astype(BF16)
                rows = pl.ds(s, w)
                xs_ref[0, i, rows, :] = xs_ref[0, i, rows, :] + part
                gate = jnp.sum(jnp.where(hit, aff_ref[0, i, :, cols], 0.0), axis=-1, keepdims=True)
                gate_ref[0, i, rows, :] = gate_ref[0, i, rows, :] + gate


def _moe_gather(start, dense, pos, aff, h, *, cap, tb, w):
    nb, ne, _, n = pos.shape
    dm = h.shape[2]
    eb = MOE_EXPERT_BLOCK
    grid_spec = pltpu.PrefetchScalarGridSpec(
        num_scalar_prefetch=2,
        grid=(nb, ne // eb),
        in_specs=[pl.BlockSpec((1, eb, 1, n), lambda b, e, *_: (b, e, 0, 0)),
                  pl.BlockSpec((1, eb, 1, n), lambda b, e, *_: (b, e, 0, 0)),
                  pl.BlockSpec((1, n, dm), lambda b, e, *_: (b, 0, 0), pipeline_mode=pl.Buffered(1))],
        out_specs=[pl.BlockSpec((1, eb, cap, dm), lambda b, e, *_: (b, e, 0, 0)),
                   pl.BlockSpec((1, eb, cap, 128), lambda b, e, *_: (b, e, 0, 0))])
    return pl.pallas_call(
        functools.partial(_moe_gather_kernel, cap=cap, tb=tb, w=w),
        grid_spec=grid_spec,
        out_shape=[jax.ShapeDtypeStruct((nb, ne, cap, dm), BF16),
                   jax.ShapeDtypeStruct((nb, ne, cap, 128), F32)],
        compiler_params=_params("parallel", "arbitrary"),
        name="moe_gather",
    )(start, dense, pos, aff, h)


def _moe_ffn_kernel(xs_ref, gate_ref, m_ref, wg_ref, wu_ref, wd_ref, y_ref, wgb_ref, wub_ref, wdb_ref):
    @pl.when(pl.program_id(1) == 0)
    def _():
        rows = 256
        for src, dst in ((wg_ref, wgb_ref), (wu_ref, wub_ref), (wd_ref, wdb_ref)):
            for r0 in range(0, dst.shape[0], rows):
                dst[r0:r0 + rows, :] = src[0, 0, r0:r0 + rows, :].astype(BF16)

    nb, _, cap, dm = xs_ref.shape
    xs = xs_ref[...].reshape(nb * cap, dm)
    a = jnp.dot(xs, wgb_ref[...], preferred_element_type=F32)
    u = jnp.dot(xs, wub_ref[...], preferred_element_type=F32)
    act = (_silu(a) * u).astype(BF16)
    out = jnp.dot(act, wdb_ref[...], preferred_element_type=F32)
    for i in range(nb):
        y = out[i * cap:(i + 1) * cap] * gate_ref[i, 0][:, 0:1]
        y_ref[i, 0] = (y * m_ref[i, 0][5:6]).astype(BF16)


def _moe_ffn(xs, gate, modsel, wg, wu, wd, layer, *, sel):
    nb_all, ne, cap, dm = xs.shape
    df = wg.shape[3]
    nb = 2 if nb_all % 2 == 0 else 1
    return pl.pallas_call(
        _moe_ffn_kernel,
        grid=(ne, nb_all // nb),
        in_specs=[pl.BlockSpec((nb, 1, cap, dm), lambda e, c: (c, e, 0, 0)),
                  pl.BlockSpec((nb, 1, cap, 128), lambda e, c: (c, e, 0, 0)),
                  pl.BlockSpec((nb, 1, 6, dm), lambda e, c: (c, sel, 0, 0)),
                  pl.BlockSpec((1, 1, dm, df), lambda e, c: (layer, e, 0, 0)),
                  pl.BlockSpec((1, 1, dm, df), lambda e, c: (layer, e, 0, 0)),
                  pl.BlockSpec((1, 1, df, dm), lambda e, c: (layer, e, 0, 0))],
        out_specs=pl.BlockSpec((nb, 1, cap, dm), lambda e, c: (c, e, 0, 0)),
        out_shape=jax.ShapeDtypeStruct(xs.shape, BF16),
        scratch_shapes=[pltpu.VMEM((dm, df), BF16), pltpu.VMEM((dm, df), BF16), pltpu.VMEM((df, dm), BF16)],
        compiler_params=_params("arbitrary", "arbitrary"),
        name="moe_ffn",
    )(xs, gate, modsel, wg, wu, wd)


def _moe_scatter_kernel(start_ref, dense_ref, pos_ref, posc_ref, y_ref, x_ref, o_ref, *, cap, rows, ts, tb, w):
    b = pl.program_id(0)
    g = pl.program_id(1)
    eb, n = pos_ref.shape[1], pos_ref.shape[3]
    nk = n // tb
    dense = dense_ref[b * pl.num_programs(1) + g]

    @pl.when(g == 0)
    def _():
        o_ref[...] = x_ref[...]

    @pl.when(dense != 0)
    def _():
        pick, _ = _slot_onehot(pos_ref, cap)
        y = y_ref[0].reshape(pick.shape[0], y_ref.shape[3])
        for t0 in range(0, n, ts):
            upd = lax.dot_general(pick[:, t0:t0 + ts], y, TN_DIMS, preferred_element_type=F32)
            i, r0 = t0 // rows, t0 % rows
            o_ref[i, r0:r0 + ts, :] = o_ref[i, r0:r0 + ts, :] + upd

    @pl.when(dense == 0)
    def _():
        lane = lax.broadcasted_iota(jnp.int32, (tb, eb * w), 1)
        for k in range(nk):
            toks = slice(k * tb, (k + 1) * tb)
            hit = None
            wins = []
            for i in range(eb):
                s = pl.multiple_of(start_ref[((b * pl.num_programs(1) + g) * eb + i) * nk + k], 16)
                rel = posc_ref[0, 0, toks, i:i + 1] - s
                col = jnp.where(jnp.logical_and(rel >= 0, rel < w), rel + i * w, -1)
                hit = (lane == col) if hit is None else jnp.logical_or(hit, lane == col)
                wins.append(y_ref[0, i, pl.ds(s, w), :])
            pick = jnp.where(hit, 1.0, 0.0).astype(BF16)
            upd = jnp.dot(pick, jnp.concatenate(wins, axis=0), preferred_element_type=F32)
            i0, r0 = (k * tb) // rows, (k * tb) % rows
            o_ref[i0, r0:r0 + tb, :] = o_ref[i0, r0:r0 + tb, :] + upd


def _moe_scatter(start, dense, pos, y, xcat, *, bb, row0, rows, tb, w):
    nb, ne, cap, dm = y.shape
    n = pos.shape[3]
    eb = MOE_EXPERT_BLOCK
    ts = min(512, rows)
    posc = jnp.transpose(pos.reshape(nb, ne // eb, eb, n), (0, 1, 3, 2))
    out_block = (bb, rows, dm)
    out_index = (lambda b, e, *_: (b, 0, 0)) if bb == 1 else (lambda b, e, *_: (0, row0 // rows, 0))
    grid_spec = pltpu.PrefetchScalarGridSpec(
        num_scalar_prefetch=2,
        grid=(nb, ne // eb),
        in_specs=[pl.BlockSpec((1, eb, 1, n), lambda b, e, *_: (b, e, 0, 0)),
                  pl.BlockSpec((1, 1, n, eb), lambda b, e, *_: (b, e, 0, 0)),
                  pl.BlockSpec((1, eb, cap, dm), lambda b, e, *_: (b, e, 0, 0)),
                  pl.BlockSpec(out_block, out_index, pipeline_mode=pl.Buffered(1))],
        out_specs=pl.BlockSpec(out_block, out_index))
    return pl.pallas_call(
        functools.partial(_moe_scatter_kernel, cap=cap, rows=rows, ts=ts, tb=tb, w=w),
        grid_spec=grid_spec,
        out_shape=jax.ShapeDtypeStruct(xcat.shape, F32),
        input_output_aliases={5: 0},
        compiler_params=_params("arbitrary", "arbitrary"),
        name="moe_scatter",
    )(start, dense, pos, posc, y, xcat)


def _ec_moe(xcat, modsel, g2, wr, upper, wg, wu, wd, layer, *, rt, seq, ctx, with_ctx):
    b = xcat.shape[0]
    ne = N_EXPERTS
    h, aff, pos = _router(xcat, modsel, g2, wr, upper, rt=rt, n=seq, row_tile0=0, sel=0)
    cap = max(1, EC_FACTOR * seq // N_EXPERTS)
    pos = pos.reshape(b, ne, 1, seq)
    tb = min(MOE_TOKEN_BLOCK, ctx)
    win = dict(tb=tb, w=min(MOE_WINDOW, cap))
    start, dense = _moe_plan(pos, cap=cap, **win)
    xs, gate = _moe_gather(start, dense, pos, aff.reshape(b, ne, 1, seq), h, cap=cap, **win)
    y = _moe_ffn(xs, gate, modsel, wg, wu, wd, layer, sel=0)
    xcat = _moe_scatter(start, dense, pos, y, xcat, bb=1, row0=0, rows=seq, **win)
    if with_ctx:
        hc, affc, posc = _router(xcat, modsel, g2, wr, upper, rt=rt, n=ctx, row_tile0=seq // rt, sel=1)
        capc = max(1, EC_FACTOR * ctx // N_EXPERTS)
        offs = (jnp.arange(b, dtype=jnp.int32) * capc)[:, None, None]
        posc = jnp.where(posc >= 0, posc + offs, -1)
        posc = jnp.transpose(posc, (1, 0, 2)).reshape(1, ne, 1, b * ctx)
        affc = jnp.transpose(affc, (1, 0, 2)).reshape(1, ne, 1, b * ctx)
        win = dict(tb=tb, w=min(MOE_WINDOW, b * capc))
        start, dense = _moe_plan(posc, cap=b * capc, **win)
        xs, gate = _moe_gather(start, dense, posc, affc, hc.reshape(1, b * ctx, -1), cap=b * capc, **win)
        y = _moe_ffn(xs, gate, modsel, wg, wu, wd, layer, sel=1)
        xcat = _moe_scatter(start, dense, posc, y, xcat, bb=b, row0=seq, rows=ctx, **win)
    return xcat


def _final_kernel(x_ref, g_ref, o_ref):
    x = x_ref[0]
    o_ref[0] = x * lax.rsqrt(jnp.mean(x * x, axis=-1, keepdims=True) + NORM_EPS) * g_ref[...]


def _final_norm(xcat, g, *, rt, seq):
    b, ta, dm = xcat.shape
    return pl.pallas_call(
        _final_kernel,
        grid=(b, seq // rt),
        in_specs=[pl.BlockSpec((1, rt, dm), lambda bb, r: (bb, r, 0)),
                  pl.BlockSpec((1, dm), lambda bb, r: (0, 0))],
        out_specs=pl.BlockSpec((1, rt, dm), lambda bb, r: (bb, r, 0)),
        out_shape=jax.ShapeDtypeStruct((b, seq, dm), F32),
        compiler_params=_params("parallel", "arbitrary"),
        name="final_norm",
    )(xcat, g)


def _rope_tables(seq, ctx):
    rows = seq // GRID_W
    row_ids = jnp.repeat(jnp.arange(rows, dtype=F32), GRID_W)
    col_ids = jnp.tile(jnp.arange(GRID_W, dtype=F32), rows)
    half = A_HEAD_DIM // 2
    inv = ROPE_THETA ** (-jnp.arange(0, half, 2, dtype=F32) / half)
    ang_r = row_ids[:, None] * inv
    ang_c = col_ids[:, None] * inv
    cos = jnp.concatenate([jnp.cos(ang_r)] * 2 + [jnp.cos(ang_c)] * 2, axis=1)
    sin = jnp.concatenate([-jnp.sin(ang_r), jnp.sin(ang_r), -jnp.sin(ang_c), jnp.sin(ang_c)], axis=1)
    cos = jnp.tile(cos, (1, A_HEADS))
    sin = jnp.tile(sin, (1, A_HEADS))
    cos = jnp.concatenate([cos, jnp.ones((ctx, A_Q), F32)], axis=0)
    sin = jnp.concatenate([sin, jnp.zeros((ctx, A_Q), F32)], axis=0)
    return cos, sin


def kernel(x, c, ctx, c_ctx, w_mod, b_mod, g_norm1, g_norm2, w_in_even, w_out_even, g_qnorm, g_knorm, w_conv,
           b_gate, g_hnorm, w_in_odd, w_pool_grp, s_pool, w_router, w_exp_gate, w_exp_up, w_exp_down, g_final):
    b, seq, dm = x.shape
    n_ctx = ctx.shape[1]
    depth = w_mod.shape[0]
    rt = 256 if (seq % 256 == 0 and n_ctx % 256 == 0) else 128
    assert seq % rt == 0 and n_ctx % rt == 0 and seq % n_ctx == 0 and seq % GRID_W == 0
    nlt = seq // rt
    nt = (seq + n_ctx) // rt

    xcat = jnp.concatenate([x, ctx, jnp.zeros((b, seq - n_ctx, dm), F32)], axis=1)

    rows = -(-(b + 1) // 8) * 8
    cond = jnp.zeros((rows, dm), F32).at[:b].set(c).at[b].set(c_ctx)
    mods = _adaln(cond, w_mod, b_mod)
    lat = mods[:, :b].reshape(depth, b, 1, 6, dm)
    cx = jnp.broadcast_to(mods[:, b].reshape(depth, 1, 1, 6, dm), (depth, b, 1, 6, dm))
    modsel = jnp.concatenate([lat, cx], axis=2)

    cos, sin = _rope_tables(seq, n_ctx)
    lane = jnp.arange(A_Q)
    gmat = (lane[:, None] // A_HEAD_DIM == lane[None, :] // A_HEAD_DIM).astype(BF16)
    blk = min(256, n_ctx)
    tri = jnp.arange(blk)
    upper = (tri[:, None] <= tri[None, :]).astype(BF16)

    wg, wu, wd = w_exp_gate, w_exp_up, w_exp_down
    wr = jnp.pad(w_router, ((0, 0), (0, 0), (0, GATE_PAD - N_EXPERTS)))

    for i in range(depth):
        ctx_next = any(j % 2 == 0 for j in range(i + 1, depth))
        n_upd = nt if ctx_next else nlt
        ms = modsel[i]
        if i % 2 == 0:
            e = i // 2
            w_in = w_in_even[e]
            pad = jnp.zeros((dm, GATE_PAD - N_GATES), F32)
            w_in = jnp.concatenate([w_in, pad], axis=1).astype(BF16)
            bg = jnp.concatenate([b_gate[e], jnp.zeros((GATE_PAD - N_GATES,), F32)])[None]
            q, k, v, qk, vbt, ob, gt, gc, gr = _inproj(
                xcat, ms, g_norm1[i][None], w_in, cos, sin,
                jnp.tile(g_qnorm[e], A_HEADS)[None], jnp.tile(g_knorm[e], A_KV_HEADS)[None], gmat, bg,
                rt=rt, ta=seq + n_ctx, n_lat_tiles=nlt)
            oa = _attention(q, k, v, rt=rt, n_tiles=n_upd, n_lat_tiles=nlt, seq=seq)
            qct = _short_conv(qk, w_conv[e], seq=seq, keys=False)
            kc = _short_conv(qk, w_conv[e], seq=seq, keys=True)
            gates_row = jnp.swapaxes(jnp.concatenate([gt[:, :, :N_GATES], gc[:, :, :N_GATES]], axis=2), 1, 2)
            hf, hb = _mlstm(qct, kc, vbt, gates_row, gr, seq=seq, blk=rt)
            xcat = _outproj(xcat, ms, oa, hf, hb, ob, g_hnorm[e][None], w_out_even[e].astype(BF16),
                            rt=rt, n_tiles=n_upd, n_lat_tiles=nlt)
        else:
            o = i // 2
            w_in = w_in_odd[o].astype(BF16)
            wgrp = w_pool_grp[o].astype(BF16)
            u = _mod_matmul(xcat, ms, g_norm1[i][None], w_in, rt=rt, tile0=0, n_tiles=nlt, sel=0)
            if ctx_next:
                uc = _mod_matmul(xcat, ms, g_norm1[i][None], w_in, rt=rt, tile0=nlt, n_tiles=nt - nlt, sel=1)
            xcat = _pool(u, xcat, ms, wgrp, s_pool[o][None], n=seq, row_block=0, sel=0)
            if ctx_next:
                xcat = _pool(uc, xcat, ms, wgrp, s_pool[o][None], n=n_ctx, row_block=seq // n_ctx, sel=1)
        xcat = _ec_moe(xcat, ms, g_norm2[i][None], wr[i], upper, wg, wu, wd, i,
                       rt=rt, seq=seq, ctx=n_ctx, with_ctx=ctx_next)
    return _final_norm(xcat, g_final[None], rt=rt, seq=seq)
```

```python
import functools

import jax
import jax.numpy as jnp
from jax import lax
from jax.experimental import pallas as pl
from jax.experimental.pallas import tpu as pltpu

D_MODEL = 1024
GRID_W = 64
A_HEADS = 8
A_KV_HEADS = 2
A_HEAD_DIM = 64
ROPE_THETA = 10000.0
B_HEADS = 4
B_HEAD_DIM = 128
B_CHUNK = 128
POOL_WINDOWS = (2, 4, 8, 16)
POOL_GROUP = D_MODEL // 4
N_EXPERTS = 16
EC_FACTOR = 2
NORM_EPS = 1e-6

A_Q = A_HEADS * A_HEAD_DIM
A_KV = A_KV_HEADS * A_HEAD_DIM
B_W = B_HEADS * B_HEAD_DIM
N_GATES = 2 * 2 * B_HEADS
GATE_PAD = 128
C_QA, C_KA, C_VA = 0, A_Q, A_Q + A_KV
C_QK = A_Q + 2 * A_KV
C_VB = C_QK + 2 * B_W
C_OB = C_VB + B_W
C_GT = C_OB + B_W
EVEN_COLS = C_GT + GATE_PAD
POOL_HALO = 16

F32 = jnp.float32
BF16 = jnp.bfloat16
HIGHEST = lax.Precision.HIGHEST
NT_DIMS = (((1,), (1,)), ((), ()))
TN_DIMS = (((0,), (0,)), ((), ()))
VMEM_LIMIT = 56 * 1024 * 1024


def _params(*sem):
    return pltpu.CompilerParams(dimension_semantics=sem, vmem_limit_bytes=VMEM_LIMIT)


def _modulate(x, g, shift, scale):
    y = x * lax.rsqrt(jnp.mean(x * x, axis=-1, keepdims=True) + NORM_EPS)
    return (y * g) * (1.0 + scale) + shift


def _silu(x):
    return x * jax.nn.sigmoid(x)


def _adaln_kernel(c_ref, w_ref, b_ref, o_ref):
    s = _silu(c_ref[...])
    o_ref[0] = jnp.dot(s, w_ref[0], precision=HIGHEST, preferred_element_type=F32) + b_ref[0]


def _adaln(cond, w_mod, b_mod):
    depth, dm, n6 = w_mod.shape
    rows = cond.shape[0]
    tn = 1536
    return pl.pallas_call(
        _adaln_kernel,
        grid=(depth, n6 // tn),
        in_specs=[pl.BlockSpec((rows, dm), lambda l, j: (0, 0)),
                  pl.BlockSpec((1, dm, tn), lambda l, j: (l, 0, j)),
                  pl.BlockSpec((1, 1, tn), lambda l, j: (l, 0, j))],
        out_specs=pl.BlockSpec((1, rows, tn), lambda l, j: (l, 0, j)),
        out_shape=jax.ShapeDtypeStruct((depth, rows, n6), F32),
        compiler_params=_params("arbitrary", "arbitrary"),
        name="adaln",
    )(cond, w_mod, b_mod.reshape(depth, 1, n6))


def _group_mean_sq(x, gmat, width):
    x2 = x * x
    hi = x2.astype(BF16)
    lo = (x2 - hi.astype(F32)).astype(BF16)
    s = jnp.dot(hi, gmat, preferred_element_type=F32) + jnp.dot(lo, gmat, preferred_element_type=F32)
    return s * (1.0 / width)


def _rope(x, cos, sin):
    w = x.shape[1]
    lane = lax.broadcasted_iota(jnp.int32, x.shape, 1)
    first = (lane % 32) < 16
    partner = jnp.where(first, pltpu.roll(x, w - 16, 1), pltpu.roll(x, 16, 1))
    return x * cos + partner * sin


def _inproj_kernel(x_ref, m_ref, g1_ref, w_ref, cos_ref, sin_ref, gq_ref, gk_ref, gm_ref, bg_ref, lp_ref, ls_ref,
                   q_ref, k_ref, v_ref, qk_ref, vb_ref, ob_ref, gt_ref, gc_ref, gr_ref):
    m = m_ref[0, 0]
    h = _modulate(x_ref[0], g1_ref[...], m[0:1], m[1:2]).astype(BF16)

    def mm(lo, hi):
        return jnp.dot(h, w_ref[:, lo:hi], preferred_element_type=F32)

    cos = cos_ref[...]
    sin = sin_ref[...]
    gm = gm_ref[...]
    qa = mm(C_QA, C_KA)
    qa = qa * lax.rsqrt(_group_mean_sq(qa, gm, A_HEAD_DIM) + NORM_EPS) * gq_ref[...]
    q_ref[0] = (_rope(qa, cos, sin) * (A_HEAD_DIM ** -0.5)).astype(BF16)
    ka = mm(C_KA, C_VA)
    ka = ka * lax.rsqrt(_group_mean_sq(ka, gm[:A_KV, :A_KV], A_HEAD_DIM) + NORM_EPS) * gk_ref[...]
    k_ref[0] = _rope(ka, cos[:, :A_KV], sin[:, :A_KV]).astype(BF16)
    va = mm(C_VA, C_QK).astype(BF16)
    ones = jnp.ones((va.shape[0], A_HEAD_DIM), BF16)
    v_ref[0] = jnp.concatenate(
        [piece for g in range(A_KV_HEADS) for piece in (va[:, g * A_HEAD_DIM:(g + 1) * A_HEAD_DIM], ones)], axis=1)
    qk_ref[0] = mm(C_QK, C_VB)
    vb_ref[0] = mm(C_VB, C_OB).T.astype(BF16)
    ob_ref[0] = mm(C_OB, C_GT)
    gt = mm(C_GT, EVEN_COLS) + bg_ref[...]
    lane = lax.broadcasted_iota(jnp.int32, gt.shape, 1)
    log_sig = jnp.minimum(gt, 0.0) - jnp.log1p(jnp.exp(-jnp.abs(gt)))
    gt = jnp.where((lane % (2 * B_HEADS)) >= B_HEADS, log_sig, gt)
    gt_ref[0] = gt
    prefix = jnp.dot(lp_ref[...], gt, precision=HIGHEST, preferred_element_type=F32)
    suffix = jnp.dot(ls_ref[...], gt, precision=HIGHEST, preferred_element_type=F32)
    cum = jnp.where((lane % (4 * B_HEADS)) < 2 * B_HEADS, prefix, suffix)
    gc_ref[0] = cum
    gr_ref[0] = gt - pltpu.roll(cum, GATE_PAD - B_HEADS, 1)


def _inproj(xcat, modsel, g1, w, cos, sin, gq, gk, gmat, bgate, *, rt, ta, n_lat_tiles):
    b, _, dm = xcat.shape
    nt = ta // rt
    row = lambda bb, r: (bb, r, 0)
    full = lambda bb, r: (0, 0)
    out_widths = (A_Q, A_KV, 2 * A_KV, 2 * B_W, B_W, B_W, GATE_PAD, GATE_PAD, GATE_PAD)
    out_dtypes = (BF16, BF16, BF16, F32, BF16, F32, F32, F32, F32)
    out_specs = [pl.BlockSpec((1, rt, wd), row) for wd in out_widths]
    out_shapes = [jax.ShapeDtypeStruct((b, ta, wd), dt) for wd, dt in zip(out_widths, out_dtypes)]
    vb_slot = 4
    out_specs[vb_slot] = pl.BlockSpec((1, B_W, rt), lambda bb, r: (bb, 0, r))
    out_shapes[vb_slot] = jax.ShapeDtypeStruct((b, B_W, ta), BF16)
    idx = jnp.arange(rt)
    same_chunk = idx[:, None] // B_CHUNK == idx[None, :] // B_CHUNK
    lower = jnp.logical_and(same_chunk, idx[None, :] <= idx[:, None]).astype(F32)
    upper = jnp.logical_and(same_chunk, idx[None, :] >= idx[:, None]).astype(F32)
    return pl.pallas_call(
        _inproj_kernel,
        grid=(b, nt),
        in_specs=[pl.BlockSpec((1, rt, dm), row),
                  pl.BlockSpec((1, 1, 6, dm), lambda bb, r: (bb, (r >= n_lat_tiles).astype(jnp.int32), 0, 0)),
                  pl.BlockSpec((1, dm), full),
                  pl.BlockSpec((dm, EVEN_COLS), full),
                  pl.BlockSpec((rt, A_Q), lambda bb, r: (r, 0)),
                  pl.BlockSpec((rt, A_Q), lambda bb, r: (r, 0)),
                  pl.BlockSpec((1, A_Q), full),
                  pl.BlockSpec((1, A_KV), full),
                  pl.BlockSpec((A_Q, A_Q), full),
                  pl.BlockSpec((1, GATE_PAD), full),
                  pl.BlockSpec((rt, rt), full),
                  pl.BlockSpec((rt, rt), full)],
        out_specs=out_specs,
        out_shape=out_shapes,
        compiler_params=_params("parallel", "arbitrary"),
        name="even_inproj",
    )(xcat, modsel, g1, w, cos, sin, gq, gk, gmat, bgate, lower, upper)


def _attn_kernel(q_ref, k_ref, v_ref, o_ref, *, n_lat_tiles, n_tiles, seq):
    group = A_HEADS // A_KV_HEADS

    def attend(key_lo):
        q = q_ref[0]
        for hd in range(A_HEADS):
            g = hd // group
            qh = q[:, hd * A_HEAD_DIM:(hd + 1) * A_HEAD_DIM]
            kg = k_ref[0, key_lo:, g * A_HEAD_DIM:(g + 1) * A_HEAD_DIM]
            vg = v_ref[0, key_lo:, 2 * g * A_HEAD_DIM:2 * (g + 1) * A_HEAD_DIM]
            s = lax.dot_general(qh, kg, NT_DIMS, preferred_element_type=F32)
            p = jnp.exp(s - jnp.max(s, axis=-1, keepdims=True))
            o = jnp.dot(p.astype(BF16), vg, preferred_element_type=F32)
            o = o[:, :A_HEAD_DIM] / o[:, A_HEAD_DIM:]
            o_ref[0, :, hd * A_HEAD_DIM:(hd + 1) * A_HEAD_DIM] = o.astype(BF16)

    if n_tiles == n_lat_tiles:
        attend(0)
    else:
        r = pl.program_id(1)
        pl.when(r < n_lat_tiles)(lambda: attend(0))
        pl.when(r >= n_lat_tiles)(lambda: attend(seq))


def _attention(q, k, v, *, rt, n_tiles, n_lat_tiles, seq):
    b, ta, _ = q.shape
    return pl.pallas_call(
        functools.partial(_attn_kernel, n_lat_tiles=n_lat_tiles, n_tiles=n_tiles, seq=seq),
        grid=(b, n_tiles),
        in_specs=[pl.BlockSpec((1, rt, A_Q), lambda bb, r: (bb, r, 0)),
                  pl.BlockSpec((1, ta, A_KV), lambda bb, r: (bb, 0, 0)),
                  pl.BlockSpec((1, ta, 2 * A_KV), lambda bb, r: (bb, 0, 0))],
        out_specs=pl.BlockSpec((1, rt, A_Q), lambda bb, r: (bb, r, 0)),
        out_shape=jax.ShapeDtypeStruct((b, n_tiles * rt, A_Q), BF16),
        compiler_params=_params("parallel", "arbitrary"),
        name="gqa_attention",
    )(q, k, v)


def _conv_kernel(x_ref, w_ref, o_ref, *, seq, keys):
    x = x_ref[0]
    ta = x.shape[0]
    w = w_ref[...]
    row = lax.broadcasted_iota(jnp.int32, (ta, 1), 0)
    prev = jnp.where(jnp.logical_or(row == 0, row == seq), 0.0, pltpu.roll(x, 1, 0))
    nxt = jnp.where(jnp.logical_or(row == seq - 1, row == ta - 1), 0.0, pltpu.roll(x, ta - 1, 0))
    y = _silu(prev * w[0:1] + x * w[1:2] + nxt * w[2:3])
    if keys:
        o_ref[0] = (y * (B_HEAD_DIM ** -0.5)).astype(BF16)
    else:
        o_ref[0] = y.T.astype(BF16)


def _short_conv(qk, w_conv, *, seq, keys):
    b, ta, _ = qk.shape
    ct = 256
    c0 = B_W // ct if keys else 0
    if keys:
        out_spec = pl.BlockSpec((1, ta, ct), lambda bb, j: (bb, 0, j))
        out_shape = jax.ShapeDtypeStruct((b, ta, B_W), BF16)
    else:
        out_spec = pl.BlockSpec((1, ct, ta), lambda bb, j: (bb, j, 0))
        out_shape = jax.ShapeDtypeStruct((b, B_W, ta), BF16)
    return pl.pallas_call(
        functools.partial(_conv_kernel, seq=seq, keys=keys),
        grid=(b, B_W // ct),
        in_specs=[pl.BlockSpec((1, ta, ct), lambda bb, j: (bb, 0, c0 + j)),
                  pl.BlockSpec((w_conv.shape[0], ct), lambda bb, j: (0, c0 + j))],
        out_specs=out_spec,
        out_shape=out_shape,
        compiler_params=_params("parallel", "arbitrary"),
        name="mlstm_conv_k" if keys else "mlstm_conv_q",
    )(qk, w_conv)


def _mlstm_chain(qt, k, vat, ig, b_row, r_col, total, ca, m_old, tri):
    log_d = jnp.where(tri, b_row + r_col, -jnp.inf)
    log_inter = m_old + b_row
    m_t = jnp.maximum(log_inter, jnp.max(log_d, axis=0, keepdims=True))
    w_intra = jnp.exp(log_d - m_t) * jnp.dot(k, qt, preferred_element_type=F32)
    w_inter = jnp.exp(log_inter - m_t)
    nd = (w_inter * jnp.dot(ca.astype(BF16), qt, preferred_element_type=F32)
          + jnp.dot(vat, w_intra.astype(BF16), preferred_element_type=F32))
    dv = nd.shape[0] // 2
    h = nd[:dv] / jnp.maximum(jnp.abs(nd[dv:]), jnp.exp(-m_t))
    log_w = total - b_row + ig
    m_new = jnp.maximum(m_old + total, jnp.max(log_w, axis=-1, keepdims=True))
    w_s = jnp.exp(log_w - m_new)
    decay = jnp.exp(m_old + total - m_new)
    ca_new = decay * ca + jnp.dot((vat * w_s).astype(BF16), k, preferred_element_type=F32)
    return h, ca_new, m_new


def _mlstm_kernel(qf_ref, kf_ref, vf_ref, grf_ref, gcf_ref, qb_ref, kb_ref, vb_ref, grb_ref, gcb_ref,
                  hf_ref, hb_ref, c_ref, m_ref):
    @pl.when(pl.program_id(1) == 0)
    def _():
        c_ref[...] = jnp.zeros(c_ref.shape, F32)
        m_ref[...] = jnp.full(m_ref.shape, -1e30, F32)

    t = B_CHUNK
    n_chunks = kf_ref.shape[1] // t
    src = lax.broadcasted_iota(jnp.int32, (t, t), 0)
    tgt = lax.broadcasted_iota(jnp.int32, (t, t), 1)
    ones = jnp.ones((B_HEAD_DIM, t), BF16)
    dirs = ((qf_ref, kf_ref, vf_ref, grf_ref, gcf_ref, hf_ref, src <= tgt, t - 1, range(n_chunks)),
            (qb_ref, kb_ref, vb_ref, grb_ref, gcb_ref, hb_ref, src >= tgt, 0, range(n_chunks - 1, -1, -1)))
    outputs = []
    states = []
    for d, (q_ref, k_ref, v_ref, gr_ref, gc_ref, h_ref, tri, last, order) in enumerate(dirs):
        for hd in range(B_HEADS):
            sl = slice(hd * B_HEAD_DIM, (hd + 1) * B_HEAD_DIM)
            gi = d * 2 * B_HEADS + hd
            fi = gi + B_HEADS
            si = d * B_HEADS + hd
            ca = c_ref[si]
            m = m_ref[si][:, 0:1]
            for ci in order:
                rows = slice(ci * t, (ci + 1) * t)
                ig = gr_ref[0, gi:gi + 1, rows]
                b_row = gr_ref[0, N_GATES + fi:N_GATES + fi + 1, rows]
                r_col = gc_ref[0, rows, gi:gi + 1]
                vat = jnp.concatenate([v_ref[0, sl, rows], ones], axis=0)
                h, ca, m = _mlstm_chain(q_ref[0, sl, rows], k_ref[0, rows, sl], vat, ig, b_row, r_col,
                                        b_row[:, last:last + 1], ca, m, tri)
                outputs.append((h_ref, rows, sl, h))
            states.append((si, ca, m))
    for h_ref, rows, sl, h in outputs:
        h_ref[0, sl, rows] = h
    for si, ca, m in states:
        c_ref[si] = ca
        m_ref[si] = jnp.broadcast_to(m, (1, 128))


def _mlstm(qct, kc, vbt, gates_row, gates_col, *, seq, blk):
    b, ta, _ = kc.shape
    nc = ta // blk
    ncl = seq // blk
    ncc = nc - ncl
    fwd = lambda j: jnp.where(j < ncc, ncl + j, j - ncc)
    bwd = lambda j: jnp.where(j < ncc, nc - 1 - j, ncl - 1 - (j - ncc))

    def specs(order):
        feature_major = pl.BlockSpec((1, B_W, blk), lambda bb, j: (bb, 0, order(j)))
        return [feature_major,
                pl.BlockSpec((1, blk, B_W), lambda bb, j: (bb, order(j), 0)),
                feature_major,
                pl.BlockSpec((1, 2 * N_GATES, blk), lambda bb, j: (bb, 0, order(j))),
                pl.BlockSpec((1, blk, GATE_PAD), lambda bb, j: (bb, order(j), 0))]

    n_state = 2 * B_HEADS
    ops = (qct, kc, vbt, gates_row, gates_col)
    return pl.pallas_call(
        _mlstm_kernel,
        grid=(b, nc),
        in_specs=specs(fwd) + specs(bwd),
        out_specs=[pl.BlockSpec((1, B_W, blk), lambda bb, j: (bb, 0, fwd(j))),
                   pl.BlockSpec((1, B_W, blk), lambda bb, j: (bb, 0, bwd(j)))],
        out_shape=[jax.ShapeDtypeStruct((b, B_W, ta), F32)] * 2,
        scratch_shapes=[pltpu.VMEM((n_state, 2 * B_HEAD_DIM, B_HEAD_DIM), F32),
                        pltpu.VMEM((n_state, 1, 128), F32)],
        compiler_params=_params("parallel", "arbitrary"),
        name="mlstm_scan",
    )(*ops, *ops)


def _outproj_kernel(x_ref, m_ref, oa_ref, hf_ref, hb_ref, ob_ref, ghn_ref, w_ref, o_ref):
    hm = (hf_ref[0] + hb_ref[0]).T
    ob = ob_ref[0]
    ghn = ghn_ref[...]
    y = jnp.dot(oa_ref[0], w_ref[0:A_Q, :], preferred_element_type=F32)
    for hd in range(B_HEADS):
        sl = slice(hd * B_HEAD_DIM, (hd + 1) * B_HEAD_DIM)
        z = hm[:, sl]
        z = z * lax.rsqrt(jnp.mean(z * z, axis=-1, keepdims=True) + NORM_EPS) * ghn[:, sl]
        z = (z * jax.nn.sigmoid(ob[:, sl])).astype(BF16)
        y = y + jnp.dot(z, w_ref[A_Q + hd * B_HEAD_DIM:A_Q + (hd + 1) * B_HEAD_DIM, :],
                        preferred_element_type=F32)
    o_ref[0] = x_ref[0] + m_ref[0, 0][2:3] * y


def _outproj(xcat, modsel, oa, hf, hb, ob, ghn, w, *, rt, n_tiles, n_lat_tiles):
    b, ta, dm = xcat.shape
    row = lambda bb, r: (bb, r, 0)
    full = lambda bb, r: (0, 0)
    return pl.pallas_call(
        _outproj_kernel,
        grid=(b, n_tiles),
        in_specs=[pl.BlockSpec((1, rt, dm), row),
                  pl.BlockSpec((1, 1, 6, dm), lambda bb, r: (bb, (r >= n_lat_tiles).astype(jnp.int32), 0, 0)),
                  pl.BlockSpec((1, rt, A_Q), row),
                  pl.BlockSpec((1, B_W, rt), lambda bb, r: (bb, 0, r)),
                  pl.BlockSpec((1, B_W, rt), lambda bb, r: (bb, 0, r)),
                  pl.BlockSpec((1, rt, B_W), row),
                  pl.BlockSpec((1, B_W), full),
                  pl.BlockSpec((A_Q + B_W, dm), full)],
        out_specs=pl.BlockSpec((1, rt, dm), row),
        out_shape=jax.ShapeDtypeStruct(xcat.shape, F32),
        input_output_aliases={0: 0},
        compiler_params=_params("parallel", "arbitrary"),
        name="even_outproj",
    )(xcat, modsel, oa, hf, hb, ob, ghn, w)


def _modmm_kernel(x_ref, m_ref, g_ref, w_ref, o_ref):
    m = m_ref[0, 0]
    h = _modulate(x_ref[0], g_ref[...], m[0:1], m[1:2]).astype(BF16)
    o_ref[0] = jnp.dot(h, w_ref[...], preferred_element_type=F32)


def _mod_matmul(xcat, modsel, g, w, *, rt, tile0, n_tiles, sel):
    b, _, dm = xcat.shape
    n_out = w.shape[1]
    return pl.pallas_call(
        _modmm_kernel,
        grid=(b, n_tiles),
        in_specs=[pl.BlockSpec((1, rt, dm), lambda bb, r: (bb, tile0 + r, 0)),
                  pl.BlockSpec((1, 1, 6, dm), lambda bb, r: (bb, sel, 0, 0)),
                  pl.BlockSpec((1, dm), lambda bb, r: (0, 0)),
                  pl.BlockSpec((dm, n_out), lambda bb, r: (0, 0))],
        out_specs=pl.BlockSpec((1, rt, n_out), lambda bb, r: (bb, r, 0)),
        out_shape=jax.ShapeDtypeStruct((b, n_tiles * rt, n_out), F32),
        compiler_params=_params("parallel", "arbitrary"),
        name="odd_inproj",
    )(xcat, modsel, g, w)


def _pool_kernel(u_ref, x_ref, m_ref, wg_ref, sp_ref, o_ref, pad_ref, *, n, rc):
    g = pl.program_id(1)
    zeros = jnp.zeros((POOL_HALO, pad_ref.shape[1]), F32)
    pad_ref[0:POOL_HALO, :] = zeros
    pad_ref[POOL_HALO + n:2 * POOL_HALO + n, :] = zeros
    pad_ref[POOL_HALO:POOL_HALO + n, :] = u_ref[0]
    gate = m_ref[0, 0][2:3]
    for gi, win in enumerate(POOL_WINDOWS):
        @pl.when(g == gi)
        def _(win=win):
            lo = win // 2
            hi = win - 1 - lo
            for r0 in range(0, n, rc):
                base = POOL_HALO + r0
                acc = pad_ref[base - lo:base - lo + rc, :]
                for j in range(-lo + 1, hi + 1):
                    acc = acc + pad_ref[base + j:base + j + rc, :]
                t = r0 + lax.broadcasted_iota(jnp.int32, (rc, 1), 0)
                cnt = jnp.minimum(t + hi, n - 1) - jnp.maximum(t - lo, 0) + 1
                p = acc / cnt.astype(F32) - pad_ref[base:base + rc, :]
                y = jnp.dot(p.astype(BF16), wg_ref[0], preferred_element_type=F32) * sp_ref[...]
                o_ref[0, r0:r0 + rc, :] = x_ref[0, r0:r0 + rc, :] + gate * y


def _pool(u, xcat, modsel, wgrp, spool, *, n, row_block, sel):
    b, _, dm = xcat.shape
    pg = POOL_GROUP
    rc = min(256, n)
    blk = lambda bb, g: (bb, row_block, g)
    return pl.pallas_call(
        functools.partial(_pool_kernel, n=n, rc=rc),
        grid=(b, dm // pg),
        in_specs=[pl.BlockSpec((1, n, pg), lambda bb, g: (bb, 0, g)),
                  pl.BlockSpec((1, n, pg), blk),
                  pl.BlockSpec((1, 1, 6, pg), lambda bb, g: (bb, sel, 0, g)),
                  pl.BlockSpec((1, pg, pg), lambda bb, g: (g, 0, 0)),
                  pl.BlockSpec((1, pg), lambda bb, g: (0, g))],
        out_specs=pl.BlockSpec((1, n, pg), blk),
        out_shape=jax.ShapeDtypeStruct(xcat.shape, F32),
        scratch_shapes=[pltpu.VMEM((n + 2 * POOL_HALO, pg), F32)],
        input_output_aliases={1: 0},
        compiler_params=_params("parallel", "arbitrary"),
        name="pool_mixer",
    )(u, xcat, modsel, wgrp, spool)


def _lane_cumsum(x01, upper, blk):
    n = x01.shape[1]
    out = []
    carry = jnp.zeros((x01.shape[0], 1), F32)
    for j in range(0, n, blk):
        cs = jnp.dot(x01[:, j:j + blk].astype(BF16), upper, preferred_element_type=F32) + carry
        carry = cs[:, blk - 1:blk]
        out.append(cs)
    return jnp.concatenate(out, axis=1) if len(out) > 1 else out[0]


def _router_kernel(x_ref, m_ref, g_ref, wr_ref, up_ref, h_ref, aff_ref, pos_ref, lg_ref, *, cap, n_tiles):
    r = pl.program_id(1)
    m = m_ref[0, 0]
    h = _modulate(x_ref[0], g_ref[...], m[3:4], m[4:5])
    h_hi = h.astype(BF16)
    h_ref[0] = h_hi
    h_lo = (h - h_hi.astype(F32)).astype(BF16)
    wr = wr_ref[...]
    w_hi = wr.astype(BF16)
    w_lo = (wr - w_hi.astype(F32)).astype(BF16)
    lg = (jnp.dot(h_hi, w_hi, preferred_element_type=F32)
          + (jnp.dot(h_hi, w_lo, preferred_element_type=F32) + jnp.dot(h_lo, w_hi, preferred_element_type=F32)))
    lg_ref[r] = lg.T[:lg_ref.shape[1]]

    @pl.when(r == n_tiles - 1)
    def _():
        lg = jnp.concatenate([lg_ref[i] for i in range(n_tiles)], axis=1) if n_tiles > 1 else lg_ref[0]
        e = jnp.exp(lg - jnp.max(lg, axis=0, keepdims=True))
        aff = e / jnp.sum(e, axis=0, keepdims=True)

        def bit_step(i, thr):
            cand = thr | lax.shift_left(jnp.int32(1), 30 - i)
            cnt = jnp.sum(jnp.where(aff >= pltpu.bitcast(cand, F32), 1.0, 0.0), axis=-1, keepdims=True)
            return jnp.where(cnt >= cap, cand, thr)

        thr = lax.fori_loop(0, 31, bit_step, jnp.zeros((aff.shape[0], 1), jnp.int32))
        thr = pltpu.bitcast(thr, F32)
        above = aff > thr
        tied = aff == thr
        room = cap - jnp.sum(jnp.where(above, 1.0, 0.0), axis=-1, keepdims=True)
        upper = up_ref[...]
        blk = upper.shape[0]
        tie_rank = _lane_cumsum(jnp.where(tied, 1.0, 0.0), upper, blk)
        sel = jnp.logical_or(above, jnp.logical_and(tied, tie_rank <= room))
        slot = _lane_cumsum(jnp.where(sel, 1.0, 0.0), upper, blk) - 1.0
        pos_ref[0] = jnp.where(sel, slot, -1.0).astype(jnp.int32)
        aff_ref[0] = aff


def _router(xcat, modsel, g2, wr, upper, *, rt, n, row_tile0, sel):
    b, ta, dm = xcat.shape
    n_tiles = n // rt
    cap = max(1, EC_FACTOR * n // N_EXPERTS)
    ne = N_EXPERTS
    return pl.pallas_call(
        functools.partial(_router_kernel, cap=cap, n_tiles=n_tiles),
        grid=(b, n_tiles),
        in_specs=[pl.BlockSpec((1, rt, dm), lambda bb, r: (bb, row_tile0 + r, 0)),
                  pl.BlockSpec((1, 1, 6, dm), lambda bb, r: (bb, sel, 0, 0)),
                  pl.BlockSpec((1, dm), lambda bb, r: (0, 0)),
                  pl.BlockSpec(wr.shape, lambda bb, r: (0, 0)),
                  pl.BlockSpec(upper.shape, lambda bb, r: (0, 0))],
        out_specs=[pl.BlockSpec((1, rt, dm), lambda bb, r: (bb, r, 0)),
                   pl.BlockSpec((1, ne, n), lambda bb, r: (bb, 0, 0)),
                   pl.BlockSpec((1, ne, n), lambda bb, r: (bb, 0, 0))],
        out_shape=[jax.ShapeDtypeStruct((b, n, dm), BF16),
                   jax.ShapeDtypeStruct((b, ne, n), F32),
                   jax.ShapeDtypeStruct((b, ne, n), jnp.int32)],
        scratch_shapes=[pltpu.VMEM((n_tiles, ne, rt), F32)],
        compiler_params=_params("parallel", "arbitrary"),
        name="moe_router",
    )(xcat, modsel, g2, wr, upper)


MOE_EXPERT_BLOCK = 4
MOE_TOKEN_BLOCK = 256
MOE_WINDOW = 64


def _slot_onehot(pos_ref, cap):
    hits = []
    for i in range(pos_ref.shape[1]):
        pos = pos_ref[0, i]
        hits.append(lax.broadcasted_iota(jnp.int32, (cap, pos.shape[1]), 0) == pos)
    pick = jnp.concatenate([jnp.where(hit, 1.0, 0.0).astype(BF16) for hit in hits], axis=0)
    return pick, hits


def _moe_plan(pos, *, cap, tb, w):
    nb, ne, _, n = pos.shape
    nk = n // tb
    cnt = jnp.sum((pos.reshape(nb, ne, nk, tb) >= 0).astype(jnp.int32), axis=-1)
    first = jnp.cumsum(cnt, axis=-1) - cnt
    start = jnp.minimum((first // 16) * 16, cap - w)
    fits = first - start + cnt <= w
    dense_group = jnp.logical_not(jnp.all(fits.reshape(nb, ne // MOE_EXPERT_BLOCK, -1), axis=-1))
    dense_block = jnp.logical_not(jnp.all(fits, axis=1))
    flat = lambda a: a.reshape(-1).astype(jnp.int32)
    return flat(start), flat(dense_group), flat(dense_block)


def _moe_gather_kernel(start_ref, dense_ref, pos_ref, aff_ref, h_ref, xs_ref, gate_ref, *, cap, tb, w):
    b = pl.program_id(0)
    g = pl.program_id(1)
    eb, n = pos_ref.shape[1], pos_ref.shape[3]
    nk = n // tb
    dense = dense_ref[b * pl.num_programs(1) + g]

    @pl.when(dense != 0)
    def _():
        pick, hits = _slot_onehot(pos_ref, cap)
        xs = jnp.dot(pick, h_ref[0], preferred_element_type=F32).astype(BF16)
        xs_ref[0] = xs.reshape(xs_ref.shape[1:])
        for i, hit in enumerate(hits):
            gate = jnp.sum(jnp.where(hit, aff_ref[0, i], 0.0), axis=-1, keepdims=True)
            gate_ref[0, i] = jnp.broadcast_to(gate, gate_ref.shape[2:])

    @pl.when(dense == 0)
    def _():
        xs_ref[...] = jnp.zeros(xs_ref.shape, BF16)
        gate_ref[...] = jnp.zeros(gate_ref.shape, F32)
        slot = lax.broadcasted_iota(jnp.int32, (w, tb), 0)
        for i in range(eb):
            for k in range(nk):
                s = pl.multiple_of(start_ref[((b * pl.num_programs(1) + g) * eb + i) * nk + k], 16)
                cols = slice(k * tb, (k + 1) * tb)
                hit = (slot + s) == pos_ref[0, i, :, cols]
                pick = jnp.where(hit, 1.0, 0.0).astype(BF16)
                part = jnp.dot(pick, h_ref[0, cols, :], preferred_element_type=F32).astype(BF16)
                rows = pl.ds(s, w)
                xs_ref[0, i, rows, :] = xs_ref[0, i, rows, :] + part
                gate = jnp.sum(jnp.where(hit, aff_ref[0, i, :, cols], 0.0), axis=-1, keepdims=True)
                gate_ref[0, i, rows, :] = gate_ref[0, i, rows, :] + gate


def _moe_gather(start, dense, pos, aff, h, *, cap, tb, w):
    nb, ne, _, n = pos.shape
    dm = h.shape[2]
    eb = MOE_EXPERT_BLOCK
    grid_spec = pltpu.PrefetchScalarGridSpec(
        num_scalar_prefetch=2,
        grid=(nb, ne // eb),
        in_specs=[pl.BlockSpec((1, eb, 1, n), lambda b, e, *_: (b, e, 0, 0)),
                  pl.BlockSpec((1, eb, 1, n), lambda b, e, *_: (b, e, 0, 0)),
                  pl.BlockSpec((1, n, dm), lambda b, e, *_: (b, 0, 0), pipeline_mode=pl.Buffered(1))],
        out_specs=[pl.BlockSpec((1, eb, cap, dm), lambda b, e, *_: (b, e, 0, 0)),
                   pl.BlockSpec((1, eb, cap, 128), lambda b, e, *_: (b, e, 0, 0))])
    return pl.pallas_call(
        functools.partial(_moe_gather_kernel, cap=cap, tb=tb, w=w),
        grid_spec=grid_spec,
        out_shape=[jax.ShapeDtypeStruct((nb, ne, cap, dm), BF16),
                   jax.ShapeDtypeStruct((nb, ne, cap, 128), F32)],
        compiler_params=_params("parallel", "arbitrary"),
        name="moe_gather",
    )(start, dense, pos, aff, h)


def _moe_ffn_kernel(xs_ref, gate_ref, m_ref, wg_ref, wu_ref, wd_ref, y_ref, wgb_ref, wub_ref, wdb_ref):
    @pl.when(pl.program_id(1) == 0)
    def _():
        rows = 256
        for src, dst in ((wg_ref, wgb_ref), (wu_ref, wub_ref), (wd_ref, wdb_ref)):
            for r0 in range(0, dst.shape[0], rows):
                dst[r0:r0 + rows, :] = src[0, 0, r0:r0 + rows, :].astype(BF16)

    nb, _, cap, dm = xs_ref.shape
    xs = xs_ref[...].reshape(nb * cap, dm)
    a = jnp.dot(xs, wgb_ref[...], preferred_element_type=F32)
    u = jnp.dot(xs, wub_ref[...], preferred_element_type=F32)
    act = (_silu(a) * u).astype(BF16)
    out = jnp.dot(act, wdb_ref[...], preferred_element_type=F32)
    for i in range(nb):
        y = out[i * cap:(i + 1) * cap] * gate_ref[i, 0][:, 0:1]
        y_ref[i, 0] = (y * m_ref[i, 0][5:6]).astype(BF16)


def _moe_ffn(xs, gate, modsel, wg, wu, wd, layer, *, sel):
    nb_all, ne, cap, dm = xs.shape
    df = wg.shape[3]
    nb = 2 if nb_all % 2 == 0 else 1
    return pl.pallas_call(
        _moe_ffn_kernel,
        grid=(ne, nb_all // nb),
        in_specs=[pl.BlockSpec((nb, 1, cap, dm), lambda e, c: (c, e, 0, 0)),
                  pl.BlockSpec((nb, 1, cap, 128), lambda e, c: (c, e, 0, 0)),
                  pl.BlockSpec((nb, 1, 6, dm), lambda e, c: (c, sel, 0, 0)),
                  pl.BlockSpec((1, 1, dm, df), lambda e, c: (layer, e, 0, 0)),
                  pl.BlockSpec((1, 1, dm, df), lambda e, c: (layer, e, 0, 0)),
                  pl.BlockSpec((1, 1, df, dm), lambda e, c: (layer, e, 0, 0))],
        out_specs=pl.BlockSpec((nb, 1, cap, dm), lambda e, c: (c, e, 0, 0)),
        out_shape=jax.ShapeDtypeStruct(xs.shape, BF16),
        scratch_shapes=[pltpu.VMEM((dm, df), BF16), pltpu.VMEM((dm, df), BF16), pltpu.VMEM((df, dm), BF16)],
        compiler_params=_params("arbitrary", "arbitrary"),
        name="moe_ffn",
    )(xs, gate, modsel, wg, wu, wd)


def _moe_scatter_kernel(start_ref, dense_ref, posc_ref, y_ref, x_ref, o_ref, *, cap, w):
    b = pl.program_id(0)
    k = pl.program_id(1)
    nk = pl.num_programs(1)
    ne = y_ref.shape[1]
    tb = posc_ref.shape[1]
    dense = dense_ref[b * nk + k]

    @pl.when(dense != 0)
    def _():
        slot = lax.broadcasted_iota(jnp.int32, (tb, cap), 1)
        pick = jnp.concatenate([jnp.where(slot == posc_ref[0, :, i:i + 1], 1.0, 0.0).astype(BF16)
                                for i in range(ne)], axis=1)
        y = y_ref[0].reshape(ne * cap, y_ref.shape[3])
        o_ref[0] = x_ref[0] + jnp.dot(pick, y, preferred_element_type=F32)

    @pl.when(dense == 0)
    def _():
        per = 128 // w
        lane = lax.broadcasted_iota(jnp.int32, (tb, 128), 1)
        tiles = []
        wins = []
        for j in range(ne // per):
            col = None
            for u in reversed(range(per)):
                i = j * per + u
                s = pl.multiple_of(start_ref[(b * ne + i) * nk + k], 16)
                rel = posc_ref[0, :, i:i + 1] - s
                target = jnp.where(jnp.logical_and(rel >= 0, rel < w), rel + u * w, -1)
                col = target if col is None else jnp.where(lane < (u + 1) * w, target, col)
            tiles.append(jnp.where(lane == col, 1.0, 0.0).astype(BF16))
            for u in range(per):
                i = j * per + u
                s = pl.multiple_of(start_ref[(b * ne + i) * nk + k], 16)
                wins.append(y_ref[0, i, pl.ds(s, w), :])
        pick = jnp.concatenate(tiles, axis=1)
        o_ref[0] = x_ref[0] + jnp.dot(pick, jnp.concatenate(wins, axis=0), preferred_element_type=F32)


def _moe_scatter(start, dense, pos, y, xcat, *, bb, row0, tb, w):
    nb, ne, cap, dm = y.shape
    n = pos.shape[3]
    assert 128 % w == 0 and ne % (128 // w) == 0
    posc = jnp.transpose(pos.reshape(nb, ne, n), (0, 2, 1))
    x_index = (lambda b, k, *_: (b, k, 0)) if bb == 1 else (lambda b, k, *_: (k, row0 // tb, 0))
    grid_spec = pltpu.PrefetchScalarGridSpec(
        num_scalar_prefetch=2,
        grid=(nb, n // tb),
        in_specs=[pl.BlockSpec((1, tb, ne), lambda b, k, *_: (b, k, 0)),
                  pl.BlockSpec((1, ne, cap, dm), lambda b, k, *_: (b, 0, 0, 0)),
                  pl.BlockSpec((1, tb, dm), x_index)],
        out_specs=pl.BlockSpec((1, tb, dm), x_index))
    return pl.pallas_call(
        functools.partial(_moe_scatter_kernel, cap=cap, w=w),
        grid_spec=grid_spec,
        out_shape=jax.ShapeDtypeStruct(xcat.shape, F32),
        input_output_aliases={4: 0},
        compiler_params=_params("parallel", "arbitrary"),
        name="moe_scatter",
    )(start, dense, posc, y, xcat)


def _ec_moe(xcat, modsel, g2, wr, upper, wg, wu, wd, layer, *, rt, seq, ctx, with_ctx):
    b = xcat.shape[0]
    ne = N_EXPERTS
    h, aff, pos = _router(xcat, modsel, g2, wr, upper, rt=rt, n=seq, row_tile0=0, sel=0)
    cap = max(1, EC_FACTOR * seq // N_EXPERTS)
    pos = pos.reshape(b, ne, 1, seq)
    tb = min(MOE_TOKEN_BLOCK, ctx)
    win = dict(tb=tb, w=min(MOE_WINDOW, cap))
    start, dense_group, dense_block = _moe_plan(pos, cap=cap, **win)
    xs, gate = _moe_gather(start, dense_group, pos, aff.reshape(b, ne, 1, seq), h, cap=cap, **win)
    y = _moe_ffn(xs, gate, modsel, wg, wu, wd, layer, sel=0)
    xcat = _moe_scatter(start, dense_block, pos, y, xcat, bb=1, row0=0, **win)
    if with_ctx:
        hc, affc, posc = _router(xcat, modsel, g2, wr, upper, rt=rt, n=ctx, row_tile0=seq // rt, sel=1)
        capc = max(1, EC_FACTOR * ctx // N_EXPERTS)
        offs = (jnp.arange(b, dtype=jnp.int32) * capc)[:, None, None]
        posc = jnp.where(posc >= 0, posc + offs, -1)
        posc = jnp.transpose(posc, (1, 0, 2)).reshape(1, ne, 1, b * ctx)
        affc = jnp.transpose(affc, (1, 0, 2)).reshape(1, ne, 1, b * ctx)
        win = dict(tb=tb, w=min(MOE_WINDOW, b * capc))
        assert tb == ctx
        start, dense_group, dense_block = _moe_plan(posc, cap=b * capc, **win)
        xs, gate = _moe_gather(start, dense_group, posc, affc, hc.reshape(1, b * ctx, -1), cap=b * capc, **win)
        y = _moe_ffn(xs, gate, modsel, wg, wu, wd, layer, sel=1)
        xcat = _moe_scatter(start, dense_block, posc, y, xcat, bb=b, row0=seq, **win)
    return xcat


def _final_kernel(x_ref, g_ref, o_ref):
    x = x_ref[0]
    o_ref[0] = x * lax.rsqrt(jnp.mean(x * x, axis=-1, keepdims=True) + NORM_EPS) * g_ref[...]


def _final_norm(xcat, g, *, rt, seq):
    b, ta, dm = xcat.shape
    return pl.pallas_call(
        _final_kernel,
        grid=(b, seq // rt),
        in_specs=[pl.BlockSpec((1, rt, dm), lambda bb, r: (bb, r, 0)),
                  pl.BlockSpec((1, dm), lambda bb, r: (0, 0))],
        out_specs=pl.BlockSpec((1, rt, dm), lambda bb, r: (bb, r, 0)),
        out_shape=jax.ShapeDtypeStruct((b, seq, dm), F32),
        compiler_params=_params("parallel", "arbitrary"),
        name="final_norm",
    )(xcat, g)


def _rope_tables(seq, ctx):
    rows = seq // GRID_W
    row_ids = jnp.repeat(jnp.arange(rows, dtype=F32), GRID_W)
    col_ids = jnp.tile(jnp.arange(GRID_W, dtype=F32), rows)
    half = A_HEAD_DIM // 2
    inv = ROPE_THETA ** (-jnp.arange(0, half, 2, dtype=F32) / half)
    ang_r = row_ids[:, None] * inv
    ang_c = col_ids[:, None] * inv
    cos = jnp.concatenate([jnp.cos(ang_r)] * 2 + [jnp.cos(ang_c)] * 2, axis=1)
    sin = jnp.concatenate([-jnp.sin(ang_r), jnp.sin(ang_r), -jnp.sin(ang_c), jnp.sin(ang_c)], axis=1)
    cos = jnp.tile(cos, (1, A_HEADS))
    sin = jnp.tile(sin, (1, A_HEADS))
    cos = jnp.concatenate([cos, jnp.ones((ctx, A_Q), F32)], axis=0)
    sin = jnp.concatenate([sin, jnp.zeros((ctx, A_Q), F32)], axis=0)
    return cos, sin


def kernel(x, c, ctx, c_ctx, w_mod, b_mod, g_norm1, g_norm2, w_in_even, w_out_even, g_qnorm, g_knorm, w_conv,
           b_gate, g_hnorm, w_in_odd, w_pool_grp, s_pool, w_router, w_exp_gate, w_exp_up, w_exp_down, g_final):
    b, seq, dm = x.shape
    n_ctx = ctx.shape[1]
    depth = w_mod.shape[0]
    rt = 256 if (seq % 256 == 0 and n_ctx % 256 == 0) else 128
    assert seq % rt == 0 and n_ctx % rt == 0 and seq % n_ctx == 0 and seq % GRID_W == 0
    nlt = seq // rt
    nt = (seq + n_ctx) // rt

    xcat = jnp.concatenate([x, ctx, jnp.zeros((b, seq - n_ctx, dm), F32)], axis=1)

    rows = -(-(b + 1) // 8) * 8
    cond = jnp.zeros((rows, dm), F32).at[:b].set(c).at[b].set(c_ctx)
    mods = _adaln(cond, w_mod, b_mod)
    lat = mods[:, :b].reshape(depth, b, 1, 6, dm)
    cx = jnp.broadcast_to(mods[:, b].reshape(depth, 1, 1, 6, dm), (depth, b, 1, 6, dm))
    modsel = jnp.concatenate([lat, cx], axis=2)

    cos, sin = _rope_tables(seq, n_ctx)
    lane = jnp.arange(A_Q)
    gmat = (lane[:, None] // A_HEAD_DIM == lane[None, :] // A_HEAD_DIM).astype(BF16)
    blk = min(256, n_ctx)
    tri = jnp.arange(blk)
    upper = (tri[:, None] <= tri[None, :]).astype(BF16)

    wg, wu, wd = w_exp_gate, w_exp_up, w_exp_down
    wr = jnp.pad(w_router, ((0, 0), (0, 0), (0, GATE_PAD - N_EXPERTS)))

    for i in range(depth):
        ctx_next = any(j % 2 == 0 for j in range(i + 1, depth))
        n_upd = nt if ctx_next else nlt
        ms = modsel[i]
        if i % 2 == 0:
            e = i // 2
            w_in = w_in_even[e]
            pad = jnp.zeros((dm, GATE_PAD - N_GATES), F32)
            w_in = jnp.concatenate([w_in, pad], axis=1).astype(BF16)
            bg = jnp.concatenate([b_gate[e], jnp.zeros((GATE_PAD - N_GATES,), F32)])[None]
            q, k, v, qk, vbt, ob, gt, gc, gr = _inproj(
                xcat, ms, g_norm1[i][None], w_in, cos, sin,
                jnp.tile(g_qnorm[e], A_HEADS)[None], jnp.tile(g_knorm[e], A_KV_HEADS)[None], gmat, bg,
                rt=rt, ta=seq + n_ctx, n_lat_tiles=nlt)
            oa = _attention(q, k, v, rt=rt, n_tiles=n_upd, n_lat_tiles=nlt, seq=seq)
            qct = _short_conv(qk, w_conv[e], seq=seq, keys=False)
            kc = _short_conv(qk, w_conv[e], seq=seq, keys=True)
            gates_row = jnp.swapaxes(jnp.concatenate([gt[:, :, :N_GATES], gc[:, :, :N_GATES]], axis=2), 1, 2)
            hf, hb = _mlstm(qct, kc, vbt, gates_row, gr, seq=seq, blk=rt)
            xcat = _outproj(xcat, ms, oa, hf, hb, ob, g_hnorm[e][None], w_out_even[e].astype(BF16),
                            rt=rt, n_tiles=n_upd, n_lat_tiles=nlt)
        else:
            o = i // 2
            w_in = w_in_odd[o].astype(BF16)
            wgrp = w_pool_grp[o].astype(BF16)
            u = _mod_matmul(xcat, ms, g_norm1[i][None], w_in, rt=rt, tile0=0, n_tiles=nlt, sel=0)
            if ctx_next:
                uc = _mod_matmul(xcat, ms, g_norm1[i][None], w_in, rt=rt, tile0=nlt, n_tiles=nt - nlt, sel=1)
            xcat = _pool(u, xcat, ms, wgrp, s_pool[o][None], n=seq, row_block=0, sel=0)
            if ctx_next:
                xcat = _pool(uc, xcat, ms, wgrp, s_pool[o][None], n=n_ctx, row_block=seq // n_ctx, sel=1)
        xcat = _ec_moe(xcat, ms, g_norm2[i][None], wr[i], upper, wg, wu, wd, i,
                       rt=rt, seq=seq, ctx=n_ctx, with_ctx=ctx_next)
    return _final_norm(xcat, g_final[None], rt=rt, seq=seq)
```

```python
import functools

import jax
import jax.numpy as jnp
from jax import lax
from jax.experimental import pallas as pl
from jax.experimental.pallas import tpu as pltpu

D_MODEL = 1024
GRID_W = 64
A_HEADS = 8
A_KV_HEADS = 2
A_HEAD_DIM = 64
ROPE_THETA = 10000.0
B_HEADS = 4
B_HEAD_DIM = 128
B_CHUNK = 128
POOL_WINDOWS = (2, 4, 8, 16)
POOL_GROUP = D_MODEL // 4
N_EXPERTS = 16
EC_FACTOR = 2
NORM_EPS = 1e-6

A_Q = A_HEADS * A_HEAD_DIM
A_KV = A_KV_HEADS * A_HEAD_DIM
B_W = B_HEADS * B_HEAD_DIM
N_GATES = 2 * 2 * B_HEADS
GATE_PAD = 128
C_QA, C_KA, C_VA = 0, A_Q, A_Q + A_KV
C_QK = A_Q + 2 * A_KV
C_VB = C_QK + 2 * B_W
C_OB = C_VB + B_W
C_GT = C_OB + B_W
EVEN_COLS = C_GT + GATE_PAD
POOL_HALO = 16

F32 = jnp.float32
BF16 = jnp.bfloat16
HIGHEST = lax.Precision.HIGHEST
NT_DIMS = (((1,), (1,)), ((), ()))
TN_DIMS = (((0,), (0,)), ((), ()))
VMEM_LIMIT = 56 * 1024 * 1024


def _params(*sem):
    return pltpu.CompilerParams(dimension_semantics=sem, vmem_limit_bytes=VMEM_LIMIT)


def _modulate(x, g, shift, scale):
    y = x * lax.rsqrt(jnp.mean(x * x, axis=-1, keepdims=True) + NORM_EPS)
    return (y * g) * (1.0 + scale) + shift


def _silu(x):
    return x * jax.nn.sigmoid(x)


def _adaln_kernel(c_ref, w_ref, b_ref, o_ref):
    s = _silu(c_ref[...])
    o_ref[0] = jnp.dot(s, w_ref[0], precision=HIGHEST, preferred_element_type=F32) + b_ref[0]


def _adaln(cond, w_mod, b_mod):
    depth, dm, n6 = w_mod.shape
    rows = cond.shape[0]
    tn = 1536
    return pl.pallas_call(
        _adaln_kernel,
        grid=(depth, n6 // tn),
        in_specs=[pl.BlockSpec((rows, dm), lambda l, j: (0, 0)),
                  pl.BlockSpec((1, dm, tn), lambda l, j: (l, 0, j)),
                  pl.BlockSpec((1, 1, tn), lambda l, j: (l, 0, j))],
        out_specs=pl.BlockSpec((1, rows, tn), lambda l, j: (l, 0, j)),
        out_shape=jax.ShapeDtypeStruct((depth, rows, n6), F32),
        compiler_params=_params("arbitrary", "arbitrary"),
        name="adaln",
    )(cond, w_mod, b_mod.reshape(depth, 1, n6))


def _group_mean_sq(x, gmat, width):
    x2 = x * x
    hi = x2.astype(BF16)
    lo = (x2 - hi.astype(F32)).astype(BF16)
    s = jnp.dot(hi, gmat, preferred_element_type=F32) + jnp.dot(lo, gmat, preferred_element_type=F32)
    return s * (1.0 / width)


def _rope(x, cos, sin):
    w = x.shape[1]
    lane = lax.broadcasted_iota(jnp.int32, x.shape, 1)
    first = (lane % 32) < 16
    partner = jnp.where(first, pltpu.roll(x, w - 16, 1), pltpu.roll(x, 16, 1))
    return x * cos + partner * sin


def _inproj_kernel(x_ref, m_ref, g1_ref, w_ref, cos_ref, sin_ref, gq_ref, gk_ref, gm_ref, bg_ref, lp_ref, ls_ref,
                   q_ref, k_ref, v_ref, qk_ref, vb_ref, ob_ref, gt_ref, gc_ref, gr_ref):
    m = m_ref[0, 0]
    h = _modulate(x_ref[0], g1_ref[...], m[0:1], m[1:2]).astype(BF16)

    def mm(lo, hi):
        return jnp.dot(h, w_ref[:, lo:hi], preferred_element_type=F32)

    cos = cos_ref[...]
    sin = sin_ref[...]
    gm = gm_ref[...]
    qa = mm(C_QA, C_KA)
    qa = qa * lax.rsqrt(_group_mean_sq(qa, gm, A_HEAD_DIM) + NORM_EPS) * gq_ref[...]
    q_ref[0] = (_rope(qa, cos, sin) * (A_HEAD_DIM ** -0.5)).astype(BF16)
    ka = mm(C_KA, C_VA)
    ka = ka * lax.rsqrt(_group_mean_sq(ka, gm[:A_KV, :A_KV], A_HEAD_DIM) + NORM_EPS) * gk_ref[...]
    k_ref[0] = _rope(ka, cos[:, :A_KV], sin[:, :A_KV]).astype(BF16)
    va = mm(C_VA, C_QK).astype(BF16)
    ones = jnp.ones((va.shape[0], A_HEAD_DIM), BF16)
    v_ref[0] = jnp.concatenate(
        [piece for g in range(A_KV_HEADS) for piece in (va[:, g * A_HEAD_DIM:(g + 1) * A_HEAD_DIM], ones)], axis=1)
    qk_ref[0] = mm(C_QK, C_VB)
    vb_ref[0] = mm(C_VB, C_OB).T.astype(BF16)
    ob_ref[0] = mm(C_OB, C_GT)
    gt = mm(C_GT, EVEN_COLS) + bg_ref[...]
    lane = lax.broadcasted_iota(jnp.int32, gt.shape, 1)
    log_sig = jnp.minimum(gt, 0.0) - jnp.log1p(jnp.exp(-jnp.abs(gt)))
    gt = jnp.where((lane % (2 * B_HEADS)) >= B_HEADS, log_sig, gt)
    gt_ref[0] = gt
    prefix = jnp.dot(lp_ref[...], gt, precision=HIGHEST, preferred_element_type=F32)
    suffix = jnp.dot(ls_ref[...], gt, precision=HIGHEST, preferred_element_type=F32)
    cum = jnp.where((lane % (4 * B_HEADS)) < 2 * B_HEADS, prefix, suffix)
    gc_ref[0] = cum
    gr_ref[0] = gt - pltpu.roll(cum, GATE_PAD - B_HEADS, 1)


def _inproj(xcat, modsel, g1, w, cos, sin, gq, gk, gmat, bgate, *, rt, ta, n_lat_tiles):
    b, _, dm = xcat.shape
    nt = ta // rt
    row = lambda bb, r: (bb, r, 0)
    full = lambda bb, r: (0, 0)
    out_widths = (A_Q, A_KV, 2 * A_KV, 2 * B_W, B_W, B_W, GATE_PAD, GATE_PAD, GATE_PAD)
    out_dtypes = (BF16, BF16, BF16, F32, BF16, F32, F32, F32, F32)
    out_specs = [pl.BlockSpec((1, rt, wd), row) for wd in out_widths]
    out_shapes = [jax.ShapeDtypeStruct((b, ta, wd), dt) for wd, dt in zip(out_widths, out_dtypes)]
    vb_slot = 4
    out_specs[vb_slot] = pl.BlockSpec((1, B_W, rt), lambda bb, r: (bb, 0, r))
    out_shapes[vb_slot] = jax.ShapeDtypeStruct((b, B_W, ta), BF16)
    idx = jnp.arange(rt)
    same_chunk = idx[:, None] // B_CHUNK == idx[None, :] // B_CHUNK
    lower = jnp.logical_and(same_chunk, idx[None, :] <= idx[:, None]).astype(F32)
    upper = jnp.logical_and(same_chunk, idx[None, :] >= idx[:, None]).astype(F32)
    return pl.pallas_call(
        _inproj_kernel,
        grid=(b, nt),
        in_specs=[pl.BlockSpec((1, rt, dm), row),
                  pl.BlockSpec((1, 1, 6, dm), lambda bb, r: (bb, (r >= n_lat_tiles).astype(jnp.int32), 0, 0)),
                  pl.BlockSpec((1, dm), full),
                  pl.BlockSpec((dm, EVEN_COLS), full),
                  pl.BlockSpec((rt, A_Q), lambda bb, r: (r, 0)),
                  pl.BlockSpec((rt, A_Q), lambda bb, r: (r, 0)),
                  pl.BlockSpec((1, A_Q), full),
                  pl.BlockSpec((1, A_KV), full),
                  pl.BlockSpec((A_Q, A_Q), full),
                  pl.BlockSpec((1, GATE_PAD), full),
                  pl.BlockSpec((rt, rt), full),
                  pl.BlockSpec((rt, rt), full)],
        out_specs=out_specs,
        out_shape=out_shapes,
        compiler_params=_params("parallel", "arbitrary"),
        name="even_inproj",
    )(xcat, modsel, g1, w, cos, sin, gq, gk, gmat, bgate, lower, upper)


def _attn_kernel(q_ref, k_ref, v_ref, o_ref, *, n_lat_tiles, n_tiles, seq):
    group = A_HEADS // A_KV_HEADS

    def attend(key_lo):
        q = q_ref[0]
        for hd in range(A_HEADS):
            g = hd // group
            qh = q[:, hd * A_HEAD_DIM:(hd + 1) * A_HEAD_DIM]
            kg = k_ref[0, key_lo:, g * A_HEAD_DIM:(g + 1) * A_HEAD_DIM]
            vg = v_ref[0, key_lo:, 2 * g * A_HEAD_DIM:2 * (g + 1) * A_HEAD_DIM]
            s = lax.dot_general(qh, kg, NT_DIMS, preferred_element_type=F32)
            p = jnp.exp(s - jnp.max(s, axis=-1, keepdims=True))
            o = jnp.dot(p.astype(BF16), vg, preferred_element_type=F32)
            o = o[:, :A_HEAD_DIM] / o[:, A_HEAD_DIM:]
            o_ref[0, :, hd * A_HEAD_DIM:(hd + 1) * A_HEAD_DIM] = o.astype(BF16)

    if n_tiles == n_lat_tiles:
        attend(0)
    else:
        r = pl.program_id(1)
        pl.when(r < n_lat_tiles)(lambda: attend(0))
        pl.when(r >= n_lat_tiles)(lambda: attend(seq))


def _attention(q, k, v, *, rt, n_tiles, n_lat_tiles, seq):
    b, ta, _ = q.shape
    return pl.pallas_call(
        functools.partial(_attn_kernel, n_lat_tiles=n_lat_tiles, n_tiles=n_tiles, seq=seq),
        grid=(b, n_tiles),
        in_specs=[pl.BlockSpec((1, rt, A_Q), lambda bb, r: (bb, r, 0)),
                  pl.BlockSpec((1, ta, A_KV), lambda bb, r: (bb, 0, 0)),
                  pl.BlockSpec((1, ta, 2 * A_KV), lambda bb, r: (bb, 0, 0))],
        out_specs=pl.BlockSpec((1, rt, A_Q), lambda bb, r: (bb, r, 0)),
        out_shape=jax.ShapeDtypeStruct((b, n_tiles * rt, A_Q), BF16),
        compiler_params=_params("parallel", "arbitrary"),
        name="gqa_attention",
    )(q, k, v)


def _conv_kernel(x_ref, w_ref, o_ref, *, seq, keys):
    x = x_ref[0]
    ta = x.shape[0]
    w = w_ref[...]
    row = lax.broadcasted_iota(jnp.int32, (ta, 1), 0)
    prev = jnp.where(jnp.logical_or(row == 0, row == seq), 0.0, pltpu.roll(x, 1, 0))
    nxt = jnp.where(jnp.logical_or(row == seq - 1, row == ta - 1), 0.0, pltpu.roll(x, ta - 1, 0))
    y = _silu(prev * w[0:1] + x * w[1:2] + nxt * w[2:3])
    if keys:
        o_ref[0] = (y * (B_HEAD_DIM ** -0.5)).astype(BF16)
    else:
        o_ref[0] = y.T.astype(BF16)


def _short_conv(qk, w_conv, *, seq, keys):
    b, ta, _ = qk.shape
    ct = 256
    c0 = B_W // ct if keys else 0
    if keys:
        out_spec = pl.BlockSpec((1, ta, ct), lambda bb, j: (bb, 0, j))
        out_shape = jax.ShapeDtypeStruct((b, ta, B_W), BF16)
    else:
        out_spec = pl.BlockSpec((1, ct, ta), lambda bb, j: (bb, j, 0))
        out_shape = jax.ShapeDtypeStruct((b, B_W, ta), BF16)
    return pl.pallas_call(
        functools.partial(_conv_kernel, seq=seq, keys=keys),
        grid=(b, B_W // ct),
        in_specs=[pl.BlockSpec((1, ta, ct), lambda bb, j: (bb, 0, c0 + j)),
                  pl.BlockSpec((w_conv.shape[0], ct), lambda bb, j: (0, c0 + j))],
        out_specs=out_spec,
        out_shape=out_shape,
        compiler_params=_params("parallel", "arbitrary"),
        name="mlstm_conv_k" if keys else "mlstm_conv_q",
    )(qk, w_conv)


def _mlstm_chain(qt, k, vat, ig, b_row, r_col, total, ca, m_old, tri):
    log_d = jnp.where(tri, b_row + r_col, -jnp.inf)
    log_inter = m_old + b_row
    m_t = jnp.maximum(log_inter, jnp.max(log_d, axis=0, keepdims=True))
    w_intra = jnp.exp(log_d - m_t) * jnp.dot(k, qt, preferred_element_type=F32)
    w_inter = jnp.exp(log_inter - m_t)
    nd = (w_inter * jnp.dot(ca.astype(BF16), qt, preferred_element_type=F32)
          + jnp.dot(vat, w_intra.astype(BF16), preferred_element_type=F32))
    dv = nd.shape[0] // 2
    h = nd[:dv] / jnp.maximum(jnp.abs(nd[dv:]), jnp.exp(-m_t))
    log_w = total - b_row + ig
    m_new = jnp.maximum(m_old + total, jnp.max(log_w, axis=-1, keepdims=True))
    w_s = jnp.exp(log_w - m_new)
    decay = jnp.exp(m_old + total - m_new)
    ca_new = decay * ca + jnp.dot((vat * w_s).astype(BF16), k, preferred_element_type=F32)
    return h, ca_new, m_new


def _mlstm_kernel(qf_ref, kf_ref, vf_ref, grf_ref, gcf_ref, qb_ref, kb_ref, vb_ref, grb_ref, gcb_ref,
                  hf_ref, hb_ref, c_ref, m_ref):
    @pl.when(pl.program_id(1) == 0)
    def _():
        c_ref[...] = jnp.zeros(c_ref.shape, F32)
        m_ref[...] = jnp.full(m_ref.shape, -1e30, F32)

    t = B_CHUNK
    n_chunks = kf_ref.shape[1] // t
    src = lax.broadcasted_iota(jnp.int32, (t, t), 0)
    tgt = lax.broadcasted_iota(jnp.int32, (t, t), 1)
    ones = jnp.ones((B_HEAD_DIM, t), BF16)
    dirs = ((qf_ref, kf_ref, vf_ref, grf_ref, gcf_ref, hf_ref, src <= tgt, t - 1, range(n_chunks)),
            (qb_ref, kb_ref, vb_ref, grb_ref, gcb_ref, hb_ref, src >= tgt, 0, range(n_chunks - 1, -1, -1)))
    outputs = []
    states = []
    for d, (q_ref, k_ref, v_ref, gr_ref, gc_ref, h_ref, tri, last, order) in enumerate(dirs):
        for hd in range(B_HEADS):
            sl = slice(hd * B_HEAD_DIM, (hd + 1) * B_HEAD_DIM)
            gi = d * 2 * B_HEADS + hd
            fi = gi + B_HEADS
            si = d * B_HEADS + hd
            ca = c_ref[si]
            m = m_ref[si][:, 0:1]
            for ci in order:
                rows = slice(ci * t, (ci + 1) * t)
                ig = gr_ref[0, gi:gi + 1, rows]
                b_row = gr_ref[0, N_GATES + fi:N_GATES + fi + 1, rows]
                r_col = gc_ref[0, rows, gi:gi + 1]
                vat = jnp.concatenate([v_ref[0, sl, rows], ones], axis=0)
                h, ca, m = _mlstm_chain(q_ref[0, sl, rows], k_ref[0, rows, sl], vat, ig, b_row, r_col,
                                        b_row[:, last:last + 1], ca, m, tri)
                outputs.append((h_ref, rows, sl, h))
            states.append((si, ca, m))
    for h_ref, rows, sl, h in outputs:
        h_ref[0, sl, rows] = h
    for si, ca, m in states:
        c_ref[si] = ca
        m_ref[si] = jnp.broadcast_to(m, (1, 128))


def _mlstm(qct, kc, vbt, gates_row, gates_col, *, seq, blk):
    b, ta, _ = kc.shape
    nc = ta // blk
    ncl = seq // blk
    ncc = nc - ncl
    fwd = lambda j: jnp.where(j < ncc, ncl + j, j - ncc)
    bwd = lambda j: jnp.where(j < ncc, nc - 1 - j, ncl - 1 - (j - ncc))

    def specs(order):
        feature_major = pl.BlockSpec((1, B_W, blk), lambda bb, j: (bb, 0, order(j)))
        return [feature_major,
                pl.BlockSpec((1, blk, B_W), lambda bb, j: (bb, order(j), 0)),
                feature_major,
                pl.BlockSpec((1, 2 * N_GATES, blk), lambda bb, j: (bb, 0, order(j))),
                pl.BlockSpec((1, blk, GATE_PAD), lambda bb, j: (bb, order(j), 0))]

    n_state = 2 * B_HEADS
    ops = (qct, kc, vbt, gates_row, gates_col)
    return pl.pallas_call(
        _mlstm_kernel,
        grid=(b, nc),
        in_specs=specs(fwd) + specs(bwd),
        out_specs=[pl.BlockSpec((1, B_W, blk), lambda bb, j: (bb, 0, fwd(j))),
                   pl.BlockSpec((1, B_W, blk), lambda bb, j: (bb, 0, bwd(j)))],
        out_shape=[jax.ShapeDtypeStruct((b, B_W, ta), F32)] * 2,
        scratch_shapes=[pltpu.VMEM((n_state, 2 * B_HEAD_DIM, B_HEAD_DIM), F32),
                        pltpu.VMEM((n_state, 1, 128), F32)],
        compiler_params=_params("parallel", "arbitrary"),
        name="mlstm_scan",
    )(*ops, *ops)


def _outproj_kernel(x_ref, m_ref, oa_ref, hf_ref, hb_ref, ob_ref, ghn_ref, w_ref, o_ref):
    hm = (hf_ref[0] + hb_ref[0]).T
    ob = ob_ref[0]
    ghn = ghn_ref[...]
    y = jnp.dot(oa_ref[0], w_ref[0:A_Q, :], preferred_element_type=F32)
    for hd in range(B_HEADS):
        sl = slice(hd * B_HEAD_DIM, (hd + 1) * B_HEAD_DIM)
        z = hm[:, sl]
        z = z * lax.rsqrt(jnp.mean(z * z, axis=-1, keepdims=True) + NORM_EPS) * ghn[:, sl]
        z = (z * jax.nn.sigmoid(ob[:, sl])).astype(BF16)
        y = y + jnp.dot(z, w_ref[A_Q + hd * B_HEAD_DIM:A_Q + (hd + 1) * B_HEAD_DIM, :],
                        preferred_element_type=F32)
    o_ref[0] = x_ref[0] + m_ref[0, 0][2:3] * y


def _outproj(xcat, modsel, oa, hf, hb, ob, ghn, w, *, rt, n_tiles, n_lat_tiles):
    b, ta, dm = xcat.shape
    row = lambda bb, r: (bb, r, 0)
    full = lambda bb, r: (0, 0)
    return pl.pallas_call(
        _outproj_kernel,
        grid=(b, n_tiles),
        in_specs=[pl.BlockSpec((1, rt, dm), row),
                  pl.BlockSpec((1, 1, 6, dm), lambda bb, r: (bb, (r >= n_lat_tiles).astype(jnp.int32), 0, 0)),
                  pl.BlockSpec((1, rt, A_Q), row),
                  pl.BlockSpec((1, B_W, rt), lambda bb, r: (bb, 0, r)),
                  pl.BlockSpec((1, B_W, rt), lambda bb, r: (bb, 0, r)),
                  pl.BlockSpec((1, rt, B_W), row),
                  pl.BlockSpec((1, B_W), full),
                  pl.BlockSpec((A_Q + B_W, dm), full)],
        out_specs=pl.BlockSpec((1, rt, dm), row),
        out_shape=jax.ShapeDtypeStruct(xcat.shape, F32),
        input_output_aliases={0: 0},
        compiler_params=_params("parallel", "arbitrary"),
        name="even_outproj",
    )(xcat, modsel, oa, hf, hb, ob, ghn, w)


def _modmm_kernel(x_ref, m_ref, g_ref, w_ref, o_ref):
    m = m_ref[0, 0]
    h = _modulate(x_ref[0], g_ref[...], m[0:1], m[1:2]).astype(BF16)
    o_ref[0] = jnp.dot(h, w_ref[...], preferred_element_type=F32)


def _mod_matmul(xcat, modsel, g, w, *, rt, tile0, n_tiles, sel):
    b, _, dm = xcat.shape
    n_out = w.shape[1]
    return pl.pallas_call(
        _modmm_kernel,
        grid=(b, n_tiles),
        in_specs=[pl.BlockSpec((1, rt, dm), lambda bb, r: (bb, tile0 + r, 0)),
                  pl.BlockSpec((1, 1, 6, dm), lambda bb, r: (bb, sel, 0, 0)),
                  pl.BlockSpec((1, dm), lambda bb, r: (0, 0)),
                  pl.BlockSpec((dm, n_out), lambda bb, r: (0, 0))],
        out_specs=pl.BlockSpec((1, rt, n_out), lambda bb, r: (bb, r, 0)),
        out_shape=jax.ShapeDtypeStruct((b, n_tiles * rt, n_out), F32),
        compiler_params=_params("parallel", "arbitrary"),
        name="odd_inproj",
    )(xcat, modsel, g, w)


def _pool_kernel(u_ref, x_ref, m_ref, wg_ref, sp_ref, o_ref, pad_ref, sa_ref, sb_ref, *, n, rc):
    g = pl.program_id(1)
    halo = POOL_HALO
    ext = n + 2 * halo
    zeros = jnp.zeros((halo, pad_ref.shape[1]), F32)
    pad_ref[0:halo, :] = zeros
    pad_ref[halo + n:ext, :] = zeros
    pad_ref[halo:halo + n, :] = u_ref[0]
    for ref in (pad_ref, sa_ref, sb_ref):
        ref[ext:ext + halo, :] = zeros
    gate = m_ref[0, 0][2:3]
    chunks = [(r0, min(rc, ext - r0)) for r0 in range(0, ext, rc)]

    def pair_sums(src, dst, shift):
        for r0, size in chunks:
            dst[r0:r0 + size, :] = src[r0:r0 + size, :] + src[r0 + shift:r0 + shift + size, :]

    for gi, win in enumerate(POOL_WINDOWS):
        @pl.when(g == gi)
        def _(win=win):
            lo = win // 2
            hi = win - 1 - lo
            sums, width = pad_ref, 1
            for dst in (sa_ref, sb_ref, sa_ref, sb_ref):
                if width == win:
                    break
                pair_sums(sums, dst, width)
                sums, width = dst, 2 * width
            assert width == win
            for r0 in range(0, n, rc):
                base = halo + r0
                acc = sums[base - lo:base - lo + rc, :]
                t = r0 + lax.broadcasted_iota(jnp.int32, (rc, 1), 0)
                cnt = jnp.minimum(t + hi, n - 1) - jnp.maximum(t - lo, 0) + 1
                p = acc / cnt.astype(F32) - pad_ref[base:base + rc, :]
                y = jnp.dot(p.astype(BF16), wg_ref[0], preferred_element_type=F32) * sp_ref[...]
                o_ref[0, r0:r0 + rc, :] = x_ref[0, r0:r0 + rc, :] + gate * y


def _pool(u, xcat, modsel, wgrp, spool, *, n, row_block, sel):
    b, _, dm = xcat.shape
    pg = POOL_GROUP
    rc = min(256, n)
    blk = lambda bb, g: (bb, row_block, g)
    return pl.pallas_call(
        functools.partial(_pool_kernel, n=n, rc=rc),
        grid=(b, dm // pg),
        in_specs=[pl.BlockSpec((1, n, pg), lambda bb, g: (bb, 0, g)),
                  pl.BlockSpec((1, n, pg), blk),
                  pl.BlockSpec((1, 1, 6, pg), lambda bb, g: (bb, sel, 0, g)),
                  pl.BlockSpec((1, pg, pg), lambda bb, g: (g, 0, 0)),
                  pl.BlockSpec((1, pg), lambda bb, g: (0, g))],
        out_specs=pl.BlockSpec((1, n, pg), blk),
        out_shape=jax.ShapeDtypeStruct(xcat.shape, F32),
        scratch_shapes=[pltpu.VMEM((n + 3 * POOL_HALO, pg), F32)] * 3,
        input_output_aliases={1: 0},
        compiler_params=_params("parallel", "arbitrary"),
        name="pool_mixer",
    )(u, xcat, modsel, wgrp, spool)


def _lane_cumsum(x01, upper, blk):
    n = x01.shape[1]
    out = []
    carry = jnp.zeros((x01.shape[0], 1), F32)
    for j in range(0, n, blk):
        cs = jnp.dot(x01[:, j:j + blk].astype(BF16), upper, preferred_element_type=F32) + carry
        carry = cs[:, blk - 1:blk]
        out.append(cs)
    return jnp.concatenate(out, axis=1) if len(out) > 1 else out[0]


def _router_kernel(x_ref, m_ref, g_ref, wr_ref, up_ref, h_ref, aff_ref, pos_ref, lg_ref, *, cap, n_tiles):
    r = pl.program_id(1)
    m = m_ref[0, 0]
    h = _modulate(x_ref[0], g_ref[...], m[3:4], m[4:5])
    h_hi = h.astype(BF16)
    h_ref[0] = h_hi
    h_lo = (h - h_hi.astype(F32)).astype(BF16)
    wr = wr_ref[...]
    w_hi = wr.astype(BF16)
    w_lo = (wr - w_hi.astype(F32)).astype(BF16)
    lg = (jnp.dot(h_hi, w_hi, preferred_element_type=F32)
          + (jnp.dot(h_hi, w_lo, preferred_element_type=F32) + jnp.dot(h_lo, w_hi, preferred_element_type=F32)))
    lg_ref[r] = lg.T[:lg_ref.shape[1]]

    @pl.when(r == n_tiles - 1)
    def _():
        lg = jnp.concatenate([lg_ref[i] for i in range(n_tiles)], axis=1) if n_tiles > 1 else lg_ref[0]
        e = jnp.exp(lg - jnp.max(lg, axis=0, keepdims=True))
        aff = e / jnp.sum(e, axis=0, keepdims=True)

        def bit_step(i, thr):
            cand = thr | lax.shift_left(jnp.int32(1), 30 - i)
            cnt = jnp.sum(jnp.where(aff >= pltpu.bitcast(cand, F32), 1.0, 0.0), axis=-1, keepdims=True)
            return jnp.where(cnt >= cap, cand, thr)

        thr = lax.fori_loop(0, 31, bit_step, jnp.zeros((aff.shape[0], 1), jnp.int32))
        thr = pltpu.bitcast(thr, F32)
        above = aff > thr
        tied = aff == thr
        room = cap - jnp.sum(jnp.where(above, 1.0, 0.0), axis=-1, keepdims=True)
        upper = up_ref[...]
        blk = upper.shape[0]
        tie_rank = _lane_cumsum(jnp.where(tied, 1.0, 0.0), upper, blk)
        sel = jnp.logical_or(above, jnp.logical_and(tied, tie_rank <= room))
        slot = _lane_cumsum(jnp.where(sel, 1.0, 0.0), upper, blk) - 1.0
        pos_ref[0] = jnp.where(sel, slot, -1.0).astype(jnp.int32)
        aff_ref[0] = aff


def _router(xcat, modsel, g2, wr, upper, *, rt, n, row_tile0, sel):
    b, ta, dm = xcat.shape
    n_tiles = n // rt
    cap = max(1, EC_FACTOR * n // N_EXPERTS)
    ne = N_EXPERTS
    return pl.pallas_call(
        functools.partial(_router_kernel, cap=cap, n_tiles=n_tiles),
        grid=(b, n_tiles),
        in_specs=[pl.BlockSpec((1, rt, dm), lambda bb, r: (bb, row_tile0 + r, 0)),
                  pl.BlockSpec((1, 1, 6, dm), lambda bb, r: (bb, sel, 0, 0)),
                  pl.BlockSpec((1, dm), lambda bb, r: (0, 0)),
                  pl.BlockSpec(wr.shape, lambda bb, r: (0, 0)),
                  pl.BlockSpec(upper.shape, lambda bb, r: (0, 0))],
        out_specs=[pl.BlockSpec((1, rt, dm), lambda bb, r: (bb, r, 0)),
                   pl.BlockSpec((1, ne, n), lambda bb, r: (bb, 0, 0)),
                   pl.BlockSpec((1, ne, n), lambda bb, r: (bb, 0, 0))],
        out_shape=[jax.ShapeDtypeStruct((b, n, dm), BF16),
                   jax.ShapeDtypeStruct((b, ne, n), F32),
                   jax.ShapeDtypeStruct((b, ne, n), jnp.int32)],
        scratch_shapes=[pltpu.VMEM((n_tiles, ne, rt), F32)],
        compiler_params=_params("parallel", "arbitrary"),
        name="moe_router",
    )(xcat, modsel, g2, wr, upper)


MOE_EXPERT_BLOCK = 4
MOE_TOKEN_BLOCK = 256
MOE_WINDOW = 64


def _slot_onehot(pos_ref, cap):
    hits = []
    for i in range(pos_ref.shape[1]):
        pos = pos_ref[0, i]
        hits.append(lax.broadcasted_iota(jnp.int32, (cap, pos.shape[1]), 0) == pos)
    pick = jnp.concatenate([jnp.where(hit, 1.0, 0.0).astype(BF16) for hit in hits], axis=0)
    return pick, hits


def _moe_plan(pos, *, cap, tb, w):
    nb, ne, _, n = pos.shape
    nk = n // tb
    cnt = jnp.sum((pos.reshape(nb, ne, nk, tb) >= 0).astype(jnp.int32), axis=-1)
    first = jnp.cumsum(cnt, axis=-1) - cnt
    start = jnp.minimum((first // 16) * 16, cap - w)
    fits = first - start + cnt <= w
    dense_group = jnp.logical_not(jnp.all(fits.reshape(nb, ne // MOE_EXPERT_BLOCK, -1), axis=-1))
    dense_block = jnp.logical_not(jnp.all(fits, axis=1))
    flat = lambda a: a.reshape(-1).astype(jnp.int32)
    return flat(start), flat(dense_group), flat(dense_block)


def _moe_gather_kernel(start_ref, dense_ref, pos_ref, aff_ref, h_ref, xs_ref, gate_ref, *, cap, tb, w):
    b = pl.program_id(0)
    g = pl.program_id(1)
    eb, n = pos_ref.shape[1], pos_ref.shape[3]
    nk = n // tb
    dense = dense_ref[b * pl.num_programs(1) + g]

    @pl.when(dense != 0)
    def _():
        pick, hits = _slot_onehot(pos_ref, cap)
        xs = jnp.dot(pick, h_ref[0], preferred_element_type=F32).astype(BF16)
        xs_ref[0] = xs.reshape(xs_ref.shape[1:])
        for i, hit in enumerate(hits):
            gate = jnp.sum(jnp.where(hit, aff_ref[0, i], 0.0), axis=-1, keepdims=True)
            gate_ref[0, i] = jnp.broadcast_to(gate, gate_ref.shape[2:])

    @pl.when(dense == 0)
    def _():
        xs_ref[...] = jnp.zeros(xs_ref.shape, BF16)
        gate_ref[...] = jnp.zeros(gate_ref.shape, F32)
        slot = lax.broadcasted_iota(jnp.int32, (w, tb), 0)
        for k in range(nk):
            cols = slice(k * tb, (k + 1) * tb)
            starts = [pl.multiple_of(start_ref[((b * pl.num_programs(1) + g) * eb + i) * nk + k], 16)
                      for i in range(eb)]
            hits = [(slot + starts[i]) == pos_ref[0, i, :, cols] for i in range(eb)]
            pick = jnp.concatenate([jnp.where(hit, 1.0, 0.0).astype(BF16) for hit in hits], axis=0)
            part = jnp.dot(pick, h_ref[0, cols, :], preferred_element_type=F32).astype(BF16)
            for i in range(eb):
                rows = pl.ds(starts[i], w)
                xs_ref[0, i, rows, :] = xs_ref[0, i, rows, :] + part[i * w:(i + 1) * w]
                gate = jnp.sum(jnp.where(hits[i], aff_ref[0, i, :, cols], 0.0), axis=-1, keepdims=True)
                gate_ref[0, i, rows, :] = gate_ref[0, i, rows, :] + gate


def _moe_gather(start, dense, pos, aff, h, *, cap, tb, w):
    nb, ne, _, n = pos.shape
    dm = h.shape[2]
    eb = MOE_EXPERT_BLOCK
    grid_spec = pltpu.PrefetchScalarGridSpec(
        num_scalar_prefetch=2,
        grid=(nb, ne // eb),
        in_specs=[pl.BlockSpec((1, eb, 1, n), lambda b, e, *_: (b, e, 0, 0)),
                  pl.BlockSpec((1, eb, 1, n), lambda b, e, *_: (b, e, 0, 0)),
                  pl.BlockSpec((1, n, dm), lambda b, e, *_: (b, 0, 0), pipeline_mode=pl.Buffered(1))],
        out_specs=[pl.BlockSpec((1, eb, cap, dm), lambda b, e, *_: (b, e, 0, 0)),
                   pl.BlockSpec((1, eb, cap, 128), lambda b, e, *_: (b, e, 0, 0))])
    return pl.pallas_call(
        functools.partial(_moe_gather_kernel, cap=cap, tb=tb, w=w),
        grid_spec=grid_spec,
        out_shape=[jax.ShapeDtypeStruct((nb, ne, cap, dm), BF16),
                   jax.ShapeDtypeStruct((nb, ne, cap, 128), F32)],
        compiler_params=_params("parallel", "arbitrary"),
        name="moe_gather",
    )(start, dense, pos, aff, h)


def _moe_ffn_kernel(xs_ref, gate_ref, m_ref, wg_ref, wu_ref, wd_ref, y_ref, wgb_ref, wub_ref, wdb_ref):
    @pl.when(pl.program_id(1) == 0)
    def _():
        rows = 256
        for src, dst in ((wg_ref, wgb_ref), (wu_ref, wub_ref), (wd_ref, wdb_ref)):
            for r0 in range(0, dst.shape[0], rows):
                dst[r0:r0 + rows, :] = src[0, 0, r0:r0 + rows, :].astype(BF16)

    nb, _, cap, dm = xs_ref.shape
    xs = xs_ref[...].reshape(nb * cap, dm)
    a = jnp.dot(xs, wgb_ref[...], preferred_element_type=F32)
    u = jnp.dot(xs, wub_ref[...], preferred_element_type=F32)
    act = (_silu(a) * u).astype(BF16)
    out = jnp.dot(act, wdb_ref[...], preferred_element_type=F32)
    for i in range(nb):
        y = out[i * cap:(i + 1) * cap] * gate_ref[i, 0][:, 0:1]
        y_ref[i, 0] = (y * m_ref[i, 0][5:6]).astype(BF16)


def _moe_ffn(xs, gate, modsel, wg, wu, wd, layer, *, sel):
    nb_all, ne, cap, dm = xs.shape
    df = wg.shape[3]
    nb = 2 if nb_all % 2 == 0 else 1
    return pl.pallas_call(
        _moe_ffn_kernel,
        grid=(ne, nb_all // nb),
        in_specs=[pl.BlockSpec((nb, 1, cap, dm), lambda e, c: (c, e, 0, 0)),
                  pl.BlockSpec((nb, 1, cap, 128), lambda e, c: (c, e, 0, 0)),
                  pl.BlockSpec((nb, 1, 6, dm), lambda e, c: (c, sel, 0, 0)),
                  pl.BlockSpec((1, 1, dm, df), lambda e, c: (layer, e, 0, 0)),
                  pl.BlockSpec((1, 1, dm, df), lambda e, c: (layer, e, 0, 0)),
                  pl.BlockSpec((1, 1, df, dm), lambda e, c: (layer, e, 0, 0))],
        out_specs=pl.BlockSpec((nb, 1, cap, dm), lambda e, c: (c, e, 0, 0)),
        out_shape=jax.ShapeDtypeStruct(xs.shape, BF16),
        scratch_shapes=[pltpu.VMEM((dm, df), BF16), pltpu.VMEM((dm, df), BF16), pltpu.VMEM((df, dm), BF16)],
        compiler_params=_params("arbitrary", "arbitrary"),
        name="moe_ffn",
    )(xs, gate, modsel, wg, wu, wd)


def _moe_scatter_kernel(start_ref, dense_ref, posc_ref, y_ref, x_ref, *rest, cap, w):
    b = pl.program_id(0)
    k = pl.program_id(1)
    nk = pl.num_programs(1)
    ne = y_ref.shape[1]
    tb = posc_ref.shape[1]
    dense = dense_ref[b * nk + k]
    o_ref = rest[-1]

    def emit(update):
        total = x_ref[0] + update
        if len(rest) == 2:
            total = total * lax.rsqrt(jnp.mean(total * total, axis=-1, keepdims=True) + NORM_EPS) * rest[0][...]
        o_ref[0] = total

    @pl.when(dense != 0)
    def _():
        slot = lax.broadcasted_iota(jnp.int32, (tb, cap), 1)
        pick = jnp.concatenate([jnp.where(slot == posc_ref[0, :, i:i + 1], 1.0, 0.0).astype(BF16)
                                for i in range(ne)], axis=1)
        y = y_ref[0].reshape(ne * cap, y_ref.shape[3])
        emit(jnp.dot(pick, y, preferred_element_type=F32))

    @pl.when(dense == 0)
    def _():
        per = 128 // w
        lane = lax.broadcasted_iota(jnp.int32, (tb, 128), 1)
        tiles = []
        wins = []
        for j in range(ne // per):
            col = None
            for u in reversed(range(per)):
                i = j * per + u
                s = pl.multiple_of(start_ref[(b * ne + i) * nk + k], 16)
                rel = posc_ref[0, :, i:i + 1] - s
                target = jnp.where(jnp.logical_and(rel >= 0, rel < w), rel + u * w, -1)
                col = target if col is None else jnp.where(lane < (u + 1) * w, target, col)
            tiles.append(jnp.where(lane == col, 1.0, 0.0).astype(BF16))
            for u in range(per):
                i = j * per + u
                s = pl.multiple_of(start_ref[(b * ne + i) * nk + k], 16)
                wins.append(y_ref[0, i, pl.ds(s, w), :])
        pick = jnp.concatenate(tiles, axis=1)
        emit(jnp.dot(pick, jnp.concatenate(wins, axis=0), preferred_element_type=F32))


def _moe_scatter(start, dense, pos, y, xcat, *, bb, row0, tb, w, final_gain=None):
    nb, ne, cap, dm = y.shape
    n = pos.shape[3]
    assert 128 % w == 0 and ne % (128 // w) == 0
    posc = jnp.transpose(pos.reshape(nb, ne, n), (0, 2, 1))
    x_index = (lambda b, k, *_: (b, k, 0)) if bb == 1 else (lambda b, k, *_: (k, row0 // tb, 0))
    in_specs = [pl.BlockSpec((1, tb, ne), lambda b, k, *_: (b, k, 0)),
                pl.BlockSpec((1, ne, cap, dm), lambda b, k, *_: (b, 0, 0, 0)),
                pl.BlockSpec((1, tb, dm), x_index)]
    operands = (start, dense, posc, y, xcat)
    if final_gain is None:
        out_shape, aliases = jax.ShapeDtypeStruct(xcat.shape, F32), {4: 0}
    else:
        assert bb == 1
        in_specs.append(pl.BlockSpec((1, dm), lambda b, k, *_: (0, 0)))
        operands += (final_gain,)
        out_shape, aliases = jax.ShapeDtypeStruct((nb, n, dm), F32), {}
    grid_spec = pltpu.PrefetchScalarGridSpec(
        num_scalar_prefetch=2,
        grid=(nb, n // tb),
        in_specs=in_specs,
        out_specs=pl.BlockSpec((1, tb, dm), x_index))
    return pl.pallas_call(
        functools.partial(_moe_scatter_kernel, cap=cap, w=w),
        grid_spec=grid_spec,
        out_shape=out_shape,
        input_output_aliases=aliases,
        compiler_params=_params("parallel", "arbitrary"),
        name="moe_scatter",
    )(*operands)


def _ec_moe(xcat, modsel, g2, wr, upper, wg, wu, wd, layer, *, rt, seq, ctx, with_ctx, final_gain=None):
    b = xcat.shape[0]
    ne = N_EXPERTS
    h, aff, pos = _router(xcat, modsel, g2, wr, upper, rt=rt, n=seq, row_tile0=0, sel=0)
    cap = max(1, EC_FACTOR * seq // N_EXPERTS)
    pos = pos.reshape(b, ne, 1, seq)
    tb = min(MOE_TOKEN_BLOCK, ctx)
    win = dict(tb=tb, w=min(MOE_WINDOW, cap))
    start, dense_group, dense_block = _moe_plan(pos, cap=cap, **win)
    xs, gate = _moe_gather(start, dense_group, pos, aff.reshape(b, ne, 1, seq), h, cap=cap, **win)
    y = _moe_ffn(xs, gate, modsel, wg, wu, wd, layer, sel=0)
    xcat = _moe_scatter(start, dense_block, pos, y, xcat, bb=1, row0=0, final_gain=final_gain, **win)
    if with_ctx:
        hc, affc, posc = _router(xcat, modsel, g2, wr, upper, rt=rt, n=ctx, row_tile0=seq // rt, sel=1)
        capc = max(1, EC_FACTOR * ctx // N_EXPERTS)
        offs = (jnp.arange(b, dtype=jnp.int32) * capc)[:, None, None]
        posc = jnp.where(posc >= 0, posc + offs, -1)
        posc = jnp.transpose(posc, (1, 0, 2)).reshape(1, ne, 1, b * ctx)
        affc = jnp.transpose(affc, (1, 0, 2)).reshape(1, ne, 1, b * ctx)
        win = dict(tb=tb, w=min(MOE_WINDOW, b * capc))
        assert tb == ctx
        start, dense_group, dense_block = _moe_plan(posc, cap=b * capc, **win)
        xs, gate = _moe_gather(start, dense_group, posc, affc, hc.reshape(1, b * ctx, -1), cap=b * capc, **win)
        y = _moe_ffn(xs, gate, modsel, wg, wu, wd, layer, sel=1)
        xcat = _moe_scatter(start, dense_block, posc, y, xcat, bb=b, row0=seq, **win)
    return xcat


def _rope_tables(seq, ctx):
    rows = seq // GRID_W
    row_ids = jnp.repeat(jnp.arange(rows, dtype=F32), GRID_W)
    col_ids = jnp.tile(jnp.arange(GRID_W, dtype=F32), rows)
    half = A_HEAD_DIM // 2
    inv = ROPE_THETA ** (-jnp.arange(0, half, 2, dtype=F32) / half)
    ang_r = row_ids[:, None] * inv
    ang_c = col_ids[:, None] * inv
    cos = jnp.concatenate([jnp.cos(ang_r)] * 2 + [jnp.cos(ang_c)] * 2, axis=1)
    sin = jnp.concatenate([-jnp.sin(ang_r), jnp.sin(ang_r), -jnp.sin(ang_c), jnp.sin(ang_c)], axis=1)
    cos = jnp.tile(cos, (1, A_HEADS))
    sin = jnp.tile(sin, (1, A_HEADS))
    cos = jnp.concatenate([cos, jnp.ones((ctx, A_Q), F32)], axis=0)
    sin = jnp.concatenate([sin, jnp.zeros((ctx, A_Q), F32)], axis=0)
    return cos, sin


def kernel(x, c, ctx, c_ctx, w_mod, b_mod, g_norm1, g_norm2, w_in_even, w_out_even, g_qnorm, g_knorm, w_conv,
           b_gate, g_hnorm, w_in_odd, w_pool_grp, s_pool, w_router, w_exp_gate, w_exp_up, w_exp_down, g_final):
    b, seq, dm = x.shape
    n_ctx = ctx.shape[1]
    depth = w_mod.shape[0]
    rt = 256 if (seq % 256 == 0 and n_ctx % 256 == 0) else 128
    assert seq % rt == 0 and n_ctx % rt == 0 and seq % n_ctx == 0 and seq % GRID_W == 0
    nlt = seq // rt
    nt = (seq + n_ctx) // rt

    xcat = jnp.concatenate([x, ctx, jnp.zeros((b, seq - n_ctx, dm), F32)], axis=1)

    rows = -(-(b + 1) // 8) * 8
    cond = jnp.zeros((rows, dm), F32).at[:b].set(c).at[b].set(c_ctx)
    mods = _adaln(cond, w_mod, b_mod)
    lat = mods[:, :b].reshape(depth, b, 1, 6, dm)
    cx = jnp.broadcast_to(mods[:, b].reshape(depth, 1, 1, 6, dm), (depth, b, 1, 6, dm))
    modsel = jnp.concatenate([lat, cx], axis=2)

    cos, sin = _rope_tables(seq, n_ctx)
    lane = jnp.arange(A_Q)
    gmat = (lane[:, None] // A_HEAD_DIM == lane[None, :] // A_HEAD_DIM).astype(BF16)
    blk = min(256, n_ctx)
    tri = jnp.arange(blk)
    upper = (tri[:, None] <= tri[None, :]).astype(BF16)

    wg, wu, wd = w_exp_gate, w_exp_up, w_exp_down
    wr = jnp.pad(w_router, ((0, 0), (0, 0), (0, GATE_PAD - N_EXPERTS)))

    for i in range(depth):
        ctx_next = any(j % 2 == 0 for j in range(i + 1, depth))
        n_upd = nt if ctx_next else nlt
        ms = modsel[i]
        if i % 2 == 0:
            e = i // 2
            w_in = w_in_even[e]
            pad = jnp.zeros((dm, GATE_PAD - N_GATES), F32)
            w_in = jnp.concatenate([w_in, pad], axis=1).astype(BF16)
            bg = jnp.concatenate([b_gate[e], jnp.zeros((GATE_PAD - N_GATES,), F32)])[None]
            q, k, v, qk, vbt, ob, gt, gc, gr = _inproj(
                xcat, ms, g_norm1[i][None], w_in, cos, sin,
                jnp.tile(g_qnorm[e], A_HEADS)[None], jnp.tile(g_knorm[e], A_KV_HEADS)[None], gmat, bg,
                rt=rt, ta=seq + n_ctx, n_lat_tiles=nlt)
            oa = _attention(q, k, v, rt=rt, n_tiles=n_upd, n_lat_tiles=nlt, seq=seq)
            qct = _short_conv(qk, w_conv[e], seq=seq, keys=False)
            kc = _short_conv(qk, w_conv[e], seq=seq, keys=True)
            gates_row = jnp.swapaxes(jnp.concatenate([gt[:, :, :N_GATES], gc[:, :, :N_GATES]], axis=2), 1, 2)
            hf, hb = _mlstm(qct, kc, vbt, gates_row, gr, seq=seq, blk=rt)
            xcat = _outproj(xcat, ms, oa, hf, hb, ob, g_hnorm[e][None], w_out_even[e].astype(BF16),
                            rt=rt, n_tiles=n_upd, n_lat_tiles=nlt)
        else:
            o = i // 2
            w_in = w_in_odd[o].astype(BF16)
            wgrp = w_pool_grp[o].astype(BF16)
            u = _mod_matmul(xcat, ms, g_norm1[i][None], w_in, rt=rt, tile0=0, n_tiles=nlt, sel=0)
            if ctx_next:
                uc = _mod_matmul(xcat, ms, g_norm1[i][None], w_in, rt=rt, tile0=nlt, n_tiles=nt - nlt, sel=1)
            xcat = _pool(u, xcat, ms, wgrp, s_pool[o][None], n=seq, row_block=0, sel=0)
            if ctx_next:
                xcat = _pool(uc, xcat, ms, wgrp, s_pool[o][None], n=n_ctx, row_block=seq // n_ctx, sel=1)
        xcat = _ec_moe(xcat, ms, g_norm2[i][None], wr[i], upper, wg, wu, wd, i,
                       rt=rt, seq=seq, ctx=n_ctx, with_ctx=ctx_next,
                       final_gain=g_final[None] if i == depth - 1 else None)
    return xcat
```

```python
import functools

import jax
import jax.numpy as jnp
from jax import lax
from jax.experimental import pallas as pl
from jax.experimental.pallas import tpu as pltpu

D_MODEL = 1024
GRID_W = 64
A_HEADS = 8
A_KV_HEADS = 2
A_HEAD_DIM = 64
ROPE_THETA = 10000.0
B_HEADS = 4
B_HEAD_DIM = 128
B_CHUNK = 128
POOL_WINDOWS = (2, 4, 8, 16)
POOL_GROUP = D_MODEL // 4
N_EXPERTS = 16
EC_FACTOR = 2
NORM_EPS = 1e-6

A_Q = A_HEADS * A_HEAD_DIM
A_KV = A_KV_HEADS * A_HEAD_DIM
B_W = B_HEADS * B_HEAD_DIM
N_GATES = 2 * 2 * B_HEADS
GATE_PAD = 128
C_QA, C_KA, C_VA = 0, A_Q, A_Q + A_KV
C_QK = A_Q + 2 * A_KV
C_VB = C_QK + 2 * B_W
C_OB = C_VB + B_W
C_GT = C_OB + B_W
EVEN_COLS = C_GT + GATE_PAD
POOL_HALO = 16

F32 = jnp.float32
BF16 = jnp.bfloat16
HIGHEST = lax.Precision.HIGHEST
NT_DIMS = (((1,), (1,)), ((), ()))
TN_DIMS = (((0,), (0,)), ((), ()))
VMEM_LIMIT = 56 * 1024 * 1024


def _params(*sem):
    return pltpu.CompilerParams(dimension_semantics=sem, vmem_limit_bytes=VMEM_LIMIT)


def _modulate(x, g, shift, scale):
    y = x * lax.rsqrt(jnp.mean(x * x, axis=-1, keepdims=True) + NORM_EPS)
    return (y * g) * (1.0 + scale) + shift


def _silu(x):
    return x * jax.nn.sigmoid(x)


def _adaln_kernel(c_ref, w_ref, b_ref, o_ref):
    s = _silu(c_ref[...])
    o_ref[0] = jnp.dot(s, w_ref[0], precision=HIGHEST, preferred_element_type=F32) + b_ref[0]


def _adaln(cond, w_mod, b_mod):
    depth, dm, n6 = w_mod.shape
    rows = cond.shape[0]
    tn = 1536
    return pl.pallas_call(
        _adaln_kernel,
        grid=(depth, n6 // tn),
        in_specs=[pl.BlockSpec((rows, dm), lambda l, j: (0, 0)),
                  pl.BlockSpec((1, dm, tn), lambda l, j: (l, 0, j)),
                  pl.BlockSpec((1, 1, tn), lambda l, j: (l, 0, j))],
        out_specs=pl.BlockSpec((1, rows, tn), lambda l, j: (l, 0, j)),
        out_shape=jax.ShapeDtypeStruct((depth, rows, n6), F32),
        compiler_params=_params("arbitrary", "arbitrary"),
        name="adaln",
    )(cond, w_mod, b_mod.reshape(depth, 1, n6))


def _group_mean_sq(x, gmat, width):
    s = jnp.dot((x * x).astype(BF16), gmat, preferred_element_type=F32)
    return s * (1.0 / width)


def _dot_exact_lhs(mat, x):
    x1 = x.astype(BF16)
    r1 = x - x1.astype(F32)
    x2 = r1.astype(BF16)
    x3 = (r1 - x2.astype(F32)).astype(BF16)
    return (jnp.dot(mat, x1, preferred_element_type=F32) + jnp.dot(mat, x2, preferred_element_type=F32)
            + jnp.dot(mat, x3, preferred_element_type=F32))


def _rope(x, cos, sin):
    w = x.shape[1]
    lane = lax.broadcasted_iota(jnp.int32, x.shape, 1)
    first = (lane % 32) < 16
    partner = jnp.where(first, pltpu.roll(x, w - 16, 1), pltpu.roll(x, 16, 1))
    return x * cos + partner * sin


def _inproj_kernel(x_ref, m_ref, g1_ref, w_ref, cos_ref, sin_ref, gq_ref, gk_ref, gm_ref, bg_ref, lp_ref, ls_ref,
                   q_ref, k_ref, v_ref, qk_ref, vb_ref, ob_ref, gt_ref, gc_ref, gr_ref):
    m = m_ref[0, 0]
    h = _modulate(x_ref[0], g1_ref[...], m[0:1], m[1:2]).astype(BF16)

    def mm(lo, hi):
        return jnp.dot(h, w_ref[:, lo:hi], preferred_element_type=F32)

    cos = cos_ref[...]
    sin = sin_ref[...]
    gm = gm_ref[...]
    qa = mm(C_QA, C_KA)
    qa = qa * lax.rsqrt(_group_mean_sq(qa, gm, A_HEAD_DIM) + NORM_EPS) * gq_ref[...]
    q_ref[0] = (_rope(qa, cos, sin) * (A_HEAD_DIM ** -0.5)).astype(BF16)
    ka = mm(C_KA, C_VA)
    ka = ka * lax.rsqrt(_group_mean_sq(ka, gm[:A_KV, :A_KV], A_HEAD_DIM) + NORM_EPS) * gk_ref[...]
    k_ref[0] = _rope(ka, cos[:, :A_KV], sin[:, :A_KV]).astype(BF16)
    va = mm(C_VA, C_QK).astype(BF16)
    ones = jnp.ones((va.shape[0], A_HEAD_DIM), BF16)
    v_ref[0] = jnp.concatenate(
        [piece for g in range(A_KV_HEADS) for piece in (va[:, g * A_HEAD_DIM:(g + 1) * A_HEAD_DIM], ones)], axis=1)
    qk_ref[0] = mm(C_QK, C_VB)
    vb_ref[0] = mm(C_VB, C_OB).T.astype(BF16)
    ob_ref[0] = mm(C_OB, C_GT)
    gt = mm(C_GT, EVEN_COLS) + bg_ref[...]
    lane = lax.broadcasted_iota(jnp.int32, gt.shape, 1)
    log_sig = jnp.minimum(gt, 0.0) - jnp.log1p(jnp.exp(-jnp.abs(gt)))
    gt = jnp.where((lane % (2 * B_HEADS)) >= B_HEADS, log_sig, gt)
    gt_ref[0] = gt
    prefix = _dot_exact_lhs(lp_ref[...], gt)
    suffix = _dot_exact_lhs(ls_ref[...], gt)
    cum = jnp.where((lane % (4 * B_HEADS)) < 2 * B_HEADS, prefix, suffix)
    gc_ref[0] = cum
    gr_ref[0] = gt - pltpu.roll(cum, GATE_PAD - B_HEADS, 1)


def _inproj(xcat, modsel, g1, w, cos, sin, gq, gk, gmat, bgate, *, rt, ta, n_lat_tiles):
    b, _, dm = xcat.shape
    nt = ta // rt
    row = lambda bb, r: (bb, r, 0)
    full = lambda bb, r: (0, 0)
    out_widths = (A_Q, A_KV, 2 * A_KV, 2 * B_W, B_W, B_W, GATE_PAD, GATE_PAD, GATE_PAD)
    out_dtypes = (BF16, BF16, BF16, F32, BF16, F32, F32, F32, F32)
    out_specs = [pl.BlockSpec((1, rt, wd), row) for wd in out_widths]
    out_shapes = [jax.ShapeDtypeStruct((b, ta, wd), dt) for wd, dt in zip(out_widths, out_dtypes)]
    vb_slot = 4
    out_specs[vb_slot] = pl.BlockSpec((1, B_W, rt), lambda bb, r: (bb, 0, r))
    out_shapes[vb_slot] = jax.ShapeDtypeStruct((b, B_W, ta), BF16)
    idx = jnp.arange(rt)
    same_chunk = idx[:, None] // B_CHUNK == idx[None, :] // B_CHUNK
    lower = jnp.logical_and(same_chunk, idx[None, :] <= idx[:, None]).astype(BF16)
    upper = jnp.logical_and(same_chunk, idx[None, :] >= idx[:, None]).astype(BF16)
    return pl.pallas_call(
        _inproj_kernel,
        grid=(b, nt),
        in_specs=[pl.BlockSpec((1, rt, dm), row),
                  pl.BlockSpec((1, 1, 6, dm), lambda bb, r: (bb, (r >= n_lat_tiles).astype(jnp.int32), 0, 0)),
                  pl.BlockSpec((1, dm), full),
                  pl.BlockSpec((dm, EVEN_COLS), full),
                  pl.BlockSpec((rt, A_Q), lambda bb, r: (r, 0)),
                  pl.BlockSpec((rt, A_Q), lambda bb, r: (r, 0)),
                  pl.BlockSpec((1, A_Q), full),
                  pl.BlockSpec((1, A_KV), full),
                  pl.BlockSpec((A_Q, A_Q), full),
                  pl.BlockSpec((1, GATE_PAD), full),
                  pl.BlockSpec((rt, rt), full),
                  pl.BlockSpec((rt, rt), full)],
        out_specs=out_specs,
        out_shape=out_shapes,
        compiler_params=_params("parallel", "arbitrary"),
        name="even_inproj",
    )(xcat, modsel, g1, w, cos, sin, gq, gk, gmat, bgate, lower, upper)


def _attn_kernel(q_ref, k_ref, v_ref, o_ref, *, n_lat_tiles, n_tiles, seq):
    group = A_HEADS // A_KV_HEADS

    def attend(key_lo):
        q = q_ref[0]
        for hd in range(A_HEADS):
            g = hd // group
            qh = q[:, hd * A_HEAD_DIM:(hd + 1) * A_HEAD_DIM]
            kg = k_ref[0, key_lo:, g * A_HEAD_DIM:(g + 1) * A_HEAD_DIM]
            vg = v_ref[0, key_lo:, 2 * g * A_HEAD_DIM:2 * (g + 1) * A_HEAD_DIM]
            s = lax.dot_general(qh, kg, NT_DIMS, preferred_element_type=F32)
            p = jnp.exp(s - jnp.max(s, axis=-1, keepdims=True))
            o = jnp.dot(p.astype(BF16), vg, preferred_element_type=F32)
            o = o[:, :A_HEAD_DIM] / o[:, A_HEAD_DIM:]
            o_ref[0, :, hd * A_HEAD_DIM:(hd + 1) * A_HEAD_DIM] = o.astype(BF16)

    if n_tiles == n_lat_tiles:
        attend(0)
    else:
        r = pl.program_id(1)
        pl.when(r < n_lat_tiles)(lambda: attend(0))
        pl.when(r >= n_lat_tiles)(lambda: attend(seq))


def _attention(q, k, v, *, rt, n_tiles, n_lat_tiles, seq):
    b, ta, _ = q.shape
    return pl.pallas_call(
        functools.partial(_attn_kernel, n_lat_tiles=n_lat_tiles, n_tiles=n_tiles, seq=seq),
        grid=(b, n_tiles),
        in_specs=[pl.BlockSpec((1, rt, A_Q), lambda bb, r: (bb, r, 0)),
                  pl.BlockSpec((1, ta, A_KV), lambda bb, r: (bb, 0, 0)),
                  pl.BlockSpec((1, ta, 2 * A_KV), lambda bb, r: (bb, 0, 0))],
        out_specs=pl.BlockSpec((1, rt, A_Q), lambda bb, r: (bb, r, 0)),
        out_shape=jax.ShapeDtypeStruct((b, n_tiles * rt, A_Q), BF16),
        compiler_params=_params("parallel", "arbitrary"),
        name="gqa_attention",
    )(q, k, v)


def _conv_kernel(x_ref, w_ref, o_ref, *, seq, keys):
    x = x_ref[0]
    ta = x.shape[0]
    w = w_ref[...]
    row = lax.broadcasted_iota(jnp.int32, (ta, 1), 0)
    prev = jnp.where(jnp.logical_or(row == 0, row == seq), 0.0, pltpu.roll(x, 1, 0))
    nxt = jnp.where(jnp.logical_or(row == seq - 1, row == ta - 1), 0.0, pltpu.roll(x, ta - 1, 0))
    y = _silu(prev * w[0:1] + x * w[1:2] + nxt * w[2:3])
    if keys:
        o_ref[0] = (y * (B_HEAD_DIM ** -0.5)).astype(BF16)
    else:
        o_ref[0] = y.T.astype(BF16)


def _short_conv(qk, w_conv, *, seq, keys):
    b, ta, _ = qk.shape
    ct = 256
    c0 = B_W // ct if keys else 0
    if keys:
        out_spec = pl.BlockSpec((1, ta, ct), lambda bb, j: (bb, 0, j))
        out_shape = jax.ShapeDtypeStruct((b, ta, B_W), BF16)
    else:
        out_spec = pl.BlockSpec((1, ct, ta), lambda bb, j: (bb, j, 0))
        out_shape = jax.ShapeDtypeStruct((b, B_W, ta), BF16)
    return pl.pallas_call(
        functools.partial(_conv_kernel, seq=seq, keys=keys),
        grid=(b, B_W // ct),
        in_specs=[pl.BlockSpec((1, ta, ct), lambda bb, j: (bb, 0, c0 + j)),
                  pl.BlockSpec((w_conv.shape[0], ct), lambda bb, j: (0, c0 + j))],
        out_specs=out_spec,
        out_shape=out_shape,
        compiler_params=_params("parallel", "arbitrary"),
        name="mlstm_conv_k" if keys else "mlstm_conv_q",
    )(qk, w_conv)


def _mlstm_chain(qt, k, vat, ig, b_row, r_col, total, ca, m_old, tri):
    log_d = jnp.where(tri, b_row + r_col, -jnp.inf)
    log_inter = m_old + b_row
    m_t = jnp.maximum(log_inter, jnp.max(log_d, axis=0, keepdims=True))
    w_intra = jnp.exp(log_d - m_t) * jnp.dot(k, qt, preferred_element_type=F32)
    w_inter = jnp.exp(log_inter - m_t)
    nd = (w_inter * jnp.dot(ca.astype(BF16), qt, preferred_element_type=F32)
          + jnp.dot(vat, w_intra.astype(BF16), preferred_element_type=F32))
    dv = nd.shape[0] // 2
    h = nd[:dv] / jnp.maximum(jnp.abs(nd[dv:]), jnp.exp(-m_t))
    log_w = total - b_row + ig
    m_new = jnp.maximum(m_old + total, jnp.max(log_w, axis=-1, keepdims=True))
    w_s = jnp.exp(log_w - m_new)
    decay = jnp.exp(m_old + total - m_new)
    ca_new = decay * ca + jnp.dot((vat * w_s).astype(BF16), k, preferred_element_type=F32)
    return h, ca_new, m_new


def _mlstm_kernel(qf_ref, kf_ref, vf_ref, grf_ref, gcf_ref, qb_ref, kb_ref, vb_ref, grb_ref, gcb_ref,
                  hf_ref, hb_ref, c_ref, m_ref):
    @pl.when(pl.program_id(1) == 0)
    def _():
        c_ref[...] = jnp.zeros(c_ref.shape, F32)
        m_ref[...] = jnp.full(m_ref.shape, -1e30, F32)

    t = B_CHUNK
    n_chunks = kf_ref.shape[1] // t
    src = lax.broadcasted_iota(jnp.int32, (t, t), 0)
    tgt = lax.broadcasted_iota(jnp.int32, (t, t), 1)
    ones = jnp.ones((B_HEAD_DIM, t), BF16)
    dirs = ((qf_ref, kf_ref, vf_ref, grf_ref, gcf_ref, hf_ref, src <= tgt, t - 1, range(n_chunks)),
            (qb_ref, kb_ref, vb_ref, grb_ref, gcb_ref, hb_ref, src >= tgt, 0, range(n_chunks - 1, -1, -1)))
    outputs = []
    states = []
    for d, (q_ref, k_ref, v_ref, gr_ref, gc_ref, h_ref, tri, last, order) in enumerate(dirs):
        for hd in range(B_HEADS):
            sl = slice(hd * B_HEAD_DIM, (hd + 1) * B_HEAD_DIM)
            gi = d * 2 * B_HEADS + hd
            fi = gi + B_HEADS
            si = d * B_HEADS + hd
            ca = c_ref[si]
            m = m_ref[si][:, 0:1]
            for ci in order:
                rows = slice(ci * t, (ci + 1) * t)
                ig = gr_ref[0, gi:gi + 1, rows]
                b_row = gr_ref[0, N_GATES + fi:N_GATES + fi + 1, rows]
                r_col = gc_ref[0, rows, gi:gi + 1]
                vat = jnp.concatenate([v_ref[0, sl, rows], ones], axis=0)
                h, ca, m = _mlstm_chain(q_ref[0, sl, rows], k_ref[0, rows, sl], vat, ig, b_row, r_col,
                                        b_row[:, last:last + 1], ca, m, tri)
                outputs.append((h_ref, rows, sl, h))
            states.append((si, ca, m))
    for h_ref, rows, sl, h in outputs:
        h_ref[0, sl, rows] = h
    for si, ca, m in states:
        c_ref[si] = ca
        m_ref[si] = jnp.broadcast_to(m, (1, 128))


def _mlstm(qct, kc, vbt, gates_row, gates_col, *, seq, blk):
    b, ta, _ = kc.shape
    nc = ta // blk
    ncl = seq // blk
    ncc = nc - ncl
    fwd = lambda j: jnp.where(j < ncc, ncl + j, j - ncc)
    bwd = lambda j: jnp.where(j < ncc, nc - 1 - j, ncl - 1 - (j - ncc))

    def specs(order):
        feature_major = pl.BlockSpec((1, B_W, blk), lambda bb, j: (bb, 0, order(j)))
        return [feature_major,
                pl.BlockSpec((1, blk, B_W), lambda bb, j: (bb, order(j), 0)),
                feature_major,
                pl.BlockSpec((1, 2 * N_GATES, blk), lambda bb, j: (bb, 0, order(j))),
                pl.BlockSpec((1, blk, GATE_PAD), lambda bb, j: (bb, order(j), 0))]

    n_state = 2 * B_HEADS
    ops = (qct, kc, vbt, gates_row, gates_col)
    return pl.pallas_call(
        _mlstm_kernel,
        grid=(b, nc),
        in_specs=specs(fwd) + specs(bwd),
        out_specs=[pl.BlockSpec((1, B_W, blk), lambda bb, j: (bb, 0, fwd(j))),
                   pl.BlockSpec((1, B_W, blk), lambda bb, j: (bb, 0, bwd(j)))],
        out_shape=[jax.ShapeDtypeStruct((b, B_W, ta), F32)] * 2,
        scratch_shapes=[pltpu.VMEM((n_state, 2 * B_HEAD_DIM, B_HEAD_DIM), F32),
                        pltpu.VMEM((n_state, 1, 128), F32)],
        compiler_params=_params("parallel", "arbitrary"),
        name="mlstm_scan",
    )(*ops, *ops)


def _outproj_kernel(x_ref, m_ref, oa_ref, hf_ref, hb_ref, ob_ref, ghn_ref, w_ref, o_ref):
    hm = (hf_ref[0] + hb_ref[0]).T
    ob = ob_ref[0]
    ghn = ghn_ref[...]
    y = jnp.dot(oa_ref[0], w_ref[0:A_Q, :], preferred_element_type=F32)
    for hd in range(B_HEADS):
        sl = slice(hd * B_HEAD_DIM, (hd + 1) * B_HEAD_DIM)
        z = hm[:, sl]
        z = z * lax.rsqrt(jnp.mean(z * z, axis=-1, keepdims=True) + NORM_EPS) * ghn[:, sl]
        z = (z * jax.nn.sigmoid(ob[:, sl])).astype(BF16)
        y = y + jnp.dot(z, w_ref[A_Q + hd * B_HEAD_DIM:A_Q + (hd + 1) * B_HEAD_DIM, :],
                        preferred_element_type=F32)
    o_ref[0] = x_ref[0] + m_ref[0, 0][2:3] * y


def _outproj(xcat, modsel, oa, hf, hb, ob, ghn, w, *, rt, n_tiles, n_lat_tiles):
    b, ta, dm = xcat.shape
    row = lambda bb, r: (bb, r, 0)
    full = lambda bb, r: (0, 0)
    return pl.pallas_call(
        _outproj_kernel,
        grid=(b, n_tiles),
        in_specs=[pl.BlockSpec((1, rt, dm), row),
                  pl.BlockSpec((1, 1, 6, dm), lambda bb, r: (bb, (r >= n_lat_tiles).astype(jnp.int32), 0, 0)),
                  pl.BlockSpec((1, rt, A_Q), row),
                  pl.BlockSpec((1, B_W, rt), lambda bb, r: (bb, 0, r)),
                  pl.BlockSpec((1, B_W, rt), lambda bb, r: (bb, 0, r)),
                  pl.BlockSpec((1, rt, B_W), row),
                  pl.BlockSpec((1, B_W), full),
                  pl.BlockSpec((A_Q + B_W, dm), full)],
        out_specs=pl.BlockSpec((1, rt, dm), row),
        out_shape=jax.ShapeDtypeStruct(xcat.shape, F32),
        input_output_aliases={0: 0},
        compiler_params=_params("parallel", "arbitrary"),
        name="even_outproj",
    )(xcat, modsel, oa, hf, hb, ob, ghn, w)


def _modmm_kernel(x_ref, m_ref, g_ref, w_ref, o_ref):
    m = m_ref[0, 0]
    h = _modulate(x_ref[0], g_ref[...], m[0:1], m[1:2]).astype(BF16)
    o_ref[0] = jnp.dot(h, w_ref[...], preferred_element_type=F32)


def _mod_matmul(xcat, modsel, g, w, *, rt, tile0, n_tiles, sel):
    b, _, dm = xcat.shape
    n_out = w.shape[1]
    return pl.pallas_call(
        _modmm_kernel,
        grid=(b, n_tiles),
        in_specs=[pl.BlockSpec((1, rt, dm), lambda bb, r: (bb, tile0 + r, 0)),
                  pl.BlockSpec((1, 1, 6, dm), lambda bb, r: (bb, sel, 0, 0)),
                  pl.BlockSpec((1, dm), lambda bb, r: (0, 0)),
                  pl.BlockSpec((dm, n_out), lambda bb, r: (0, 0))],
        out_specs=pl.BlockSpec((1, rt, n_out), lambda bb, r: (bb, r, 0)),
        out_shape=jax.ShapeDtypeStruct((b, n_tiles * rt, n_out), F32),
        compiler_params=_params("parallel", "arbitrary"),
        name="odd_inproj",
    )(xcat, modsel, g, w)


def _pool_kernel(u_ref, x_ref, m_ref, wg_ref, sp_ref, o_ref, pad_ref, sa_ref, sb_ref, *, n, rc):
    g = pl.program_id(1)
    halo = POOL_HALO
    ext = n + 2 * halo
    zeros = jnp.zeros((halo, pad_ref.shape[1]), F32)
    pad_ref[0:halo, :] = zeros
    pad_ref[halo + n:ext, :] = zeros
    pad_ref[halo:halo + n, :] = u_ref[0]
    for ref in (pad_ref, sa_ref, sb_ref):
        ref[ext:ext + halo, :] = zeros
    gate = m_ref[0, 0][2:3]
    chunks = [(r0, min(rc, ext - r0)) for r0 in range(0, ext, rc)]

    def pair_sums(src, dst, shift):
        for r0, size in chunks:
            dst[r0:r0 + size, :] = src[r0:r0 + size, :] + src[r0 + shift:r0 + shift + size, :]

    for gi, win in enumerate(POOL_WINDOWS):
        @pl.when(g == gi)
        def _(win=win):
            lo = win // 2
            hi = win - 1 - lo
            sums, width = pad_ref, 1
            for dst in (sa_ref, sb_ref, sa_ref, sb_ref):
                if width == win:
                    break
                pair_sums(sums, dst, width)
                sums, width = dst, 2 * width
            assert width == win
            for r0 in range(0, n, rc):
                base = halo + r0
                acc = sums[base - lo:base - lo + rc, :]
                t = r0 + lax.broadcasted_iota(jnp.int32, (rc, 1), 0)
                cnt = jnp.minimum(t + hi, n - 1) - jnp.maximum(t - lo, 0) + 1
                p = acc / cnt.astype(F32) - pad_ref[base:base + rc, :]
                y = jnp.dot(p.astype(BF16), wg_ref[0], preferred_element_type=F32) * sp_ref[...]
                o_ref[0, r0:r0 + rc, :] = x_ref[0, r0:r0 + rc, :] + gate * y


def _pool(u, xcat, modsel, wgrp, spool, *, n, row_block, sel):
    b, _, dm = xcat.shape
    pg = POOL_GROUP
    rc = min(256, n)
    blk = lambda bb, g: (bb, row_block, g)
    return pl.pallas_call(
        functools.partial(_pool_kernel, n=n, rc=rc),
        grid=(b, dm // pg),
        in_specs=[pl.BlockSpec((1, n, pg), lambda bb, g: (bb, 0, g)),
                  pl.BlockSpec((1, n, pg), blk),
                  pl.BlockSpec((1, 1, 6, pg), lambda bb, g: (bb, sel, 0, g)),
                  pl.BlockSpec((1, pg, pg), lambda bb, g: (g, 0, 0)),
                  pl.BlockSpec((1, pg), lambda bb, g: (0, g))],
        out_specs=pl.BlockSpec((1, n, pg), blk),
        out_shape=jax.ShapeDtypeStruct(xcat.shape, F32),
        scratch_shapes=[pltpu.VMEM((n + 3 * POOL_HALO, pg), F32)] * 3,
        input_output_aliases={1: 0},
        compiler_params=_params("parallel", "arbitrary"),
        name="pool_mixer",
    )(u, xcat, modsel, wgrp, spool)


def _lane_cumsum(x01, upper, blk):
    n = x01.shape[1]
    out = []
    carry = jnp.zeros((x01.shape[0], 1), F32)
    for j in range(0, n, blk):
        cs = jnp.dot(x01[:, j:j + blk].astype(BF16), upper, preferred_element_type=F32) + carry
        carry = cs[:, blk - 1:blk]
        out.append(cs)
    return jnp.concatenate(out, axis=1) if len(out) > 1 else out[0]


def _router_kernel(x_ref, m_ref, g_ref, wr_ref, h_ref, lg_ref):
    m = m_ref[0, 0]
    h = _modulate(x_ref[0], g_ref[...], m[3:4], m[4:5])
    h_hi = h.astype(BF16)
    h_ref[0] = h_hi
    h_lo = (h - h_hi.astype(F32)).astype(BF16)
    wr = wr_ref[...]
    w_hi = wr.astype(BF16)
    w_lo = (wr - w_hi.astype(F32)).astype(BF16)
    lg = (jnp.dot(h_hi, w_hi, preferred_element_type=F32)
          + (jnp.dot(h_hi, w_lo, preferred_element_type=F32) + jnp.dot(h_lo, w_hi, preferred_element_type=F32)))
    lg_ref[0] = lg.T[:lg_ref.shape[1]]


def _topk_kernel(lg_ref, up_ref, aff_ref, pos_ref, *, cap):
    lg = lg_ref[...]
    nb, ne, n = lg.shape
    e = jnp.exp(lg - jnp.max(lg, axis=1, keepdims=True))
    aff3 = e / jnp.sum(e, axis=1, keepdims=True)
    aff_ref[...] = aff3
    aff = aff3.reshape(nb * ne, n)

    def bit_step(i, thr):
        cand = thr | lax.shift_left(jnp.int32(1), 30 - i)
        cnt = jnp.sum(jnp.where(aff >= pltpu.bitcast(cand, F32), 1.0, 0.0), axis=-1, keepdims=True)
        return jnp.where(cnt >= cap, cand, thr)

    thr = lax.fori_loop(0, 31, bit_step, jnp.zeros((aff.shape[0], 1), jnp.int32))
    thr = pltpu.bitcast(thr, F32)
    above = aff > thr
    tied = aff == thr
    room = cap - jnp.sum(jnp.where(above, 1.0, 0.0), axis=-1, keepdims=True)
    upper = up_ref[...]
    blk = upper.shape[0]
    tie_rank = _lane_cumsum(jnp.where(tied, 1.0, 0.0), upper, blk)
    sel = jnp.logical_or(above, jnp.logical_and(tied, tie_rank <= room))
    slot = _lane_cumsum(jnp.where(sel, 1.0, 0.0), upper, blk) - 1.0
    pos_ref[...] = jnp.where(sel, slot, -1.0).astype(jnp.int32).reshape(nb, ne, n)


def _router(xcat, modsel, g2, wr, upper, *, rt, n, row_tile0, sel):
    b, ta, dm = xcat.shape
    n_tiles = n // rt
    cap = max(1, EC_FACTOR * n // N_EXPERTS)
    ne = N_EXPERTS
    h, logits = pl.pallas_call(
        _router_kernel,
        grid=(b, n_tiles),
        in_specs=[pl.BlockSpec((1, rt, dm), lambda bb, r: (bb, row_tile0 + r, 0)),
                  pl.BlockSpec((1, 1, 6, dm), lambda bb, r: (bb, sel, 0, 0)),
                  pl.BlockSpec((1, dm), lambda bb, r: (0, 0)),
                  pl.BlockSpec(wr.shape, lambda bb, r: (0, 0))],
        out_specs=[pl.BlockSpec((1, rt, dm), lambda bb, r: (bb, r, 0)),
                   pl.BlockSpec((1, ne, rt), lambda bb, r: (bb, 0, r))],
        out_shape=[jax.ShapeDtypeStruct((b, n, dm), BF16),
                   jax.ShapeDtypeStruct((b, ne, n), F32)],
        compiler_params=_params("parallel", "arbitrary"),
        name="moe_router",
    )(xcat, modsel, g2, wr)
    whole = lambda i: (0, 0, 0)
    aff, pos = pl.pallas_call(
        functools.partial(_topk_kernel, cap=cap),
        grid=(1,),
        in_specs=[pl.BlockSpec((b, ne, n), whole),
                  pl.BlockSpec(upper.shape, lambda i: (0, 0))],
        out_specs=[pl.BlockSpec((b, ne, n), whole),
                   pl.BlockSpec((b, ne, n), whole)],
        out_shape=[jax.ShapeDtypeStruct((b, ne, n), F32),
                   jax.ShapeDtypeStruct((b, ne, n), jnp.int32)],
        compiler_params=_params("arbitrary"),
        name="moe_topk",
    )(logits, upper)
    return h, aff, pos


MOE_EXPERT_BLOCK = 4
MOE_TOKEN_BLOCK = 256
MOE_WINDOW = 64


def _slot_onehot(pos_ref, cap):
    hits = []
    for i in range(pos_ref.shape[1]):
        pos = pos_ref[0, i]
        hits.append(lax.broadcasted_iota(jnp.int32, (cap, pos.shape[1]), 0) == pos)
    pick = jnp.concatenate([jnp.where(hit, 1.0, 0.0).astype(BF16) for hit in hits], axis=0)
    return pick, hits


def _moe_plan(pos, *, cap, tb, w):
    nb, ne, _, n = pos.shape
    nk = n // tb
    cnt = jnp.sum((pos.reshape(nb, ne, nk, tb) >= 0).astype(jnp.int32), axis=-1)
    first = jnp.cumsum(cnt, axis=-1) - cnt
    start = jnp.minimum((first // 16) * 16, cap - w)
    fits = first - start + cnt <= w
    dense_group = jnp.logical_not(jnp.all(fits.reshape(nb, ne // MOE_EXPERT_BLOCK, -1), axis=-1))
    dense_block = jnp.logical_not(jnp.all(fits, axis=1))
    flat = lambda a: a.reshape(-1).astype(jnp.int32)
    return flat(start), flat(dense_group), flat(dense_block)


def _moe_gather_kernel(start_ref, dense_ref, pos_ref, aff_ref, h_ref, xs_ref, gate_ref, *, cap, tb, w):
    b = pl.program_id(0)
    g = pl.program_id(1)
    eb, n = pos_ref.shape[1], pos_ref.shape[3]
    nk = n // tb
    dense = dense_ref[b * pl.num_programs(1) + g]

    @pl.when(dense != 0)
    def _():
        pick, hits = _slot_onehot(pos_ref, cap)
        xs = jnp.dot(pick, h_ref[0], preferred_element_type=F32).astype(BF16)
        xs_ref[0] = xs.reshape(xs_ref.shape[1:])
        for i, hit in enumerate(hits):
            gate = jnp.sum(jnp.where(hit, aff_ref[0, i], 0.0), axis=-1, keepdims=True)
            gate_ref[0, i] = jnp.broadcast_to(gate, gate_ref.shape[2:])

    @pl.when(dense == 0)
    def _():
        xs_ref[...] = jnp.zeros(xs_ref.shape, BF16)
        gate_ref[...] = jnp.zeros(gate_ref.shape, F32)
        slot = lax.broadcasted_iota(jnp.int32, (w, tb), 0)
        for k in range(nk):
            cols = slice(k * tb, (k + 1) * tb)
            starts = [pl.multiple_of(start_ref[((b * pl.num_programs(1) + g) * eb + i) * nk + k], 16)
                      for i in range(eb)]
            hits = [(slot + starts[i]) == pos_ref[0, i, :, cols] for i in range(eb)]
            pick = jnp.concatenate([jnp.where(hit, 1.0, 0.0).astype(BF16) for hit in hits], axis=0)
            part = jnp.dot(pick, h_ref[0, cols, :], preferred_element_type=F32).astype(BF16)
            for i in range(eb):
                rows = pl.ds(starts[i], w)
                xs_ref[0, i, rows, :] = xs_ref[0, i, rows, :] + part[i * w:(i + 1) * w]
                gate = jnp.sum(jnp.where(hits[i], aff_ref[0, i, :, cols], 0.0), axis=-1, keepdims=True)
                gate_ref[0, i, rows, :] = gate_ref[0, i, rows, :] + gate


def _moe_gather(start, dense, pos, aff, h, *, cap, tb, w):
    nb, ne, _, n = pos.shape
    dm = h.shape[2]
    eb = MOE_EXPERT_BLOCK
    grid_spec = pltpu.PrefetchScalarGridSpec(
        num_scalar_prefetch=2,
        grid=(nb, ne // eb),
        in_specs=[pl.BlockSpec((1, eb, 1, n), lambda b, e, *_: (b, e, 0, 0)),
                  pl.BlockSpec((1, eb, 1, n), lambda b, e, *_: (b, e, 0, 0)),
                  pl.BlockSpec((1, n, dm), lambda b, e, *_: (b, 0, 0), pipeline_mode=pl.Buffered(1))],
        out_specs=[pl.BlockSpec((1, eb, cap, dm), lambda b, e, *_: (b, e, 0, 0)),
                   pl.BlockSpec((1, eb, cap, 128), lambda b, e, *_: (b, e, 0, 0))])
    return pl.pallas_call(
        functools.partial(_moe_gather_kernel, cap=cap, tb=tb, w=w),
        grid_spec=grid_spec,
        out_shape=[jax.ShapeDtypeStruct((nb, ne, cap, dm), BF16),
                   jax.ShapeDtypeStruct((nb, ne, cap, 128), F32)],
        compiler_params=_params("parallel", "arbitrary"),
        name="moe_gather",
    )(start, dense, pos, aff, h)


def _moe_ffn_kernel(xs_ref, gate_ref, m_ref, wg_ref, wu_ref, wd_ref, y_ref, wgb_ref, wub_ref, wdb_ref):
    @pl.when(pl.program_id(1) == 0)
    def _():
        rows = 256
        for src, dst in ((wg_ref, wgb_ref), (wu_ref, wub_ref), (wd_ref, wdb_ref)):
            for r0 in range(0, dst.shape[0], rows):
                dst[r0:r0 + rows, :] = src[0, 0, r0:r0 + rows, :].astype(BF16)

    nb, _, cap, dm = xs_ref.shape
    xs = xs_ref[...].reshape(nb * cap, dm)
    a = jnp.dot(xs, wgb_ref[...], preferred_element_type=F32)
    u = jnp.dot(xs, wub_ref[...], preferred_element_type=F32)
    act = (_silu(a) * u).astype(BF16)
    out = jnp.dot(act, wdb_ref[...], preferred_element_type=F32)
    for i in range(nb):
        y = out[i * cap:(i + 1) * cap] * gate_ref[i, 0][:, 0:1]
        y_ref[i, 0] = (y * m_ref[i, 0][5:6]).astype(BF16)


def _moe_ffn(xs, gate, modsel, wg, wu, wd, layer, *, sel):
    nb_all, ne, cap, dm = xs.shape
    df = wg.shape[3]
    nb = 2 if nb_all % 2 == 0 else 1
    return pl.pallas_call(
        _moe_ffn_kernel,
        grid=(ne, nb_all // nb),
        in_specs=[pl.BlockSpec((nb, 1, cap, dm), lambda e, c: (c, e, 0, 0)),
                  pl.BlockSpec((nb, 1, cap, 128), lambda e, c: (c, e, 0, 0)),
                  pl.BlockSpec((nb, 1, 6, dm), lambda e, c: (c, sel, 0, 0)),
                  pl.BlockSpec((1, 1, dm, df), lambda e, c: (layer, e, 0, 0)),
                  pl.BlockSpec((1, 1, dm, df), lambda e, c: (layer, e, 0, 0)),
                  pl.BlockSpec((1, 1, df, dm), lambda e, c: (layer, e, 0, 0))],
        out_specs=pl.BlockSpec((nb, 1, cap, dm), lambda e, c: (c, e, 0, 0)),
        out_shape=jax.ShapeDtypeStruct(xs.shape, BF16),
        scratch_shapes=[pltpu.VMEM((dm, df), BF16), pltpu.VMEM((dm, df), BF16), pltpu.VMEM((df, dm), BF16)],
        compiler_params=_params("arbitrary", "arbitrary"),
        name="moe_ffn",
    )(xs, gate, modsel, wg, wu, wd)


def _moe_scatter_kernel(start_ref, dense_ref, posc_ref, y_ref, x_ref, *rest, cap, w):
    b = pl.program_id(0)
    k = pl.program_id(1)
    nk = pl.num_programs(1)
    ne = y_ref.shape[1]
    tb = posc_ref.shape[1]
    dense = dense_ref[b * nk + k]
    o_ref = rest[-1]

    def emit(update):
        total = x_ref[0] + update
        if len(rest) == 2:
            total = total * lax.rsqrt(jnp.mean(total * total, axis=-1, keepdims=True) + NORM_EPS) * rest[0][...]
        o_ref[0] = total

    @pl.when(dense != 0)
    def _():
        slot = lax.broadcasted_iota(jnp.int32, (tb, cap), 1)
        pick = jnp.concatenate([jnp.where(slot == posc_ref[0, :, i:i + 1], 1.0, 0.0).astype(BF16)
                                for i in range(ne)], axis=1)
        y = y_ref[0].reshape(ne * cap, y_ref.shape[3])
        emit(jnp.dot(pick, y, preferred_element_type=F32))

    @pl.when(dense == 0)
    def _():
        per = 128 // w
        lane = lax.broadcasted_iota(jnp.int32, (tb, 128), 1)
        tiles = []
        wins = []
        for j in range(ne // per):
            col = None
            for u in reversed(range(per)):
                i = j * per + u
                s = pl.multiple_of(start_ref[(b * ne + i) * nk + k], 16)
                rel = posc_ref[0, :, i:i + 1] - s
                target = jnp.where(jnp.logical_and(rel >= 0, rel < w), rel + u * w, -1)
                col = target if col is None else jnp.where(lane < (u + 1) * w, target, col)
            tiles.append(jnp.where(lane == col, 1.0, 0.0).astype(BF16))
            for u in range(per):
                i = j * per + u
                s = pl.multiple_of(start_ref[(b * ne + i) * nk + k], 16)
                wins.append(y_ref[0, i, pl.ds(s, w), :])
        pick = jnp.concatenate(tiles, axis=1)
        emit(jnp.dot(pick, jnp.concatenate(wins, axis=0), preferred_element_type=F32))


def _moe_scatter(start, dense, pos, y, xcat, *, bb, row0, tb, w, final_gain=None):
    nb, ne, cap, dm = y.shape
    n = pos.shape[3]
    assert 128 % w == 0 and ne % (128 // w) == 0
    posc = jnp.transpose(pos.reshape(nb, ne, n), (0, 2, 1))
    x_index = (lambda b, k, *_: (b, k, 0)) if bb == 1 else (lambda b, k, *_: (k, row0 // tb, 0))
    in_specs = [pl.BlockSpec((1, tb, ne), lambda b, k, *_: (b, k, 0)),
                pl.BlockSpec((1, ne, cap, dm), lambda b, k, *_: (b, 0, 0, 0)),
                pl.BlockSpec((1, tb, dm), x_index)]
    operands = (start, dense, posc, y, xcat)
    if final_gain is None:
        out_shape, aliases = jax.ShapeDtypeStruct(xcat.shape, F32), {4: 0}
    else:
        assert bb == 1
        in_specs.append(pl.BlockSpec((1, dm), lambda b, k, *_: (0, 0)))
        operands += (final_gain,)
        out_shape, aliases = jax.ShapeDtypeStruct((nb, n, dm), F32), {}
    grid_spec = pltpu.PrefetchScalarGridSpec(
        num_scalar_prefetch=2,
        grid=(nb, n // tb),
        in_specs=in_specs,
        out_specs=pl.BlockSpec((1, tb, dm), x_index))
    return pl.pallas_call(
        functools.partial(_moe_scatter_kernel, cap=cap, w=w),
        grid_spec=grid_spec,
        out_shape=out_shape,
        input_output_aliases=aliases,
        compiler_params=_params("parallel", "arbitrary"),
        name="moe_scatter",
    )(*operands)


def _ec_moe(xcat, modsel, g2, wr, upper, wg, wu, wd, layer, *, rt, seq, ctx, with_ctx, final_gain=None):
    b = xcat.shape[0]
    ne = N_EXPERTS
    h, aff, pos = _router(xcat, modsel, g2, wr, upper, rt=rt, n=seq, row_tile0=0, sel=0)
    cap = max(1, EC_FACTOR * seq // N_EXPERTS)
    pos = pos.reshape(b, ne, 1, seq)
    tb = min(MOE_TOKEN_BLOCK, ctx)
    win = dict(tb=tb, w=min(MOE_WINDOW, cap))
    start, dense_group, dense_block = _moe_plan(pos, cap=cap, **win)
    xs, gate = _moe_gather(start, dense_group, pos, aff.reshape(b, ne, 1, seq), h, cap=cap, **win)
    y = _moe_ffn(xs, gate, modsel, wg, wu, wd, layer, sel=0)
    xcat = _moe_scatter(start, dense_block, pos, y, xcat, bb=1, row0=0, final_gain=final_gain, **win)
    if with_ctx:
        hc, affc, posc = _router(xcat, modsel, g2, wr, upper, rt=rt, n=ctx, row_tile0=seq // rt, sel=1)
        capc = max(1, EC_FACTOR * ctx // N_EXPERTS)
        offs = (jnp.arange(b, dtype=jnp.int32) * capc)[:, None, None]
        posc = jnp.where(posc >= 0, posc + offs, -1)
        posc = jnp.transpose(posc, (1, 0, 2)).reshape(1, ne, 1, b * ctx)
        affc = jnp.transpose(affc, (1, 0, 2)).reshape(1, ne, 1, b * ctx)
        win = dict(tb=tb, w=min(MOE_WINDOW, b * capc))
        assert tb == ctx
        start, dense_group, dense_block = _moe_plan(posc, cap=b * capc, **win)
        xs, gate = _moe_gather(start, dense_group, posc, affc, hc.reshape(1, b * ctx, -1), cap=b * capc, **win)
        y = _moe_ffn(xs, gate, modsel, wg, wu, wd, layer, sel=1)
        xcat = _moe_scatter(start, dense_block, posc, y, xcat, bb=b, row0=seq, **win)
    return xcat


def _rope_tables(seq, ctx):
    rows = seq // GRID_W
    row_ids = jnp.repeat(jnp.arange(rows, dtype=F32), GRID_W)
    col_ids = jnp.tile(jnp.arange(GRID_W, dtype=F32), rows)
    half = A_HEAD_DIM // 2
    inv = ROPE_THETA ** (-jnp.arange(0, half, 2, dtype=F32) / half)
    ang_r = row_ids[:, None] * inv
    ang_c = col_ids[:, None] * inv
    cos = jnp.concatenate([jnp.cos(ang_r)] * 2 + [jnp.cos(ang_c)] * 2, axis=1)
    sin = jnp.concatenate([-jnp.sin(ang_r), jnp.sin(ang_r), -jnp.sin(ang_c), jnp.sin(ang_c)], axis=1)
    cos = jnp.tile(cos, (1, A_HEADS))
    sin = jnp.tile(sin, (1, A_HEADS))
    cos = jnp.concatenate([cos, jnp.ones((ctx, A_Q), F32)], axis=0)
    sin = jnp.concatenate([sin, jnp.zeros((ctx, A_Q), F32)], axis=0)
    return cos, sin


def kernel(x, c, ctx, c_ctx, w_mod, b_mod, g_norm1, g_norm2, w_in_even, w_out_even, g_qnorm, g_knorm, w_conv,
           b_gate, g_hnorm, w_in_odd, w_pool_grp, s_pool, w_router, w_exp_gate, w_exp_up, w_exp_down, g_final):
    b, seq, dm = x.shape
    n_ctx = ctx.shape[1]
    depth = w_mod.shape[0]
    rt = 256 if (seq % 256 == 0 and n_ctx % 256 == 0) else 128
    assert seq % rt == 0 and n_ctx % rt == 0 and seq % n_ctx == 0 and seq % GRID_W == 0
    nlt = seq // rt
    nt = (seq + n_ctx) // rt

    xcat = jnp.concatenate([x, ctx, jnp.zeros((b, seq - n_ctx, dm), F32)], axis=1)

    rows = -(-(b + 1) // 8) * 8
    cond = jnp.zeros((rows, dm), F32).at[:b].set(c).at[b].set(c_ctx)
    mods = _adaln(cond, w_mod, b_mod)
    lat = mods[:, :b].reshape(depth, b, 1, 6, dm)
    cx = jnp.broadcast_to(mods[:, b].reshape(depth, 1, 1, 6, dm), (depth, b, 1, 6, dm))
    modsel = jnp.concatenate([lat, cx], axis=2)

    cos, sin = _rope_tables(seq, n_ctx)
    lane = jnp.arange(A_Q)
    gmat = (lane[:, None] // A_HEAD_DIM == lane[None, :] // A_HEAD_DIM).astype(BF16)
    blk = min(256, n_ctx)
    tri = jnp.arange(blk)
    upper = (tri[:, None] <= tri[None, :]).astype(BF16)

    wg, wu, wd = w_exp_gate, w_exp_up, w_exp_down
    wr = jnp.pad(w_router, ((0, 0), (0, 0), (0, GATE_PAD - N_EXPERTS)))

    for i in range(depth):
        ctx_next = any(j % 2 == 0 for j in range(i + 1, depth))
        n_upd = nt if ctx_next else nlt
        ms = modsel[i]
        if i % 2 == 0:
            e = i // 2
            w_in = w_in_even[e]
            pad = jnp.zeros((dm, GATE_PAD - N_GATES), F32)
            w_in = jnp.concatenate([w_in, pad], axis=1).astype(BF16)
            bg = jnp.concatenate([b_gate[e], jnp.zeros((GATE_PAD - N_GATES,), F32)])[None]
            q, k, v, qk, vbt, ob, gt, gc, gr = _inproj(
                xcat, ms, g_norm1[i][None], w_in, cos, sin,
                jnp.tile(g_qnorm[e], A_HEADS)[None], jnp.tile(g_knorm[e], A_KV_HEADS)[None], gmat, bg,
                rt=rt, ta=seq + n_ctx, n_lat_tiles=nlt)
            oa = _attention(q, k, v, rt=rt, n_tiles=n_upd, n_lat_tiles=nlt, seq=seq)
            qct = _short_conv(qk, w_conv[e], seq=seq, keys=False)
            kc = _short_conv(qk, w_conv[e], seq=seq, keys=True)
            gates_row = jnp.swapaxes(jnp.concatenate([gt[:, :, :N_GATES], gc[:, :, :N_GATES]], axis=2), 1, 2)
            hf, hb = _mlstm(qct, kc, vbt, gates_row, gr, seq=seq, blk=rt)
            xcat = _outproj(xcat, ms, oa, hf, hb, ob, g_hnorm[e][None], w_out_even[e].astype(BF16),
                            rt=rt, n_tiles=n_upd, n_lat_tiles=nlt)
        else:
            o = i // 2
            w_in = w_in_odd[o].astype(BF16)
            wgrp = w_pool_grp[o].astype(BF16)
            u = _mod_matmul(xcat, ms, g_norm1[i][None], w_in, rt=rt, tile0=0, n_tiles=nlt, sel=0)
            if ctx_next:
                uc = _mod_matmul(xcat, ms, g_norm1[i][None], w_in, rt=rt, tile0=nlt, n_tiles=nt - nlt, sel=1)
            xcat = _pool(u, xcat, ms, wgrp, s_pool[o][None], n=seq, row_block=0, sel=0)
            if ctx_next:
                xcat = _pool(uc, xcat, ms, wgrp, s_pool[o][None], n=n_ctx, row_block=seq // n_ctx, sel=1)
        xcat = _ec_moe(xcat, ms, g_norm2[i][None], wr[i], upper, wg, wu, wd, i,
                       rt=rt, seq=seq, ctx=n_ctx, with_ctx=ctx_next,
                       final_gain=g_final[None] if i == depth - 1 else None)
    return xcat
```

```python
import functools

import jax
import jax.numpy as jnp
from jax import lax
from jax.experimental import pallas as pl
from jax.experimental.pallas import tpu as pltpu

D_MODEL = 1024
GRID_W = 64
A_HEADS = 8
A_KV_HEADS = 2
A_HEAD_DIM = 64
ROPE_THETA = 10000.0
B_HEADS = 4
B_HEAD_DIM = 128
B_CHUNK = 128
POOL_WINDOWS = (2, 4, 8, 16)
POOL_GROUP = D_MODEL // 4
N_EXPERTS = 16
EC_FACTOR = 2
NORM_EPS = 1e-6

A_Q = A_HEADS * A_HEAD_DIM
A_KV = A_KV_HEADS * A_HEAD_DIM
B_W = B_HEADS * B_HEAD_DIM
N_GATES = 2 * 2 * B_HEADS
GATE_PAD = 128
C_QA, C_KA, C_VA = 0, A_Q, A_Q + A_KV
C_QK = A_Q + 2 * A_KV
C_VB = C_QK + 2 * B_W
C_OB = C_VB + B_W
C_GT = C_OB + B_W
EVEN_COLS = C_GT + GATE_PAD
POOL_HALO = 16

F32 = jnp.float32
BF16 = jnp.bfloat16
HIGHEST = lax.Precision.HIGHEST
NT_DIMS = (((1,), (1,)), ((), ()))
TN_DIMS = (((0,), (0,)), ((), ()))
VMEM_LIMIT = 56 * 1024 * 1024


def _params(*sem):
    return pltpu.CompilerParams(dimension_semantics=sem, vmem_limit_bytes=VMEM_LIMIT)


def _modulate(x, g, shift, scale):
    y = x * lax.rsqrt(jnp.mean(x * x, axis=-1, keepdims=True) + NORM_EPS)
    return (y * g) * (1.0 + scale) + shift


def _silu(x):
    return x * jax.nn.sigmoid(x)


def _adaln_kernel(c_ref, w_ref, b_ref, o_ref):
    s = _silu(c_ref[...])
    o_ref[0] = jnp.dot(s, w_ref[0], precision=HIGHEST, preferred_element_type=F32) + b_ref[0]


def _adaln(cond, w_mod, b_mod):
    depth, dm, n6 = w_mod.shape
    rows = cond.shape[0]
    tn = 1536
    return pl.pallas_call(
        _adaln_kernel,
        grid=(depth, n6 // tn),
        in_specs=[pl.BlockSpec((rows, dm), lambda l, j: (0, 0)),
                  pl.BlockSpec((1, dm, tn), lambda l, j: (l, 0, j)),
                  pl.BlockSpec((1, 1, tn), lambda l, j: (l, 0, j))],
        out_specs=pl.BlockSpec((1, rows, tn), lambda l, j: (l, 0, j)),
        out_shape=jax.ShapeDtypeStruct((depth, rows, n6), F32),
        compiler_params=_params("arbitrary", "arbitrary"),
        name="adaln",
    )(cond, w_mod, b_mod.reshape(depth, 1, n6))


def _group_mean_sq(x, gmat, width):
    s = jnp.dot((x * x).astype(BF16), gmat, preferred_element_type=F32)
    return s * (1.0 / width)


def _dot_exact_lhs(mat, x):
    x1 = x.astype(BF16)
    r1 = x - x1.astype(F32)
    x2 = r1.astype(BF16)
    x3 = (r1 - x2.astype(F32)).astype(BF16)
    return (jnp.dot(mat, x1, preferred_element_type=F32) + jnp.dot(mat, x2, preferred_element_type=F32)
            + jnp.dot(mat, x3, preferred_element_type=F32))


def _rope(x, cos, sin):
    w = x.shape[1]
    lane = lax.broadcasted_iota(jnp.int32, x.shape, 1)
    first = (lane % 32) < 16
    partner = jnp.where(first, pltpu.roll(x, w - 16, 1), pltpu.roll(x, 16, 1))
    return x * cos + partner * sin


def _inproj_kernel(x_ref, m_ref, g1_ref, w_ref, cos_ref, sin_ref, gq_ref, gk_ref, gm_ref, bg_ref, lp_ref, ls_ref,
                   q_ref, k_ref, v_ref, qk_ref, vb_ref, ob_ref, gt_ref, gc_ref, gr_ref):
    m = m_ref[0, 0]
    h = _modulate(x_ref[0], g1_ref[...], m[0:1], m[1:2]).astype(BF16)

    def mm(lo, hi):
        return jnp.dot(h, w_ref[:, lo:hi], preferred_element_type=F32)

    cos = cos_ref[...]
    sin = sin_ref[...]
    gm = gm_ref[...]
    qa = mm(C_QA, C_KA)
    qa = qa * lax.rsqrt(_group_mean_sq(qa, gm, A_HEAD_DIM) + NORM_EPS) * gq_ref[...]
    q_ref[0] = (_rope(qa, cos, sin) * (A_HEAD_DIM ** -0.5)).astype(BF16)
    ka = mm(C_KA, C_VA)
    ka = ka * lax.rsqrt(_group_mean_sq(ka, gm[:A_KV, :A_KV], A_HEAD_DIM) + NORM_EPS) * gk_ref[...]
    k_ref[0] = _rope(ka, cos[:, :A_KV], sin[:, :A_KV]).astype(BF16)
    va = mm(C_VA, C_QK).astype(BF16)
    ones = jnp.ones((va.shape[0], A_HEAD_DIM), BF16)
    v_ref[0] = jnp.concatenate(
        [piece for g in range(A_KV_HEADS) for piece in (va[:, g * A_HEAD_DIM:(g + 1) * A_HEAD_DIM], ones)], axis=1)
    qk_ref[0] = mm(C_QK, C_VB)
    vb_ref[0] = mm(C_VB, C_OB).T.astype(BF16)
    ob_ref[0] = mm(C_OB, C_GT)
    gt = mm(C_GT, EVEN_COLS) + bg_ref[...]
    lane = lax.broadcasted_iota(jnp.int32, gt.shape, 1)
    log_sig = jnp.minimum(gt, 0.0) - jnp.log1p(jnp.exp(-jnp.abs(gt)))
    gt = jnp.where((lane % (2 * B_HEADS)) >= B_HEADS, log_sig, gt)
    gt_ref[0] = gt
    prefix = _dot_exact_lhs(lp_ref[...], gt)
    suffix = _dot_exact_lhs(ls_ref[...], gt)
    cum = jnp.where((lane % (4 * B_HEADS)) < 2 * B_HEADS, prefix, suffix)
    gc_ref[0] = cum
    gr_ref[0] = gt - pltpu.roll(cum, GATE_PAD - B_HEADS, 1)


def _inproj(xcat, modsel, g1, w, cos, sin, gq, gk, gmat, bgate, *, rt, ta, n_lat_tiles):
    b, _, dm = xcat.shape
    nt = ta // rt
    row = lambda bb, r: (bb, r, 0)
    full = lambda bb, r: (0, 0)
    out_widths = (A_Q, A_KV, 2 * A_KV, 2 * B_W, B_W, B_W, GATE_PAD, GATE_PAD, GATE_PAD)
    out_dtypes = (BF16, BF16, BF16, F32, BF16, F32, F32, F32, F32)
    out_specs = [pl.BlockSpec((1, rt, wd), row) for wd in out_widths]
    out_shapes = [jax.ShapeDtypeStruct((b, ta, wd), dt) for wd, dt in zip(out_widths, out_dtypes)]
    vb_slot = 4
    out_specs[vb_slot] = pl.BlockSpec((1, B_W, rt), lambda bb, r: (bb, 0, r))
    out_shapes[vb_slot] = jax.ShapeDtypeStruct((b, B_W, ta), BF16)
    idx = jnp.arange(rt)
    same_chunk = idx[:, None] // B_CHUNK == idx[None, :] // B_CHUNK
    lower = jnp.logical_and(same_chunk, idx[None, :] <= idx[:, None]).astype(BF16)
    upper = jnp.logical_and(same_chunk, idx[None, :] >= idx[:, None]).astype(BF16)
    return pl.pallas_call(
        _inproj_kernel,
        grid=(b, nt),
        in_specs=[pl.BlockSpec((1, rt, dm), row),
                  pl.BlockSpec((1, 1, 6, dm), lambda bb, r: (bb, (r >= n_lat_tiles).astype(jnp.int32), 0, 0)),
                  pl.BlockSpec((1, dm), full),
                  pl.BlockSpec((dm, EVEN_COLS), full),
                  pl.BlockSpec((rt, A_Q), lambda bb, r: (r, 0)),
                  pl.BlockSpec((rt, A_Q), lambda bb, r: (r, 0)),
                  pl.BlockSpec((1, A_Q), full),
                  pl.BlockSpec((1, A_KV), full),
                  pl.BlockSpec((A_Q, A_Q), full),
                  pl.BlockSpec((1, GATE_PAD), full),
                  pl.BlockSpec((rt, rt), full),
                  pl.BlockSpec((rt, rt), full)],
        out_specs=out_specs,
        out_shape=out_shapes,
        compiler_params=_params("parallel", "arbitrary"),
        name="even_inproj",
    )(xcat, modsel, g1, w, cos, sin, gq, gk, gmat, bgate, lower, upper)


def _attn_kernel(q_ref, k_ref, v_ref, o_ref, *, n_lat_tiles, n_tiles, seq):
    group = A_HEADS // A_KV_HEADS

    def attend(key_lo):
        q = q_ref[0]
        for hd in range(A_HEADS):
            g = hd // group
            qh = q[:, hd * A_HEAD_DIM:(hd + 1) * A_HEAD_DIM]
            kg = k_ref[0, key_lo:, g * A_HEAD_DIM:(g + 1) * A_HEAD_DIM]
            vg = v_ref[0, key_lo:, 2 * g * A_HEAD_DIM:2 * (g + 1) * A_HEAD_DIM]
            s = lax.dot_general(qh, kg, NT_DIMS, preferred_element_type=F32)
            p = jnp.exp(s - jnp.max(s, axis=-1, keepdims=True))
            o = jnp.dot(p.astype(BF16), vg, preferred_element_type=F32)
            o = o[:, :A_HEAD_DIM] / o[:, A_HEAD_DIM:]
            o_ref[0, :, hd * A_HEAD_DIM:(hd + 1) * A_HEAD_DIM] = o.astype(BF16)

    if n_tiles == n_lat_tiles:
        attend(0)
    else:
        r = pl.program_id(1)
        pl.when(r < n_lat_tiles)(lambda: attend(0))
        pl.when(r >= n_lat_tiles)(lambda: attend(seq))


def _attention(q, k, v, *, rt, n_tiles, n_lat_tiles, seq):
    b, ta, _ = q.shape
    return pl.pallas_call(
        functools.partial(_attn_kernel, n_lat_tiles=n_lat_tiles, n_tiles=n_tiles, seq=seq),
        grid=(b, n_tiles),
        in_specs=[pl.BlockSpec((1, rt, A_Q), lambda bb, r: (bb, r, 0)),
                  pl.BlockSpec((1, ta, A_KV), lambda bb, r: (bb, 0, 0)),
                  pl.BlockSpec((1, ta, 2 * A_KV), lambda bb, r: (bb, 0, 0))],
        out_specs=pl.BlockSpec((1, rt, A_Q), lambda bb, r: (bb, r, 0)),
        out_shape=jax.ShapeDtypeStruct((b, n_tiles * rt, A_Q), BF16),
        compiler_params=_params("parallel", "arbitrary"),
        name="gqa_attention",
    )(q, k, v)


def _conv_kernel(x_ref, w_ref, o_ref, *, seq, keys):
    x = x_ref[0]
    ta = x.shape[0]
    w = w_ref[...]
    row = lax.broadcasted_iota(jnp.int32, (ta, 1), 0)
    prev = jnp.where(jnp.logical_or(row == 0, row == seq), 0.0, pltpu.roll(x, 1, 0))
    nxt = jnp.where(jnp.logical_or(row == seq - 1, row == ta - 1), 0.0, pltpu.roll(x, ta - 1, 0))
    y = _silu(prev * w[0:1] + x * w[1:2] + nxt * w[2:3])
    if keys:
        o_ref[0] = (y * (B_HEAD_DIM ** -0.5)).astype(BF16)
    else:
        o_ref[0] = y.T.astype(BF16)


def _short_conv(qk, w_conv, *, seq, keys):
    b, ta, _ = qk.shape
    ct = 256
    c0 = B_W // ct if keys else 0
    if keys:
        out_spec = pl.BlockSpec((1, ta, ct), lambda bb, j: (bb, 0, j))
        out_shape = jax.ShapeDtypeStruct((b, ta, B_W), BF16)
    else:
        out_spec = pl.BlockSpec((1, ct, ta), lambda bb, j: (bb, j, 0))
        out_shape = jax.ShapeDtypeStruct((b, B_W, ta), BF16)
    return pl.pallas_call(
        functools.partial(_conv_kernel, seq=seq, keys=keys),
        grid=(b, B_W // ct),
        in_specs=[pl.BlockSpec((1, ta, ct), lambda bb, j: (bb, 0, c0 + j)),
                  pl.BlockSpec((w_conv.shape[0], ct), lambda bb, j: (0, c0 + j))],
        out_specs=out_spec,
        out_shape=out_shape,
        compiler_params=_params("parallel", "arbitrary"),
        name="mlstm_conv_k" if keys else "mlstm_conv_q",
    )(qk, w_conv)


def _mlstm_chain(qt, k, vat, ig, b_row, r_col, total, ca, m_old, tri):
    log_d = jnp.where(tri, b_row + r_col, -jnp.inf)
    log_inter = m_old + b_row
    m_t = jnp.maximum(log_inter, jnp.max(log_d, axis=0, keepdims=True))
    w_intra = jnp.exp(log_d - m_t) * jnp.dot(k, qt, preferred_element_type=F32)
    w_inter = jnp.exp(log_inter - m_t)
    nd = (w_inter * jnp.dot(ca.astype(BF16), qt, preferred_element_type=F32)
          + jnp.dot(vat, w_intra.astype(BF16), preferred_element_type=F32))
    dv = nd.shape[0] // 2
    h = nd[:dv] / jnp.maximum(jnp.abs(nd[dv:]), jnp.exp(-m_t))
    log_w = total - b_row + ig
    m_new = jnp.maximum(m_old + total, jnp.max(log_w, axis=-1, keepdims=True))
    w_s = jnp.exp(log_w - m_new)
    decay = jnp.exp(m_old + total - m_new)
    ca_new = decay * ca + jnp.dot((vat * w_s).astype(BF16), k, preferred_element_type=F32)
    return h, ca_new, m_new


def _mlstm_kernel(qf_ref, kf_ref, vf_ref, grf_ref, gcf_ref, qb_ref, kb_ref, vb_ref, grb_ref, gcb_ref,
                  hf_ref, hb_ref, c_ref, m_ref):
    @pl.when(pl.program_id(1) == 0)
    def _():
        c_ref[...] = jnp.zeros(c_ref.shape, F32)
        m_ref[...] = jnp.full(m_ref.shape, -1e30, F32)

    t = B_CHUNK
    n_chunks = kf_ref.shape[1] // t
    src = lax.broadcasted_iota(jnp.int32, (t, t), 0)
    tgt = lax.broadcasted_iota(jnp.int32, (t, t), 1)
    ones = jnp.ones((B_HEAD_DIM, t), BF16)
    dirs = ((qf_ref, kf_ref, vf_ref, grf_ref, gcf_ref, hf_ref, src <= tgt, t - 1, range(n_chunks)),
            (qb_ref, kb_ref, vb_ref, grb_ref, gcb_ref, hb_ref, src >= tgt, 0, range(n_chunks - 1, -1, -1)))
    outputs = []
    states = []
    for d, (q_ref, k_ref, v_ref, gr_ref, gc_ref, h_ref, tri, last, order) in enumerate(dirs):
        for hd in range(B_HEADS):
            sl = slice(hd * B_HEAD_DIM, (hd + 1) * B_HEAD_DIM)
            gi = d * 2 * B_HEADS + hd
            fi = gi + B_HEADS
            si = d * B_HEADS + hd
            ca = c_ref[si]
            m = m_ref[si][:, 0:1]
            for ci in order:
                rows = slice(ci * t, (ci + 1) * t)
                ig = gr_ref[0, gi:gi + 1, rows]
                b_row = gr_ref[0, N_GATES + fi:N_GATES + fi + 1, rows]
                r_col = gc_ref[0, rows, gi:gi + 1]
                vat = jnp.concatenate([v_ref[0, sl, rows], ones], axis=0)
                h, ca, m = _mlstm_chain(q_ref[0, sl, rows], k_ref[0, rows, sl], vat, ig, b_row, r_col,
                                        b_row[:, last:last + 1], ca, m, tri)
                outputs.append((h_ref, rows, sl, h))
            states.append((si, ca, m))
    for h_ref, rows, sl, h in outputs:
        h_ref[0, sl, rows] = h
    for si, ca, m in states:
        c_ref[si] = ca
        m_ref[si] = jnp.broadcast_to(m, (1, 128))


def _mlstm(qct, kc, vbt, gates_row, gates_col, *, seq, blk):
    b, ta, _ = kc.shape
    nc = ta // blk
    ncl = seq // blk
    ncc = nc - ncl
    fwd = lambda j: jnp.where(j < ncc, ncl + j, j - ncc)
    bwd = lambda j: jnp.where(j < ncc, nc - 1 - j, ncl - 1 - (j - ncc))

    def specs(order):
        feature_major = pl.BlockSpec((1, B_W, blk), lambda bb, j: (bb, 0, order(j)))
        return [feature_major,
                pl.BlockSpec((1, blk, B_W), lambda bb, j: (bb, order(j), 0)),
                feature_major,
                pl.BlockSpec((1, 2 * N_GATES, blk), lambda bb, j: (bb, 0, order(j))),
                pl.BlockSpec((1, blk, GATE_PAD), lambda bb, j: (bb, order(j), 0))]

    n_state = 2 * B_HEADS
    ops = (qct, kc, vbt, gates_row, gates_col)
    return pl.pallas_call(
        _mlstm_kernel,
        grid=(b, nc),
        in_specs=specs(fwd) + specs(bwd),
        out_specs=[pl.BlockSpec((1, B_W, blk), lambda bb, j: (bb, 0, fwd(j))),
                   pl.BlockSpec((1, B_W, blk), lambda bb, j: (bb, 0, bwd(j)))],
        out_shape=[jax.ShapeDtypeStruct((b, B_W, ta), F32)] * 2,
        scratch_shapes=[pltpu.VMEM((n_state, 2 * B_HEAD_DIM, B_HEAD_DIM), F32),
                        pltpu.VMEM((n_state, 1, 128), F32)],
        compiler_params=_params("parallel", "arbitrary"),
        name="mlstm_scan",
    )(*ops, *ops)


def _outproj_kernel(x_ref, m_ref, oa_ref, hf_ref, hb_ref, ob_ref, ghn_ref, w_ref, o_ref):
    hm = (hf_ref[0] + hb_ref[0]).T
    ob = ob_ref[0]
    ghn = ghn_ref[...]
    y = jnp.dot(oa_ref[0], w_ref[0:A_Q, :], preferred_element_type=F32)
    for hd in range(B_HEADS):
        sl = slice(hd * B_HEAD_DIM, (hd + 1) * B_HEAD_DIM)
        z = hm[:, sl]
        z = z * lax.rsqrt(jnp.mean(z * z, axis=-1, keepdims=True) + NORM_EPS) * ghn[:, sl]
        z = (z * jax.nn.sigmoid(ob[:, sl])).astype(BF16)
        y = y + jnp.dot(z, w_ref[A_Q + hd * B_HEAD_DIM:A_Q + (hd + 1) * B_HEAD_DIM, :],
                        preferred_element_type=F32)
    o_ref[0] = x_ref[0] + m_ref[0, 0][2:3] * y


def _outproj(xcat, modsel, oa, hf, hb, ob, ghn, w, *, rt, n_tiles, n_lat_tiles):
    b, ta, dm = xcat.shape
    row = lambda bb, r: (bb, r, 0)
    full = lambda bb, r: (0, 0)
    return pl.pallas_call(
        _outproj_kernel,
        grid=(b, n_tiles),
        in_specs=[pl.BlockSpec((1, rt, dm), row),
                  pl.BlockSpec((1, 1, 6, dm), lambda bb, r: (bb, (r >= n_lat_tiles).astype(jnp.int32), 0, 0)),
                  pl.BlockSpec((1, rt, A_Q), row),
                  pl.BlockSpec((1, B_W, rt), lambda bb, r: (bb, 0, r)),
                  pl.BlockSpec((1, B_W, rt), lambda bb, r: (bb, 0, r)),
                  pl.BlockSpec((1, rt, B_W), row),
                  pl.BlockSpec((1, B_W), full),
                  pl.BlockSpec((A_Q + B_W, dm), full)],
        out_specs=pl.BlockSpec((1, rt, dm), row),
        out_shape=jax.ShapeDtypeStruct(xcat.shape, F32),
        input_output_aliases={0: 0},
        compiler_params=_params("parallel", "arbitrary"),
        name="even_outproj",
    )(xcat, modsel, oa, hf, hb, ob, ghn, w)


def _modmm_kernel(x_ref, m_ref, g_ref, w_ref, o_ref):
    m = m_ref[0, 0]
    h = _modulate(x_ref[0], g_ref[...], m[0:1], m[1:2]).astype(BF16)
    o_ref[0] = jnp.dot(h, w_ref[...], preferred_element_type=F32)


def _mod_matmul(xcat, modsel, g, w, *, rt, tile0, n_tiles, sel):
    b, _, dm = xcat.shape
    n_out = w.shape[1]
    return pl.pallas_call(
        _modmm_kernel,
        grid=(b, n_tiles),
        in_specs=[pl.BlockSpec((1, rt, dm), lambda bb, r: (bb, tile0 + r, 0)),
                  pl.BlockSpec((1, 1, 6, dm), lambda bb, r: (bb, sel, 0, 0)),
                  pl.BlockSpec((1, dm), lambda bb, r: (0, 0)),
                  pl.BlockSpec((dm, n_out), lambda bb, r: (0, 0))],
        out_specs=pl.BlockSpec((1, rt, n_out), lambda bb, r: (bb, r, 0)),
        out_shape=jax.ShapeDtypeStruct((b, n_tiles * rt, n_out), F32),
        compiler_params=_params("parallel", "arbitrary"),
        name="odd_inproj",
    )(xcat, modsel, g, w)


def _pool_kernel(u_ref, x_ref, m_ref, wg_ref, sp_ref, o_ref, pad_ref, sa_ref, sb_ref, *, n, rc):
    g = pl.program_id(1)
    halo = POOL_HALO
    ext = n + 2 * halo
    zeros = jnp.zeros((halo, pad_ref.shape[1]), F32)
    pad_ref[0:halo, :] = zeros
    pad_ref[halo + n:ext, :] = zeros
    pad_ref[halo:halo + n, :] = u_ref[0]
    for ref in (pad_ref, sa_ref, sb_ref):
        ref[ext:ext + halo, :] = zeros
    gate = m_ref[0, 0][2:3]
    chunks = [(r0, min(rc, ext - r0)) for r0 in range(0, ext, rc)]

    def pair_sums(src, dst, shift):
        for r0, size in chunks:
            dst[r0:r0 + size, :] = src[r0:r0 + size, :] + src[r0 + shift:r0 + shift + size, :]

    for gi, win in enumerate(POOL_WINDOWS):
        @pl.when(g == gi)
        def _(win=win):
            lo = win // 2
            hi = win - 1 - lo
            sums, width = pad_ref, 1
            for dst in (sa_ref, sb_ref, sa_ref, sb_ref):
                if width == win:
                    break
                pair_sums(sums, dst, width)
                sums, width = dst, 2 * width
            assert width == win
            for r0 in range(0, n, rc):
                base = halo + r0
                acc = sums[base - lo:base - lo + rc, :]
                t = r0 + lax.broadcasted_iota(jnp.int32, (rc, 1), 0)
                cnt = jnp.minimum(t + hi, n - 1) - jnp.maximum(t - lo, 0) + 1
                p = acc / cnt.astype(F32) - pad_ref[base:base + rc, :]
                y = jnp.dot(p.astype(BF16), wg_ref[0], preferred_element_type=F32) * sp_ref[...]
                o_ref[0, r0:r0 + rc, :] = x_ref[0, r0:r0 + rc, :] + gate * y


def _pool(u, xcat, modsel, wgrp, spool, *, n, row_block, sel):
    b, _, dm = xcat.shape
    pg = POOL_GROUP
    rc = min(256, n)
    blk = lambda bb, g: (bb, row_block, g)
    return pl.pallas_call(
        functools.partial(_pool_kernel, n=n, rc=rc),
        grid=(b, dm // pg),
        in_specs=[pl.BlockSpec((1, n, pg), lambda bb, g: (bb, 0, g)),
                  pl.BlockSpec((1, n, pg), blk),
                  pl.BlockSpec((1, 1, 6, pg), lambda bb, g: (bb, sel, 0, g)),
                  pl.BlockSpec((1, pg, pg), lambda bb, g: (g, 0, 0)),
                  pl.BlockSpec((1, pg), lambda bb, g: (0, g))],
        out_specs=pl.BlockSpec((1, n, pg), blk),
        out_shape=jax.ShapeDtypeStruct(xcat.shape, F32),
        scratch_shapes=[pltpu.VMEM((n + 3 * POOL_HALO, pg), F32)] * 3,
        input_output_aliases={1: 0},
        compiler_params=_params("parallel", "arbitrary"),
        name="pool_mixer",
    )(u, xcat, modsel, wgrp, spool)


def _lane_cumsum(x01, upper, blk):
    n = x01.shape[1]
    out = []
    carry = jnp.zeros((x01.shape[0], 1), F32)
    for j in range(0, n, blk):
        cs = jnp.dot(x01[:, j:j + blk].astype(BF16), upper, preferred_element_type=F32) + carry
        carry = cs[:, blk - 1:blk]
        out.append(cs)
    return jnp.concatenate(out, axis=1) if len(out) > 1 else out[0]


def _router_kernel(x_ref, m_ref, g_ref, wr_ref, h_ref, lg_ref):
    m = m_ref[0, 0]
    h = _modulate(x_ref[0], g_ref[...], m[3:4], m[4:5])
    h_hi = h.astype(BF16)
    h_ref[0] = h_hi
    h_lo = (h - h_hi.astype(F32)).astype(BF16)
    wr = wr_ref[...]
    w_hi = wr.astype(BF16)
    w_lo = (wr - w_hi.astype(F32)).astype(BF16)
    lg = (jnp.dot(h_hi, w_hi, preferred_element_type=F32)
          + (jnp.dot(h_hi, w_lo, preferred_element_type=F32) + jnp.dot(h_lo, w_hi, preferred_element_type=F32)))
    lg_ref[0] = lg.T[:lg_ref.shape[1]]


def _topk_kernel(lg_ref, up_ref, aff_ref, pos_ref, *, cap):
    lg = lg_ref[...]
    nb, ne, n = lg.shape
    e = jnp.exp(lg - jnp.max(lg, axis=1, keepdims=True))
    aff3 = e / jnp.sum(e, axis=1, keepdims=True)
    aff_ref[...] = aff3
    aff = aff3.reshape(nb * ne, n)

    def bit_step(i, thr):
        cand = thr | lax.shift_left(jnp.int32(1), 30 - i)
        cnt = jnp.sum(jnp.where(aff >= pltpu.bitcast(cand, F32), 1.0, 0.0), axis=-1, keepdims=True)
        return jnp.where(cnt >= cap, cand, thr)

    thr = lax.fori_loop(0, 31, bit_step, jnp.zeros((aff.shape[0], 1), jnp.int32))
    thr = pltpu.bitcast(thr, F32)
    above = aff > thr
    tied = aff == thr
    room = cap - jnp.sum(jnp.where(above, 1.0, 0.0), axis=-1, keepdims=True)
    upper = up_ref[...]
    blk = upper.shape[0]
    tie_rank = _lane_cumsum(jnp.where(tied, 1.0, 0.0), upper, blk)
    sel = jnp.logical_or(above, jnp.logical_and(tied, tie_rank <= room))
    slot = _lane_cumsum(jnp.where(sel, 1.0, 0.0), upper, blk) - 1.0
    pos_ref[...] = jnp.where(sel, slot, -1.0).astype(jnp.int32).reshape(nb, ne, n)


def _router(xcat, modsel, g2, wr, upper, *, rt, n, row_tile0, sel):
    b, ta, dm = xcat.shape
    n_tiles = n // rt
    cap = max(1, EC_FACTOR * n // N_EXPERTS)
    ne = N_EXPERTS
    h, logits = pl.pallas_call(
        _router_kernel,
        grid=(b, n_tiles),
        in_specs=[pl.BlockSpec((1, rt, dm), lambda bb, r: (bb, row_tile0 + r, 0)),
                  pl.BlockSpec((1, 1, 6, dm), lambda bb, r: (bb, sel, 0, 0)),
                  pl.BlockSpec((1, dm), lambda bb, r: (0, 0)),
                  pl.BlockSpec(wr.shape, lambda bb, r: (0, 0))],
        out_specs=[pl.BlockSpec((1, rt, dm), lambda bb, r: (bb, r, 0)),
                   pl.BlockSpec((1, ne, rt), lambda bb, r: (bb, 0, r))],
        out_shape=[jax.ShapeDtypeStruct((b, n, dm), BF16),
                   jax.ShapeDtypeStruct((b, ne, n), F32)],
        compiler_params=_params("parallel", "arbitrary"),
        name="moe_router",
    )(xcat, modsel, g2, wr)
    whole = lambda i: (0, 0, 0)
    aff, pos = pl.pallas_call(
        functools.partial(_topk_kernel, cap=cap),
        grid=(1,),
        in_specs=[pl.BlockSpec((b, ne, n), whole),
                  pl.BlockSpec(upper.shape, lambda i: (0, 0))],
        out_specs=[pl.BlockSpec((b, ne, n), whole),
                   pl.BlockSpec((b, ne, n), whole)],
        out_shape=[jax.ShapeDtypeStruct((b, ne, n), F32),
                   jax.ShapeDtypeStruct((b, ne, n), jnp.int32)],
        compiler_params=_params("arbitrary"),
        name="moe_topk",
    )(logits, upper)
    return h, aff, pos


MOE_EXPERT_BLOCK = 4
MOE_TOKEN_BLOCK = 256
MOE_WINDOW = 64


def _slot_onehot(pos_ref, cap):
    hits = []
    for i in range(pos_ref.shape[1]):
        pos = pos_ref[0, i]
        hits.append(lax.broadcasted_iota(jnp.int32, (cap, pos.shape[1]), 0) == pos)
    pick = jnp.concatenate([jnp.where(hit, 1.0, 0.0).astype(BF16) for hit in hits], axis=0)
    return pick, hits


def _moe_plan(pos, *, cap, tb, w):
    nb, ne, _, n = pos.shape
    nk = n // tb
    cnt = jnp.sum((pos.reshape(nb, ne, nk, tb) >= 0).astype(jnp.int32), axis=-1)
    first = jnp.cumsum(cnt, axis=-1) - cnt
    start = jnp.minimum((first // 16) * 16, cap - w)
    fits = first - start + cnt <= w
    dense_group = jnp.logical_not(jnp.all(fits.reshape(nb, ne // MOE_EXPERT_BLOCK, -1), axis=-1))
    dense_block = jnp.logical_not(jnp.all(fits, axis=1))
    flat = lambda a: a.reshape(-1).astype(jnp.int32)
    return flat(start), flat(dense_group), flat(dense_block)


def _moe_gather_kernel(start_ref, dense_ref, pos_ref, aff_ref, h_ref, xs_ref, gate_ref, *, cap, tb, w):
    b = pl.program_id(0)
    g = pl.program_id(1)
    eb, n = pos_ref.shape[1], pos_ref.shape[3]
    nk = n // tb
    dense = dense_ref[b * pl.num_programs(1) + g]

    @pl.when(dense != 0)
    def _():
        pick, hits = _slot_onehot(pos_ref, cap)
        xs = jnp.dot(pick, h_ref[0], preferred_element_type=F32).astype(BF16)
        xs_ref[0] = xs.reshape(xs_ref.shape[1:])
        for i, hit in enumerate(hits):
            gate = jnp.sum(jnp.where(hit, aff_ref[0, i], 0.0), axis=-1, keepdims=True)
            gate_ref[0, i] = jnp.broadcast_to(gate, gate_ref.shape[2:])

    @pl.when(dense == 0)
    def _():
        xs_ref[...] = jnp.zeros(xs_ref.shape, BF16)
        gate_ref[...] = jnp.zeros(gate_ref.shape, F32)
        slot = lax.broadcasted_iota(jnp.int32, (w, tb), 0)
        for k in range(nk):
            cols = slice(k * tb, (k + 1) * tb)
            starts = [pl.multiple_of(start_ref[((b * pl.num_programs(1) + g) * eb + i) * nk + k], 16)
                      for i in range(eb)]
            hits = [(slot + starts[i]) == pos_ref[0, i, :, cols] for i in range(eb)]
            pick = jnp.concatenate([jnp.where(hit, 1.0, 0.0).astype(BF16) for hit in hits], axis=0)
            part = jnp.dot(pick, h_ref[0, cols, :], preferred_element_type=F32).astype(BF16)
            for i in range(eb):
                rows = pl.ds(starts[i], w)
                xs_ref[0, i, rows, :] = xs_ref[0, i, rows, :] + part[i * w:(i + 1) * w]
                gate = jnp.sum(jnp.where(hits[i], aff_ref[0, i, :, cols], 0.0), axis=-1, keepdims=True)
                gate_ref[0, i, rows, :] = gate_ref[0, i, rows, :] + gate


def _moe_gather(start, dense, pos, aff, h, *, cap, tb, w):
    nb, ne, _, n = pos.shape
    dm = h.shape[2]
    eb = MOE_EXPERT_BLOCK
    grid_spec = pltpu.PrefetchScalarGridSpec(
        num_scalar_prefetch=2,
        grid=(nb, ne // eb),
        in_specs=[pl.BlockSpec((1, eb, 1, n), lambda b, e, *_: (b, e, 0, 0)),
                  pl.BlockSpec((1, eb, 1, n), lambda b, e, *_: (b, e, 0, 0)),
                  pl.BlockSpec((1, n, dm), lambda b, e, *_: (b, 0, 0), pipeline_mode=pl.Buffered(1))],
        out_specs=[pl.BlockSpec((1, eb, cap, dm), lambda b, e, *_: (b, e, 0, 0)),
                   pl.BlockSpec((1, eb, cap, 128), lambda b, e, *_: (b, e, 0, 0))])
    return pl.pallas_call(
        functools.partial(_moe_gather_kernel, cap=cap, tb=tb, w=w),
        grid_spec=grid_spec,
        out_shape=[jax.ShapeDtypeStruct((nb, ne, cap, dm), BF16),
                   jax.ShapeDtypeStruct((nb, ne, cap, 128), F32)],
        compiler_params=_params("parallel", "arbitrary"),
        name="moe_gather",
    )(start, dense, pos, aff, h)


def _moe_ffn_kernel(*refs, n_sets):
    sets = [refs[3 * i:3 * i + 3] for i in range(n_sets)]
    wg_ref, wu_ref, wd_ref = refs[3 * n_sets:3 * n_sets + 3]
    outs = refs[3 * n_sets + 3:4 * n_sets + 3]
    wgb_ref, wub_ref, wdb_ref = refs[4 * n_sets + 3:]
    c = pl.program_id(1)

    @pl.when(c == 0)
    def _():
        rows = 256
        for src, dst in ((wg_ref, wgb_ref), (wu_ref, wub_ref), (wd_ref, wdb_ref)):
            for r0 in range(0, dst.shape[0], rows):
                dst[r0:r0 + rows, :] = src[0, 0, r0:r0 + rows, :].astype(BF16)

    def run(xs_ref, gate_ref, m_ref, y_ref):
        nb, _, cap, dm = xs_ref.shape
        xs = xs_ref[...].reshape(nb * cap, dm)
        a = jnp.dot(xs, wgb_ref[...], preferred_element_type=F32)
        u = jnp.dot(xs, wub_ref[...], preferred_element_type=F32)
        act = (_silu(a) * u).astype(BF16)
        out = jnp.dot(act, wdb_ref[...], preferred_element_type=F32)
        for i in range(nb):
            y = out[i * cap:(i + 1) * cap] * gate_ref[i, 0][:, 0:1]
            y_ref[i, 0] = (y * m_ref[i, 0][5:6]).astype(BF16)

    if n_sets == 1:
        run(*sets[0], outs[0])
    else:
        last = pl.num_programs(1) - 1
        pl.when(c < last)(lambda: run(*sets[0], outs[0]))
        pl.when(c == last)(lambda: run(*sets[1], outs[1]))


def _moe_ffn(slot_sets, modsel, wg, wu, wd, layer):
    xs, gate, _ = slot_sets[0]
    nb_all, ne, cap, dm = xs.shape
    df = wg.shape[3]
    nb = 2 if nb_all % 2 == 0 else 1
    n_lat = nb_all // nb
    lat = lambda e, c: (jnp.minimum(c, n_lat - 1), e, 0, 0)
    in_specs = [pl.BlockSpec((nb, 1, cap, dm), lat),
                pl.BlockSpec((nb, 1, cap, 128), lat),
                pl.BlockSpec((nb, 1, 6, dm), lambda e, c: (jnp.minimum(c, n_lat - 1), slot_sets[0][2], 0, 0))]
    out_specs = [pl.BlockSpec((nb, 1, cap, dm), lat)]
    out_shape = [jax.ShapeDtypeStruct(xs.shape, BF16)]
    operands = [xs, gate, modsel]
    if len(slot_sets) == 2:
        xs2, gate2, sel2 = slot_sets[1]
        nb2, _, cap2, _ = xs2.shape
        fixed = lambda e, c: (0, e, 0, 0)
        in_specs += [pl.BlockSpec((nb2, 1, cap2, dm), fixed),
                     pl.BlockSpec((nb2, 1, cap2, 128), fixed),
                     pl.BlockSpec((nb2, 1, 6, dm), lambda e, c: (0, sel2, 0, 0))]
        out_specs.append(pl.BlockSpec((nb2, 1, cap2, dm), fixed))
        out_shape.append(jax.ShapeDtypeStruct(xs2.shape, BF16))
        operands += [xs2, gate2, modsel]
    weight = lambda e, c: (layer, e, 0, 0)
    in_specs += [pl.BlockSpec((1, 1, dm, df), weight), pl.BlockSpec((1, 1, dm, df), weight),
                 pl.BlockSpec((1, 1, df, dm), weight)]
    return pl.pallas_call(
        functools.partial(_moe_ffn_kernel, n_sets=len(slot_sets)),
        grid=(ne, n_lat + len(slot_sets) - 1),
        in_specs=in_specs,
        out_specs=out_specs,
        out_shape=out_shape,
        scratch_shapes=[pltpu.VMEM((dm, df), BF16), pltpu.VMEM((dm, df), BF16), pltpu.VMEM((df, dm), BF16)],
        compiler_params=_params("arbitrary", "arbitrary"),
        name="moe_ffn",
    )(*operands, wg, wu, wd)


def _moe_scatter_kernel(start_ref, dense_ref, posc_ref, y_ref, x_ref, *rest, cap, w):
    b = pl.program_id(0)
    k = pl.program_id(1)
    nk = pl.num_programs(1)
    ne = y_ref.shape[1]
    tb = posc_ref.shape[1]
    dense = dense_ref[b * nk + k]
    o_ref = rest[-1]

    def emit(update):
        total = x_ref[0] + update
        if len(rest) == 2:
            total = total * lax.rsqrt(jnp.mean(total * total, axis=-1, keepdims=True) + NORM_EPS) * rest[0][...]
        o_ref[0] = total

    @pl.when(dense != 0)
    def _():
        slot = lax.broadcasted_iota(jnp.int32, (tb, cap), 1)
        pick = jnp.concatenate([jnp.where(slot == posc_ref[0, :, i:i + 1], 1.0, 0.0).astype(BF16)
                                for i in range(ne)], axis=1)
        y = y_ref[0].reshape(ne * cap, y_ref.shape[3])
        emit(jnp.dot(pick, y, preferred_element_type=F32))

    @pl.when(dense == 0)
    def _():
        per = 128 // w
        lane = lax.broadcasted_iota(jnp.int32, (tb, 128), 1)
        tiles = []
        wins = []
        for j in range(ne // per):
            col = None
            for u in reversed(range(per)):
                i = j * per + u
                s = pl.multiple_of(start_ref[(b * ne + i) * nk + k], 16)
                rel = posc_ref[0, :, i:i + 1] - s
                target = jnp.where(jnp.logical_and(rel >= 0, rel < w), rel + u * w, -1)
                col = target if col is None else jnp.where(lane < (u + 1) * w, target, col)
            tiles.append(jnp.where(lane == col, 1.0, 0.0).astype(BF16))
            for u in range(per):
                i = j * per + u
                s = pl.multiple_of(start_ref[(b * ne + i) * nk + k], 16)
                wins.append(y_ref[0, i, pl.ds(s, w), :])
        pick = jnp.concatenate(tiles, axis=1)
        emit(jnp.dot(pick, jnp.concatenate(wins, axis=0), preferred_element_type=F32))


def _moe_scatter(start, dense, pos, y, xcat, *, bb, row0, tb, w, final_gain=None):
    nb, ne, cap, dm = y.shape
    n = pos.shape[3]
    assert 128 % w == 0 and ne % (128 // w) == 0
    posc = jnp.transpose(pos.reshape(nb, ne, n), (0, 2, 1))
    x_index = (lambda b, k, *_: (b, k, 0)) if bb == 1 else (lambda b, k, *_: (k, row0 // tb, 0))
    in_specs = [pl.BlockSpec((1, tb, ne), lambda b, k, *_: (b, k, 0)),
                pl.BlockSpec((1, ne, cap, dm), lambda b, k, *_: (b, 0, 0, 0)),
                pl.BlockSpec((1, tb, dm), x_index)]
    operands = (start, dense, posc, y, xcat)
    if final_gain is None:
        out_shape, aliases = jax.ShapeDtypeStruct(xcat.shape, F32), {4: 0}
    else:
        assert bb == 1
        in_specs.append(pl.BlockSpec((1, dm), lambda b, k, *_: (0, 0)))
        operands += (final_gain,)
        out_shape, aliases = jax.ShapeDtypeStruct((nb, n, dm), F32), {}
    grid_spec = pltpu.PrefetchScalarGridSpec(
        num_scalar_prefetch=2,
        grid=(nb, n // tb),
        in_specs=in_specs,
        out_specs=pl.BlockSpec((1, tb, dm), x_index))
    return pl.pallas_call(
        functools.partial(_moe_scatter_kernel, cap=cap, w=w),
        grid_spec=grid_spec,
        out_shape=out_shape,
        input_output_aliases=aliases,
        compiler_params=_params("parallel", "arbitrary"),
        name="moe_scatter",
    )(*operands)


def _ec_moe(xcat, modsel, g2, wr, upper, wg, wu, wd, layer, *, rt, seq, ctx, with_ctx, final_gain=None):
    b = xcat.shape[0]
    ne = N_EXPERTS
    rt_lat = 2 * rt if seq % (2 * rt) == 0 else rt
    h, aff, pos = _router(xcat, modsel, g2, wr, upper, rt=rt_lat, n=seq, row_tile0=0, sel=0)
    cap = max(1, EC_FACTOR * seq // N_EXPERTS)
    pos = pos.reshape(b, ne, 1, seq)
    tb = min(MOE_TOKEN_BLOCK, ctx)
    win = dict(tb=tb, w=min(MOE_WINDOW, cap))
    start, dense_group, dense_block = _moe_plan(pos, cap=cap, **win)
    xs, gate = _moe_gather(start, dense_group, pos, aff.reshape(b, ne, 1, seq), h, cap=cap, **win)
    if not with_ctx:
        (y,) = _moe_ffn([(xs, gate, 0)], modsel, wg, wu, wd, layer)
        return _moe_scatter(start, dense_block, pos, y, xcat, bb=1, row0=0, final_gain=final_gain, **win)
    hc, affc, posc = _router(xcat, modsel, g2, wr, upper, rt=rt, n=ctx, row_tile0=seq // rt, sel=1)
    capc = max(1, EC_FACTOR * ctx // N_EXPERTS)
    offs = (jnp.arange(b, dtype=jnp.int32) * capc)[:, None, None]
    posc = jnp.where(posc >= 0, posc + offs, -1)
    posc = jnp.transpose(posc, (1, 0, 2)).reshape(1, ne, 1, b * ctx)
    affc = jnp.transpose(affc, (1, 0, 2)).reshape(1, ne, 1, b * ctx)
    winc = dict(tb=tb, w=min(MOE_WINDOW, b * capc))
    assert tb == ctx
    startc, dense_groupc, dense_blockc = _moe_plan(posc, cap=b * capc, **winc)
    xsc, gatec = _moe_gather(startc, dense_groupc, posc, affc, hc.reshape(1, b * ctx, -1), cap=b * capc, **winc)
    y, yc = _moe_ffn([(xs, gate, 0), (xsc, gatec, 1)], modsel, wg, wu, wd, layer)
    xcat = _moe_scatter(start, dense_block, pos, y, xcat, bb=1, row0=0, final_gain=final_gain, **win)
    return _moe_scatter(startc, dense_blockc, posc, yc, xcat, bb=b, row0=seq, **winc)


def _rope_tables(seq, ctx):
    rows = seq // GRID_W
    row_ids = jnp.repeat(jnp.arange(rows, dtype=F32), GRID_W)
    col_ids = jnp.tile(jnp.arange(GRID_W, dtype=F32), rows)
    half = A_HEAD_DIM // 2
    inv = ROPE_THETA ** (-jnp.arange(0, half, 2, dtype=F32) / half)
    ang_r = row_ids[:, None] * inv
    ang_c = col_ids[:, None] * inv
    cos = jnp.concatenate([jnp.cos(ang_r)] * 2 + [jnp.cos(ang_c)] * 2, axis=1)
    sin = jnp.concatenate([-jnp.sin(ang_r), jnp.sin(ang_r), -jnp.sin(ang_c), jnp.sin(ang_c)], axis=1)
    cos = jnp.tile(cos, (1, A_HEADS))
    sin = jnp.tile(sin, (1, A_HEADS))
    cos = jnp.concatenate([cos, jnp.ones((ctx, A_Q), F32)], axis=0)
    sin = jnp.concatenate([sin, jnp.zeros((ctx, A_Q), F32)], axis=0)
    return cos, sin


def kernel(x, c, ctx, c_ctx, w_mod, b_mod, g_norm1, g_norm2, w_in_even, w_out_even, g_qnorm, g_knorm, w_conv,
           b_gate, g_hnorm, w_in_odd, w_pool_grp, s_pool, w_router, w_exp_gate, w_exp_up, w_exp_down, g_final):
    b, seq, dm = x.shape
    n_ctx = ctx.shape[1]
    depth = w_mod.shape[0]
    rt = 256 if (seq % 256 == 0 and n_ctx % 256 == 0) else 128
    assert seq % rt == 0 and n_ctx % rt == 0 and seq % n_ctx == 0 and seq % GRID_W == 0
    nlt = seq // rt
    nt = (seq + n_ctx) // rt

    xcat = jnp.concatenate([x, ctx, jnp.zeros((b, seq - n_ctx, dm), F32)], axis=1)

    rows = -(-(b + 1) // 8) * 8
    cond = jnp.zeros((rows, dm), F32).at[:b].set(c).at[b].set(c_ctx)
    mods = _adaln(cond, w_mod, b_mod)
    lat = mods[:, :b].reshape(depth, b, 1, 6, dm)
    cx = jnp.broadcast_to(mods[:, b].reshape(depth, 1, 1, 6, dm), (depth, b, 1, 6, dm))
    modsel = jnp.concatenate([lat, cx], axis=2)

    cos, sin = _rope_tables(seq, n_ctx)
    lane = jnp.arange(A_Q)
    gmat = (lane[:, None] // A_HEAD_DIM == lane[None, :] // A_HEAD_DIM).astype(BF16)
    blk = min(256, n_ctx)
    tri = jnp.arange(blk)
    upper = (tri[:, None] <= tri[None, :]).astype(BF16)

    wg, wu, wd = w_exp_gate, w_exp_up, w_exp_down
    wr = jnp.pad(w_router, ((0, 0), (0, 0), (0, GATE_PAD - N_EXPERTS)))

    for i in range(depth):
        ctx_next = any(j % 2 == 0 for j in range(i + 1, depth))
        n_upd = nt if ctx_next else nlt
        ms = modsel[i]
        if i % 2 == 0:
            e = i // 2
            w_in = w_in_even[e]
            pad = jnp.zeros((dm, GATE_PAD - N_GATES), F32)
            w_in = jnp.concatenate([w_in, pad], axis=1).astype(BF16)
            bg = jnp.concatenate([b_gate[e], jnp.zeros((GATE_PAD - N_GATES,), F32)])[None]
            q, k, v, qk, vbt, ob, gt, gc, gr = _inproj(
                xcat, ms, g_norm1[i][None], w_in, cos, sin,
                jnp.tile(g_qnorm[e], A_HEADS)[None], jnp.tile(g_knorm[e], A_KV_HEADS)[None], gmat, bg,
                rt=rt, ta=seq + n_ctx, n_lat_tiles=nlt)
            oa = _attention(q, k, v, rt=rt, n_tiles=n_upd, n_lat_tiles=nlt, seq=seq)
            qct = _short_conv(qk, w_conv[e], seq=seq, keys=False)
            kc = _short_conv(qk, w_conv[e], seq=seq, keys=True)
            gates_row = jnp.swapaxes(jnp.concatenate([gt[:, :, :N_GATES], gc[:, :, :N_GATES]], axis=2), 1, 2)
            hf, hb = _mlstm(qct, kc, vbt, gates_row, gr, seq=seq, blk=rt)
            xcat = _outproj(xcat, ms, oa, hf, hb, ob, g_hnorm[e][None], w_out_even[e].astype(BF16),
                            rt=rt, n_tiles=n_upd, n_lat_tiles=nlt)
        else:
            o = i // 2
            w_in = w_in_odd[o].astype(BF16)
            wgrp = w_pool_grp[o].astype(BF16)
            rt_lat = 2 * rt if nlt % 2 == 0 else rt
            u = _mod_matmul(xcat, ms, g_norm1[i][None], w_in, rt=rt_lat, tile0=0, n_tiles=seq // rt_lat, sel=0)
            if ctx_next:
                uc = _mod_matmul(xcat, ms, g_norm1[i][None], w_in, rt=rt, tile0=nlt, n_tiles=nt - nlt, sel=1)
            xcat = _pool(u, xcat, ms, wgrp, s_pool[o][None], n=seq, row_block=0, sel=0)
            if ctx_next:
                xcat = _pool(uc, xcat, ms, wgrp, s_pool[o][None], n=n_ctx, row_block=seq // n_ctx, sel=1)
        xcat = _ec_moe(xcat, ms, g_norm2[i][None], wr[i], upper, wg, wu, wd, i,
                       rt=rt, seq=seq, ctx=n_ctx, with_ctx=ctx_next,
                       final_gain=g_final[None] if i == depth - 1 else None)
    return xcat
```

```python
import functools

import jax
import jax.numpy as jnp
from jax import lax
from jax.experimental import pallas as pl
from jax.experimental.pallas import tpu as pltpu

D_MODEL = 1024
GRID_W = 64
A_HEADS = 8
A_KV_HEADS = 2
A_HEAD_DIM = 64
ROPE_THETA = 10000.0
B_HEADS = 4
B_HEAD_DIM = 128
B_CHUNK = 128
POOL_WINDOWS = (2, 4, 8, 16)
POOL_GROUP = D_MODEL // 4
N_EXPERTS = 16
EC_FACTOR = 2
NORM_EPS = 1e-6

A_Q = A_HEADS * A_HEAD_DIM
A_KV = A_KV_HEADS * A_HEAD_DIM
B_W = B_HEADS * B_HEAD_DIM
N_GATES = 2 * 2 * B_HEADS
GATE_PAD = 128
C_QA, C_KA, C_VA = 0, A_Q, A_Q + A_KV
C_QK = A_Q + 2 * A_KV
C_VB = C_QK + 2 * B_W
C_OB = C_VB + B_W
C_GT = C_OB + B_W
EVEN_COLS = C_GT + GATE_PAD
POOL_HALO = 16

F32 = jnp.float32
BF16 = jnp.bfloat16
HIGHEST = lax.Precision.HIGHEST
NT_DIMS = (((1,), (1,)), ((), ()))
TN_DIMS = (((0,), (0,)), ((), ()))
VMEM_LIMIT = 56 * 1024 * 1024


def _params(*sem):
    return pltpu.CompilerParams(dimension_semantics=sem, vmem_limit_bytes=VMEM_LIMIT)


def _modulate(x, g, shift, scale):
    y = x * lax.rsqrt(jnp.mean(x * x, axis=-1, keepdims=True) + NORM_EPS)
    return (y * g) * (1.0 + scale) + shift


def _silu(x):
    return x * jax.nn.sigmoid(x)


def _adaln_kernel(c_ref, w_ref, b_ref, o_ref):
    s = _silu(c_ref[...])
    o_ref[0] = jnp.dot(s, w_ref[0], precision=HIGHEST, preferred_element_type=F32) + b_ref[0]


def _adaln(cond, w_mod, b_mod):
    depth, dm, n6 = w_mod.shape
    rows = cond.shape[0]
    tn = 1536
    return pl.pallas_call(
        _adaln_kernel,
        grid=(depth, n6 // tn),
        in_specs=[pl.BlockSpec((rows, dm), lambda l, j: (0, 0)),
                  pl.BlockSpec((1, dm, tn), lambda l, j: (l, 0, j)),
                  pl.BlockSpec((1, 1, tn), lambda l, j: (l, 0, j))],
        out_specs=pl.BlockSpec((1, rows, tn), lambda l, j: (l, 0, j)),
        out_shape=jax.ShapeDtypeStruct((depth, rows, n6), F32),
        compiler_params=_params("arbitrary", "arbitrary"),
        name="adaln",
    )(cond, w_mod, b_mod.reshape(depth, 1, n6))


def _group_mean_sq(x, gmat, width):
    s = jnp.dot((x * x).astype(BF16), gmat, preferred_element_type=F32)
    return s * (1.0 / width)


def _dot_exact_lhs(mat, x):
    x1 = x.astype(BF16)
    r1 = x - x1.astype(F32)
    x2 = r1.astype(BF16)
    x3 = (r1 - x2.astype(F32)).astype(BF16)
    return (jnp.dot(mat, x1, preferred_element_type=F32) + jnp.dot(mat, x2, preferred_element_type=F32)
            + jnp.dot(mat, x3, preferred_element_type=F32))


def _rope(x, cos, sin):
    w = x.shape[1]
    lane = lax.broadcasted_iota(jnp.int32, x.shape, 1)
    first = (lane % 32) < 16
    partner = jnp.where(first, pltpu.roll(x, w - 16, 1), pltpu.roll(x, 16, 1))
    return x * cos + partner * sin


def _inproj_kernel(x_ref, m_ref, g1_ref, w_ref, cos_ref, sin_ref, gq_ref, gk_ref, gm_ref, bg_ref, lp_ref, ls_ref,
                   q_ref, k_ref, v_ref, qk_ref, vb_ref, ob_ref, gt_ref, gc_ref, gr_ref):
    m = m_ref[0, 0]
    h = _modulate(x_ref[0], g1_ref[...], m[0:1], m[1:2]).astype(BF16)

    def mm(lo, hi):
        return jnp.dot(h, w_ref[:, lo:hi], preferred_element_type=F32)

    cos = cos_ref[...]
    sin = sin_ref[...]
    gm = gm_ref[...]
    qa = mm(C_QA, C_KA)
    qa = qa * lax.rsqrt(_group_mean_sq(qa, gm, A_HEAD_DIM) + NORM_EPS) * gq_ref[...]
    q_ref[0] = (_rope(qa, cos, sin) * (A_HEAD_DIM ** -0.5)).astype(BF16)
    ka = mm(C_KA, C_VA)
    ka = ka * lax.rsqrt(_group_mean_sq(ka, gm[:A_KV, :A_KV], A_HEAD_DIM) + NORM_EPS) * gk_ref[...]
    k_ref[0] = _rope(ka, cos[:, :A_KV], sin[:, :A_KV]).astype(BF16)
    va = mm(C_VA, C_QK).astype(BF16)
    ones = jnp.ones((va.shape[0], A_HEAD_DIM), BF16)
    v_ref[0] = jnp.concatenate(
        [piece for g in range(A_KV_HEADS) for piece in (va[:, g * A_HEAD_DIM:(g + 1) * A_HEAD_DIM], ones)], axis=1)
    qk_ref[0] = mm(C_QK, C_VB).astype(BF16)
    vb_ref[0] = mm(C_VB, C_OB).T.astype(BF16)
    ob_ref[0] = mm(C_OB, C_GT).astype(BF16)
    gt = mm(C_GT, EVEN_COLS) + bg_ref[...]
    lane = lax.broadcasted_iota(jnp.int32, gt.shape, 1)
    log_sig = jnp.minimum(gt, 0.0) - jnp.log1p(jnp.exp(-jnp.abs(gt)))
    gt = jnp.where((lane % (2 * B_HEADS)) >= B_HEADS, log_sig, gt)
    gt_ref[0] = gt
    prefix = _dot_exact_lhs(lp_ref[...], gt)
    suffix = _dot_exact_lhs(ls_ref[...], gt)
    cum = jnp.where((lane % (4 * B_HEADS)) < 2 * B_HEADS, prefix, suffix)
    gc_ref[0] = cum
    gr_ref[0] = gt - pltpu.roll(cum, GATE_PAD - B_HEADS, 1)


def _inproj(xcat, modsel, g1, w, cos, sin, gq, gk, gmat, bgate, *, rt, ta, n_lat_tiles):
    b, _, dm = xcat.shape
    nt = ta // rt
    row = lambda bb, r: (bb, r, 0)
    full = lambda bb, r: (0, 0)
    out_widths = (A_Q, A_KV, 2 * A_KV, 2 * B_W, B_W, B_W, GATE_PAD, GATE_PAD, GATE_PAD)
    out_dtypes = (BF16, BF16, BF16, BF16, BF16, BF16, F32, F32, F32)
    out_specs = [pl.BlockSpec((1, rt, wd), row) for wd in out_widths]
    out_shapes = [jax.ShapeDtypeStruct((b, ta, wd), dt) for wd, dt in zip(out_widths, out_dtypes)]
    vb_slot = 4
    out_specs[vb_slot] = pl.BlockSpec((1, B_W, rt), lambda bb, r: (bb, 0, r))
    out_shapes[vb_slot] = jax.ShapeDtypeStruct((b, B_W, ta), BF16)
    idx = jnp.arange(rt)
    same_chunk = idx[:, None] // B_CHUNK == idx[None, :] // B_CHUNK
    lower = jnp.logical_and(same_chunk, idx[None, :] <= idx[:, None]).astype(BF16)
    upper = jnp.logical_and(same_chunk, idx[None, :] >= idx[:, None]).astype(BF16)
    return pl.pallas_call(
        _inproj_kernel,
        grid=(b, nt),
        in_specs=[pl.BlockSpec((1, rt, dm), row),
                  pl.BlockSpec((1, 1, 6, dm), lambda bb, r: (bb, (r >= n_lat_tiles).astype(jnp.int32), 0, 0)),
                  pl.BlockSpec((1, dm), full),
                  pl.BlockSpec((dm, EVEN_COLS), full),
                  pl.BlockSpec((rt, A_Q), lambda bb, r: (r, 0)),
                  pl.BlockSpec((rt, A_Q), lambda bb, r: (r, 0)),
                  pl.BlockSpec((1, A_Q), full),
                  pl.BlockSpec((1, A_KV), full),
                  pl.BlockSpec((A_Q, A_Q), full),
                  pl.BlockSpec((1, GATE_PAD), full),
                  pl.BlockSpec((rt, rt), full),
                  pl.BlockSpec((rt, rt), full)],
        out_specs=out_specs,
        out_shape=out_shapes,
        compiler_params=_params("parallel", "arbitrary"),
        name="even_inproj",
    )(xcat, modsel, g1, w, cos, sin, gq, gk, gmat, bgate, lower, upper)


def _attn_kernel(q_ref, k_ref, v_ref, o_ref, *, n_lat_tiles, n_tiles, seq):
    group = A_HEADS // A_KV_HEADS

    def attend(key_lo):
        q = q_ref[0]
        for hd in range(A_HEADS):
            g = hd // group
            qh = q[:, hd * A_HEAD_DIM:(hd + 1) * A_HEAD_DIM]
            kg = k_ref[0, key_lo:, g * A_HEAD_DIM:(g + 1) * A_HEAD_DIM]
            vg = v_ref[0, key_lo:, 2 * g * A_HEAD_DIM:2 * (g + 1) * A_HEAD_DIM]
            s = lax.dot_general(qh, kg, NT_DIMS, preferred_element_type=F32)
            p = jnp.exp(s - jnp.max(s, axis=-1, keepdims=True))
            o = jnp.dot(p.astype(BF16), vg, preferred_element_type=F32)
            o = o[:, :A_HEAD_DIM] / o[:, A_HEAD_DIM:]
            o_ref[0, :, hd * A_HEAD_DIM:(hd + 1) * A_HEAD_DIM] = o.astype(BF16)

    if n_tiles == n_lat_tiles:
        attend(0)
    else:
        r = pl.program_id(1)
        pl.when(r < n_lat_tiles)(lambda: attend(0))
        pl.when(r >= n_lat_tiles)(lambda: attend(seq))


def _attention(q, k, v, *, rt, n_tiles, n_lat_tiles, seq):
    b, ta, _ = q.shape
    return pl.pallas_call(
        functools.partial(_attn_kernel, n_lat_tiles=n_lat_tiles, n_tiles=n_tiles, seq=seq),
        grid=(b, n_tiles),
        in_specs=[pl.BlockSpec((1, rt, A_Q), lambda bb, r: (bb, r, 0)),
                  pl.BlockSpec((1, ta, A_KV), lambda bb, r: (bb, 0, 0)),
                  pl.BlockSpec((1, ta, 2 * A_KV), lambda bb, r: (bb, 0, 0))],
        out_specs=pl.BlockSpec((1, rt, A_Q), lambda bb, r: (bb, r, 0)),
        out_shape=jax.ShapeDtypeStruct((b, n_tiles * rt, A_Q), BF16),
        compiler_params=_params("parallel", "arbitrary"),
        name="gqa_attention",
    )(q, k, v)


def _conv_kernel(x_ref, w_ref, o_ref, *, seq, keys):
    x = x_ref[0].astype(F32)
    ta = x.shape[0]
    w = w_ref[...]
    row = lax.broadcasted_iota(jnp.int32, (ta, 1), 0)
    prev = jnp.where(jnp.logical_or(row == 0, row == seq), 0.0, pltpu.roll(x, 1, 0))
    nxt = jnp.where(jnp.logical_or(row == seq - 1, row == ta - 1), 0.0, pltpu.roll(x, ta - 1, 0))
    y = _silu(prev * w[0:1] + x * w[1:2] + nxt * w[2:3])
    if keys:
        o_ref[0] = (y * (B_HEAD_DIM ** -0.5)).astype(BF16)
    else:
        o_ref[0] = y.T.astype(BF16)


def _short_conv(qk, w_conv, *, seq, keys):
    b, ta, _ = qk.shape
    ct = 256
    c0 = B_W // ct if keys else 0
    if keys:
        out_spec = pl.BlockSpec((1, ta, ct), lambda bb, j: (bb, 0, j))
        out_shape = jax.ShapeDtypeStruct((b, ta, B_W), BF16)
    else:
        out_spec = pl.BlockSpec((1, ct, ta), lambda bb, j: (bb, j, 0))
        out_shape = jax.ShapeDtypeStruct((b, B_W, ta), BF16)
    return pl.pallas_call(
        functools.partial(_conv_kernel, seq=seq, keys=keys),
        grid=(b, B_W // ct),
        in_specs=[pl.BlockSpec((1, ta, ct), lambda bb, j: (bb, 0, c0 + j)),
                  pl.BlockSpec((w_conv.shape[0], ct), lambda bb, j: (0, c0 + j))],
        out_specs=out_spec,
        out_shape=out_shape,
        compiler_params=_params("parallel", "arbitrary"),
        name="mlstm_conv_k" if keys else "mlstm_conv_q",
    )(qk, w_conv)


def _mlstm_chain(qt, k, vat, ig, b_row, r_col, total, ca, m_old, tri):
    log_d = jnp.where(tri, b_row + r_col, -jnp.inf)
    log_inter = m_old + b_row
    m_t = jnp.maximum(log_inter, jnp.max(log_d, axis=0, keepdims=True))
    w_intra = jnp.exp(log_d - m_t) * jnp.dot(k, qt, preferred_element_type=F32)
    w_inter = jnp.exp(log_inter - m_t)
    nd = (w_inter * jnp.dot(ca.astype(BF16), qt, preferred_element_type=F32)
          + jnp.dot(vat, w_intra.astype(BF16), preferred_element_type=F32))
    dv = nd.shape[0] // 2
    h = nd[:dv] / jnp.maximum(jnp.abs(nd[dv:]), jnp.exp(-m_t))
    log_w = total - b_row + ig
    m_new = jnp.maximum(m_old + total, jnp.max(log_w, axis=-1, keepdims=True))
    w_s = jnp.exp(log_w - m_new)
    decay = jnp.exp(m_old + total - m_new)
    ca_new = decay * ca + jnp.dot((vat * w_s).astype(BF16), k, preferred_element_type=F32)
    return h, ca_new, m_new


def _mlstm_kernel(qf_ref, kf_ref, vf_ref, grf_ref, gcf_ref, qb_ref, kb_ref, vb_ref, grb_ref, gcb_ref,
                  hf_ref, hb_ref, c_ref, m_ref):
    @pl.when(pl.program_id(1) == 0)
    def _():
        c_ref[...] = jnp.zeros(c_ref.shape, F32)
        m_ref[...] = jnp.full(m_ref.shape, -1e30, F32)

    t = B_CHUNK
    n_chunks = kf_ref.shape[1] // t
    src = lax.broadcasted_iota(jnp.int32, (t, t), 0)
    tgt = lax.broadcasted_iota(jnp.int32, (t, t), 1)
    ones = jnp.ones((B_HEAD_DIM, t), BF16)
    dirs = ((qf_ref, kf_ref, vf_ref, grf_ref, gcf_ref, hf_ref, src <= tgt, t - 1, range(n_chunks)),
            (qb_ref, kb_ref, vb_ref, grb_ref, gcb_ref, hb_ref, src >= tgt, 0, range(n_chunks - 1, -1, -1)))
    outputs = []
    states = []
    for d, (q_ref, k_ref, v_ref, gr_ref, gc_ref, h_ref, tri, last, order) in enumerate(dirs):
        for hd in range(B_HEADS):
            sl = slice(hd * B_HEAD_DIM, (hd + 1) * B_HEAD_DIM)
            gi = d * 2 * B_HEADS + hd
            fi = gi + B_HEADS
            si = d * B_HEADS + hd
            ca = c_ref[si]
            m = m_ref[si][:, 0:1]
            for ci in order:
                rows = slice(ci * t, (ci + 1) * t)
                ig = gr_ref[0, gi:gi + 1, rows]
                b_row = gr_ref[0, N_GATES + fi:N_GATES + fi + 1, rows]
                r_col = gc_ref[0, rows, gi:gi + 1]
                vat = jnp.concatenate([v_ref[0, sl, rows], ones], axis=0)
                h, ca, m = _mlstm_chain(q_ref[0, sl, rows], k_ref[0, rows, sl], vat, ig, b_row, r_col,
                                        b_row[:, last:last + 1], ca, m, tri)
                outputs.append((h_ref, rows, sl, h))
            states.append((si, ca, m))
    for h_ref, rows, sl, h in outputs:
        h_ref[0, sl, rows] = h.astype(BF16)
    for si, ca, m in states:
        c_ref[si] = ca
        m_ref[si] = jnp.broadcast_to(m, (1, 128))


def _mlstm(qct, kc, vbt, gates_row, gates_col, *, seq, blk):
    b, ta, _ = kc.shape
    nc = ta // blk
    ncl = seq // blk
    ncc = nc - ncl
    fwd = lambda j: jnp.where(j < ncc, ncl + j, j - ncc)
    bwd = lambda j: jnp.where(j < ncc, nc - 1 - j, ncl - 1 - (j - ncc))

    def specs(order):
        feature_major = pl.BlockSpec((1, B_W, blk), lambda bb, j: (bb, 0, order(j)))
        return [feature_major,
                pl.BlockSpec((1, blk, B_W), lambda bb, j: (bb, order(j), 0)),
                feature_major,
                pl.BlockSpec((1, 2 * N_GATES, blk), lambda bb, j: (bb, 0, order(j))),
                pl.BlockSpec((1, blk, GATE_PAD), lambda bb, j: (bb, order(j), 0))]

    n_state = 2 * B_HEADS
    ops = (qct, kc, vbt, gates_row, gates_col)
    return pl.pallas_call(
        _mlstm_kernel,
        grid=(b, nc),
        in_specs=specs(fwd) + specs(bwd),
        out_specs=[pl.BlockSpec((1, B_W, blk), lambda bb, j: (bb, 0, fwd(j))),
                   pl.BlockSpec((1, B_W, blk), lambda bb, j: (bb, 0, bwd(j)))],
        out_shape=[jax.ShapeDtypeStruct((b, B_W, ta), BF16)] * 2,
        scratch_shapes=[pltpu.VMEM((n_state, 2 * B_HEAD_DIM, B_HEAD_DIM), F32),
                        pltpu.VMEM((n_state, 1, 128), F32)],
        compiler_params=_params("parallel", "arbitrary"),
        name="mlstm_scan",
    )(*ops, *ops)


def _outproj_kernel(x_ref, m_ref, oa_ref, hf_ref, hb_ref, ob_ref, ghn_ref, w_ref, o_ref):
    hm = (hf_ref[0].astype(F32) + hb_ref[0].astype(F32)).T
    ob = ob_ref[0].astype(F32)
    ghn = ghn_ref[...]
    y = jnp.dot(oa_ref[0], w_ref[0:A_Q, :], preferred_element_type=F32)
    for hd in range(B_HEADS):
        sl = slice(hd * B_HEAD_DIM, (hd + 1) * B_HEAD_DIM)
        z = hm[:, sl]
        z = z * lax.rsqrt(jnp.mean(z * z, axis=-1, keepdims=True) + NORM_EPS) * ghn[:, sl]
        z = (z * jax.nn.sigmoid(ob[:, sl])).astype(BF16)
        y = y + jnp.dot(z, w_ref[A_Q + hd * B_HEAD_DIM:A_Q + (hd + 1) * B_HEAD_DIM, :],
                        preferred_element_type=F32)
    o_ref[0] = x_ref[0] + m_ref[0, 0][2:3] * y


def _outproj(xcat, modsel, oa, hf, hb, ob, ghn, w, *, rt, n_tiles, n_lat_tiles):
    b, ta, dm = xcat.shape
    row = lambda bb, r: (bb, r, 0)
    full = lambda bb, r: (0, 0)
    return pl.pallas_call(
        _outproj_kernel,
        grid=(b, n_tiles),
        in_specs=[pl.BlockSpec((1, rt, dm), row),
                  pl.BlockSpec((1, 1, 6, dm), lambda bb, r: (bb, (r >= n_lat_tiles).astype(jnp.int32), 0, 0)),
                  pl.BlockSpec((1, rt, A_Q), row),
                  pl.BlockSpec((1, B_W, rt), lambda bb, r: (bb, 0, r)),
                  pl.BlockSpec((1, B_W, rt), lambda bb, r: (bb, 0, r)),
                  pl.BlockSpec((1, rt, B_W), row),
                  pl.BlockSpec((1, B_W), full),
                  pl.BlockSpec((A_Q + B_W, dm), full)],
        out_specs=pl.BlockSpec((1, rt, dm), row),
        out_shape=jax.ShapeDtypeStruct(xcat.shape, F32),
        input_output_aliases={0: 0},
        compiler_params=_params("parallel", "arbitrary"),
        name="even_outproj",
    )(xcat, modsel, oa, hf, hb, ob, ghn, w)


def _modmm_kernel(x_ref, m_ref, g_ref, w_ref, o_ref):
    m = m_ref[0, 0]
    h = _modulate(x_ref[0], g_ref[...], m[0:1], m[1:2]).astype(BF16)
    o_ref[0] = jnp.dot(h, w_ref[...], preferred_element_type=F32)


def _mod_matmul(xcat, modsel, g, w, *, rt, tile0, n_tiles, sel):
    b, _, dm = xcat.shape
    n_out = w.shape[1]
    return pl.pallas_call(
        _modmm_kernel,
        grid=(b, n_tiles),
        in_specs=[pl.BlockSpec((1, rt, dm), lambda bb, r: (bb, tile0 + r, 0)),
                  pl.BlockSpec((1, 1, 6, dm), lambda bb, r: (bb, sel, 0, 0)),
                  pl.BlockSpec((1, dm), lambda bb, r: (0, 0)),
                  pl.BlockSpec((dm, n_out), lambda bb, r: (0, 0))],
        out_specs=pl.BlockSpec((1, rt, n_out), lambda bb, r: (bb, r, 0)),
        out_shape=jax.ShapeDtypeStruct((b, n_tiles * rt, n_out), F32),
        compiler_params=_params("parallel", "arbitrary"),
        name="odd_inproj",
    )(xcat, modsel, g, w)


def _pool_kernel(u_ref, x_ref, m_ref, wg_ref, sp_ref, o_ref, pad_ref, sa_ref, sb_ref, *, n, rc):
    g = pl.program_id(1)
    halo = POOL_HALO
    ext = n + 2 * halo
    zeros = jnp.zeros((halo, pad_ref.shape[1]), F32)
    pad_ref[0:halo, :] = zeros
    pad_ref[halo + n:ext, :] = zeros
    pad_ref[halo:halo + n, :] = u_ref[0]
    for ref in (pad_ref, sa_ref, sb_ref):
        ref[ext:ext + halo, :] = zeros
    gate = m_ref[0, 0][2:3]
    chunks = [(r0, min(rc, ext - r0)) for r0 in range(0, ext, rc)]

    def pair_sums(src, dst, shift):
        for r0, size in chunks:
            dst[r0:r0 + size, :] = src[r0:r0 + size, :] + src[r0 + shift:r0 + shift + size, :]

    for gi, win in enumerate(POOL_WINDOWS):
        @pl.when(g == gi)
        def _(win=win):
            lo = win // 2
            hi = win - 1 - lo
            sums, width = pad_ref, 1
            for dst in (sa_ref, sb_ref, sa_ref, sb_ref):
                if width == win:
                    break
                pair_sums(sums, dst, width)
                sums, width = dst, 2 * width
            assert width == win
            for r0 in range(0, n, rc):
                base = halo + r0
                acc = sums[base - lo:base - lo + rc, :]
                t = r0 + lax.broadcasted_iota(jnp.int32, (rc, 1), 0)
                cnt = jnp.minimum(t + hi, n - 1) - jnp.maximum(t - lo, 0) + 1
                p = acc / cnt.astype(F32) - pad_ref[base:base + rc, :]
                y = jnp.dot(p.astype(BF16), wg_ref[0], preferred_element_type=F32) * sp_ref[...]
                o_ref[0, r0:r0 + rc, :] = x_ref[0, r0:r0 + rc, :] + gate * y


def _pool(u, xcat, modsel, wgrp, spool, *, n, row_block, sel):
    b, _, dm = xcat.shape
    pg = POOL_GROUP
    rc = min(256, n)
    blk = lambda bb, g: (bb, row_block, g)
    return pl.pallas_call(
        functools.partial(_pool_kernel, n=n, rc=rc),
        grid=(b, dm // pg),
        in_specs=[pl.BlockSpec((1, n, pg), lambda bb, g: (bb, 0, g)),
                  pl.BlockSpec((1, n, pg), blk),
                  pl.BlockSpec((1, 1, 6, pg), lambda bb, g: (bb, sel, 0, g)),
                  pl.BlockSpec((1, pg, pg), lambda bb, g: (g, 0, 0)),
                  pl.BlockSpec((1, pg), lambda bb, g: (0, g))],
        out_specs=pl.BlockSpec((1, n, pg), blk),
        out_shape=jax.ShapeDtypeStruct(xcat.shape, F32),
        scratch_shapes=[pltpu.VMEM((n + 3 * POOL_HALO, pg), F32)] * 3,
        input_output_aliases={1: 0},
        compiler_params=_params("parallel", "arbitrary"),
        name="pool_mixer",
    )(u, xcat, modsel, wgrp, spool)


def _lane_cumsum(x01, upper, blk):
    n = x01.shape[1]
    out = []
    carry = jnp.zeros((x01.shape[0], 1), F32)
    for j in range(0, n, blk):
        cs = jnp.dot(x01[:, j:j + blk].astype(BF16), upper, preferred_element_type=F32) + carry
        carry = cs[:, blk - 1:blk]
        out.append(cs)
    return jnp.concatenate(out, axis=1) if len(out) > 1 else out[0]


def _router_kernel(x_ref, m_ref, g_ref, wr_ref, h_ref, lg_ref):
    m = m_ref[0, 0]
    h = _modulate(x_ref[0], g_ref[...], m[3:4], m[4:5])
    h_hi = h.astype(BF16)
    h_ref[0] = h_hi
    h_lo = (h - h_hi.astype(F32)).astype(BF16)
    wr = wr_ref[...]
    w_hi = wr.astype(BF16)
    w_lo = (wr - w_hi.astype(F32)).astype(BF16)
    lg = (jnp.dot(h_hi, w_hi, preferred_element_type=F32)
          + (jnp.dot(h_hi, w_lo, preferred_element_type=F32) + jnp.dot(h_lo, w_hi, preferred_element_type=F32)))
    lg_ref[0] = lg.T[:lg_ref.shape[1]]


def _topk_kernel(lg_ref, up_ref, aff_ref, pos_ref, *, cap):
    lg = lg_ref[...]
    nb, ne, n = lg.shape
    e = jnp.exp(lg - jnp.max(lg, axis=1, keepdims=True))
    aff3 = e / jnp.sum(e, axis=1, keepdims=True)
    aff_ref[...] = aff3
    aff = aff3.reshape(nb * ne, n)

    def bit_step(i, thr):
        cand = thr | lax.shift_left(jnp.int32(1), 30 - i)
        cnt = jnp.sum(jnp.where(aff >= pltpu.bitcast(cand, F32), 1.0, 0.0), axis=-1, keepdims=True)
        return jnp.where(cnt >= cap, cand, thr)

    thr = lax.fori_loop(0, 31, bit_step, jnp.zeros((aff.shape[0], 1), jnp.int32))
    thr = pltpu.bitcast(thr, F32)
    above = aff > thr
    tied = aff == thr
    room = cap - jnp.sum(jnp.where(above, 1.0, 0.0), axis=-1, keepdims=True)
    upper = up_ref[...]
    blk = upper.shape[0]
    tie_rank = _lane_cumsum(jnp.where(tied, 1.0, 0.0), upper, blk)
    sel = jnp.logical_or(above, jnp.logical_and(tied, tie_rank <= room))
    slot = _lane_cumsum(jnp.where(sel, 1.0, 0.0), upper, blk) - 1.0
    pos_ref[...] = jnp.where(sel, slot, -1.0).astype(jnp.int32).reshape(nb, ne, n)


def _router(xcat, modsel, g2, wr, upper, *, rt, n, row_tile0, sel):
    b, ta, dm = xcat.shape
    n_tiles = n // rt
    cap = max(1, EC_FACTOR * n // N_EXPERTS)
    ne = N_EXPERTS
    h, logits = pl.pallas_call(
        _router_kernel,
        grid=(b, n_tiles),
        in_specs=[pl.BlockSpec((1, rt, dm), lambda bb, r: (bb, row_tile0 + r, 0)),
                  pl.BlockSpec((1, 1, 6, dm), lambda bb, r: (bb, sel, 0, 0)),
                  pl.BlockSpec((1, dm), lambda bb, r: (0, 0)),
                  pl.BlockSpec(wr.shape, lambda bb, r: (0, 0))],
        out_specs=[pl.BlockSpec((1, rt, dm), lambda bb, r: (bb, r, 0)),
                   pl.BlockSpec((1, ne, rt), lambda bb, r: (bb, 0, r))],
        out_shape=[jax.ShapeDtypeStruct((b, n, dm), BF16),
                   jax.ShapeDtypeStruct((b, ne, n), F32)],
        compiler_params=_params("parallel", "arbitrary"),
        name="moe_router",
    )(xcat, modsel, g2, wr)
    whole = lambda i: (0, 0, 0)
    aff, pos = pl.pallas_call(
        functools.partial(_topk_kernel, cap=cap),
        grid=(1,),
        in_specs=[pl.BlockSpec((b, ne, n), whole),
                  pl.BlockSpec(upper.shape, lambda i: (0, 0))],
        out_specs=[pl.BlockSpec((b, ne, n), whole),
                   pl.BlockSpec((b, ne, n), whole)],
        out_shape=[jax.ShapeDtypeStruct((b, ne, n), F32),
                   jax.ShapeDtypeStruct((b, ne, n), jnp.int32)],
        compiler_params=_params("arbitrary"),
        name="moe_topk",
    )(logits, upper)
    return h, aff, pos


MOE_EXPERT_BLOCK = 4
MOE_TOKEN_BLOCK = 256
MOE_WINDOW = 64


def _slot_onehot(pos_ref, cap):
    hits = []
    for i in range(pos_ref.shape[1]):
        pos = pos_ref[0, i]
        hits.append(lax.broadcasted_iota(jnp.int32, (cap, pos.shape[1]), 0) == pos)
    pick = jnp.concatenate([jnp.where(hit, 1.0, 0.0).astype(BF16) for hit in hits], axis=0)
    return pick, hits


def _moe_plan(pos, *, cap, tb, w):
    nb, ne, _, n = pos.shape
    nk = n // tb
    assert w % 16 == 0 and (cap - w) % 16 == 0 and n % tb == 0
    cnt = jnp.sum((pos.reshape(nb, ne, nk, tb) >= 0).astype(jnp.int32), axis=-1)
    first = jnp.cumsum(cnt, axis=-1) - cnt
    start = jnp.minimum((first // 16) * 16, cap - w)
    fits = first - start + cnt <= w
    dense_group = jnp.logical_not(jnp.all(fits.reshape(nb, ne // MOE_EXPERT_BLOCK, -1), axis=-1))
    dense_block = jnp.logical_not(jnp.all(fits, axis=1))
    flat = lambda a: a.reshape(-1).astype(jnp.int32)
    return flat(start), flat(dense_group), flat(dense_block)


def _moe_gather_kernel(start_ref, dense_ref, pos_ref, aff_ref, h_ref, xs_ref, gate_ref, *, cap, tb, w):
    b = pl.program_id(0)
    g = pl.program_id(1)
    eb, n = pos_ref.shape[1], pos_ref.shape[3]
    nk = n // tb
    dense = dense_ref[b * pl.num_programs(1) + g]

    @pl.when(dense != 0)
    def _():
        pick, hits = _slot_onehot(pos_ref, cap)
        xs = jnp.dot(pick, h_ref[0], preferred_element_type=F32).astype(BF16)
        xs_ref[0] = xs.reshape(xs_ref.shape[1:])
        for i, hit in enumerate(hits):
            gate = jnp.sum(jnp.where(hit, aff_ref[0, i], 0.0), axis=-1, keepdims=True)
            gate_ref[0, i] = jnp.broadcast_to(gate, gate_ref.shape[2:])

    @pl.when(dense == 0)
    def _():
        xs_ref[...] = jnp.zeros(xs_ref.shape, BF16)
        gate_ref[...] = jnp.zeros(gate_ref.shape, F32)
        slot = lax.broadcasted_iota(jnp.int32, (w, tb), 0)
        for k in range(nk):
            cols = slice(k * tb, (k + 1) * tb)
            starts = [pl.multiple_of(start_ref[((b * pl.num_programs(1) + g) * eb + i) * nk + k], 16)
                      for i in range(eb)]
            hits = [(slot + starts[i]) == pos_ref[0, i, :, cols] for i in range(eb)]
            pick = jnp.concatenate([jnp.where(hit, 1.0, 0.0).astype(BF16) for hit in hits], axis=0)
            part = jnp.dot(pick, h_ref[0, cols, :], preferred_element_type=F32).astype(BF16)
            for i in range(eb):
                rows = pl.ds(starts[i], w)
                xs_ref[0, i, rows, :] = xs_ref[0, i, rows, :] + part[i * w:(i + 1) * w]
                gate = jnp.sum(jnp.where(hits[i], aff_ref[0, i, :, cols], 0.0), axis=-1, keepdims=True)
                gate_ref[0, i, rows, :] = gate_ref[0, i, rows, :] + gate


def _moe_gather(start, dense, pos, aff, h, *, cap, tb, w):
    nb, ne, _, n = pos.shape
    dm = h.shape[2]
    eb = MOE_EXPERT_BLOCK
    grid_spec = pltpu.PrefetchScalarGridSpec(
        num_scalar_prefetch=2,
        grid=(nb, ne // eb),
        in_specs=[pl.BlockSpec((1, eb, 1, n), lambda b, e, *_: (b, e, 0, 0)),
                  pl.BlockSpec((1, eb, 1, n), lambda b, e, *_: (b, e, 0, 0)),
                  pl.BlockSpec((1, n, dm), lambda b, e, *_: (b, 0, 0), pipeline_mode=pl.Buffered(1))],
        out_specs=[pl.BlockSpec((1, eb, cap, dm), lambda b, e, *_: (b, e, 0, 0)),
                   pl.BlockSpec((1, eb, cap, 128), lambda b, e, *_: (b, e, 0, 0))])
    return pl.pallas_call(
        functools.partial(_moe_gather_kernel, cap=cap, tb=tb, w=w),
        grid_spec=grid_spec,
        out_shape=[jax.ShapeDtypeStruct((nb, ne, cap, dm), BF16),
                   jax.ShapeDtypeStruct((nb, ne, cap, 128), F32)],
        compiler_params=_params("parallel", "arbitrary"),
        name="moe_gather",
    )(start, dense, pos, aff, h)


def _moe_ffn_kernel(*refs, n_sets):
    sets = [refs[3 * i:3 * i + 3] for i in range(n_sets)]
    wg_ref, wu_ref, wd_ref = refs[3 * n_sets:3 * n_sets + 3]
    outs = refs[3 * n_sets + 3:4 * n_sets + 3]
    wgb_ref, wub_ref, wdb_ref = refs[4 * n_sets + 3:]
    c = pl.program_id(1)

    @pl.when(c == 0)
    def _():
        rows = 256
        for src, dst in ((wg_ref, wgb_ref), (wu_ref, wub_ref), (wd_ref, wdb_ref)):
            for r0 in range(0, dst.shape[0], rows):
                dst[r0:r0 + rows, :] = src[0, 0, r0:r0 + rows, :].astype(BF16)

    def run(xs_ref, gate_ref, m_ref, y_ref):
        nb, _, cap, dm = xs_ref.shape
        xs = xs_ref[...].reshape(nb * cap, dm)
        a = jnp.dot(xs, wgb_ref[...], preferred_element_type=F32)
        u = jnp.dot(xs, wub_ref[...], preferred_element_type=F32)
        act = (_silu(a) * u).astype(BF16)
        out = jnp.dot(act, wdb_ref[...], preferred_element_type=F32)
        for i in range(nb):
            y = out[i * cap:(i + 1) * cap] * gate_ref[i, 0][:, 0:1]
            y_ref[i, 0] = (y * m_ref[i, 0][5:6]).astype(BF16)

    if n_sets == 1:
        run(*sets[0], outs[0])
    else:
        last = pl.num_programs(1) - 1
        pl.when(c < last)(lambda: run(*sets[0], outs[0]))
        pl.when(c == last)(lambda: run(*sets[1], outs[1]))


def _moe_ffn(slot_sets, modsel, wg, wu, wd, layer):
    xs, gate, _ = slot_sets[0]
    nb_all, ne, cap, dm = xs.shape
    df = wg.shape[3]
    nb = 2 if nb_all % 2 == 0 else 1
    n_lat = nb_all // nb
    lat = lambda e, c: (jnp.minimum(c, n_lat - 1), e, 0, 0)
    in_specs = [pl.BlockSpec((nb, 1, cap, dm), lat),
                pl.BlockSpec((nb, 1, cap, 128), lat),
                pl.BlockSpec((nb, 1, 6, dm), lambda e, c: (jnp.minimum(c, n_lat - 1), slot_sets[0][2], 0, 0))]
    out_specs = [pl.BlockSpec((nb, 1, cap, dm), lat)]
    out_shape = [jax.ShapeDtypeStruct(xs.shape, BF16)]
    operands = [xs, gate, modsel]
    if len(slot_sets) == 2:
        xs2, gate2, sel2 = slot_sets[1]
        nb2, _, cap2, _ = xs2.shape
        fixed = lambda e, c: (0, e, 0, 0)
        in_specs += [pl.BlockSpec((nb2, 1, cap2, dm), fixed),
                     pl.BlockSpec((nb2, 1, cap2, 128), fixed),
                     pl.BlockSpec((nb2, 1, 6, dm), lambda e, c: (0, sel2, 0, 0))]
        out_specs.append(pl.BlockSpec((nb2, 1, cap2, dm), fixed))
        out_shape.append(jax.ShapeDtypeStruct(xs2.shape, BF16))
        operands += [xs2, gate2, modsel]
    weight = lambda e, c: (layer, e, 0, 0)
    in_specs += [pl.BlockSpec((1, 1, dm, df), weight), pl.BlockSpec((1, 1, dm, df), weight),
                 pl.BlockSpec((1, 1, df, dm), weight)]
    return pl.pallas_call(
        functools.partial(_moe_ffn_kernel, n_sets=len(slot_sets)),
        grid=(ne, n_lat + len(slot_sets) - 1),
        in_specs=in_specs,
        out_specs=out_specs,
        out_shape=out_shape,
        scratch_shapes=[pltpu.VMEM((dm, df), BF16), pltpu.VMEM((dm, df), BF16), pltpu.VMEM((df, dm), BF16)],
        compiler_params=_params("arbitrary", "arbitrary"),
        name="moe_ffn",
    )(*operands, wg, wu, wd)


def _moe_scatter_kernel(start_ref, dense_ref, posc_ref, y_ref, x_ref, *rest, cap, w):
    b = pl.program_id(0)
    k = pl.program_id(1)
    nk = pl.num_programs(1)
    ne = y_ref.shape[1]
    tb = posc_ref.shape[1]
    dense = dense_ref[b * nk + k]
    o_ref = rest[-1]

    def emit(update):
        total = x_ref[0] + update
        if len(rest) == 2:
            total = total * lax.rsqrt(jnp.mean(total * total, axis=-1, keepdims=True) + NORM_EPS) * rest[0][...]
        o_ref[0] = total

    @pl.when(dense != 0)
    def _():
        slot = lax.broadcasted_iota(jnp.int32, (tb, cap), 1)
        pick = jnp.concatenate([jnp.where(slot == posc_ref[0, :, i:i + 1], 1.0, 0.0).astype(BF16)
                                for i in range(ne)], axis=1)
        y = y_ref[0].reshape(ne * cap, y_ref.shape[3])
        emit(jnp.dot(pick, y, preferred_element_type=F32))

    @pl.when(dense == 0)
    def _():
        per = 128 // w
        lane = lax.broadcasted_iota(jnp.int32, (tb, 128), 1)
        tiles = []
        wins = []
        for j in range(ne // per):
            col = None
            for u in reversed(range(per)):
                i = j * per + u
                s = pl.multiple_of(start_ref[(b * ne + i) * nk + k], 16)
                rel = posc_ref[0, :, i:i + 1] - s
                target = jnp.where(jnp.logical_and(rel >= 0, rel < w), rel + u * w, -1)
                col = target if col is None else jnp.where(lane < (u + 1) * w, target, col)
            tiles.append(jnp.where(lane == col, 1.0, 0.0).astype(BF16))
            for u in range(per):
                i = j * per + u
                s = pl.multiple_of(start_ref[(b * ne + i) * nk + k], 16)
                wins.append(y_ref[0, i, pl.ds(s, w), :])
        pick = jnp.concatenate(tiles, axis=1)
        emit(jnp.dot(pick, jnp.concatenate(wins, axis=0), preferred_element_type=F32))


def _moe_scatter(start, dense, pos, y, xcat, *, bb, row0, tb, w, final_gain=None):
    nb, ne, cap, dm = y.shape
    n = pos.shape[3]
    assert 128 % w == 0 and ne % (128 // w) == 0
    posc = jnp.transpose(pos.reshape(nb, ne, n), (0, 2, 1))
    x_index = (lambda b, k, *_: (b, k, 0)) if bb == 1 else (lambda b, k, *_: (k, row0 // tb, 0))
    in_specs = [pl.BlockSpec((1, tb, ne), lambda b, k, *_: (b, k, 0)),
                pl.BlockSpec((1, ne, cap, dm), lambda b, k, *_: (b, 0, 0, 0)),
                pl.BlockSpec((1, tb, dm), x_index)]
    operands = (start, dense, posc, y, xcat)
    if final_gain is None:
        out_shape, aliases = jax.ShapeDtypeStruct(xcat.shape, F32), {4: 0}
    else:
        assert bb == 1
        in_specs.append(pl.BlockSpec((1, dm), lambda b, k, *_: (0, 0)))
        operands += (final_gain,)
        out_shape, aliases = jax.ShapeDtypeStruct((nb, n, dm), F32), {}
    grid_spec = pltpu.PrefetchScalarGridSpec(
        num_scalar_prefetch=2,
        grid=(nb, n // tb),
        in_specs=in_specs,
        out_specs=pl.BlockSpec((1, tb, dm), x_index))
    return pl.pallas_call(
        functools.partial(_moe_scatter_kernel, cap=cap, w=w),
        grid_spec=grid_spec,
        out_shape=out_shape,
        input_output_aliases=aliases,
        compiler_params=_params("parallel", "arbitrary"),
        name="moe_scatter",
    )(*operands)


def _ec_moe(xcat, modsel, g2, wr, upper, wg, wu, wd, layer, *, rt, seq, ctx, with_ctx, final_gain=None):
    b = xcat.shape[0]
    ne = N_EXPERTS
    rt_lat = 2 * rt if seq % (2 * rt) == 0 else rt
    h, aff, pos = _router(xcat, modsel, g2, wr, upper, rt=rt_lat, n=seq, row_tile0=0, sel=0)
    cap = max(1, EC_FACTOR * seq // N_EXPERTS)
    pos = pos.reshape(b, ne, 1, seq)
    tb = min(MOE_TOKEN_BLOCK, ctx)
    win = dict(tb=tb, w=min(MOE_WINDOW, cap))
    start, dense_group, dense_block = _moe_plan(pos, cap=cap, **win)
    xs, gate = _moe_gather(start, dense_group, pos, aff.reshape(b, ne, 1, seq), h, cap=cap, **win)
    if not with_ctx:
        (y,) = _moe_ffn([(xs, gate, 0)], modsel, wg, wu, wd, layer)
        return _moe_scatter(start, dense_block, pos, y, xcat, bb=1, row0=0, final_gain=final_gain, **win)
    hc, affc, posc = _router(xcat, modsel, g2, wr, upper, rt=rt, n=ctx, row_tile0=seq // rt, sel=1)
    capc = max(1, EC_FACTOR * ctx // N_EXPERTS)
    offs = (jnp.arange(b, dtype=jnp.int32) * capc)[:, None, None]
    posc = jnp.where(posc >= 0, posc + offs, -1)
    posc = jnp.transpose(posc, (1, 0, 2)).reshape(1, ne, 1, b * ctx)
    affc = jnp.transpose(affc, (1, 0, 2)).reshape(1, ne, 1, b * ctx)
    winc = dict(tb=tb, w=min(MOE_WINDOW, b * capc))
    assert tb == ctx
    startc, dense_groupc, dense_blockc = _moe_plan(posc, cap=b * capc, **winc)
    xsc, gatec = _moe_gather(startc, dense_groupc, posc, affc, hc.reshape(1, b * ctx, -1), cap=b * capc, **winc)
    y, yc = _moe_ffn([(xs, gate, 0), (xsc, gatec, 1)], modsel, wg, wu, wd, layer)
    xcat = _moe_scatter(start, dense_block, pos, y, xcat, bb=1, row0=0, final_gain=final_gain, **win)
    return _moe_scatter(startc, dense_blockc, posc, yc, xcat, bb=b, row0=seq, **winc)


def _rope_tables(seq, ctx):
    rows = seq // GRID_W
    row_ids = jnp.repeat(jnp.arange(rows, dtype=F32), GRID_W)
    col_ids = jnp.tile(jnp.arange(GRID_W, dtype=F32), rows)
    half = A_HEAD_DIM // 2
    inv = ROPE_THETA ** (-jnp.arange(0, half, 2, dtype=F32) / half)
    ang_r = row_ids[:, None] * inv
    ang_c = col_ids[:, None] * inv
    cos = jnp.concatenate([jnp.cos(ang_r)] * 2 + [jnp.cos(ang_c)] * 2, axis=1)
    sin = jnp.concatenate([-jnp.sin(ang_r), jnp.sin(ang_r), -jnp.sin(ang_c), jnp.sin(ang_c)], axis=1)
    cos = jnp.tile(cos, (1, A_HEADS))
    sin = jnp.tile(sin, (1, A_HEADS))
    cos = jnp.concatenate([cos, jnp.ones((ctx, A_Q), F32)], axis=0)
    sin = jnp.concatenate([sin, jnp.zeros((ctx, A_Q), F32)], axis=0)
    return cos, sin


def kernel(x, c, ctx, c_ctx, w_mod, b_mod, g_norm1, g_norm2, w_in_even, w_out_even, g_qnorm, g_knorm, w_conv,
           b_gate, g_hnorm, w_in_odd, w_pool_grp, s_pool, w_router, w_exp_gate, w_exp_up, w_exp_down, g_final):
    b, seq, dm = x.shape
    n_ctx = ctx.shape[1]
    depth = w_mod.shape[0]
    rt = 256 if (seq % 256 == 0 and n_ctx % 256 == 0) else 128
    assert seq % rt == 0 and n_ctx % rt == 0 and seq % n_ctx == 0 and seq % GRID_W == 0
    nlt = seq // rt
    nt = (seq + n_ctx) // rt

    xcat = jnp.concatenate([x, ctx, jnp.zeros((b, seq - n_ctx, dm), F32)], axis=1)

    rows = -(-(b + 1) // 8) * 8
    cond = jnp.zeros((rows, dm), F32).at[:b].set(c).at[b].set(c_ctx)
    mods = _adaln(cond, w_mod, b_mod)
    lat = mods[:, :b].reshape(depth, b, 1, 6, dm)
    cx = jnp.broadcast_to(mods[:, b].reshape(depth, 1, 1, 6, dm), (depth, b, 1, 6, dm))
    modsel = jnp.concatenate([lat, cx], axis=2)

    cos, sin = _rope_tables(seq, n_ctx)
    lane = jnp.arange(A_Q)
    gmat = (lane[:, None] // A_HEAD_DIM == lane[None, :] // A_HEAD_DIM).astype(BF16)
    blk = min(256, n_ctx)
    tri = jnp.arange(blk)
    upper = (tri[:, None] <= tri[None, :]).astype(BF16)

    wg, wu, wd = w_exp_gate, w_exp_up, w_exp_down
    wr = jnp.pad(w_router, ((0, 0), (0, 0), (0, GATE_PAD - N_EXPERTS)))

    for i in range(depth):
        ctx_next = any(j % 2 == 0 for j in range(i + 1, depth))
        n_upd = nt if ctx_next else nlt
        ms = modsel[i]
        if i % 2 == 0:
            e = i // 2
            w_in = w_in_even[e]
            pad = jnp.zeros((dm, GATE_PAD - N_GATES), F32)
            w_in = jnp.concatenate([w_in, pad], axis=1).astype(BF16)
            bg = jnp.concatenate([b_gate[e], jnp.zeros((GATE_PAD - N_GATES,), F32)])[None]
            q, k, v, qk, vbt, ob, gt, gc, gr = _inproj(
                xcat, ms, g_norm1[i][None], w_in, cos, sin,
                jnp.tile(g_qnorm[e], A_HEADS)[None], jnp.tile(g_knorm[e], A_KV_HEADS)[None], gmat, bg,
                rt=rt, ta=seq + n_ctx, n_lat_tiles=nlt)
            oa = _attention(q, k, v, rt=rt, n_tiles=n_upd, n_lat_tiles=nlt, seq=seq)
            qct = _short_conv(qk, w_conv[e], seq=seq, keys=False)
            kc = _short_conv(qk, w_conv[e], seq=seq, keys=True)
            gates_row = jnp.swapaxes(jnp.concatenate([gt[:, :, :N_GATES], gc[:, :, :N_GATES]], axis=2), 1, 2)
            hf, hb = _mlstm(qct, kc, vbt, gates_row, gr, seq=seq, blk=rt)
            xcat = _outproj(xcat, ms, oa, hf, hb, ob, g_hnorm[e][None], w_out_even[e].astype(BF16),
                            rt=rt, n_tiles=n_upd, n_lat_tiles=nlt)
        else:
            o = i // 2
            w_in = w_in_odd[o].astype(BF16)
            wgrp = w_pool_grp[o].astype(BF16)
            rt_lat = 2 * rt if nlt % 2 == 0 else rt
            u = _mod_matmul(xcat, ms, g_norm1[i][None], w_in, rt=rt_lat, tile0=0, n_tiles=seq // rt_lat, sel=0)
            if ctx_next:
                uc = _mod_matmul(xcat, ms, g_norm1[i][None], w_in, rt=rt, tile0=nlt, n_tiles=nt - nlt, sel=1)
            xcat = _pool(u, xcat, ms, wgrp, s_pool[o][None], n=seq, row_block=0, sel=0)
            if ctx_next:
                xcat = _pool(uc, xcat, ms, wgrp, s_pool[o][None], n=n_ctx, row_block=seq // n_ctx, sel=1)
        xcat = _ec_moe(xcat, ms, g_norm2[i][None], wr[i], upper, wg, wu, wd, i,
                       rt=rt, seq=seq, ctx=n_ctx, with_ctx=ctx_next,
                       final_gain=g_final[None] if i == depth - 1 else None)
    return xcat
```

```python
import functools

import jax
import jax.numpy as jnp
from jax import lax
from jax.experimental import pallas as pl
from jax.experimental.pallas import tpu as pltpu

D_MODEL = 1024
GRID_W = 64
A_HEADS = 8
A_KV_HEADS = 2
A_HEAD_DIM = 64
ROPE_THETA = 10000.0
B_HEADS = 4
B_HEAD_DIM = 128
B_CHUNK = 128
POOL_WINDOWS = (2, 4, 8, 16)
POOL_GROUP = D_MODEL // 4
N_EXPERTS = 16
EC_FACTOR = 2
NORM_EPS = 1e-6

A_Q = A_HEADS * A_HEAD_DIM
A_KV = A_KV_HEADS * A_HEAD_DIM
B_W = B_HEADS * B_HEAD_DIM
N_GATES = 2 * 2 * B_HEADS
GATE_PAD = 128
C_QA, C_KA, C_VA = 0, A_Q, A_Q + A_KV
C_QK = A_Q + 2 * A_KV
C_VB = C_QK + 2 * B_W
C_OB = C_VB + B_W
C_GT = C_OB + B_W
EVEN_COLS = C_GT + GATE_PAD
POOL_HALO = 16

F32 = jnp.float32
BF16 = jnp.bfloat16
HIGHEST = lax.Precision.HIGHEST
NT_DIMS = (((1,), (1,)), ((), ()))
TN_DIMS = (((0,), (0,)), ((), ()))
VMEM_LIMIT = 56 * 1024 * 1024


def _params(*sem):
    return pltpu.CompilerParams(dimension_semantics=sem, vmem_limit_bytes=VMEM_LIMIT)


def _modulate(x, g, shift, scale):
    y = x * lax.rsqrt(jnp.mean(x * x, axis=-1, keepdims=True) + NORM_EPS)
    return (y * g) * (1.0 + scale) + shift


def _silu(x):
    return x * jax.nn.sigmoid(x)


def _adaln_kernel(c_ref, w_ref, b_ref, o_ref):
    s = _silu(c_ref[...])
    o_ref[0] = jnp.dot(s, w_ref[0], precision=HIGHEST, preferred_element_type=F32) + b_ref[0]


def _adaln(cond, w_mod, b_mod):
    depth, dm, n6 = w_mod.shape
    rows = cond.shape[0]
    tn = 1536
    return pl.pallas_call(
        _adaln_kernel,
        grid=(depth, n6 // tn),
        in_specs=[pl.BlockSpec((rows, dm), lambda l, j: (0, 0)),
                  pl.BlockSpec((1, dm, tn), lambda l, j: (l, 0, j)),
                  pl.BlockSpec((1, 1, tn), lambda l, j: (l, 0, j))],
        out_specs=pl.BlockSpec((1, rows, tn), lambda l, j: (l, 0, j)),
        out_shape=jax.ShapeDtypeStruct((depth, rows, n6), F32),
        compiler_params=_params("arbitrary", "arbitrary"),
        name="adaln",
    )(cond, w_mod, b_mod.reshape(depth, 1, n6))


def _group_mean_sq(x, gmat, width):
    s = jnp.dot((x * x).astype(BF16), gmat, preferred_element_type=F32)
    return s * (1.0 / width)


def _dot_exact_lhs(mat, x):
    x1 = x.astype(BF16)
    r1 = x - x1.astype(F32)
    x2 = r1.astype(BF16)
    x3 = (r1 - x2.astype(F32)).astype(BF16)
    return (jnp.dot(mat, x1, preferred_element_type=F32) + jnp.dot(mat, x2, preferred_element_type=F32)
            + jnp.dot(mat, x3, preferred_element_type=F32))


def _rope(x, cos, sin):
    w = x.shape[1]
    lane = lax.broadcasted_iota(jnp.int32, x.shape, 1)
    first = (lane % 32) < 16
    partner = jnp.where(first, pltpu.roll(x, w - 16, 1), pltpu.roll(x, 16, 1))
    return x * cos + partner * sin


def _inproj_kernel(x_ref, m_ref, g1_ref, w_ref, cos_ref, sin_ref, gq_ref, gk_ref, gm_ref, bg_ref, lp_ref, ls_ref,
                   q_ref, k_ref, v_ref, qk_ref, vb_ref, ob_ref, gt_ref, gc_ref, gr_ref):
    m = m_ref[0, 0]
    h = _modulate(x_ref[0], g1_ref[...], m[0:1], m[1:2]).astype(BF16)

    def mm(lo, hi):
        return jnp.dot(h, w_ref[:, lo:hi], preferred_element_type=F32)

    cos = cos_ref[...]
    sin = sin_ref[...]
    gm = gm_ref[...]
    qa = mm(C_QA, C_KA)
    qa = qa * lax.rsqrt(_group_mean_sq(qa, gm, A_HEAD_DIM) + NORM_EPS) * gq_ref[...]
    q_ref[0] = (_rope(qa, cos, sin) * (A_HEAD_DIM ** -0.5)).astype(BF16)
    ka = mm(C_KA, C_VA)
    ka = ka * lax.rsqrt(_group_mean_sq(ka, gm[:A_KV, :A_KV], A_HEAD_DIM) + NORM_EPS) * gk_ref[...]
    k_ref[0] = _rope(ka, cos[:, :A_KV], sin[:, :A_KV]).astype(BF16)
    va = mm(C_VA, C_QK).astype(BF16)
    ones = jnp.ones((va.shape[0], A_HEAD_DIM), BF16)
    v_ref[0] = jnp.concatenate(
        [piece for g in range(A_KV_HEADS) for piece in (va[:, g * A_HEAD_DIM:(g + 1) * A_HEAD_DIM], ones)], axis=1)
    qk_ref[0] = mm(C_QK, C_VB).astype(BF16)
    vb_ref[0] = mm(C_VB, C_OB).T.astype(BF16)
    ob_ref[0] = mm(C_OB, C_GT).astype(BF16)
    gt = mm(C_GT, EVEN_COLS) + bg_ref[...]
    lane = lax.broadcasted_iota(jnp.int32, gt.shape, 1)
    log_sig = jnp.minimum(gt, 0.0) - jnp.log1p(jnp.exp(-jnp.abs(gt)))
    gt = jnp.where((lane % (2 * B_HEADS)) >= B_HEADS, log_sig, gt)
    gt_ref[0] = gt
    prefix = _dot_exact_lhs(lp_ref[...], gt)
    suffix = _dot_exact_lhs(ls_ref[...], gt)
    cum = jnp.where((lane % (4 * B_HEADS)) < 2 * B_HEADS, prefix, suffix)
    gc_ref[0] = cum
    gr_ref[0] = gt - pltpu.roll(cum, GATE_PAD - B_HEADS, 1)


def _inproj(xcat, modsel, g1, w, cos, sin, gq, gk, gmat, bgate, *, rt, ta, n_lat_tiles):
    b, _, dm = xcat.shape
    nt = ta // rt
    row = lambda bb, r: (bb, r, 0)
    full = lambda bb, r: (0, 0)
    out_widths = (A_Q, A_KV, 2 * A_KV, 2 * B_W, B_W, B_W, GATE_PAD, GATE_PAD, GATE_PAD)
    out_dtypes = (BF16, BF16, BF16, BF16, BF16, BF16, F32, F32, F32)
    out_specs = [pl.BlockSpec((1, rt, wd), row) for wd in out_widths]
    out_shapes = [jax.ShapeDtypeStruct((b, ta, wd), dt) for wd, dt in zip(out_widths, out_dtypes)]
    vb_slot = 4
    out_specs[vb_slot] = pl.BlockSpec((1, B_W, rt), lambda bb, r: (bb, 0, r))
    out_shapes[vb_slot] = jax.ShapeDtypeStruct((b, B_W, ta), BF16)
    idx = jnp.arange(rt)
    same_chunk = idx[:, None] // B_CHUNK == idx[None, :] // B_CHUNK
    lower = jnp.logical_and(same_chunk, idx[None, :] <= idx[:, None]).astype(BF16)
    upper = jnp.logical_and(same_chunk, idx[None, :] >= idx[:, None]).astype(BF16)
    return pl.pallas_call(
        _inproj_kernel,
        grid=(b, nt),
        in_specs=[pl.BlockSpec((1, rt, dm), row),
                  pl.BlockSpec((1, 1, 6, dm), lambda bb, r: (bb, (r >= n_lat_tiles).astype(jnp.int32), 0, 0)),
                  pl.BlockSpec((1, dm), full),
                  pl.BlockSpec((dm, EVEN_COLS), full),
                  pl.BlockSpec((rt, A_Q), lambda bb, r: (r, 0)),
                  pl.BlockSpec((rt, A_Q), lambda bb, r: (r, 0)),
                  pl.BlockSpec((1, A_Q), full),
                  pl.BlockSpec((1, A_KV), full),
                  pl.BlockSpec((A_Q, A_Q), full),
                  pl.BlockSpec((1, GATE_PAD), full),
                  pl.BlockSpec((rt, rt), full),
                  pl.BlockSpec((rt, rt), full)],
        out_specs=out_specs,
        out_shape=out_shapes,
        compiler_params=_params("parallel", "arbitrary"),
        name="even_inproj",
    )(xcat, modsel, g1, w, cos, sin, gq, gk, gmat, bgate, lower, upper)


def _attn_kernel(q_ref, k_ref, v_ref, o_ref, *, n_lat_tiles, n_tiles, seq):
    group = A_HEADS // A_KV_HEADS

    def attend(key_lo):
        q = q_ref[0]
        for hd in range(A_HEADS):
            g = hd // group
            qh = q[:, hd * A_HEAD_DIM:(hd + 1) * A_HEAD_DIM]
            kg = k_ref[0, key_lo:, g * A_HEAD_DIM:(g + 1) * A_HEAD_DIM]
            vg = v_ref[0, key_lo:, 2 * g * A_HEAD_DIM:2 * (g + 1) * A_HEAD_DIM]
            s = lax.dot_general(qh, kg, NT_DIMS, preferred_element_type=F32)
            p = jnp.exp(s - jnp.max(s, axis=-1, keepdims=True))
            o = jnp.dot(p.astype(BF16), vg, preferred_element_type=F32)
            o = o[:, :A_HEAD_DIM] / o[:, A_HEAD_DIM:]
            o_ref[0, :, hd * A_HEAD_DIM:(hd + 1) * A_HEAD_DIM] = o.astype(BF16)

    if n_tiles == n_lat_tiles:
        attend(0)
    else:
        r = pl.program_id(1)
        pl.when(r < n_lat_tiles)(lambda: attend(0))
        pl.when(r >= n_lat_tiles)(lambda: attend(seq))


def _attention(q, k, v, *, rt, n_tiles, n_lat_tiles, seq):
    b, ta, _ = q.shape
    return pl.pallas_call(
        functools.partial(_attn_kernel, n_lat_tiles=n_lat_tiles, n_tiles=n_tiles, seq=seq),
        grid=(b, n_tiles),
        in_specs=[pl.BlockSpec((1, rt, A_Q), lambda bb, r: (bb, r, 0)),
                  pl.BlockSpec((1, ta, A_KV), lambda bb, r: (bb, 0, 0)),
                  pl.BlockSpec((1, ta, 2 * A_KV), lambda bb, r: (bb, 0, 0))],
        out_specs=pl.BlockSpec((1, rt, A_Q), lambda bb, r: (bb, r, 0)),
        out_shape=jax.ShapeDtypeStruct((b, n_tiles * rt, A_Q), BF16),
        compiler_params=_params("parallel", "arbitrary"),
        name="gqa_attention",
    )(q, k, v)


def _conv_kernel(x_ref, w_ref, o_ref, *, seq, keys):
    x = x_ref[0].astype(F32)
    ta = x.shape[0]
    w = w_ref[...]
    row = lax.broadcasted_iota(jnp.int32, (ta, 1), 0)
    prev = jnp.where(jnp.logical_or(row == 0, row == seq), 0.0, pltpu.roll(x, 1, 0))
    nxt = jnp.where(jnp.logical_or(row == seq - 1, row == ta - 1), 0.0, pltpu.roll(x, ta - 1, 0))
    y = _silu(prev * w[0:1] + x * w[1:2] + nxt * w[2:3])
    if keys:
        o_ref[0] = (y * (B_HEAD_DIM ** -0.5)).astype(BF16)
    else:
        o_ref[0] = y.T.astype(BF16)


def _short_conv(qk, w_conv, *, seq, keys):
    b, ta, _ = qk.shape
    ct = 256
    c0 = B_W // ct if keys else 0
    if keys:
        out_spec = pl.BlockSpec((1, ta, ct), lambda bb, j: (bb, 0, j))
        out_shape = jax.ShapeDtypeStruct((b, ta, B_W), BF16)
    else:
        out_spec = pl.BlockSpec((1, ct, ta), lambda bb, j: (bb, j, 0))
        out_shape = jax.ShapeDtypeStruct((b, B_W, ta), BF16)
    return pl.pallas_call(
        functools.partial(_conv_kernel, seq=seq, keys=keys),
        grid=(b, B_W // ct),
        in_specs=[pl.BlockSpec((1, ta, ct), lambda bb, j: (bb, 0, c0 + j)),
                  pl.BlockSpec((w_conv.shape[0], ct), lambda bb, j: (0, c0 + j))],
        out_specs=out_spec,
        out_shape=out_shape,
        compiler_params=_params("parallel", "arbitrary"),
        name="mlstm_conv_k" if keys else "mlstm_conv_q",
    )(qk, w_conv)


def _mlstm_chain(qt, k, vat, ig, b_row, r_col, total, ca, m_old, tri):
    log_d = jnp.where(tri, b_row + r_col, -jnp.inf)
    log_inter = m_old + b_row
    m_t = jnp.maximum(log_inter, jnp.max(log_d, axis=0, keepdims=True))
    w_intra = jnp.exp(log_d - m_t) * jnp.dot(k, qt, preferred_element_type=F32)
    w_inter = jnp.exp(log_inter - m_t)
    nd = (w_inter * jnp.dot(ca.astype(BF16), qt, preferred_element_type=F32)
          + jnp.dot(vat, w_intra.astype(BF16), preferred_element_type=F32))
    dv = nd.shape[0] // 2
    h = nd[:dv] / jnp.maximum(jnp.abs(nd[dv:]), jnp.exp(-m_t))
    log_w = total - b_row + ig
    m_new = jnp.maximum(m_old + total, jnp.max(log_w, axis=-1, keepdims=True))
    w_s = jnp.exp(log_w - m_new)
    decay = jnp.exp(m_old + total - m_new)
    ca_new = decay * ca + jnp.dot((vat * w_s).astype(BF16), k, preferred_element_type=F32)
    return h, ca_new, m_new


def _mlstm_kernel(qf_ref, kf_ref, vf_ref, grf_ref, gcf_ref, qb_ref, kb_ref, vb_ref, grb_ref, gcb_ref,
                  hf_ref, hb_ref, c_ref, m_ref):
    @pl.when(pl.program_id(1) == 0)
    def _():
        c_ref[...] = jnp.zeros(c_ref.shape, F32)
        m_ref[...] = jnp.full(m_ref.shape, -1e30, F32)

    t = B_CHUNK
    n_chunks = kf_ref.shape[1] // t
    src = lax.broadcasted_iota(jnp.int32, (t, t), 0)
    tgt = lax.broadcasted_iota(jnp.int32, (t, t), 1)
    ones = jnp.ones((B_HEAD_DIM, t), BF16)
    dirs = ((qf_ref, kf_ref, vf_ref, grf_ref, gcf_ref, hf_ref, src <= tgt, t - 1, range(n_chunks)),
            (qb_ref, kb_ref, vb_ref, grb_ref, gcb_ref, hb_ref, src >= tgt, 0, range(n_chunks - 1, -1, -1)))
    outputs = []
    states = []
    for d, (q_ref, k_ref, v_ref, gr_ref, gc_ref, h_ref, tri, last, order) in enumerate(dirs):
        for hd in range(B_HEADS):
            sl = slice(hd * B_HEAD_DIM, (hd + 1) * B_HEAD_DIM)
            gi = d * 2 * B_HEADS + hd
            fi = gi + B_HEADS
            si = d * B_HEADS + hd
            ca = c_ref[si]
            m = m_ref[si][:, 0:1]
            for ci in order:
                rows = slice(ci * t, (ci + 1) * t)
                ig = gr_ref[0, gi:gi + 1, rows]
                b_row = gr_ref[0, N_GATES + fi:N_GATES + fi + 1, rows]
                r_col = gc_ref[0, rows, gi:gi + 1]
                vat = jnp.concatenate([v_ref[0, sl, rows], ones], axis=0)
                h, ca, m = _mlstm_chain(q_ref[0, sl, rows], k_ref[0, rows, sl], vat, ig, b_row, r_col,
                                        b_row[:, last:last + 1], ca, m, tri)
                outputs.append((h_ref, rows, sl, h))
            states.append((si, ca, m))
    for h_ref, rows, sl, h in outputs:
        h_ref[0, sl, rows] = h.astype(BF16)
    for si, ca, m in states:
        c_ref[si] = ca
        m_ref[si] = jnp.broadcast_to(m, (1, 128))


def _mlstm(qct, kc, vbt, gates_row, gates_col, *, seq, blk):
    b, ta, _ = kc.shape
    nc = ta // blk
    ncl = seq // blk
    ncc = nc - ncl
    fwd = lambda j: jnp.where(j < ncc, ncl + j, j - ncc)
    bwd = lambda j: jnp.where(j < ncc, nc - 1 - j, ncl - 1 - (j - ncc))

    def specs(order):
        feature_major = pl.BlockSpec((1, B_W, blk), lambda bb, j: (bb, 0, order(j)))
        return [feature_major,
                pl.BlockSpec((1, blk, B_W), lambda bb, j: (bb, order(j), 0)),
                feature_major,
                pl.BlockSpec((1, 2 * N_GATES, blk), lambda bb, j: (bb, 0, order(j))),
                pl.BlockSpec((1, blk, GATE_PAD), lambda bb, j: (bb, order(j), 0))]

    n_state = 2 * B_HEADS
    ops = (qct, kc, vbt, gates_row, gates_col)
    return pl.pallas_call(
        _mlstm_kernel,
        grid=(b, nc),
        in_specs=specs(fwd) + specs(bwd),
        out_specs=[pl.BlockSpec((1, B_W, blk), lambda bb, j: (bb, 0, fwd(j))),
                   pl.BlockSpec((1, B_W, blk), lambda bb, j: (bb, 0, bwd(j)))],
        out_shape=[jax.ShapeDtypeStruct((b, B_W, ta), BF16)] * 2,
        scratch_shapes=[pltpu.VMEM((n_state, 2 * B_HEAD_DIM, B_HEAD_DIM), F32),
                        pltpu.VMEM((n_state, 1, 128), F32)],
        compiler_params=_params("parallel", "arbitrary"),
        name="mlstm_scan",
    )(*ops, *ops)


def _outproj_kernel(x_ref, m_ref, oa_ref, hf_ref, hb_ref, ob_ref, ghn_ref, w_ref, o_ref):
    hm = (hf_ref[0].astype(F32) + hb_ref[0].astype(F32)).T
    ob = ob_ref[0].astype(F32)
    ghn = ghn_ref[...]
    y = jnp.dot(oa_ref[0], w_ref[0:A_Q, :], preferred_element_type=F32)
    for hd in range(B_HEADS):
        sl = slice(hd * B_HEAD_DIM, (hd + 1) * B_HEAD_DIM)
        z = hm[:, sl]
        z = z * lax.rsqrt(jnp.mean(z * z, axis=-1, keepdims=True) + NORM_EPS) * ghn[:, sl]
        z = (z * jax.nn.sigmoid(ob[:, sl])).astype(BF16)
        y = y + jnp.dot(z, w_ref[A_Q + hd * B_HEAD_DIM:A_Q + (hd + 1) * B_HEAD_DIM, :],
                        preferred_element_type=F32)
    o_ref[0] = x_ref[0] + m_ref[0, 0][2:3] * y


def _outproj(xcat, modsel, oa, hf, hb, ob, ghn, w, *, rt, n_tiles, n_lat_tiles):
    b, ta, dm = xcat.shape
    row = lambda bb, r: (bb, r, 0)
    full = lambda bb, r: (0, 0)
    return pl.pallas_call(
        _outproj_kernel,
        grid=(b, n_tiles),
        in_specs=[pl.BlockSpec((1, rt, dm), row),
                  pl.BlockSpec((1, 1, 6, dm), lambda bb, r: (bb, (r >= n_lat_tiles).astype(jnp.int32), 0, 0)),
                  pl.BlockSpec((1, rt, A_Q), row),
                  pl.BlockSpec((1, B_W, rt), lambda bb, r: (bb, 0, r)),
                  pl.BlockSpec((1, B_W, rt), lambda bb, r: (bb, 0, r)),
                  pl.BlockSpec((1, rt, B_W), row),
                  pl.BlockSpec((1, B_W), full),
                  pl.BlockSpec((A_Q + B_W, dm), full)],
        out_specs=pl.BlockSpec((1, rt, dm), row),
        out_shape=jax.ShapeDtypeStruct(xcat.shape, F32),
        input_output_aliases={0: 0},
        compiler_params=_params("parallel", "arbitrary"),
        name="even_outproj",
    )(xcat, modsel, oa, hf, hb, ob, ghn, w)


def _modmm_kernel(x_ref, m_ref, g_ref, w_ref, o_ref):
    m = m_ref[0, 0]
    h = _modulate(x_ref[0], g_ref[...], m[0:1], m[1:2]).astype(BF16)
    o_ref[0] = jnp.dot(h, w_ref[...], preferred_element_type=F32)


def _mod_matmul(xcat, modsel, g, w, *, rt, tile0, n_tiles, sel):
    b, _, dm = xcat.shape
    n_out = w.shape[1]
    return pl.pallas_call(
        _modmm_kernel,
        grid=(b, n_tiles),
        in_specs=[pl.BlockSpec((1, rt, dm), lambda bb, r: (bb, tile0 + r, 0)),
                  pl.BlockSpec((1, 1, 6, dm), lambda bb, r: (bb, sel, 0, 0)),
                  pl.BlockSpec((1, dm), lambda bb, r: (0, 0)),
                  pl.BlockSpec((dm, n_out), lambda bb, r: (0, 0))],
        out_specs=pl.BlockSpec((1, rt, n_out), lambda bb, r: (bb, r, 0)),
        out_shape=jax.ShapeDtypeStruct((b, n_tiles * rt, n_out), F32),
        compiler_params=_params("parallel", "arbitrary"),
        name="odd_inproj",
    )(xcat, modsel, g, w)


def _pool_kernel(u_ref, x_ref, m_ref, wg_ref, sp_ref, o_ref, pad_ref, sa_ref, sb_ref, *, n, rc):
    g = pl.program_id(1)
    halo = POOL_HALO
    ext = n + 2 * halo
    zeros = jnp.zeros((halo, pad_ref.shape[1]), F32)
    pad_ref[0:halo, :] = zeros
    pad_ref[halo + n:ext, :] = zeros
    pad_ref[halo:halo + n, :] = u_ref[0]
    for ref in (pad_ref, sa_ref, sb_ref):
        ref[ext:ext + halo, :] = zeros
    gate = m_ref[0, 0][2:3]
    chunks = [(r0, min(rc, ext - r0)) for r0 in range(0, ext, rc)]

    def pair_sums(src, dst, shift):
        for r0, size in chunks:
            dst[r0:r0 + size, :] = src[r0:r0 + size, :] + src[r0 + shift:r0 + shift + size, :]

    for gi, win in enumerate(POOL_WINDOWS):
        @pl.when(g == gi)
        def _(win=win):
            lo = win // 2
            hi = win - 1 - lo
            sums, width = pad_ref, 1
            for dst in (sa_ref, sb_ref, sa_ref, sb_ref):
                if width == win:
                    break
                pair_sums(sums, dst, width)
                sums, width = dst, 2 * width
            assert width == win
            for r0 in range(0, n, rc):
                base = halo + r0
                acc = sums[base - lo:base - lo + rc, :]
                t = r0 + lax.broadcasted_iota(jnp.int32, (rc, 1), 0)
                cnt = jnp.minimum(t + hi, n - 1) - jnp.maximum(t - lo, 0) + 1
                p = acc / cnt.astype(F32) - pad_ref[base:base + rc, :]
                y = jnp.dot(p.astype(BF16), wg_ref[0], preferred_element_type=F32) * sp_ref[...]
                o_ref[0, r0:r0 + rc, :] = x_ref[0, r0:r0 + rc, :] + gate * y


def _pool(u, xcat, modsel, wgrp, spool, *, n, row_block, sel):
    b, _, dm = xcat.shape
    pg = POOL_GROUP
    rc = min(256, n)
    blk = lambda bb, g: (bb, row_block, g)
    return pl.pallas_call(
        functools.partial(_pool_kernel, n=n, rc=rc),
        grid=(b, dm // pg),
        in_specs=[pl.BlockSpec((1, n, pg), lambda bb, g: (bb, 0, g)),
                  pl.BlockSpec((1, n, pg), blk),
                  pl.BlockSpec((1, 1, 6, pg), lambda bb, g: (bb, sel, 0, g)),
                  pl.BlockSpec((1, pg, pg), lambda bb, g: (g, 0, 0)),
                  pl.BlockSpec((1, pg), lambda bb, g: (0, g))],
        out_specs=pl.BlockSpec((1, n, pg), blk),
        out_shape=jax.ShapeDtypeStruct(xcat.shape, F32),
        scratch_shapes=[pltpu.VMEM((n + 3 * POOL_HALO, pg), F32)] * 3,
        input_output_aliases={1: 0},
        compiler_params=_params("parallel", "arbitrary"),
        name="pool_mixer",
    )(u, xcat, modsel, wgrp, spool)


def _lane_cumsum(x01, upper, blk):
    n = x01.shape[1]
    out = []
    carry = jnp.zeros((x01.shape[0], 1), F32)
    for j in range(0, n, blk):
        cs = jnp.dot(x01[:, j:j + blk].astype(BF16), upper, preferred_element_type=F32) + carry
        carry = cs[:, blk - 1:blk]
        out.append(cs)
    return jnp.concatenate(out, axis=1) if len(out) > 1 else out[0]


def _router_kernel(x_ref, m_ref, g_ref, wr_ref, h_ref, lg_ref):
    m = m_ref[0, 0]
    h = _modulate(x_ref[0], g_ref[...], m[3:4], m[4:5])
    h_hi = h.astype(BF16)
    h_ref[0] = h_hi
    h_lo = (h - h_hi.astype(F32)).astype(BF16)
    wr = wr_ref[...]
    w_hi = wr.astype(BF16)
    w_lo = (wr - w_hi.astype(F32)).astype(BF16)
    lg = (jnp.dot(h_hi, w_hi, preferred_element_type=F32)
          + (jnp.dot(h_hi, w_lo, preferred_element_type=F32) + jnp.dot(h_lo, w_hi, preferred_element_type=F32)))
    lg_ref[0] = lg.T[:lg_ref.shape[1]]


def _topk_kernel(lg_ref, up_ref, aff_ref, pos_ref, *, cap):
    lg = lg_ref[...]
    nb, ne, n = lg.shape
    e = jnp.exp(lg - jnp.max(lg, axis=1, keepdims=True))
    aff3 = e / jnp.sum(e, axis=1, keepdims=True)
    aff_ref[...] = aff3
    aff = aff3.reshape(nb * ne, n)

    def bit_step(i, thr):
        cand = thr | lax.shift_left(jnp.int32(1), 30 - i)
        cnt = jnp.sum(jnp.where(aff >= pltpu.bitcast(cand, F32), 1.0, 0.0), axis=-1, keepdims=True)
        return jnp.where(cnt >= cap, cand, thr)

    thr = lax.fori_loop(0, 31, bit_step, jnp.zeros((aff.shape[0], 1), jnp.int32))
    thr = pltpu.bitcast(thr, F32)
    above = aff > thr
    tied = aff == thr
    room = cap - jnp.sum(jnp.where(above, 1.0, 0.0), axis=-1, keepdims=True)
    upper = up_ref[...]
    blk = upper.shape[0]
    tie_rank = _lane_cumsum(jnp.where(tied, 1.0, 0.0), upper, blk)
    sel = jnp.logical_or(above, jnp.logical_and(tied, tie_rank <= room))
    slot = _lane_cumsum(jnp.where(sel, 1.0, 0.0), upper, blk) - 1.0
    pos_ref[...] = jnp.where(sel, slot, -1.0).astype(jnp.int32).reshape(nb, ne, n)


def _router(xcat, modsel, g2, wr, upper, *, rt, n, row_tile0, sel):
    b, ta, dm = xcat.shape
    n_tiles = n // rt
    cap = max(1, EC_FACTOR * n // N_EXPERTS)
    ne = N_EXPERTS
    h, logits = pl.pallas_call(
        _router_kernel,
        grid=(b, n_tiles),
        in_specs=[pl.BlockSpec((1, rt, dm), lambda bb, r: (bb, row_tile0 + r, 0)),
                  pl.BlockSpec((1, 1, 6, dm), lambda bb, r: (bb, sel, 0, 0)),
                  pl.BlockSpec((1, dm), lambda bb, r: (0, 0)),
                  pl.BlockSpec(wr.shape, lambda bb, r: (0, 0))],
        out_specs=[pl.BlockSpec((1, rt, dm), lambda bb, r: (bb, r, 0)),
                   pl.BlockSpec((1, ne, rt), lambda bb, r: (bb, 0, r))],
        out_shape=[jax.ShapeDtypeStruct((b, n, dm), BF16),
                   jax.ShapeDtypeStruct((b, ne, n), F32)],
        compiler_params=_params("parallel", "arbitrary"),
        name="moe_router",
    )(xcat, modsel, g2, wr)
    whole = lambda i: (0, 0, 0)
    aff, pos = pl.pallas_call(
        functools.partial(_topk_kernel, cap=cap),
        grid=(1,),
        in_specs=[pl.BlockSpec((b, ne, n), whole),
                  pl.BlockSpec(upper.shape, lambda i: (0, 0))],
        out_specs=[pl.BlockSpec((b, ne, n), whole),
                   pl.BlockSpec((b, ne, n), whole)],
        out_shape=[jax.ShapeDtypeStruct((b, ne, n), F32),
                   jax.ShapeDtypeStruct((b, ne, n), jnp.int32)],
        compiler_params=_params("arbitrary"),
        name="moe_topk",
    )(logits, upper)
    return h, aff, pos


MOE_EXPERT_BLOCK = 4
MOE_TOKEN_BLOCK = 256
MOE_WINDOW = 64


def _slot_onehot(pos_ref, cap):
    hits = []
    for i in range(pos_ref.shape[1]):
        pos = pos_ref[0, i]
        hits.append(lax.broadcasted_iota(jnp.int32, (cap, pos.shape[1]), 0) == pos)
    pick = jnp.concatenate([jnp.where(hit, 1.0, 0.0).astype(BF16) for hit in hits], axis=0)
    return pick, hits


def _moe_plan(pos, *, cap, tb, w):
    nb, ne, _, n = pos.shape
    nk = n // tb
    assert w % 16 == 0 and (cap - w) % 16 == 0 and n % tb == 0
    cnt = jnp.sum((pos.reshape(nb, ne, nk, tb) >= 0).astype(jnp.int32), axis=-1)
    first = jnp.cumsum(cnt, axis=-1) - cnt
    start = jnp.minimum((first // 16) * 16, cap - w)
    fits = first - start + cnt <= w
    dense_group = jnp.logical_not(jnp.all(fits.reshape(nb, ne // MOE_EXPERT_BLOCK, -1), axis=-1))
    dense_block = jnp.logical_not(jnp.all(fits, axis=1))
    flat = lambda a: a.reshape(-1).astype(jnp.int32)
    return flat(start), flat(dense_group), flat(dense_block)


def _moe_gather_kernel(start_ref, dense_ref, pos_ref, aff_ref, h_ref, xs_ref, gate_ref, *, cap, tb, w):
    b = pl.program_id(0)
    g = pl.program_id(1)
    eb, n = pos_ref.shape[1], pos_ref.shape[3]
    nk = n // tb
    dense = dense_ref[b * pl.num_programs(1) + g]

    @pl.when(dense != 0)
    def _():
        pick, hits = _slot_onehot(pos_ref, cap)
        xs = jnp.dot(pick, h_ref[0], preferred_element_type=F32).astype(BF16)
        xs_ref[0] = xs.reshape(xs_ref.shape[1:])
        for i, hit in enumerate(hits):
            gate = jnp.sum(jnp.where(hit, aff_ref[0, i], 0.0), axis=-1, keepdims=True)
            gate_ref[0, i] = jnp.broadcast_to(gate, gate_ref.shape[2:])

    @pl.when(dense == 0)
    def _():
        xs_ref[...] = jnp.zeros(xs_ref.shape, BF16)
        gate_ref[...] = jnp.zeros(gate_ref.shape, F32)
        slot = lax.broadcasted_iota(jnp.int32, (w, tb), 0)
        for k in range(nk):
            cols = slice(k * tb, (k + 1) * tb)
            starts = [pl.multiple_of(start_ref[((b * pl.num_programs(1) + g) * eb + i) * nk + k], 16)
                      for i in range(eb)]
            hits = [(slot + starts[i]) == pos_ref[0, i, :, cols] for i in range(eb)]
            pick = jnp.concatenate([jnp.where(hit, 1.0, 0.0).astype(BF16) for hit in hits], axis=0)
            part = jnp.dot(pick, h_ref[0, cols, :], preferred_element_type=F32).astype(BF16)
            for i in range(eb):
                rows = pl.ds(starts[i], w)
                xs_ref[0, i, rows, :] = xs_ref[0, i, rows, :] + part[i * w:(i + 1) * w]
                gate = jnp.sum(jnp.where(hits[i], aff_ref[0, i, :, cols], 0.0), axis=-1, keepdims=True)
                gate_ref[0, i, rows, :] = gate_ref[0, i, rows, :] + gate


def _moe_gather(start, dense, pos, aff, h, *, cap, tb, w):
    nb, ne, _, n = pos.shape
    dm = h.shape[2]
    eb = MOE_EXPERT_BLOCK
    grid_spec = pltpu.PrefetchScalarGridSpec(
        num_scalar_prefetch=2,
        grid=(nb, ne // eb),
        in_specs=[pl.BlockSpec((1, eb, 1, n), lambda b, e, *_: (b, e, 0, 0)),
                  pl.BlockSpec((1, eb, 1, n), lambda b, e, *_: (b, e, 0, 0)),
                  pl.BlockSpec((1, n, dm), lambda b, e, *_: (b, 0, 0), pipeline_mode=pl.Buffered(1))],
        out_specs=[pl.BlockSpec((1, eb, cap, dm), lambda b, e, *_: (b, e, 0, 0)),
                   pl.BlockSpec((1, eb, cap, 128), lambda b, e, *_: (b, e, 0, 0))])
    return pl.pallas_call(
        functools.partial(_moe_gather_kernel, cap=cap, tb=tb, w=w),
        grid_spec=grid_spec,
        out_shape=[jax.ShapeDtypeStruct((nb, ne, cap, dm), BF16),
                   jax.ShapeDtypeStruct((nb, ne, cap, 128), F32)],
        compiler_params=_params("parallel", "arbitrary"),
        name="moe_gather",
    )(start, dense, pos, aff, h)


def _moe_ffn_kernel(*refs, n_sets):
    sets = [refs[3 * i:3 * i + 3] for i in range(n_sets)]
    wg_ref, wu_ref, wd_ref = refs[3 * n_sets:3 * n_sets + 3]
    outs = refs[3 * n_sets + 3:4 * n_sets + 3]
    wgb_ref, wub_ref, wdb_ref = refs[4 * n_sets + 3:]
    c = pl.program_id(1)

    @pl.when(c == 0)
    def _():
        rows = 256
        for src, dst in ((wg_ref, wgb_ref), (wu_ref, wub_ref), (wd_ref, wdb_ref)):
            for r0 in range(0, dst.shape[0], rows):
                dst[r0:r0 + rows, :] = src[0, 0, r0:r0 + rows, :].astype(BF16)

    def run(xs_ref, gate_ref, m_ref, y_ref):
        nb, _, cap, dm = xs_ref.shape
        xs = xs_ref[...].reshape(nb * cap, dm)
        a = jnp.dot(xs, wgb_ref[...], preferred_element_type=F32)
        u = jnp.dot(xs, wub_ref[...], preferred_element_type=F32)
        act = (_silu(a) * u).astype(BF16)
        out = jnp.dot(act, wdb_ref[...], preferred_element_type=F32)
        for i in range(nb):
            y = out[i * cap:(i + 1) * cap] * gate_ref[i, 0][:, 0:1]
            y_ref[i, 0] = (y * m_ref[i, 0][5:6]).astype(BF16)

    if n_sets == 1:
        run(*sets[0], outs[0])
    else:
        last = pl.num_programs(1) - 1
        pl.when(c < last)(lambda: run(*sets[0], outs[0]))
        pl.when(c == last)(lambda: run(*sets[1], outs[1]))


def _moe_ffn(slot_sets, modsel, wg, wu, wd, layer):
    xs, gate, _ = slot_sets[0]
    nb_all, ne, cap, dm = xs.shape
    df = wg.shape[3]
    nb = 2 if nb_all % 2 == 0 else 1
    n_lat = nb_all // nb
    lat = lambda e, c: (jnp.minimum(c, n_lat - 1), e, 0, 0)
    in_specs = [pl.BlockSpec((nb, 1, cap, dm), lat),
                pl.BlockSpec((nb, 1, cap, 128), lat),
                pl.BlockSpec((nb, 1, 6, dm), lambda e, c: (jnp.minimum(c, n_lat - 1), slot_sets[0][2], 0, 0))]
    out_specs = [pl.BlockSpec((nb, 1, cap, dm), lat)]
    out_shape = [jax.ShapeDtypeStruct(xs.shape, BF16)]
    operands = [xs, gate, modsel]
    if len(slot_sets) == 2:
        xs2, gate2, sel2 = slot_sets[1]
        nb2, _, cap2, _ = xs2.shape
        fixed = lambda e, c: (0, e, 0, 0)
        in_specs += [pl.BlockSpec((nb2, 1, cap2, dm), fixed),
                     pl.BlockSpec((nb2, 1, cap2, 128), fixed),
                     pl.BlockSpec((nb2, 1, 6, dm), lambda e, c: (0, sel2, 0, 0))]
        out_specs.append(pl.BlockSpec((nb2, 1, cap2, dm), fixed))
        out_shape.append(jax.ShapeDtypeStruct(xs2.shape, BF16))
        operands += [xs2, gate2, modsel]
    weight = lambda e, c: (layer, e, 0, 0)
    in_specs += [pl.BlockSpec((1, 1, dm, df), weight), pl.BlockSpec((1, 1, dm, df), weight),
                 pl.BlockSpec((1, 1, df, dm), weight)]
    return pl.pallas_call(
        functools.partial(_moe_ffn_kernel, n_sets=len(slot_sets)),
        grid=(ne, n_lat + len(slot_sets) - 1),
        in_specs=in_specs,
        out_specs=out_specs,
        out_shape=out_shape,
        scratch_shapes=[pltpu.VMEM((dm, df), BF16), pltpu.VMEM((dm, df), BF16), pltpu.VMEM((df, dm), BF16)],
        compiler_params=_params("arbitrary", "arbitrary"),
        name="moe_ffn",
    )(*operands, wg, wu, wd)


def _moe_scatter_kernel(start_ref, dense_ref, posc_ref, y_ref, x_ref, *rest, cap, w):
    b = pl.program_id(0)
    k = pl.program_id(1)
    nk = pl.num_programs(1)
    ne = y_ref.shape[1]
    tb = posc_ref.shape[1]
    dense = dense_ref[b * nk + k]
    o_ref = rest[-1]

    def emit(update):
        total = x_ref[0] + update
        if len(rest) == 2:
            total = total * lax.rsqrt(jnp.mean(total * total, axis=-1, keepdims=True) + NORM_EPS) * rest[0][...]
        o_ref[0] = total

    @pl.when(dense != 0)
    def _():
        slot = lax.broadcasted_iota(jnp.int32, (tb, cap), 1)
        pick = jnp.concatenate([jnp.where(slot == posc_ref[0, :, i:i + 1], 1.0, 0.0).astype(BF16)
                                for i in range(ne)], axis=1)
        y = y_ref[0].reshape(ne * cap, y_ref.shape[3])
        emit(jnp.dot(pick, y, preferred_element_type=F32))

    @pl.when(dense == 0)
    def _():
        per = 128 // w
        lane = lax.broadcasted_iota(jnp.int32, (tb, 128), 1)
        tiles = []
        wins = []
        for j in range(ne // per):
            col = None
            for u in reversed(range(per)):
                i = j * per + u
                s = pl.multiple_of(start_ref[(b * ne + i) * nk + k], 16)
                rel = posc_ref[0, :, i:i + 1] - s
                target = jnp.where(jnp.logical_and(rel >= 0, rel < w), rel + u * w, -1)
                col = target if col is None else jnp.where(lane < (u + 1) * w, target, col)
            tiles.append(jnp.where(lane == col, 1.0, 0.0).astype(BF16))
            for u in range(per):
                i = j * per + u
                s = pl.multiple_of(start_ref[(b * ne + i) * nk + k], 16)
                wins.append(y_ref[0, i, pl.ds(s, w), :])
        pick = jnp.concatenate(tiles, axis=1)
        emit(jnp.dot(pick, jnp.concatenate(wins, axis=0), preferred_element_type=F32))


def _moe_scatter(start, dense, pos, y, xcat, *, bb, row0, tb, w, final_gain=None):
    nb, ne, cap, dm = y.shape
    n = pos.shape[3]
    assert 128 % w == 0 and ne % (128 // w) == 0
    posc = jnp.transpose(pos.reshape(nb, ne, n), (0, 2, 1))
    posc = jnp.pad(posc, ((0, 0), (0, 0), (0, GATE_PAD - ne)), constant_values=-1)
    x_index = (lambda b, k, *_: (b, k, 0)) if bb == 1 else (lambda b, k, *_: (k, row0 // tb, 0))
    in_specs = [pl.BlockSpec((1, tb, GATE_PAD), lambda b, k, *_: (b, k, 0)),
                pl.BlockSpec((1, ne, cap, dm), lambda b, k, *_: (b, 0, 0, 0)),
                pl.BlockSpec((1, tb, dm), x_index)]
    operands = (start, dense, posc, y, xcat)
    if final_gain is None:
        out_shape, aliases = jax.ShapeDtypeStruct(xcat.shape, F32), {4: 0}
    else:
        assert bb == 1
        in_specs.append(pl.BlockSpec((1, dm), lambda b, k, *_: (0, 0)))
        operands += (final_gain,)
        out_shape, aliases = jax.ShapeDtypeStruct((nb, n, dm), F32), {}
    grid_spec = pltpu.PrefetchScalarGridSpec(
        num_scalar_prefetch=2,
        grid=(nb, n // tb),
        in_specs=in_specs,
        out_specs=pl.BlockSpec((1, tb, dm), x_index))
    return pl.pallas_call(
        functools.partial(_moe_scatter_kernel, cap=cap, w=w),
        grid_spec=grid_spec,
        out_shape=out_shape,
        input_output_aliases=aliases,
        compiler_params=_params("parallel", "arbitrary"),
        name="moe_scatter",
    )(*operands)


def _ec_moe(xcat, modsel, g2, wr, upper, wg, wu, wd, layer, *, rt, seq, ctx, with_ctx, final_gain=None):
    b = xcat.shape[0]
    ne = N_EXPERTS
    rt_lat = 2 * rt if seq % (2 * rt) == 0 else rt
    h, aff, pos = _router(xcat, modsel, g2, wr, upper, rt=rt_lat, n=seq, row_tile0=0, sel=0)
    cap = max(1, EC_FACTOR * seq // N_EXPERTS)
    pos = pos.reshape(b, ne, 1, seq)
    tb = min(MOE_TOKEN_BLOCK, ctx)
    win = dict(tb=tb, w=min(MOE_WINDOW, cap))
    start, dense_group, dense_block = _moe_plan(pos, cap=cap, **win)
    xs, gate = _moe_gather(start, dense_group, pos, aff.reshape(b, ne, 1, seq), h, cap=cap, **win)
    if not with_ctx:
        (y,) = _moe_ffn([(xs, gate, 0)], modsel, wg, wu, wd, layer)
        return _moe_scatter(start, dense_block, pos, y, xcat, bb=1, row0=0, final_gain=final_gain, **win)
    hc, affc, posc = _router(xcat, modsel, g2, wr, upper, rt=rt, n=ctx, row_tile0=seq // rt, sel=1)
    capc = max(1, EC_FACTOR * ctx // N_EXPERTS)
    offs = (jnp.arange(b, dtype=jnp.int32) * capc)[:, None, None]
    posc = jnp.where(posc >= 0, posc + offs, -1)
    posc = jnp.transpose(posc, (1, 0, 2)).reshape(1, ne, 1, b * ctx)
    affc = jnp.transpose(affc, (1, 0, 2)).reshape(1, ne, 1, b * ctx)
    winc = dict(tb=tb, w=min(MOE_WINDOW, b * capc))
    assert tb == ctx
    startc, dense_groupc, dense_blockc = _moe_plan(posc, cap=b * capc, **winc)
    xsc, gatec = _moe_gather(startc, dense_groupc, posc, affc, hc.reshape(1, b * ctx, -1), cap=b * capc, **winc)
    y, yc = _moe_ffn([(xs, gate, 0), (xsc, gatec, 1)], modsel, wg, wu, wd, layer)
    xcat = _moe_scatter(start, dense_block, pos, y, xcat, bb=1, row0=0, final_gain=final_gain, **win)
    return _moe_scatter(startc, dense_blockc, posc, yc, xcat, bb=b, row0=seq, **winc)


def _rope_tables(seq, ctx):
    rows = seq // GRID_W
    row_ids = jnp.repeat(jnp.arange(rows, dtype=F32), GRID_W)
    col_ids = jnp.tile(jnp.arange(GRID_W, dtype=F32), rows)
    half = A_HEAD_DIM // 2
    inv = ROPE_THETA ** (-jnp.arange(0, half, 2, dtype=F32) / half)
    ang_r = row_ids[:, None] * inv
    ang_c = col_ids[:, None] * inv
    cos = jnp.concatenate([jnp.cos(ang_r)] * 2 + [jnp.cos(ang_c)] * 2, axis=1)
    sin = jnp.concatenate([-jnp.sin(ang_r), jnp.sin(ang_r), -jnp.sin(ang_c), jnp.sin(ang_c)], axis=1)
    cos = jnp.tile(cos, (1, A_HEADS))
    sin = jnp.tile(sin, (1, A_HEADS))
    cos = jnp.concatenate([cos, jnp.ones((ctx, A_Q), F32)], axis=0)
    sin = jnp.concatenate([sin, jnp.zeros((ctx, A_Q), F32)], axis=0)
    return cos, sin


def kernel(x, c, ctx, c_ctx, w_mod, b_mod, g_norm1, g_norm2, w_in_even, w_out_even, g_qnorm, g_knorm, w_conv,
           b_gate, g_hnorm, w_in_odd, w_pool_grp, s_pool, w_router, w_exp_gate, w_exp_up, w_exp_down, g_final):
    b, seq, dm = x.shape
    n_ctx = ctx.shape[1]
    depth = w_mod.shape[0]
    rt = 256 if (seq % 256 == 0 and n_ctx % 256 == 0) else 128
    assert seq % rt == 0 and n_ctx % rt == 0 and seq % n_ctx == 0 and seq % GRID_W == 0
    nlt = seq // rt
    nt = (seq + n_ctx) // rt

    xcat = jnp.concatenate([x, ctx, jnp.zeros((b, seq - n_ctx, dm), F32)], axis=1)

    rows = -(-(b + 1) // 8) * 8
    cond = jnp.zeros((rows, dm), F32).at[:b].set(c).at[b].set(c_ctx)
    mods = _adaln(cond, w_mod, b_mod)
    lat = mods[:, :b].reshape(depth, b, 1, 6, dm)
    cx = jnp.broadcast_to(mods[:, b].reshape(depth, 1, 1, 6, dm), (depth, b, 1, 6, dm))
    modsel = jnp.concatenate([lat, cx], axis=2)

    cos, sin = _rope_tables(seq, n_ctx)
    lane = jnp.arange(A_Q)
    gmat = (lane[:, None] // A_HEAD_DIM == lane[None, :] // A_HEAD_DIM).astype(BF16)
    blk = min(256, n_ctx)
    tri = jnp.arange(blk)
    upper = (tri[:, None] <= tri[None, :]).astype(BF16)

    wg, wu, wd = w_exp_gate, w_exp_up, w_exp_down
    wr = jnp.pad(w_router, ((0, 0), (0, 0), (0, GATE_PAD - N_EXPERTS)))

    for i in range(depth):
        ctx_next = any(j % 2 == 0 for j in range(i + 1, depth))
        n_upd = nt if ctx_next else nlt
        ms = modsel[i]
        if i % 2 == 0:
            e = i // 2
            w_in = w_in_even[e]
            pad = jnp.zeros((dm, GATE_PAD - N_GATES), F32)
            w_in = jnp.concatenate([w_in, pad], axis=1).astype(BF16)
            bg = jnp.concatenate([b_gate[e], jnp.zeros((GATE_PAD - N_GATES,), F32)])[None]
            q, k, v, qk, vbt, ob, gt, gc, gr = _inproj(
                xcat, ms, g_norm1[i][None], w_in, cos, sin,
                jnp.tile(g_qnorm[e], A_HEADS)[None], jnp.tile(g_knorm[e], A_KV_HEADS)[None], gmat, bg,
                rt=rt, ta=seq + n_ctx, n_lat_tiles=nlt)
            oa = _attention(q, k, v, rt=rt, n_tiles=n_upd, n_lat_tiles=nlt, seq=seq)
            qct = _short_conv(qk, w_conv[e], seq=seq, keys=False)
            kc = _short_conv(qk, w_conv[e], seq=seq, keys=True)
            gates_row = jnp.swapaxes(jnp.concatenate([gt[:, :, :N_GATES], gc[:, :, :N_GATES]], axis=2), 1, 2)
            hf, hb = _mlstm(qct, kc, vbt, gates_row, gr, seq=seq, blk=rt)
            xcat = _outproj(xcat, ms, oa, hf, hb, ob, g_hnorm[e][None], w_out_even[e].astype(BF16),
                            rt=rt, n_tiles=n_upd, n_lat_tiles=nlt)
        else:
            o = i // 2
            w_in = w_in_odd[o].astype(BF16)
            wgrp = w_pool_grp[o].astype(BF16)
            rt_lat = 2 * rt if nlt % 2 == 0 else rt
            u = _mod_matmul(xcat, ms, g_norm1[i][None], w_in, rt=rt_lat, tile0=0, n_tiles=seq // rt_lat, sel=0)
            if ctx_next:
                uc = _mod_matmul(xcat, ms, g_norm1[i][None], w_in, rt=rt, tile0=nlt, n_tiles=nt - nlt, sel=1)
            xcat = _pool(u, xcat, ms, wgrp, s_pool[o][None], n=seq, row_block=0, sel=0)
            if ctx_next:
                xcat = _pool(uc, xcat, ms, wgrp, s_pool[o][None], n=n_ctx, row_block=seq // n_ctx, sel=1)
        xcat = _ec_moe(xcat, ms, g_norm2[i][None], wr[i], upper, wg, wu, wd, i,
                       rt=rt, seq=seq, ctx=n_ctx, with_ctx=ctx_next,
                       final_gain=g_final[None] if i == depth - 1 else None)
    return xcat
```

```python
import functools

import jax
import jax.numpy as jnp
from jax import lax
from jax.experimental import pallas as pl
from jax.experimental.pallas import tpu as pltpu

D_MODEL = 1024
GRID_W = 64
A_HEADS = 8
A_KV_HEADS = 2
A_HEAD_DIM = 64
ROPE_THETA = 10000.0
B_HEADS = 4
B_HEAD_DIM = 128
B_CHUNK = 128
POOL_WINDOWS = (2, 4, 8, 16)
POOL_GROUP = D_MODEL // 4
N_EXPERTS = 16
EC_FACTOR = 2
NORM_EPS = 1e-6

A_Q = A_HEADS * A_HEAD_DIM
A_KV = A_KV_HEADS * A_HEAD_DIM
B_W = B_HEADS * B_HEAD_DIM
N_GATES = 2 * 2 * B_HEADS
GATE_PAD = 128
C_QA, C_KA, C_VA = 0, A_Q, A_Q + A_KV
C_QK = A_Q + 2 * A_KV
C_VB = C_QK + 2 * B_W
C_OB = C_VB + B_W
C_GT = C_OB + B_W
EVEN_COLS = C_GT + GATE_PAD
POOL_HALO = 16

F32 = jnp.float32
BF16 = jnp.bfloat16
HIGHEST = lax.Precision.HIGHEST
NT_DIMS = (((1,), (1,)), ((), ()))
TN_DIMS = (((0,), (0,)), ((), ()))
VMEM_LIMIT = 56 * 1024 * 1024


def _params(*sem):
    return pltpu.CompilerParams(dimension_semantics=sem, vmem_limit_bytes=VMEM_LIMIT)


def _modulate(x, g, shift, scale):
    y = x * lax.rsqrt(jnp.mean(x * x, axis=-1, keepdims=True) + NORM_EPS)
    return (y * g) * (1.0 + scale) + shift


def _silu(x):
    return x * jax.nn.sigmoid(x)


def _adaln_kernel(c_ref, w_ref, b_ref, o_ref):
    s = _silu(c_ref[...])
    o_ref[0] = jnp.dot(s, w_ref[0], precision=HIGHEST, preferred_element_type=F32) + b_ref[0]


def _adaln(cond, w_mod, b_mod):
    depth, dm, n6 = w_mod.shape
    rows = cond.shape[0]
    tn = 1536
    return pl.pallas_call(
        _adaln_kernel,
        grid=(depth, n6 // tn),
        in_specs=[pl.BlockSpec((rows, dm), lambda l, j: (0, 0)),
                  pl.BlockSpec((1, dm, tn), lambda l, j: (l, 0, j)),
                  pl.BlockSpec((1, 1, tn), lambda l, j: (l, 0, j))],
        out_specs=pl.BlockSpec((1, rows, tn), lambda l, j: (l, 0, j)),
        out_shape=jax.ShapeDtypeStruct((depth, rows, n6), F32),
        compiler_params=_params("arbitrary", "arbitrary"),
        name="adaln",
    )(cond, w_mod, b_mod.reshape(depth, 1, n6))


def _group_mean_sq(x, gmat, width):
    s = jnp.dot((x * x).astype(BF16), gmat, preferred_element_type=F32)
    return s * (1.0 / width)


def _dot_exact_lhs(mat, x):
    x1 = x.astype(BF16)
    r1 = x - x1.astype(F32)
    x2 = r1.astype(BF16)
    x3 = (r1 - x2.astype(F32)).astype(BF16)
    return (jnp.dot(mat, x1, preferred_element_type=F32) + jnp.dot(mat, x2, preferred_element_type=F32)
            + jnp.dot(mat, x3, preferred_element_type=F32))


def _rope(x, cos, sin):
    w = x.shape[1]
    lane = lax.broadcasted_iota(jnp.int32, x.shape, 1)
    first = (lane % 32) < 16
    partner = jnp.where(first, pltpu.roll(x, w - 16, 1), pltpu.roll(x, 16, 1))
    return x * cos + partner * sin


def _inproj_kernel(x_ref, m_ref, g1_ref, w_ref, cos_ref, sin_ref, gq_ref, gk_ref, gm_ref, bg_ref, lp_ref,
                   q_ref, k_ref, v_ref, qk_ref, vb_ref, ob_ref, gt_ref, gc_ref, gr_ref):
    m = m_ref[0, 0]
    h = _modulate(x_ref[0], g1_ref[...], m[0:1], m[1:2]).astype(BF16)

    def mm(lo, hi):
        return jnp.dot(h, w_ref[:, lo:hi], preferred_element_type=F32)

    cos = cos_ref[...]
    sin = sin_ref[...]
    gm = gm_ref[...]
    qa = mm(C_QA, C_KA)
    qa = qa * lax.rsqrt(_group_mean_sq(qa, gm, A_HEAD_DIM) + NORM_EPS) * gq_ref[...]
    q_ref[0] = (_rope(qa, cos, sin) * (A_HEAD_DIM ** -0.5)).astype(BF16)
    ka = mm(C_KA, C_VA)
    ka = ka * lax.rsqrt(_group_mean_sq(ka, gm[:A_KV, :A_KV], A_HEAD_DIM) + NORM_EPS) * gk_ref[...]
    k_ref[0] = _rope(ka, cos[:, :A_KV], sin[:, :A_KV]).astype(BF16)
    va = mm(C_VA, C_QK).astype(BF16)
    ones = jnp.ones((va.shape[0], A_HEAD_DIM), BF16)
    v_ref[0] = jnp.concatenate(
        [piece for g in range(A_KV_HEADS) for piece in (va[:, g * A_HEAD_DIM:(g + 1) * A_HEAD_DIM], ones)], axis=1)
    qk_ref[0] = mm(C_QK, C_VB).astype(BF16)
    vb_ref[0] = mm(C_VB, C_OB).T.astype(BF16)
    ob_ref[0] = mm(C_OB, C_GT).astype(BF16)
    gt = mm(C_GT, EVEN_COLS) + bg_ref[...]
    lane = lax.broadcasted_iota(jnp.int32, gt.shape, 1)
    log_sig = jnp.minimum(gt, 0.0) - jnp.log1p(jnp.exp(-jnp.abs(gt)))
    gt = jnp.where((lane % (2 * B_HEADS)) >= B_HEADS, log_sig, gt)
    gt_ref[0] = gt
    prefix = _dot_exact_lhs(lp_ref[...], gt)
    row = lax.broadcasted_iota(jnp.int32, gt.shape, 0)
    total = prefix[gt.shape[0] - 1:gt.shape[0]]
    for c in range(gt.shape[0] // B_CHUNK - 2, -1, -1):
        total = jnp.where(row < (c + 1) * B_CHUNK, prefix[(c + 1) * B_CHUNK - 1:(c + 1) * B_CHUNK], total)
    suffix = total - prefix + gt
    cum = jnp.where((lane % (4 * B_HEADS)) < 2 * B_HEADS, prefix, suffix)
    gc_ref[0] = cum
    gr_ref[0] = gt - pltpu.roll(cum, GATE_PAD - B_HEADS, 1)


def _inproj(xcat, modsel, g1, w, cos, sin, gq, gk, gmat, bgate, *, rt, ta, n_lat_tiles):
    b, _, dm = xcat.shape
    nt = ta // rt
    row = lambda bb, r: (bb, r, 0)
    full = lambda bb, r: (0, 0)
    out_widths = (A_Q, A_KV, 2 * A_KV, 2 * B_W, B_W, B_W, GATE_PAD, GATE_PAD, GATE_PAD)
    out_dtypes = (BF16, BF16, BF16, BF16, BF16, BF16, F32, F32, F32)
    out_specs = [pl.BlockSpec((1, rt, wd), row) for wd in out_widths]
    out_shapes = [jax.ShapeDtypeStruct((b, ta, wd), dt) for wd, dt in zip(out_widths, out_dtypes)]
    vb_slot = 4
    out_specs[vb_slot] = pl.BlockSpec((1, B_W, rt), lambda bb, r: (bb, 0, r))
    out_shapes[vb_slot] = jax.ShapeDtypeStruct((b, B_W, ta), BF16)
    idx = jnp.arange(rt)
    same_chunk = idx[:, None] // B_CHUNK == idx[None, :] // B_CHUNK
    lower = jnp.logical_and(same_chunk, idx[None, :] <= idx[:, None]).astype(BF16)
    return pl.pallas_call(
        _inproj_kernel,
        grid=(b, nt),
        in_specs=[pl.BlockSpec((1, rt, dm), row),
                  pl.BlockSpec((1, 1, 6, dm), lambda bb, r: (bb, (r >= n_lat_tiles).astype(jnp.int32), 0, 0)),
                  pl.BlockSpec((1, dm), full),
                  pl.BlockSpec((dm, EVEN_COLS), full),
                  pl.BlockSpec((rt, A_Q), lambda bb, r: (r, 0)),
                  pl.BlockSpec((rt, A_Q), lambda bb, r: (r, 0)),
                  pl.BlockSpec((1, A_Q), full),
                  pl.BlockSpec((1, A_KV), full),
                  pl.BlockSpec((A_Q, A_Q), full),
                  pl.BlockSpec((1, GATE_PAD), full),
                  pl.BlockSpec((rt, rt), full)],
        out_specs=out_specs,
        out_shape=out_shapes,
        compiler_params=_params("parallel", "arbitrary"),
        name="even_inproj",
    )(xcat, modsel, g1, w, cos, sin, gq, gk, gmat, bgate, lower)


def _attn_kernel(q_ref, k_ref, v_ref, o_ref, *, n_lat_tiles, n_tiles, seq):
    group = A_HEADS // A_KV_HEADS

    def attend(key_lo):
        q = q_ref[0]
        for hd in range(A_HEADS):
            g = hd // group
            qh = q[:, hd * A_HEAD_DIM:(hd + 1) * A_HEAD_DIM]
            kg = k_ref[0, key_lo:, g * A_HEAD_DIM:(g + 1) * A_HEAD_DIM]
            vg = v_ref[0, key_lo:, 2 * g * A_HEAD_DIM:2 * (g + 1) * A_HEAD_DIM]
            s = lax.dot_general(qh, kg, NT_DIMS, preferred_element_type=F32)
            p = jnp.exp(s - jnp.max(s, axis=-1, keepdims=True))
            o = jnp.dot(p.astype(BF16), vg, preferred_element_type=F32)
            o = o[:, :A_HEAD_DIM] / o[:, A_HEAD_DIM:]
            o_ref[0, :, hd * A_HEAD_DIM:(hd + 1) * A_HEAD_DIM] = o.astype(BF16)

    if n_tiles == n_lat_tiles:
        attend(0)
    else:
        r = pl.program_id(1)
        pl.when(r < n_lat_tiles)(lambda: attend(0))
        pl.when(r >= n_lat_tiles)(lambda: attend(seq))


def _attention(q, k, v, *, rt, n_tiles, n_lat_tiles, seq):
    b, ta, _ = q.shape
    return pl.pallas_call(
        functools.partial(_attn_kernel, n_lat_tiles=n_lat_tiles, n_tiles=n_tiles, seq=seq),
        grid=(b, n_tiles),
        in_specs=[pl.BlockSpec((1, rt, A_Q), lambda bb, r: (bb, r, 0)),
                  pl.BlockSpec((1, ta, A_KV), lambda bb, r: (bb, 0, 0)),
                  pl.BlockSpec((1, ta, 2 * A_KV), lambda bb, r: (bb, 0, 0))],
        out_specs=pl.BlockSpec((1, rt, A_Q), lambda bb, r: (bb, r, 0)),
        out_shape=jax.ShapeDtypeStruct((b, n_tiles * rt, A_Q), BF16),
        compiler_params=_params("parallel", "arbitrary"),
        name="gqa_attention",
    )(q, k, v)


def _conv_kernel(x_ref, w_ref, o_ref, *, seq, keys):
    x = x_ref[0].astype(F32)
    ta = x.shape[0]
    w = w_ref[...]
    row = lax.broadcasted_iota(jnp.int32, (ta, 1), 0)
    prev = jnp.where(jnp.logical_or(row == 0, row == seq), 0.0, pltpu.roll(x, 1, 0))
    nxt = jnp.where(jnp.logical_or(row == seq - 1, row == ta - 1), 0.0, pltpu.roll(x, ta - 1, 0))
    y = _silu(prev * w[0:1] + x * w[1:2] + nxt * w[2:3])
    if keys:
        o_ref[0] = (y * (B_HEAD_DIM ** -0.5)).astype(BF16)
    else:
        o_ref[0] = y.T.astype(BF16)


def _short_conv(qk, w_conv, *, seq, keys):
    b, ta, _ = qk.shape
    ct = 256
    c0 = B_W // ct if keys else 0
    if keys:
        out_spec = pl.BlockSpec((1, ta, ct), lambda bb, j: (bb, 0, j))
        out_shape = jax.ShapeDtypeStruct((b, ta, B_W), BF16)
    else:
        out_spec = pl.BlockSpec((1, ct, ta), lambda bb, j: (bb, j, 0))
        out_shape = jax.ShapeDtypeStruct((b, B_W, ta), BF16)
    return pl.pallas_call(
        functools.partial(_conv_kernel, seq=seq, keys=keys),
        grid=(b, B_W // ct),
        in_specs=[pl.BlockSpec((1, ta, ct), lambda bb, j: (bb, 0, c0 + j)),
                  pl.BlockSpec((w_conv.shape[0], ct), lambda bb, j: (0, c0 + j))],
        out_specs=out_spec,
        out_shape=out_shape,
        compiler_params=_params("parallel", "arbitrary"),
        name="mlstm_conv_k" if keys else "mlstm_conv_q",
    )(qk, w_conv)


def _mlstm_chain(qt, k, vat, ig, b_row, r_col, total, ca, m_old, tri):
    log_d = jnp.where(tri, b_row + r_col, -jnp.inf)
    log_inter = m_old + b_row
    m_t = jnp.maximum(log_inter, jnp.max(log_d, axis=0, keepdims=True))
    w_intra = jnp.exp(log_d - m_t) * jnp.dot(k, qt, preferred_element_type=F32)
    w_inter = jnp.exp(log_inter - m_t)
    nd = (w_inter * jnp.dot(ca.astype(BF16), qt, preferred_element_type=F32)
          + jnp.dot(vat, w_intra.astype(BF16), preferred_element_type=F32))
    dv = nd.shape[0] // 2
    h = nd[:dv] / jnp.maximum(jnp.abs(nd[dv:]), jnp.exp(-m_t))
    log_w = total - b_row + ig
    m_new = jnp.maximum(m_old + total, jnp.max(log_w, axis=-1, keepdims=True))
    w_s = jnp.exp(log_w - m_new)
    decay = jnp.exp(m_old + total - m_new)
    ca_new = decay * ca + jnp.dot((vat * w_s).astype(BF16), k, preferred_element_type=F32)
    return h, ca_new, m_new


def _mlstm_kernel(qf_ref, kf_ref, vf_ref, grf_ref, gcf_ref, qb_ref, kb_ref, vb_ref, grb_ref, gcb_ref,
                  hf_ref, hb_ref, c_ref, m_ref):
    @pl.when(pl.program_id(1) == 0)
    def _():
        c_ref[...] = jnp.zeros(c_ref.shape, F32)
        m_ref[...] = jnp.full(m_ref.shape, -1e30, F32)

    t = B_CHUNK
    n_chunks = kf_ref.shape[1] // t
    src = lax.broadcasted_iota(jnp.int32, (t, t), 0)
    tgt = lax.broadcasted_iota(jnp.int32, (t, t), 1)
    ones = jnp.ones((B_HEAD_DIM, t), BF16)
    dirs = ((qf_ref, kf_ref, vf_ref, grf_ref, gcf_ref, hf_ref, src <= tgt, t - 1, range(n_chunks)),
            (qb_ref, kb_ref, vb_ref, grb_ref, gcb_ref, hb_ref, src >= tgt, 0, range(n_chunks - 1, -1, -1)))
    outputs = []
    states = []
    for d, (q_ref, k_ref, v_ref, gr_ref, gc_ref, h_ref, tri, last, order) in enumerate(dirs):
        for hd in range(B_HEADS):
            sl = slice(hd * B_HEAD_DIM, (hd + 1) * B_HEAD_DIM)
            gi = d * 2 * B_HEADS + hd
            fi = gi + B_HEADS
            si = d * B_HEADS + hd
            ca = c_ref[si]
            m = m_ref[si][:, 0:1]
            for ci in order:
                rows = slice(ci * t, (ci + 1) * t)
                ig = gr_ref[0, gi:gi + 1, rows]
                b_row = gr_ref[0, N_GATES + fi:N_GATES + fi + 1, rows]
                r_col = gc_ref[0, rows, gi:gi + 1]
                vat = jnp.concatenate([v_ref[0, sl, rows], ones], axis=0)
                h, ca, m = _mlstm_chain(q_ref[0, sl, rows], k_ref[0, rows, sl], vat, ig, b_row, r_col,
                                        b_row[:, last:last + 1], ca, m, tri)
                outputs.append((h_ref, rows, sl, h))
            states.append((si, ca, m))
    for h_ref, rows, sl, h in outputs:
        h_ref[0, sl, rows] = h.astype(BF16)
    for si, ca, m in states:
        c_ref[si] = ca
        m_ref[si] = jnp.broadcast_to(m, (1, 128))


def _mlstm(qct, kc, vbt, gates_row, gates_col, *, seq, blk):
    b, ta, _ = kc.shape
    nc = ta // blk
    ncl = seq // blk
    ncc = nc - ncl
    fwd = lambda j: jnp.where(j < ncc, ncl + j, j - ncc)
    bwd = lambda j: jnp.where(j < ncc, nc - 1 - j, ncl - 1 - (j - ncc))

    def specs(order):
        feature_major = pl.BlockSpec((1, B_W, blk), lambda bb, j: (bb, 0, order(j)))
        return [feature_major,
                pl.BlockSpec((1, blk, B_W), lambda bb, j: (bb, order(j), 0)),
                feature_major,
                pl.BlockSpec((1, 2 * N_GATES, blk), lambda bb, j: (bb, 0, order(j))),
                pl.BlockSpec((1, blk, GATE_PAD), lambda bb, j: (bb, order(j), 0))]

    n_state = 2 * B_HEADS
    ops = (qct, kc, vbt, gates_row, gates_col)
    return pl.pallas_call(
        _mlstm_kernel,
        grid=(b, nc),
        in_specs=specs(fwd) + specs(bwd),
        out_specs=[pl.BlockSpec((1, B_W, blk), lambda bb, j: (bb, 0, fwd(j))),
                   pl.BlockSpec((1, B_W, blk), lambda bb, j: (bb, 0, bwd(j)))],
        out_shape=[jax.ShapeDtypeStruct((b, B_W, ta), BF16)] * 2,
        scratch_shapes=[pltpu.VMEM((n_state, 2 * B_HEAD_DIM, B_HEAD_DIM), F32),
                        pltpu.VMEM((n_state, 1, 128), F32)],
        compiler_params=_params("parallel", "arbitrary"),
        name="mlstm_scan",
    )(*ops, *ops)


def _outproj_kernel(x_ref, m_ref, oa_ref, hf_ref, hb_ref, ob_ref, ghn_ref, w_ref, o_ref):
    hm = (hf_ref[0].astype(F32) + hb_ref[0].astype(F32)).T
    ob = ob_ref[0].astype(F32)
    ghn = ghn_ref[...]
    y = jnp.dot(oa_ref[0], w_ref[0:A_Q, :], preferred_element_type=F32)
    for hd in range(B_HEADS):
        sl = slice(hd * B_HEAD_DIM, (hd + 1) * B_HEAD_DIM)
        z = hm[:, sl]
        z = z * lax.rsqrt(jnp.mean(z * z, axis=-1, keepdims=True) + NORM_EPS) * ghn[:, sl]
        z = (z * jax.nn.sigmoid(ob[:, sl])).astype(BF16)
        y = y + jnp.dot(z, w_ref[A_Q + hd * B_HEAD_DIM:A_Q + (hd + 1) * B_HEAD_DIM, :],
                        preferred_element_type=F32)
    o_ref[0] = x_ref[0] + m_ref[0, 0][2:3] * y


def _outproj(xcat, modsel, oa, hf, hb, ob, ghn, w, *, rt, n_tiles, n_lat_tiles):
    b, ta, dm = xcat.shape
    row = lambda bb, r: (bb, r, 0)
    full = lambda bb, r: (0, 0)
    return pl.pallas_call(
        _outproj_kernel,
        grid=(b, n_tiles),
        in_specs=[pl.BlockSpec((1, rt, dm), row),
                  pl.BlockSpec((1, 1, 6, dm), lambda bb, r: (bb, (r >= n_lat_tiles).astype(jnp.int32), 0, 0)),
                  pl.BlockSpec((1, rt, A_Q), row),
                  pl.BlockSpec((1, B_W, rt), lambda bb, r: (bb, 0, r)),
                  pl.BlockSpec((1, B_W, rt), lambda bb, r: (bb, 0, r)),
                  pl.BlockSpec((1, rt, B_W), row),
                  pl.BlockSpec((1, B_W), full),
                  pl.BlockSpec((A_Q + B_W, dm), full)],
        out_specs=pl.BlockSpec((1, rt, dm), row),
        out_shape=jax.ShapeDtypeStruct(xcat.shape, F32),
        input_output_aliases={0: 0},
        compiler_params=_params("parallel", "arbitrary"),
        name="even_outproj",
    )(xcat, modsel, oa, hf, hb, ob, ghn, w)


def _modmm_kernel(x_ref, m_ref, g_ref, w_ref, o_ref):
    m = m_ref[0, 0]
    h = _modulate(x_ref[0], g_ref[...], m[0:1], m[1:2]).astype(BF16)
    o_ref[0] = jnp.dot(h, w_ref[...], preferred_element_type=F32)


def _mod_matmul(xcat, modsel, g, w, *, rt, tile0, n_tiles, sel):
    b, _, dm = xcat.shape
    n_out = w.shape[1]
    return pl.pallas_call(
        _modmm_kernel,
        grid=(b, n_tiles),
        in_specs=[pl.BlockSpec((1, rt, dm), lambda bb, r: (bb, tile0 + r, 0)),
                  pl.BlockSpec((1, 1, 6, dm), lambda bb, r: (bb, sel, 0, 0)),
                  pl.BlockSpec((1, dm), lambda bb, r: (0, 0)),
                  pl.BlockSpec((dm, n_out), lambda bb, r: (0, 0))],
        out_specs=pl.BlockSpec((1, rt, n_out), lambda bb, r: (bb, r, 0)),
        out_shape=jax.ShapeDtypeStruct((b, n_tiles * rt, n_out), F32),
        compiler_params=_params("parallel", "arbitrary"),
        name="odd_inproj",
    )(xcat, modsel, g, w)


def _pool_kernel(u_ref, x_ref, m_ref, wg_ref, sp_ref, o_ref, pad_ref, sa_ref, sb_ref, *, n, rc):
    g = pl.program_id(1)
    halo = POOL_HALO
    ext = n + 2 * halo
    zeros = jnp.zeros((halo, pad_ref.shape[1]), F32)
    pad_ref[0:halo, :] = zeros
    pad_ref[halo + n:ext, :] = zeros
    pad_ref[halo:halo + n, :] = u_ref[0]
    for ref in (pad_ref, sa_ref, sb_ref):
        ref[ext:ext + halo, :] = zeros
    gate = m_ref[0, 0][2:3]
    chunks = [(r0, min(rc, ext - r0)) for r0 in range(0, ext, rc)]

    def pair_sums(src, dst, shift):
        for r0, size in chunks:
            dst[r0:r0 + size, :] = src[r0:r0 + size, :] + src[r0 + shift:r0 + shift + size, :]

    for gi, win in enumerate(POOL_WINDOWS):
        @pl.when(g == gi)
        def _(win=win):
            lo = win // 2
            hi = win - 1 - lo
            sums, width = pad_ref, 1
            for dst in (sa_ref, sb_ref, sa_ref, sb_ref):
                if width == win:
                    break
                pair_sums(sums, dst, width)
                sums, width = dst, 2 * width
            assert width == win
            for r0 in range(0, n, rc):
                base = halo + r0
                acc = sums[base - lo:base - lo + rc, :]
                t = r0 + lax.broadcasted_iota(jnp.int32, (rc, 1), 0)
                cnt = jnp.minimum(t + hi, n - 1) - jnp.maximum(t - lo, 0) + 1
                p = acc / cnt.astype(F32) - pad_ref[base:base + rc, :]
                y = jnp.dot(p.astype(BF16), wg_ref[0], preferred_element_type=F32) * sp_ref[...]
                o_ref[0, r0:r0 + rc, :] = x_ref[0, r0:r0 + rc, :] + gate * y


def _pool(u, xcat, modsel, wgrp, spool, *, n, row_block, sel):
    b, _, dm = xcat.shape
    pg = POOL_GROUP
    rc = min(256, n)
    blk = lambda bb, g: (bb, row_block, g)
    return pl.pallas_call(
        functools.partial(_pool_kernel, n=n, rc=rc),
        grid=(b, dm // pg),
        in_specs=[pl.BlockSpec((1, n, pg), lambda bb, g: (bb, 0, g)),
                  pl.BlockSpec((1, n, pg), blk),
                  pl.BlockSpec((1, 1, 6, pg), lambda bb, g: (bb, sel, 0, g)),
                  pl.BlockSpec((1, pg, pg), lambda bb, g: (g, 0, 0)),
                  pl.BlockSpec((1, pg), lambda bb, g: (0, g))],
        out_specs=pl.BlockSpec((1, n, pg), blk),
        out_shape=jax.ShapeDtypeStruct(xcat.shape, F32),
        scratch_shapes=[pltpu.VMEM((n + 3 * POOL_HALO, pg), F32)] * 3,
        input_output_aliases={1: 0},
        compiler_params=_params("parallel", "arbitrary"),
        name="pool_mixer",
    )(u, xcat, modsel, wgrp, spool)


def _lane_cumsum(x01, upper, blk):
    n = x01.shape[1]
    out = []
    carry = jnp.zeros((x01.shape[0], 1), F32)
    for j in range(0, n, blk):
        cs = jnp.dot(x01[:, j:j + blk].astype(BF16), upper, preferred_element_type=F32) + carry
        carry = cs[:, blk - 1:blk]
        out.append(cs)
    return jnp.concatenate(out, axis=1) if len(out) > 1 else out[0]


def _router_kernel(x_ref, m_ref, g_ref, wr_ref, h_ref, lg_ref):
    m = m_ref[0, 0]
    h = _modulate(x_ref[0], g_ref[...], m[3:4], m[4:5])
    h_hi = h.astype(BF16)
    h_ref[0] = h_hi
    h_lo = (h - h_hi.astype(F32)).astype(BF16)
    wr = wr_ref[...]
    w_hi = wr.astype(BF16)
    w_lo = (wr - w_hi.astype(F32)).astype(BF16)
    lg = (jnp.dot(h_hi, w_hi, preferred_element_type=F32)
          + (jnp.dot(h_hi, w_lo, preferred_element_type=F32) + jnp.dot(h_lo, w_hi, preferred_element_type=F32)))
    lg_ref[0] = lg.T[:lg_ref.shape[1]]


def _topk_kernel(lg_ref, up_ref, aff_ref, pos_ref, *, cap):
    lg = lg_ref[...]
    nb, ne, n = lg.shape
    e = jnp.exp(lg - jnp.max(lg, axis=1, keepdims=True))
    aff3 = e / jnp.sum(e, axis=1, keepdims=True)
    aff_ref[...] = aff3
    aff = aff3.reshape(nb * ne, n)

    def bit_step(i, thr):
        cand = thr | lax.shift_left(jnp.int32(1), 30 - i)
        cnt = jnp.sum(jnp.where(aff >= pltpu.bitcast(cand, F32), 1.0, 0.0), axis=-1, keepdims=True)
        return jnp.where(cnt >= cap, cand, thr)

    thr = lax.fori_loop(0, 31, bit_step, jnp.zeros((aff.shape[0], 1), jnp.int32))
    thr = pltpu.bitcast(thr, F32)
    above = aff > thr
    tied = aff == thr
    room = cap - jnp.sum(jnp.where(above, 1.0, 0.0), axis=-1, keepdims=True)
    upper = up_ref[...]
    blk = upper.shape[0]
    tie_rank = _lane_cumsum(jnp.where(tied, 1.0, 0.0), upper, blk)
    sel = jnp.logical_or(above, jnp.logical_and(tied, tie_rank <= room))
    slot = _lane_cumsum(jnp.where(sel, 1.0, 0.0), upper, blk) - 1.0
    pos_ref[...] = jnp.where(sel, slot, -1.0).astype(jnp.int32).reshape(nb, ne, n)


def _router(xcat, modsel, g2, wr, upper, *, rt, n, row_tile0, sel):
    b, ta, dm = xcat.shape
    n_tiles = n // rt
    cap = max(1, EC_FACTOR * n // N_EXPERTS)
    ne = N_EXPERTS
    h, logits = pl.pallas_call(
        _router_kernel,
        grid=(b, n_tiles),
        in_specs=[pl.BlockSpec((1, rt, dm), lambda bb, r: (bb, row_tile0 + r, 0)),
                  pl.BlockSpec((1, 1, 6, dm), lambda bb, r: (bb, sel, 0, 0)),
                  pl.BlockSpec((1, dm), lambda bb, r: (0, 0)),
                  pl.BlockSpec(wr.shape, lambda bb, r: (0, 0))],
        out_specs=[pl.BlockSpec((1, rt, dm), lambda bb, r: (bb, r, 0)),
                   pl.BlockSpec((1, ne, rt), lambda bb, r: (bb, 0, r))],
        out_shape=[jax.ShapeDtypeStruct((b, n, dm), BF16),
                   jax.ShapeDtypeStruct((b, ne, n), F32)],
        compiler_params=_params("parallel", "arbitrary"),
        name="moe_router",
    )(xcat, modsel, g2, wr)
    whole = lambda i: (0, 0, 0)
    aff, pos = pl.pallas_call(
        functools.partial(_topk_kernel, cap=cap),
        grid=(1,),
        in_specs=[pl.BlockSpec((b, ne, n), whole),
                  pl.BlockSpec(upper.shape, lambda i: (0, 0))],
        out_specs=[pl.BlockSpec((b, ne, n), whole),
                   pl.BlockSpec((b, ne, n), whole)],
        out_shape=[jax.ShapeDtypeStruct((b, ne, n), F32),
                   jax.ShapeDtypeStruct((b, ne, n), jnp.int32)],
        compiler_params=_params("arbitrary"),
        name="moe_topk",
    )(logits, upper)
    return h, aff, pos


MOE_EXPERT_BLOCK = 4
MOE_TOKEN_BLOCK = 256
MOE_WINDOW = 64


def _slot_onehot(pos_ref, first, count, cap):
    hits = []
    for i in range(count):
        pos = pos_ref[0, pl.ds(first + i, 1), :]
        hits.append(lax.broadcasted_iota(jnp.int32, (cap, pos.shape[1]), 0) == pos)
    pick = jnp.concatenate([jnp.where(hit, 1.0, 0.0).astype(BF16) for hit in hits], axis=0)
    return pick, hits


def _moe_plan(pos, *, cap, tb, w):
    nb, ne, n = pos.shape
    nk = n // tb
    assert w % 16 == 0 and (cap - w) % 16 == 0 and n % tb == 0
    cnt = jnp.sum((pos.reshape(nb, ne, nk, tb) >= 0).astype(jnp.int32), axis=-1)
    first = jnp.cumsum(cnt, axis=-1) - cnt
    start = jnp.minimum((first // 16) * 16, cap - w)
    fits = first - start + cnt <= w
    dense_group = jnp.logical_not(jnp.all(fits.reshape(nb, ne // MOE_EXPERT_BLOCK, -1), axis=-1))
    dense_block = jnp.logical_not(jnp.all(fits, axis=1))
    flat = lambda a: a.reshape(-1).astype(jnp.int32)
    return flat(start), flat(dense_group), flat(dense_block)


def _moe_gather_kernel(start_ref, dense_ref, pos_ref, aff_ref, h_ref, xs_ref, gate_ref, *, cap, tb, w):
    b = pl.program_id(0)
    g = pl.program_id(1)
    eb, n = xs_ref.shape[1], pos_ref.shape[2]
    nk = n // tb
    dense = dense_ref[b * pl.num_programs(1) + g]
    expert_row = lambda i: pl.ds(g * eb + i, 1)

    @pl.when(dense != 0)
    def _():
        pick, hits = _slot_onehot(pos_ref, g * eb, eb, cap)
        xs = jnp.dot(pick, h_ref[0], preferred_element_type=F32).astype(BF16)
        xs_ref[0] = xs.reshape(xs_ref.shape[1:])
        for i, hit in enumerate(hits):
            gate = jnp.sum(jnp.where(hit, aff_ref[0, expert_row(i), :], 0.0), axis=-1, keepdims=True)
            gate_ref[0, i] = jnp.broadcast_to(gate, gate_ref.shape[2:])

    @pl.when(dense == 0)
    def _():
        xs_ref[...] = jnp.zeros(xs_ref.shape, BF16)
        gate_ref[...] = jnp.zeros(gate_ref.shape, F32)
        slot = lax.broadcasted_iota(jnp.int32, (w, tb), 0)
        for k in range(nk):
            cols = slice(k * tb, (k + 1) * tb)
            starts = [pl.multiple_of(start_ref[((b * pl.num_programs(1) + g) * eb + i) * nk + k], 16)
                      for i in range(eb)]
            hits = [(slot + starts[i]) == pos_ref[0, expert_row(i), cols] for i in range(eb)]
            pick = jnp.concatenate([jnp.where(hit, 1.0, 0.0).astype(BF16) for hit in hits], axis=0)
            part = jnp.dot(pick, h_ref[0, cols, :], preferred_element_type=F32).astype(BF16)
            for i in range(eb):
                rows = pl.ds(starts[i], w)
                xs_ref[0, i, rows, :] = xs_ref[0, i, rows, :] + part[i * w:(i + 1) * w]
                gate = jnp.sum(jnp.where(hits[i], aff_ref[0, expert_row(i), cols], 0.0), axis=-1, keepdims=True)
                gate_ref[0, i, rows, :] = gate_ref[0, i, rows, :] + gate


def _moe_gather(start, dense, pos, aff, h, *, cap, tb, w):
    nb, ne, n = pos.shape
    dm = h.shape[2]
    eb = MOE_EXPERT_BLOCK
    grid_spec = pltpu.PrefetchScalarGridSpec(
        num_scalar_prefetch=2,
        grid=(nb, ne // eb),
        in_specs=[pl.BlockSpec((1, ne, n), lambda b, e, *_: (b, 0, 0)),
                  pl.BlockSpec((1, ne, n), lambda b, e, *_: (b, 0, 0)),
                  pl.BlockSpec((1, n, dm), lambda b, e, *_: (b, 0, 0), pipeline_mode=pl.Buffered(1))],
        out_specs=[pl.BlockSpec((1, eb, cap, dm), lambda b, e, *_: (b, e, 0, 0)),
                   pl.BlockSpec((1, eb, cap, 128), lambda b, e, *_: (b, e, 0, 0))])
    return pl.pallas_call(
        functools.partial(_moe_gather_kernel, cap=cap, tb=tb, w=w),
        grid_spec=grid_spec,
        out_shape=[jax.ShapeDtypeStruct((nb, ne, cap, dm), BF16),
                   jax.ShapeDtypeStruct((nb, ne, cap, 128), F32)],
        compiler_params=_params("parallel", "arbitrary"),
        name="moe_gather",
    )(start, dense, pos, aff, h)


def _moe_ffn_kernel(*refs, n_sets):
    sets = [refs[3 * i:3 * i + 3] for i in range(n_sets)]
    wg_ref, wu_ref, wd_ref = refs[3 * n_sets:3 * n_sets + 3]
    outs = refs[3 * n_sets + 3:4 * n_sets + 3]
    wgb_ref, wub_ref, wdb_ref = refs[4 * n_sets + 3:]
    c = pl.program_id(1)

    @pl.when(c == 0)
    def _():
        rows = 256
        for src, dst in ((wg_ref, wgb_ref), (wu_ref, wub_ref), (wd_ref, wdb_ref)):
            for r0 in range(0, dst.shape[0], rows):
                dst[r0:r0 + rows, :] = src[0, 0, r0:r0 + rows, :].astype(BF16)

    def run(xs_ref, gate_ref, m_ref, y_ref):
        nb, _, cap, dm = xs_ref.shape
        xs = xs_ref[...].reshape(nb * cap, dm)
        a = jnp.dot(xs, wgb_ref[...], preferred_element_type=F32)
        u = jnp.dot(xs, wub_ref[...], preferred_element_type=F32)
        act = (_silu(a) * u).astype(BF16)
        out = jnp.dot(act, wdb_ref[...], preferred_element_type=F32)
        for i in range(nb):
            y = out[i * cap:(i + 1) * cap] * gate_ref[i, 0][:, 0:1]
            y_ref[i, 0] = (y * m_ref[i, 0][5:6]).astype(BF16)

    if n_sets == 1:
        run(*sets[0], outs[0])
    else:
        last = pl.num_programs(1) - 1
        pl.when(c < last)(lambda: run(*sets[0], outs[0]))
        pl.when(c == last)(lambda: run(*sets[1], outs[1]))


def _moe_ffn(slot_sets, modsel, wg, wu, wd, layer):
    xs, gate, _ = slot_sets[0]
    nb_all, ne, cap, dm = xs.shape
    df = wg.shape[3]
    nb = 2 if nb_all % 2 == 0 else 1
    n_lat = nb_all // nb
    lat = lambda e, c: (jnp.minimum(c, n_lat - 1), e, 0, 0)
    in_specs = [pl.BlockSpec((nb, 1, cap, dm), lat),
                pl.BlockSpec((nb, 1, cap, 128), lat),
                pl.BlockSpec((nb, 1, 6, dm), lambda e, c: (jnp.minimum(c, n_lat - 1), slot_sets[0][2], 0, 0))]
    out_specs = [pl.BlockSpec((nb, 1, cap, dm), lat)]
    out_shape = [jax.ShapeDtypeStruct(xs.shape, BF16)]
    operands = [xs, gate, modsel]
    if len(slot_sets) == 2:
        xs2, gate2, sel2 = slot_sets[1]
        nb2, _, cap2, _ = xs2.shape
        fixed = lambda e, c: (0, e, 0, 0)
        in_specs += [pl.BlockSpec((nb2, 1, cap2, dm), fixed),
                     pl.BlockSpec((nb2, 1, cap2, 128), fixed),
                     pl.BlockSpec((nb2, 1, 6, dm), lambda e, c: (0, sel2, 0, 0))]
        out_specs.append(pl.BlockSpec((nb2, 1, cap2, dm), fixed))
        out_shape.append(jax.ShapeDtypeStruct(xs2.shape, BF16))
        operands += [xs2, gate2, modsel]
    weight = lambda e, c: (layer, e, 0, 0)
    in_specs += [pl.BlockSpec((1, 1, dm, df), weight), pl.BlockSpec((1, 1, dm, df), weight),
                 pl.BlockSpec((1, 1, df, dm), weight)]
    return pl.pallas_call(
        functools.partial(_moe_ffn_kernel, n_sets=len(slot_sets)),
        grid=(ne, n_lat + len(slot_sets) - 1),
        in_specs=in_specs,
        out_specs=out_specs,
        out_shape=out_shape,
        scratch_shapes=[pltpu.VMEM((dm, df), BF16), pltpu.VMEM((dm, df), BF16), pltpu.VMEM((df, dm), BF16)],
        compiler_params=_params("arbitrary", "arbitrary"),
        name="moe_ffn",
    )(*operands, wg, wu, wd)


def _moe_scatter_kernel(start_ref, dense_ref, posc_ref, y_ref, x_ref, *rest, cap, w):
    b = pl.program_id(0)
    k = pl.program_id(1)
    nk = pl.num_programs(1)
    ne = y_ref.shape[1]
    tb = posc_ref.shape[1]
    dense = dense_ref[b * nk + k]
    o_ref = rest[-1]

    def emit(update):
        total = x_ref[0] + update
        if len(rest) == 2:
            total = total * lax.rsqrt(jnp.mean(total * total, axis=-1, keepdims=True) + NORM_EPS) * rest[0][...]
        o_ref[0] = total

    @pl.when(dense != 0)
    def _():
        slot = lax.broadcasted_iota(jnp.int32, (tb, cap), 1)
        pick = jnp.concatenate([jnp.where(slot == posc_ref[0, :, i:i + 1], 1.0, 0.0).astype(BF16)
                                for i in range(ne)], axis=1)
        y = y_ref[0].reshape(ne * cap, y_ref.shape[3])
        emit(jnp.dot(pick, y, preferred_element_type=F32))

    @pl.when(dense == 0)
    def _():
        per = 128 // w
        lane = lax.broadcasted_iota(jnp.int32, (tb, 128), 1)
        tiles = []
        wins = []
        for j in range(ne // per):
            col = None
            for u in reversed(range(per)):
                i = j * per + u
                s = pl.multiple_of(start_ref[(b * ne + i) * nk + k], 16)
                rel = posc_ref[0, :, i:i + 1] - s
                target = jnp.where(jnp.logical_and(rel >= 0, rel < w), rel + u * w, -1)
                col = target if col is None else jnp.where(lane < (u + 1) * w, target, col)
            tiles.append(jnp.where(lane == col, 1.0, 0.0).astype(BF16))
            for u in range(per):
                i = j * per + u
                s = pl.multiple_of(start_ref[(b * ne + i) * nk + k], 16)
                wins.append(y_ref[0, i, pl.ds(s, w), :])
        pick = jnp.concatenate(tiles, axis=1)
        emit(jnp.dot(pick, jnp.concatenate(wins, axis=0), preferred_element_type=F32))


def _moe_scatter(start, dense, pos, y, xcat, *, bb, row0, tb, w, final_gain=None):
    nb, ne, cap, dm = y.shape
    n = pos.shape[2]
    assert 128 % w == 0 and ne % (128 // w) == 0
    posc = jnp.transpose(pos, (0, 2, 1))
    x_index = (lambda b, k, *_: (b, k, 0)) if bb == 1 else (lambda b, k, *_: (k, row0 // tb, 0))
    in_specs = [pl.BlockSpec((1, tb, ne), lambda b, k, *_: (b, k, 0)),
                pl.BlockSpec((1, ne, cap, dm), lambda b, k, *_: (b, 0, 0, 0)),
                pl.BlockSpec((1, tb, dm), x_index)]
    operands = (start, dense, posc, y, xcat)
    if final_gain is None:
        out_shape, aliases = jax.ShapeDtypeStruct(xcat.shape, F32), {4: 0}
    else:
        assert bb == 1
        in_specs.append(pl.BlockSpec((1, dm), lambda b, k, *_: (0, 0)))
        operands += (final_gain,)
        out_shape, aliases = jax.ShapeDtypeStruct((nb, n, dm), F32), {}
    grid_spec = pltpu.PrefetchScalarGridSpec(
        num_scalar_prefetch=2,
        grid=(nb, n // tb),
        in_specs=in_specs,
        out_specs=pl.BlockSpec((1, tb, dm), x_index))
    return pl.pallas_call(
        functools.partial(_moe_scatter_kernel, cap=cap, w=w),
        grid_spec=grid_spec,
        out_shape=out_shape,
        input_output_aliases=aliases,
        compiler_params=_params("parallel", "arbitrary"),
        name="moe_scatter",
    )(*operands)


def _ec_moe(xcat, modsel, g2, wr, upper, wg, wu, wd, layer, *, rt, seq, ctx, with_ctx, final_gain=None):
    b = xcat.shape[0]
    ne = N_EXPERTS
    rt_lat = 2 * rt if seq % (2 * rt) == 0 else rt
    h, aff, pos = _router(xcat, modsel, g2, wr, upper, rt=rt_lat, n=seq, row_tile0=0, sel=0)
    cap = max(1, EC_FACTOR * seq // N_EXPERTS)
    tb = min(MOE_TOKEN_BLOCK, ctx)
    win = dict(tb=tb, w=min(MOE_WINDOW, cap))
    start, dense_group, dense_block = _moe_plan(pos, cap=cap, **win)
    xs, gate = _moe_gather(start, dense_group, pos, aff, h, cap=cap, **win)
    if not with_ctx:
        (y,) = _moe_ffn([(xs, gate, 0)], modsel, wg, wu, wd, layer)
        return _moe_scatter(start, dense_block, pos, y, xcat, bb=1, row0=0, final_gain=final_gain, **win)
    hc, affc, posc = _router(xcat, modsel, g2, wr, upper, rt=rt, n=ctx, row_tile0=seq // rt, sel=1)
    capc = max(1, EC_FACTOR * ctx // N_EXPERTS)
    offs = (jnp.arange(b, dtype=jnp.int32) * capc)[:, None, None]
    posc = jnp.where(posc >= 0, posc + offs, -1)
    posc = jnp.transpose(posc, (1, 0, 2)).reshape(1, ne, b * ctx)
    affc = jnp.transpose(affc, (1, 0, 2)).reshape(1, ne, b * ctx)
    winc = dict(tb=tb, w=min(MOE_WINDOW, b * capc))
    assert tb == ctx
    startc, dense_groupc, dense_blockc = _moe_plan(posc, cap=b * capc, **winc)
    xsc, gatec = _moe_gather(startc, dense_groupc, posc, affc, hc.reshape(1, b * ctx, -1), cap=b * capc, **winc)
    y, yc = _moe_ffn([(xs, gate, 0), (xsc, gatec, 1)], modsel, wg, wu, wd, layer)
    xcat = _moe_scatter(start, dense_block, pos, y, xcat, bb=1, row0=0, final_gain=final_gain, **win)
    return _moe_scatter(startc, dense_blockc, posc, yc, xcat, bb=b, row0=seq, **winc)


def _rope_tables(seq, ctx):
    rows = seq // GRID_W
    row_ids = jnp.repeat(jnp.arange(rows, dtype=F32), GRID_W)
    col_ids = jnp.tile(jnp.arange(GRID_W, dtype=F32), rows)
    half = A_HEAD_DIM // 2
    inv = ROPE_THETA ** (-jnp.arange(0, half, 2, dtype=F32) / half)
    ang_r = row_ids[:, None] * inv
    ang_c = col_ids[:, None] * inv
    cos = jnp.concatenate([jnp.cos(ang_r)] * 2 + [jnp.cos(ang_c)] * 2, axis=1)
    sin = jnp.concatenate([-jnp.sin(ang_r), jnp.sin(ang_r), -jnp.sin(ang_c), jnp.sin(ang_c)], axis=1)
    cos = jnp.tile(cos, (1, A_HEADS))
    sin = jnp.tile(sin, (1, A_HEADS))
    cos = jnp.concatenate([cos, jnp.ones((ctx, A_Q), F32)], axis=0)
    sin = jnp.concatenate([sin, jnp.zeros((ctx, A_Q), F32)], axis=0)
    return cos, sin


def kernel(x, c, ctx, c_ctx, w_mod, b_mod, g_norm1, g_norm2, w_in_even, w_out_even, g_qnorm, g_knorm, w_conv,
           b_gate, g_hnorm, w_in_odd, w_pool_grp, s_pool, w_router, w_exp_gate, w_exp_up, w_exp_down, g_final):
    b, seq, dm = x.shape
    n_ctx = ctx.shape[1]
    depth = w_mod.shape[0]
    rt = 256 if (seq % 256 == 0 and n_ctx % 256 == 0) else 128
    assert seq % rt == 0 and n_ctx % rt == 0 and seq % n_ctx == 0 and seq % GRID_W == 0
    nlt = seq // rt
    nt = (seq + n_ctx) // rt

    xcat = jnp.concatenate([x, ctx, jnp.zeros((b, seq - n_ctx, dm), F32)], axis=1)

    rows = -(-(b + 1) // 8) * 8
    cond = jnp.zeros((rows, dm), F32).at[:b].set(c).at[b].set(c_ctx)
    mods = _adaln(cond, w_mod, b_mod)
    lat = mods[:, :b].reshape(depth, b, 1, 6, dm)
    cx = jnp.broadcast_to(mods[:, b].reshape(depth, 1, 1, 6, dm), (depth, b, 1, 6, dm))
    modsel = jnp.concatenate([lat, cx], axis=2)

    cos, sin = _rope_tables(seq, n_ctx)
    lane = jnp.arange(A_Q)
    gmat = (lane[:, None] // A_HEAD_DIM == lane[None, :] // A_HEAD_DIM).astype(BF16)
    blk = min(256, n_ctx)
    tri = jnp.arange(blk)
    upper = (tri[:, None] <= tri[None, :]).astype(BF16)

    wg, wu, wd = w_exp_gate, w_exp_up, w_exp_down
    wr = jnp.pad(w_router, ((0, 0), (0, 0), (0, GATE_PAD - N_EXPERTS)))

    for i in range(depth):
        ctx_next = any(j % 2 == 0 for j in range(i + 1, depth))
        n_upd = nt if ctx_next else nlt
        ms = modsel[i]
        if i % 2 == 0:
            e = i // 2
            w_in = w_in_even[e]
            pad = jnp.zeros((dm, GATE_PAD - N_GATES), F32)
            w_in = jnp.concatenate([w_in, pad], axis=1).astype(BF16)
            bg = jnp.concatenate([b_gate[e], jnp.zeros((GATE_PAD - N_GATES,), F32)])[None]
            q, k, v, qk, vbt, ob, gt, gc, gr = _inproj(
                xcat, ms, g_norm1[i][None], w_in, cos, sin,
                jnp.tile(g_qnorm[e], A_HEADS)[None], jnp.tile(g_knorm[e], A_KV_HEADS)[None], gmat, bg,
                rt=rt, ta=seq + n_ctx, n_lat_tiles=nlt)
            oa = _attention(q, k, v, rt=rt, n_tiles=n_upd, n_lat_tiles=nlt, seq=seq)
            qct = _short_conv(qk, w_conv[e], seq=seq, keys=False)
            kc = _short_conv(qk, w_conv[e], seq=seq, keys=True)
            gates_row = jnp.swapaxes(jnp.concatenate([gt[:, :, :N_GATES], gc[:, :, :N_GATES]], axis=2), 1, 2)
            hf, hb = _mlstm(qct, kc, vbt, gates_row, gr, seq=seq, blk=rt)
            xcat = _outproj(xcat, ms, oa, hf, hb, ob, g_hnorm[e][None], w_out_even[e].astype(BF16),
                            rt=rt, n_tiles=n_upd, n_lat_tiles=nlt)
        else:
            o = i // 2
            w_in = w_in_odd[o].astype(BF16)
            wgrp = w_pool_grp[o].astype(BF16)
            rt_lat = 2 * rt if nlt % 2 == 0 else rt
            u = _mod_matmul(xcat, ms, g_norm1[i][None], w_in, rt=rt_lat, tile0=0, n_tiles=seq // rt_lat, sel=0)
            if ctx_next:
                uc = _mod_matmul(xcat, ms, g_norm1[i][None], w_in, rt=rt, tile0=nlt, n_tiles=nt - nlt, sel=1)
            xcat = _pool(u, xcat, ms, wgrp, s_pool[o][None], n=seq, row_block=0, sel=0)
            if ctx_next:
                xcat = _pool(uc, xcat, ms, wgrp, s_pool[o][None], n=n_ctx, row_block=seq // n_ctx, sel=1)
        xcat = _ec_moe(xcat, ms, g_norm2[i][None], wr[i], upper, wg, wu, wd, i,
                       rt=rt, seq=seq, ctx=n_ctx, with_ctx=ctx_next,
                       final_gain=g_final[None] if i == depth - 1 else None)
    return xcat
```

```python
import functools

import jax
import jax.numpy as jnp
from jax import lax
from jax.experimental import pallas as pl
from jax.experimental.pallas import tpu as pltpu

D_MODEL = 1024
GRID_W = 64
A_HEADS = 8
A_KV_HEADS = 2
A_HEAD_DIM = 64
ROPE_THETA = 10000.0
B_HEADS = 4
B_HEAD_DIM = 128
B_CHUNK = 128
POOL_WINDOWS = (2, 4, 8, 16)
POOL_GROUP = D_MODEL // 4
N_EXPERTS = 16
EC_FACTOR = 2
NORM_EPS = 1e-6

A_Q = A_HEADS * A_HEAD_DIM
A_KV = A_KV_HEADS * A_HEAD_DIM
B_W = B_HEADS * B_HEAD_DIM
N_GATES = 2 * 2 * B_HEADS
GATE_PAD = 128
C_QA, C_KA, C_VA = 0, A_Q, A_Q + A_KV
C_QK = A_Q + 2 * A_KV
C_VB = C_QK + 2 * B_W
C_OB = C_VB + B_W
C_GT = C_OB + B_W
EVEN_COLS = C_GT + GATE_PAD
POOL_HALO = 16

F32 = jnp.float32
BF16 = jnp.bfloat16
HIGHEST = lax.Precision.HIGHEST
NT_DIMS = (((1,), (1,)), ((), ()))
TN_DIMS = (((0,), (0,)), ((), ()))
VMEM_LIMIT = 56 * 1024 * 1024


def _params(*sem):
    return pltpu.CompilerParams(dimension_semantics=sem, vmem_limit_bytes=VMEM_LIMIT)


def _modulate(x, g, shift, scale):
    y = x * lax.rsqrt(jnp.mean(x * x, axis=-1, keepdims=True) + NORM_EPS)
    return (y * g) * (1.0 + scale) + shift


def _silu(x):
    return x * jax.nn.sigmoid(x)


def _adaln_kernel(c_ref, w_ref, b_ref, o_ref):
    s = _silu(c_ref[...])
    o_ref[0] = jnp.dot(s, w_ref[0], precision=HIGHEST, preferred_element_type=F32) + b_ref[0]


def _adaln(cond, w_mod, b_mod):
    depth, dm, n6 = w_mod.shape
    rows = cond.shape[0]
    tn = 1536
    return pl.pallas_call(
        _adaln_kernel,
        grid=(depth, n6 // tn),
        in_specs=[pl.BlockSpec((rows, dm), lambda l, j: (0, 0)),
                  pl.BlockSpec((1, dm, tn), lambda l, j: (l, 0, j)),
                  pl.BlockSpec((1, 1, tn), lambda l, j: (l, 0, j))],
        out_specs=pl.BlockSpec((1, rows, tn), lambda l, j: (l, 0, j)),
        out_shape=jax.ShapeDtypeStruct((depth, rows, n6), F32),
        compiler_params=_params("arbitrary", "arbitrary"),
        name="adaln",
    )(cond, w_mod, b_mod.reshape(depth, 1, n6))


def _group_mean_sq(x, gmat, width):
    s = jnp.dot((x * x).astype(BF16), gmat, preferred_element_type=F32)
    return s * (1.0 / width)


def _dot_exact_lhs(mat, x):
    x1 = x.astype(BF16)
    r1 = x - x1.astype(F32)
    x2 = r1.astype(BF16)
    x3 = (r1 - x2.astype(F32)).astype(BF16)
    return (jnp.dot(mat, x1, preferred_element_type=F32) + jnp.dot(mat, x2, preferred_element_type=F32)
            + jnp.dot(mat, x3, preferred_element_type=F32))


def _rope(x, cos, sin):
    w = x.shape[1]
    lane = lax.broadcasted_iota(jnp.int32, x.shape, 1)
    first = (lane % 32) < 16
    partner = jnp.where(first, pltpu.roll(x, w - 16, 1), pltpu.roll(x, 16, 1))
    return x * cos + partner * sin


def _inproj_kernel(x_ref, m_ref, g1_ref, w_ref, cos_ref, sin_ref, gq_ref, gk_ref, gm_ref, bg_ref, lp_ref,
                   q_ref, k_ref, v_ref, qk_ref, vb_ref, ob_ref, gt_ref, gc_ref, gr_ref):
    m = m_ref[0, 0]
    h = _modulate(x_ref[0], g1_ref[...], m[0:1], m[1:2]).astype(BF16)

    def mm(lo, hi):
        return jnp.dot(h, w_ref[:, lo:hi], preferred_element_type=F32)

    cos = cos_ref[...]
    sin = sin_ref[...]
    gm = gm_ref[...]
    qa = mm(C_QA, C_KA)
    qa = qa * lax.rsqrt(_group_mean_sq(qa, gm, A_HEAD_DIM) + NORM_EPS) * gq_ref[...]
    q_ref[0] = (_rope(qa, cos, sin) * (A_HEAD_DIM ** -0.5)).astype(BF16)
    ka = mm(C_KA, C_VA)
    ka = ka * lax.rsqrt(_group_mean_sq(ka, gm[:A_KV, :A_KV], A_HEAD_DIM) + NORM_EPS) * gk_ref[...]
    k_ref[0] = _rope(ka, cos[:, :A_KV], sin[:, :A_KV]).astype(BF16)
    va = mm(C_VA, C_QK).astype(BF16)
    ones = jnp.ones((va.shape[0], A_HEAD_DIM), BF16)
    v_ref[0] = jnp.concatenate(
        [piece for g in range(A_KV_HEADS) for piece in (va[:, g * A_HEAD_DIM:(g + 1) * A_HEAD_DIM], ones)], axis=1)
    qk_ref[0] = mm(C_QK, C_VB).astype(BF16)
    vb_ref[0] = mm(C_VB, C_OB).T.astype(BF16)
    ob_ref[0] = mm(C_OB, C_GT).astype(BF16)
    gt = mm(C_GT, EVEN_COLS) + bg_ref[...]
    lane = lax.broadcasted_iota(jnp.int32, gt.shape, 1)
    log_sig = jnp.minimum(gt, 0.0) - jnp.log1p(jnp.exp(-jnp.abs(gt)))
    gt = jnp.where((lane % (2 * B_HEADS)) >= B_HEADS, log_sig, gt)
    gt_ref[0] = gt
    prefix = _dot_exact_lhs(lp_ref[...], gt)
    row = lax.broadcasted_iota(jnp.int32, gt.shape, 0)
    total = prefix[gt.shape[0] - 1:gt.shape[0]]
    for c in range(gt.shape[0] // B_CHUNK - 2, -1, -1):
        total = jnp.where(row < (c + 1) * B_CHUNK, prefix[(c + 1) * B_CHUNK - 1:(c + 1) * B_CHUNK], total)
    suffix = total - prefix + gt
    cum = jnp.where((lane % (4 * B_HEADS)) < 2 * B_HEADS, prefix, suffix)
    gc_ref[0] = cum
    gr_ref[0] = gt - pltpu.roll(cum, GATE_PAD - B_HEADS, 1)


def _inproj(xcat, modsel, g1, w, cos, sin, gq, gk, gmat, bgate, *, rt, ta, n_lat_tiles):
    b, _, dm = xcat.shape
    nt = ta // rt
    row = lambda bb, r: (bb, r, 0)
    full = lambda bb, r: (0, 0)
    out_widths = (A_Q, A_KV, 2 * A_KV, 2 * B_W, B_W, B_W, GATE_PAD, GATE_PAD, GATE_PAD)
    out_dtypes = (BF16, BF16, BF16, BF16, BF16, BF16, F32, F32, F32)
    out_specs = [pl.BlockSpec((1, rt, wd), row) for wd in out_widths]
    out_shapes = [jax.ShapeDtypeStruct((b, ta, wd), dt) for wd, dt in zip(out_widths, out_dtypes)]
    vb_slot = 4
    out_specs[vb_slot] = pl.BlockSpec((1, B_W, rt), lambda bb, r: (bb, 0, r))
    out_shapes[vb_slot] = jax.ShapeDtypeStruct((b, B_W, ta), BF16)
    idx = jnp.arange(rt)
    same_chunk = idx[:, None] // B_CHUNK == idx[None, :] // B_CHUNK
    lower = jnp.logical_and(same_chunk, idx[None, :] <= idx[:, None]).astype(BF16)
    return pl.pallas_call(
        _inproj_kernel,
        grid=(b, nt),
        in_specs=[pl.BlockSpec((1, rt, dm), row),
                  pl.BlockSpec((1, 1, 6, dm), lambda bb, r: (bb, (r >= n_lat_tiles).astype(jnp.int32), 0, 0)),
                  pl.BlockSpec((1, dm), full),
                  pl.BlockSpec((dm, EVEN_COLS), full),
                  pl.BlockSpec((rt, A_Q), lambda bb, r: (r, 0)),
                  pl.BlockSpec((rt, A_Q), lambda bb, r: (r, 0)),
                  pl.BlockSpec((1, A_Q), full),
                  pl.BlockSpec((1, A_KV), full),
                  pl.BlockSpec((A_Q, A_Q), full),
                  pl.BlockSpec((1, GATE_PAD), full),
                  pl.BlockSpec((rt, rt), full)],
        out_specs=out_specs,
        out_shape=out_shapes,
        compiler_params=_params("parallel", "arbitrary"),
        name="even_inproj",
    )(xcat, modsel, g1, w, cos, sin, gq, gk, gmat, bgate, lower)


def _attn_kernel(q_ref, k_ref, v_ref, o_ref, *, n_lat_tiles, n_tiles, seq):
    group = A_HEADS // A_KV_HEADS

    def attend(key_lo):
        q = q_ref[0]
        for hd in range(A_HEADS):
            g = hd // group
            qh = q[:, hd * A_HEAD_DIM:(hd + 1) * A_HEAD_DIM]
            kg = k_ref[0, key_lo:, g * A_HEAD_DIM:(g + 1) * A_HEAD_DIM]
            vg = v_ref[0, key_lo:, 2 * g * A_HEAD_DIM:2 * (g + 1) * A_HEAD_DIM]
            s = lax.dot_general(qh, kg, NT_DIMS, preferred_element_type=F32)
            p = jnp.exp(s - jnp.max(s, axis=-1, keepdims=True))
            o = jnp.dot(p.astype(BF16), vg, preferred_element_type=F32)
            o = o[:, :A_HEAD_DIM] / o[:, A_HEAD_DIM:]
            o_ref[0, :, hd * A_HEAD_DIM:(hd + 1) * A_HEAD_DIM] = o.astype(BF16)

    if n_tiles == n_lat_tiles:
        attend(0)
    else:
        r = pl.program_id(1)
        pl.when(r < n_lat_tiles)(lambda: attend(0))
        pl.when(r >= n_lat_tiles)(lambda: attend(seq))


def _attention(q, k, v, *, rt, n_tiles, n_lat_tiles, seq):
    b, ta, _ = q.shape
    return pl.pallas_call(
        functools.partial(_attn_kernel, n_lat_tiles=n_lat_tiles, n_tiles=n_tiles, seq=seq),
        grid=(b, n_tiles),
        in_specs=[pl.BlockSpec((1, rt, A_Q), lambda bb, r: (bb, r, 0)),
                  pl.BlockSpec((1, ta, A_KV), lambda bb, r: (bb, 0, 0)),
                  pl.BlockSpec((1, ta, 2 * A_KV), lambda bb, r: (bb, 0, 0))],
        out_specs=pl.BlockSpec((1, rt, A_Q), lambda bb, r: (bb, r, 0)),
        out_shape=jax.ShapeDtypeStruct((b, n_tiles * rt, A_Q), BF16),
        compiler_params=_params("parallel", "arbitrary"),
        name="gqa_attention",
    )(q, k, v)


def _conv_kernel(x_ref, w_ref, o_ref, *, seq, keys):
    x = x_ref[0].astype(F32)
    ta = x.shape[0]
    w = w_ref[...]
    row = lax.broadcasted_iota(jnp.int32, (ta, 1), 0)
    prev = jnp.where(jnp.logical_or(row == 0, row == seq), 0.0, pltpu.roll(x, 1, 0))
    nxt = jnp.where(jnp.logical_or(row == seq - 1, row == ta - 1), 0.0, pltpu.roll(x, ta - 1, 0))
    y = _silu(prev * w[0:1] + x * w[1:2] + nxt * w[2:3])
    if keys:
        o_ref[0] = (y * (B_HEAD_DIM ** -0.5)).astype(BF16)
    else:
        o_ref[0] = y.T.astype(BF16)


def _short_conv(qk, w_conv, *, seq, keys):
    b, ta, _ = qk.shape
    ct = 256
    c0 = B_W // ct if keys else 0
    if keys:
        out_spec = pl.BlockSpec((1, ta, ct), lambda bb, j: (bb, 0, j))
        out_shape = jax.ShapeDtypeStruct((b, ta, B_W), BF16)
    else:
        out_spec = pl.BlockSpec((1, ct, ta), lambda bb, j: (bb, j, 0))
        out_shape = jax.ShapeDtypeStruct((b, B_W, ta), BF16)
    return pl.pallas_call(
        functools.partial(_conv_kernel, seq=seq, keys=keys),
        grid=(b, B_W // ct),
        in_specs=[pl.BlockSpec((1, ta, ct), lambda bb, j: (bb, 0, c0 + j)),
                  pl.BlockSpec((w_conv.shape[0], ct), lambda bb, j: (0, c0 + j))],
        out_specs=out_spec,
        out_shape=out_shape,
        compiler_params=_params("parallel", "arbitrary"),
        name="mlstm_conv_k" if keys else "mlstm_conv_q",
    )(qk, w_conv)


def _mlstm_chain(qt, k, vat, ig, b_row, r_col, total, ca, m_old, tri):
    log_d = jnp.where(tri, b_row + r_col, -jnp.inf)
    log_inter = m_old + b_row
    m_t = jnp.maximum(log_inter, jnp.max(log_d, axis=0, keepdims=True))
    w_intra = jnp.exp(log_d - m_t) * jnp.dot(k, qt, preferred_element_type=F32)
    w_inter = jnp.exp(log_inter - m_t)
    nd = (w_inter * jnp.dot(ca.astype(BF16), qt, preferred_element_type=F32)
          + jnp.dot(vat, w_intra.astype(BF16), preferred_element_type=F32))
    dv = nd.shape[0] // 2
    h = nd[:dv] / jnp.maximum(jnp.abs(nd[dv:]), jnp.exp(-m_t))
    log_w = total - b_row + ig
    m_new = jnp.maximum(m_old + total, jnp.max(log_w, axis=-1, keepdims=True))
    w_s = jnp.exp(log_w - m_new)
    decay = jnp.exp(m_old + total - m_new)
    ca_new = decay * ca + jnp.dot((vat * w_s).astype(BF16), k, preferred_element_type=F32)
    return h, ca_new, m_new


def _mlstm_kernel(qf_ref, kf_ref, vf_ref, grf_ref, gcf_ref, qb_ref, kb_ref, vb_ref, grb_ref, gcb_ref,
                  hf_ref, hb_ref, c_ref, m_ref):
    @pl.when(pl.program_id(1) == 0)
    def _():
        c_ref[...] = jnp.zeros(c_ref.shape, F32)
        m_ref[...] = jnp.full(m_ref.shape, -1e30, F32)

    t = B_CHUNK
    n_chunks = kf_ref.shape[1] // t
    src = lax.broadcasted_iota(jnp.int32, (t, t), 0)
    tgt = lax.broadcasted_iota(jnp.int32, (t, t), 1)
    ones = jnp.ones((B_HEAD_DIM, t), BF16)
    dirs = ((qf_ref, kf_ref, vf_ref, grf_ref, gcf_ref, hf_ref, src <= tgt, t - 1, range(n_chunks)),
            (qb_ref, kb_ref, vb_ref, grb_ref, gcb_ref, hb_ref, src >= tgt, 0, range(n_chunks - 1, -1, -1)))
    outputs = []
    states = []
    for d, (q_ref, k_ref, v_ref, gr_ref, gc_ref, h_ref, tri, last, order) in enumerate(dirs):
        for hd in range(B_HEADS):
            sl = slice(hd * B_HEAD_DIM, (hd + 1) * B_HEAD_DIM)
            gi = d * 2 * B_HEADS + hd
            fi = gi + B_HEADS
            si = d * B_HEADS + hd
            ca = c_ref[si]
            m = m_ref[si][:, 0:1]
            for ci in order:
                rows = slice(ci * t, (ci + 1) * t)
                ig = gr_ref[0, gi:gi + 1, rows]
                b_row = gr_ref[0, N_GATES + fi:N_GATES + fi + 1, rows]
                r_col = gc_ref[0, rows, gi:gi + 1]
                vat = jnp.concatenate([v_ref[0, sl, rows], ones], axis=0)
                h, ca, m = _mlstm_chain(q_ref[0, sl, rows], k_ref[0, rows, sl], vat, ig, b_row, r_col,
                                        b_row[:, last:last + 1], ca, m, tri)
                outputs.append((h_ref, rows, sl, h))
            states.append((si, ca, m))
    for h_ref, rows, sl, h in outputs:
        h_ref[0, sl, rows] = h.astype(BF16)
    for si, ca, m in states:
        c_ref[si] = ca
        m_ref[si] = jnp.broadcast_to(m, (1, 128))


def _mlstm(qct, kc, vbt, gates_row, gates_col, *, seq, blk):
    b, ta, _ = kc.shape
    nc = ta // blk
    ncl = seq // blk
    ncc = nc - ncl
    fwd = lambda j: jnp.where(j < ncc, ncl + j, j - ncc)
    bwd = lambda j: jnp.where(j < ncc, nc - 1 - j, ncl - 1 - (j - ncc))

    def specs(order):
        feature_major = pl.BlockSpec((1, B_W, blk), lambda bb, j: (bb, 0, order(j)))
        return [feature_major,
                pl.BlockSpec((1, blk, B_W), lambda bb, j: (bb, order(j), 0)),
                feature_major,
                pl.BlockSpec((1, 2 * N_GATES, blk), lambda bb, j: (bb, 0, order(j))),
                pl.BlockSpec((1, blk, GATE_PAD), lambda bb, j: (bb, order(j), 0))]

    n_state = 2 * B_HEADS
    ops = (qct, kc, vbt, gates_row, gates_col)
    return pl.pallas_call(
        _mlstm_kernel,
        grid=(b, nc),
        in_specs=specs(fwd) + specs(bwd),
        out_specs=[pl.BlockSpec((1, B_W, blk), lambda bb, j: (bb, 0, fwd(j))),
                   pl.BlockSpec((1, B_W, blk), lambda bb, j: (bb, 0, bwd(j)))],
        out_shape=[jax.ShapeDtypeStruct((b, B_W, ta), BF16)] * 2,
        scratch_shapes=[pltpu.VMEM((n_state, 2 * B_HEAD_DIM, B_HEAD_DIM), F32),
                        pltpu.VMEM((n_state, 1, 128), F32)],
        compiler_params=_params("parallel", "arbitrary"),
        name="mlstm_scan",
    )(*ops, *ops)


def _outproj_kernel(x_ref, m_ref, oa_ref, hf_ref, hb_ref, ob_ref, ghn_ref, w_ref, o_ref):
    hm = (hf_ref[0].astype(F32) + hb_ref[0].astype(F32)).T
    ob = ob_ref[0].astype(F32)
    ghn = ghn_ref[...]
    y = jnp.dot(oa_ref[0], w_ref[0:A_Q, :], preferred_element_type=F32)
    for hd in range(B_HEADS):
        sl = slice(hd * B_HEAD_DIM, (hd + 1) * B_HEAD_DIM)
        z = hm[:, sl]
        z = z * lax.rsqrt(jnp.mean(z * z, axis=-1, keepdims=True) + NORM_EPS) * ghn[:, sl]
        z = (z * jax.nn.sigmoid(ob[:, sl])).astype(BF16)
        y = y + jnp.dot(z, w_ref[A_Q + hd * B_HEAD_DIM:A_Q + (hd + 1) * B_HEAD_DIM, :],
                        preferred_element_type=F32)
    o_ref[0] = x_ref[0] + m_ref[0, 0][2:3] * y


def _outproj(xcat, modsel, oa, hf, hb, ob, ghn, w, *, rt, n_tiles, n_lat_tiles):
    b, ta, dm = xcat.shape
    row = lambda bb, r: (bb, r, 0)
    full = lambda bb, r: (0, 0)
    return pl.pallas_call(
        _outproj_kernel,
        grid=(b, n_tiles),
        in_specs=[pl.BlockSpec((1, rt, dm), row),
                  pl.BlockSpec((1, 1, 6, dm), lambda bb, r: (bb, (r >= n_lat_tiles).astype(jnp.int32), 0, 0)),
                  pl.BlockSpec((1, rt, A_Q), row),
                  pl.BlockSpec((1, B_W, rt), lambda bb, r: (bb, 0, r)),
                  pl.BlockSpec((1, B_W, rt), lambda bb, r: (bb, 0, r)),
                  pl.BlockSpec((1, rt, B_W), row),
                  pl.BlockSpec((1, B_W), full),
                  pl.BlockSpec((A_Q + B_W, dm), full)],
        out_specs=pl.BlockSpec((1, rt, dm), row),
        out_shape=jax.ShapeDtypeStruct(xcat.shape, F32),
        input_output_aliases={0: 0},
        compiler_params=_params("parallel", "arbitrary"),
        name="even_outproj",
    )(xcat, modsel, oa, hf, hb, ob, ghn, w)


def _modmm_kernel(x_ref, m_ref, g_ref, w_ref, o_ref):
    m = m_ref[0, 0]
    h = _modulate(x_ref[0], g_ref[...], m[0:1], m[1:2]).astype(BF16)
    o_ref[0] = jnp.dot(h, w_ref[...], preferred_element_type=F32)


def _mod_matmul(xcat, modsel, g, w, *, rt, tile0, n_tiles, sel):
    b, _, dm = xcat.shape
    n_out = w.shape[1]
    return pl.pallas_call(
        _modmm_kernel,
        grid=(b, n_tiles),
        in_specs=[pl.BlockSpec((1, rt, dm), lambda bb, r: (bb, tile0 + r, 0)),
                  pl.BlockSpec((1, 1, 6, dm), lambda bb, r: (bb, sel, 0, 0)),
                  pl.BlockSpec((1, dm), lambda bb, r: (0, 0)),
                  pl.BlockSpec((dm, n_out), lambda bb, r: (0, 0))],
        out_specs=pl.BlockSpec((1, rt, n_out), lambda bb, r: (bb, r, 0)),
        out_shape=jax.ShapeDtypeStruct((b, n_tiles * rt, n_out), F32),
        compiler_params=_params("parallel", "arbitrary"),
        name="odd_inproj",
    )(xcat, modsel, g, w)


def _pool_kernel(u_ref, x_ref, m_ref, wg_ref, sp_ref, o_ref, pad_ref, sa_ref, sb_ref, *, n, rc):
    g = pl.program_id(1)
    halo = POOL_HALO
    ext = n + 2 * halo
    zeros = jnp.zeros((halo, pad_ref.shape[1]), F32)
    pad_ref[0:halo, :] = zeros
    pad_ref[halo + n:ext, :] = zeros
    pad_ref[halo:halo + n, :] = u_ref[0]
    for ref in (pad_ref, sa_ref, sb_ref):
        ref[ext:ext + halo, :] = zeros
    gate = m_ref[0, 0][2:3]
    chunks = [(r0, min(rc, ext - r0)) for r0 in range(0, ext, rc)]

    def pair_sums(src, dst, shift):
        for r0, size in chunks:
            dst[r0:r0 + size, :] = src[r0:r0 + size, :] + src[r0 + shift:r0 + shift + size, :]

    for gi, win in enumerate(POOL_WINDOWS):
        @pl.when(g == gi)
        def _(win=win):
            lo = win // 2
            hi = win - 1 - lo
            sums, width = pad_ref, 1
            for dst in (sa_ref, sb_ref, sa_ref, sb_ref):
                if width == win:
                    break
                pair_sums(sums, dst, width)
                sums, width = dst, 2 * width
            assert width == win
            for r0 in range(0, n, rc):
                base = halo + r0
                acc = sums[base - lo:base - lo + rc, :]
                t = r0 + lax.broadcasted_iota(jnp.int32, (rc, 1), 0)
                cnt = jnp.minimum(t + hi, n - 1) - jnp.maximum(t - lo, 0) + 1
                p = acc / cnt.astype(F32) - pad_ref[base:base + rc, :]
                y = jnp.dot(p.astype(BF16), wg_ref[0], preferred_element_type=F32) * sp_ref[...]
                o_ref[0, r0:r0 + rc, :] = x_ref[0, r0:r0 + rc, :] + gate * y


def _pool(u, xcat, modsel, wgrp, spool, *, n, row_block, sel):
    b, _, dm = xcat.shape
    pg = POOL_GROUP
    rc = min(256, n)
    blk = lambda bb, g: (bb, row_block, g)
    return pl.pallas_call(
        functools.partial(_pool_kernel, n=n, rc=rc),
        grid=(b, dm // pg),
        in_specs=[pl.BlockSpec((1, n, pg), lambda bb, g: (bb, 0, g)),
                  pl.BlockSpec((1, n, pg), blk),
                  pl.BlockSpec((1, 1, 6, pg), lambda bb, g: (bb, sel, 0, g)),
                  pl.BlockSpec((1, pg, pg), lambda bb, g: (g, 0, 0)),
                  pl.BlockSpec((1, pg), lambda bb, g: (0, g))],
        out_specs=pl.BlockSpec((1, n, pg), blk),
        out_shape=jax.ShapeDtypeStruct(xcat.shape, F32),
        scratch_shapes=[pltpu.VMEM((n + 3 * POOL_HALO, pg), F32)] * 3,
        input_output_aliases={1: 0},
        compiler_params=_params("parallel", "arbitrary"),
        name="pool_mixer",
    )(u, xcat, modsel, wgrp, spool)


def _lane_cumsum(x01, upper, blk):
    n = x01.shape[1]
    out = []
    carry = jnp.zeros((x01.shape[0], 1), F32)
    for j in range(0, n, blk):
        cs = jnp.dot(x01[:, j:j + blk].astype(BF16), upper, preferred_element_type=F32) + carry
        carry = cs[:, blk - 1:blk]
        out.append(cs)
    return jnp.concatenate(out, axis=1) if len(out) > 1 else out[0]


def _router_kernel(x_ref, m_ref, g_ref, wr_ref, h_ref, lg_ref):
    m = m_ref[0, 0]
    h = _modulate(x_ref[0], g_ref[...], m[3:4], m[4:5])
    h_hi = h.astype(BF16)
    h_ref[0] = h_hi
    h_lo = (h - h_hi.astype(F32)).astype(BF16)
    wr = wr_ref[...]
    w_hi = wr.astype(BF16)
    w_lo = (wr - w_hi.astype(F32)).astype(BF16)
    lg = (jnp.dot(h_hi, w_hi, preferred_element_type=F32)
          + (jnp.dot(h_hi, w_lo, preferred_element_type=F32) + jnp.dot(h_lo, w_hi, preferred_element_type=F32)))
    lg_ref[0] = lg.T[:lg_ref.shape[1]]


def _topk_kernel(lg_ref, up_ref, aff_ref, pos_ref, *, cap):
    lg = lg_ref[...]
    nb, ne, n = lg.shape
    e = jnp.exp(lg - jnp.max(lg, axis=1, keepdims=True))
    aff3 = e / jnp.sum(e, axis=1, keepdims=True)
    aff_ref[...] = aff3
    aff = aff3.reshape(nb * ne, n)

    def bit_step(i, thr):
        cand = thr | lax.shift_left(jnp.int32(1), 30 - i)
        cnt = jnp.sum(jnp.where(aff >= pltpu.bitcast(cand, F32), 1.0, 0.0), axis=-1, keepdims=True)
        return jnp.where(cnt >= cap, cand, thr)

    thr = lax.fori_loop(0, 31, bit_step, jnp.zeros((aff.shape[0], 1), jnp.int32))
    thr = pltpu.bitcast(thr, F32)
    above = aff > thr
    tied = aff == thr
    room = cap - jnp.sum(jnp.where(above, 1.0, 0.0), axis=-1, keepdims=True)
    upper = up_ref[...]
    blk = upper.shape[0]
    tie_rank = _lane_cumsum(jnp.where(tied, 1.0, 0.0), upper, blk)
    sel = jnp.logical_or(above, jnp.logical_and(tied, tie_rank <= room))
    slot = _lane_cumsum(jnp.where(sel, 1.0, 0.0), upper, blk) - 1.0
    pos_ref[...] = jnp.where(sel, slot, -1.0).astype(jnp.int32).reshape(nb, ne, n)


def _router(xcat, modsel, g2, wr, upper, *, rt, n, row_tile0, sel):
    b, ta, dm = xcat.shape
    n_tiles = n // rt
    cap = max(1, EC_FACTOR * n // N_EXPERTS)
    ne = N_EXPERTS
    h, logits = pl.pallas_call(
        _router_kernel,
        grid=(b, n_tiles),
        in_specs=[pl.BlockSpec((1, rt, dm), lambda bb, r: (bb, row_tile0 + r, 0)),
                  pl.BlockSpec((1, 1, 6, dm), lambda bb, r: (bb, sel, 0, 0)),
                  pl.BlockSpec((1, dm), lambda bb, r: (0, 0)),
                  pl.BlockSpec(wr.shape, lambda bb, r: (0, 0))],
        out_specs=[pl.BlockSpec((1, rt, dm), lambda bb, r: (bb, r, 0)),
                   pl.BlockSpec((1, ne, rt), lambda bb, r: (bb, 0, r))],
        out_shape=[jax.ShapeDtypeStruct((b, n, dm), BF16),
                   jax.ShapeDtypeStruct((b, ne, n), F32)],
        compiler_params=_params("parallel", "arbitrary"),
        name="moe_router",
    )(xcat, modsel, g2, wr)
    whole = lambda i: (0, 0, 0)
    aff, pos = pl.pallas_call(
        functools.partial(_topk_kernel, cap=cap),
        grid=(1,),
        in_specs=[pl.BlockSpec((b, ne, n), whole),
                  pl.BlockSpec(upper.shape, lambda i: (0, 0))],
        out_specs=[pl.BlockSpec((b, ne, n), whole),
                   pl.BlockSpec((b, ne, n), whole)],
        out_shape=[jax.ShapeDtypeStruct((b, ne, n), F32),
                   jax.ShapeDtypeStruct((b, ne, n), jnp.int32)],
        compiler_params=_params("arbitrary"),
        name="moe_topk",
    )(logits, upper)
    return h, aff, pos


MOE_EXPERT_BLOCK = 4
MOE_TOKEN_BLOCK = 256
MOE_WINDOW = 64


def _slot_onehot(pos_ref, first, count, cap):
    hits = []
    for i in range(count):
        pos = pos_ref[0, pl.ds(first + i, 1), :]
        hits.append(lax.broadcasted_iota(jnp.int32, (cap, pos.shape[1]), 0) == pos)
    pick = jnp.concatenate([jnp.where(hit, 1.0, 0.0).astype(BF16) for hit in hits], axis=0)
    return pick, hits


def _moe_plan(pos, *, cap, tb, w):
    nb, ne, n = pos.shape
    nk = n // tb
    assert w % 16 == 0 and (cap - w) % 16 == 0 and n % tb == 0
    cnt = jnp.sum((pos.reshape(nb, ne, nk, tb) >= 0).astype(jnp.int32), axis=-1)
    first = jnp.cumsum(cnt, axis=-1) - cnt
    start = jnp.minimum((first // 16) * 16, cap - w)
    fits = first - start + cnt <= w
    dense_group = jnp.logical_not(jnp.all(fits.reshape(nb, ne // MOE_EXPERT_BLOCK, -1), axis=-1))
    dense_block = jnp.logical_not(jnp.all(fits, axis=1))
    flat = lambda a: a.reshape(-1).astype(jnp.int32)
    return flat(start), flat(dense_group), flat(dense_block)


def _moe_gather_kernel(start_ref, dense_ref, pos_ref, aff_ref, h_ref, xs_ref, gate_ref, *, cap, tb, w):
    b = pl.program_id(0)
    g = pl.program_id(1)
    eb, n = xs_ref.shape[1], pos_ref.shape[2]
    nk = n // tb
    dense = dense_ref[b * pl.num_programs(1) + g]
    expert_row = lambda i: pl.ds(g * eb + i, 1)

    @pl.when(dense != 0)
    def _():
        pick, hits = _slot_onehot(pos_ref, g * eb, eb, cap)
        xs = jnp.dot(pick, h_ref[0], preferred_element_type=F32).astype(BF16)
        xs_ref[0] = xs.reshape(xs_ref.shape[1:])
        for i, hit in enumerate(hits):
            gate = jnp.sum(jnp.where(hit, aff_ref[0, expert_row(i), :], 0.0), axis=-1, keepdims=True)
            gate_ref[0, i] = jnp.broadcast_to(gate, gate_ref.shape[2:])

    @pl.when(dense == 0)
    def _():
        xs_ref[...] = jnp.zeros(xs_ref.shape, BF16)
        gate_ref[...] = jnp.zeros(gate_ref.shape, F32)
        slot = lax.broadcasted_iota(jnp.int32, (w, tb), 0)
        for k in range(nk):
            cols = slice(k * tb, (k + 1) * tb)
            starts = [pl.multiple_of(start_ref[((b * pl.num_programs(1) + g) * eb + i) * nk + k], 16)
                      for i in range(eb)]
            hits = [(slot + starts[i]) == pos_ref[0, expert_row(i), cols] for i in range(eb)]
            pick = jnp.concatenate([jnp.where(hit, 1.0, 0.0).astype(BF16) for hit in hits], axis=0)
            part = jnp.dot(pick, h_ref[0, cols, :], preferred_element_type=F32).astype(BF16)
            for i in range(eb):
                rows = pl.ds(starts[i], w)
                xs_ref[0, i, rows, :] = xs_ref[0, i, rows, :] + part[i * w:(i + 1) * w]
                gate = jnp.sum(jnp.where(hits[i], aff_ref[0, expert_row(i), cols], 0.0), axis=-1, keepdims=True)
                gate_ref[0, i, rows, :] = gate_ref[0, i, rows, :] + gate


def _moe_gather(start, dense, pos, aff, h, *, cap, tb, w):
    nb, ne, n = pos.shape
    dm = h.shape[2]
    eb = MOE_EXPERT_BLOCK
    grid_spec = pltpu.PrefetchScalarGridSpec(
        num_scalar_prefetch=2,
        grid=(nb, ne // eb),
        in_specs=[pl.BlockSpec((1, ne, n), lambda b, e, *_: (b, 0, 0)),
                  pl.BlockSpec((1, ne, n), lambda b, e, *_: (b, 0, 0)),
                  pl.BlockSpec((1, n, dm), lambda b, e, *_: (b, 0, 0), pipeline_mode=pl.Buffered(1))],
        out_specs=[pl.BlockSpec((1, eb, cap, dm), lambda b, e, *_: (b, e, 0, 0)),
                   pl.BlockSpec((1, eb, cap, 128), lambda b, e, *_: (b, e, 0, 0))])
    return pl.pallas_call(
        functools.partial(_moe_gather_kernel, cap=cap, tb=tb, w=w),
        grid_spec=grid_spec,
        out_shape=[jax.ShapeDtypeStruct((nb, ne, cap, dm), BF16),
                   jax.ShapeDtypeStruct((nb, ne, cap, 128), F32)],
        compiler_params=_params("parallel", "arbitrary"),
        name="moe_gather",
    )(start, dense, pos, aff, h)


def _moe_ffn_kernel(*refs, n_sets):
    sets = [refs[3 * i:3 * i + 3] for i in range(n_sets)]
    wg_ref, wu_ref, wd_ref = refs[3 * n_sets:3 * n_sets + 3]
    outs = refs[3 * n_sets + 3:4 * n_sets + 3]
    wgb_ref, wub_ref, wdb_ref = refs[4 * n_sets + 3:]
    c = pl.program_id(1)

    @pl.when(c == 0)
    def _():
        rows = 256
        for src, dst in ((wg_ref, wgb_ref), (wu_ref, wub_ref), (wd_ref, wdb_ref)):
            for r0 in range(0, dst.shape[0], rows):
                dst[r0:r0 + rows, :] = src[0, 0, r0:r0 + rows, :].astype(BF16)

    def run(xs_ref, gate_ref, m_ref, y_ref):
        nb, _, cap, dm = xs_ref.shape
        xs = xs_ref[...].reshape(nb * cap, dm)
        a = jnp.dot(xs, wgb_ref[...], preferred_element_type=F32)
        u = jnp.dot(xs, wub_ref[...], preferred_element_type=F32)
        act = (_silu(a) * u).astype(BF16)
        out = jnp.dot(act, wdb_ref[...], preferred_element_type=F32)
        for i in range(nb):
            y = out[i * cap:(i + 1) * cap] * gate_ref[i, 0][:, 0:1]
            y_ref[i, 0] = (y * m_ref[i, 0][5:6]).astype(BF16)

    if n_sets == 1:
        run(*sets[0], outs[0])
    else:
        last = pl.num_programs(1) - 1
        pl.when(c < last)(lambda: run(*sets[0], outs[0]))
        pl.when(c == last)(lambda: run(*sets[1], outs[1]))


def _moe_ffn(slot_sets, modsel, wg, wu, wd, layer):
    xs, gate, _ = slot_sets[0]
    nb_all, ne, cap, dm = xs.shape
    df = wg.shape[3]
    nb = 2 if nb_all % 2 == 0 else 1
    n_lat = nb_all // nb
    lat = lambda e, c: (jnp.minimum(c, n_lat - 1), e, 0, 0)
    in_specs = [pl.BlockSpec((nb, 1, cap, dm), lat),
                pl.BlockSpec((nb, 1, cap, 128), lat),
                pl.BlockSpec((nb, 1, 6, dm), lambda e, c: (jnp.minimum(c, n_lat - 1), slot_sets[0][2], 0, 0))]
    out_specs = [pl.BlockSpec((nb, 1, cap, dm), lat)]
    out_shape = [jax.ShapeDtypeStruct(xs.shape, BF16)]
    operands = [xs, gate, modsel]
    if len(slot_sets) == 2:
        xs2, gate2, sel2 = slot_sets[1]
        nb2, _, cap2, _ = xs2.shape
        fixed = lambda e, c: (0, e, 0, 0)
        in_specs += [pl.BlockSpec((nb2, 1, cap2, dm), fixed),
                     pl.BlockSpec((nb2, 1, cap2, 128), fixed),
                     pl.BlockSpec((nb2, 1, 6, dm), lambda e, c: (0, sel2, 0, 0))]
        out_specs.append(pl.BlockSpec((nb2, 1, cap2, dm), fixed))
        out_shape.append(jax.ShapeDtypeStruct(xs2.shape, BF16))
        operands += [xs2, gate2, modsel]
    weight = lambda e, c: (layer, e, 0, 0)
    in_specs += [pl.BlockSpec((1, 1, dm, df), weight), pl.BlockSpec((1, 1, dm, df), weight),
                 pl.BlockSpec((1, 1, df, dm), weight)]
    return pl.pallas_call(
        functools.partial(_moe_ffn_kernel, n_sets=len(slot_sets)),
        grid=(ne, n_lat + len(slot_sets) - 1),
        in_specs=in_specs,
        out_specs=out_specs,
        out_shape=out_shape,
        scratch_shapes=[pltpu.VMEM((dm, df), BF16), pltpu.VMEM((dm, df), BF16), pltpu.VMEM((df, dm), BF16)],
        compiler_params=_params("arbitrary", "arbitrary"),
        name="moe_ffn",
    )(*operands, wg, wu, wd)


def _moe_scatter_kernel(start_ref, dense_ref, posc_ref, y_ref, x_ref, *rest, cap, w):
    b = pl.program_id(0)
    k = pl.program_id(1)
    nk = pl.num_programs(1)
    ne = y_ref.shape[1]
    tb = posc_ref.shape[1]
    dense = dense_ref[b * nk + k]
    o_ref = rest[-1]

    def emit(update):
        total = x_ref[0] + update
        if len(rest) == 2:
            total = total * lax.rsqrt(jnp.mean(total * total, axis=-1, keepdims=True) + NORM_EPS) * rest[0][...]
        o_ref[0] = total

    @pl.when(dense != 0)
    def _():
        slot = lax.broadcasted_iota(jnp.int32, (tb, cap), 1)
        pick = jnp.concatenate([jnp.where(slot == posc_ref[0, :, i:i + 1], 1.0, 0.0).astype(BF16)
                                for i in range(ne)], axis=1)
        y = y_ref[0].reshape(ne * cap, y_ref.shape[3])
        emit(jnp.dot(pick, y, preferred_element_type=F32))

    @pl.when(dense == 0)
    def _():
        per = 128 // w
        lane = lax.broadcasted_iota(jnp.int32, (tb, 128), 1)
        tiles = []
        wins = []
        for j in range(ne // per):
            col = None
            for u in reversed(range(per)):
                i = j * per + u
                s = pl.multiple_of(start_ref[(b * ne + i) * nk + k], 16)
                rel = posc_ref[0, :, i:i + 1] - s
                target = jnp.where(jnp.logical_and(rel >= 0, rel < w), rel + u * w, -1)
                col = target if col is None else jnp.where(lane < (u + 1) * w, target, col)
            tiles.append(jnp.where(lane == col, 1.0, 0.0).astype(BF16))
            for u in range(per):
                i = j * per + u
                s = pl.multiple_of(start_ref[(b * ne + i) * nk + k], 16)
                wins.append(y_ref[0, i, pl.ds(s, w), :])
        pick = jnp.concatenate(tiles, axis=1)
        emit(jnp.dot(pick, jnp.concatenate(wins, axis=0), preferred_element_type=F32))


def _moe_scatter(start, dense, pos, y, xcat, *, bb, row0, tb, w, final_gain=None):
    nb, ne, cap, dm = y.shape
    n = pos.shape[2]
    assert 128 % w == 0 and ne % (128 // w) == 0
    posc = jnp.transpose(pos, (0, 2, 1))
    x_index = (lambda b, k, *_: (b, k, 0)) if bb == 1 else (lambda b, k, *_: (k, row0 // tb, 0))
    in_specs = [pl.BlockSpec((1, tb, ne), lambda b, k, *_: (b, k, 0)),
                pl.BlockSpec((1, ne, cap, dm), lambda b, k, *_: (b, 0, 0, 0)),
                pl.BlockSpec((1, tb, dm), x_index)]
    operands = (start, dense, posc, y, xcat)
    if final_gain is None:
        out_shape, aliases = jax.ShapeDtypeStruct(xcat.shape, F32), {4: 0}
    else:
        assert bb == 1
        in_specs.append(pl.BlockSpec((1, dm), lambda b, k, *_: (0, 0)))
        operands += (final_gain,)
        out_shape, aliases = jax.ShapeDtypeStruct((nb, n, dm), F32), {}
    grid_spec = pltpu.PrefetchScalarGridSpec(
        num_scalar_prefetch=2,
        grid=(nb, n // tb),
        in_specs=in_specs,
        out_specs=pl.BlockSpec((1, tb, dm), x_index))
    return pl.pallas_call(
        functools.partial(_moe_scatter_kernel, cap=cap, w=w),
        grid_spec=grid_spec,
        out_shape=out_shape,
        input_output_aliases=aliases,
        compiler_params=_params("parallel", "arbitrary"),
        name="moe_scatter",
    )(*operands)


def _ec_moe(xcat, modsel, g2, wr, upper, wg, wu, wd, layer, *, rt, seq, ctx, with_ctx, final_gain=None):
    b = xcat.shape[0]
    ne = N_EXPERTS
    rt_lat = 2 * rt if seq % (2 * rt) == 0 else rt
    h, aff, pos = _router(xcat, modsel, g2, wr, upper, rt=rt_lat, n=seq, row_tile0=0, sel=0)
    cap = max(1, EC_FACTOR * seq // N_EXPERTS)
    tb = min(MOE_TOKEN_BLOCK, ctx)
    win = dict(tb=tb, w=min(MOE_WINDOW, cap))
    start, dense_group, dense_block = _moe_plan(pos, cap=cap, **win)
    xs, gate = _moe_gather(start, dense_group, pos, aff, h, cap=cap, **win)
    if not with_ctx:
        (y,) = _moe_ffn([(xs, gate, 0)], modsel, wg, wu, wd, layer)
        return _moe_scatter(start, dense_block, pos, y, xcat, bb=1, row0=0, final_gain=final_gain, **win)
    hc, affc, posc = _router(xcat, modsel, g2, wr, upper, rt=rt, n=ctx, row_tile0=seq // rt, sel=1)
    capc = max(1, EC_FACTOR * ctx // N_EXPERTS)
    offs = (jnp.arange(b, dtype=jnp.int32) * capc)[:, None, None]
    posc = jnp.where(posc >= 0, posc + offs, -1)
    posc = jnp.transpose(posc, (1, 0, 2)).reshape(1, ne, b * ctx)
    affc = jnp.transpose(affc, (1, 0, 2)).reshape(1, ne, b * ctx)
    winc = dict(tb=tb, w=min(MOE_WINDOW, b * capc))
    assert tb == ctx
    startc, dense_groupc, dense_blockc = _moe_plan(posc, cap=b * capc, **winc)
    xsc, gatec = _moe_gather(startc, dense_groupc, posc, affc, hc.reshape(1, b * ctx, -1), cap=b * capc, **winc)
    y, yc = _moe_ffn([(xs, gate, 0), (xsc, gatec, 1)], modsel, wg, wu, wd, layer)
    xcat = _moe_scatter(start, dense_block, pos, y, xcat, bb=1, row0=0, final_gain=final_gain, **win)
    return _moe_scatter(startc, dense_blockc, posc, yc, xcat, bb=b, row0=seq, **winc)


def _rope_tables(seq, ctx):
    rows = seq // GRID_W
    row_ids = jnp.repeat(jnp.arange(rows, dtype=F32), GRID_W)
    col_ids = jnp.tile(jnp.arange(GRID_W, dtype=F32), rows)
    half = A_HEAD_DIM // 2
    inv = ROPE_THETA ** (-jnp.arange(0, half, 2, dtype=F32) / half)
    ang_r = row_ids[:, None] * inv
    ang_c = col_ids[:, None] * inv
    cos = jnp.concatenate([jnp.cos(ang_r)] * 2 + [jnp.cos(ang_c)] * 2, axis=1)
    sin = jnp.concatenate([-jnp.sin(ang_r), jnp.sin(ang_r), -jnp.sin(ang_c), jnp.sin(ang_c)], axis=1)
    cos = jnp.tile(cos, (1, A_HEADS))
    sin = jnp.tile(sin, (1, A_HEADS))
    cos = jnp.concatenate([cos, jnp.ones((ctx, A_Q), F32)], axis=0)
    sin = jnp.concatenate([sin, jnp.zeros((ctx, A_Q), F32)], axis=0)
    return cos, sin


def kernel(x, c, ctx, c_ctx, w_mod, b_mod, g_norm1, g_norm2, w_in_even, w_out_even, g_qnorm, g_knorm, w_conv,
           b_gate, g_hnorm, w_in_odd, w_pool_grp, s_pool, w_router, w_exp_gate, w_exp_up, w_exp_down, g_final):
    b, seq, dm = x.shape
    n_ctx = ctx.shape[1]
    depth = w_mod.shape[0]
    rt = 256 if (seq % 256 == 0 and n_ctx % 256 == 0) else 128
    assert seq % rt == 0 and n_ctx % rt == 0 and seq % n_ctx == 0 and seq % GRID_W == 0
    nlt = seq // rt
    nt = (seq + n_ctx) // rt

    xcat = jnp.concatenate([x, ctx], axis=1)

    rows = -(-(b + 1) // 8) * 8
    cond = jnp.zeros((rows, dm), F32).at[:b].set(c).at[b].set(c_ctx)
    mods = _adaln(cond, w_mod, b_mod)
    lat = mods[:, :b].reshape(depth, b, 1, 6, dm)
    cx = jnp.broadcast_to(mods[:, b].reshape(depth, 1, 1, 6, dm), (depth, b, 1, 6, dm))
    modsel = jnp.concatenate([lat, cx], axis=2)

    cos, sin = _rope_tables(seq, n_ctx)
    lane = jnp.arange(A_Q)
    gmat = (lane[:, None] // A_HEAD_DIM == lane[None, :] // A_HEAD_DIM).astype(BF16)
    blk = min(256, n_ctx)
    tri = jnp.arange(blk)
    upper = (tri[:, None] <= tri[None, :]).astype(BF16)

    wg, wu, wd = w_exp_gate, w_exp_up, w_exp_down
    wr = jnp.pad(w_router, ((0, 0), (0, 0), (0, GATE_PAD - N_EXPERTS)))

    for i in range(depth):
        ctx_next = any(j % 2 == 0 for j in range(i + 1, depth))
        n_upd = nt if ctx_next else nlt
        ms = modsel[i]
        if i % 2 == 0:
            e = i // 2
            w_in = w_in_even[e]
            pad = jnp.zeros((dm, GATE_PAD - N_GATES), F32)
            w_in = jnp.concatenate([w_in, pad], axis=1).astype(BF16)
            bg = jnp.concatenate([b_gate[e], jnp.zeros((GATE_PAD - N_GATES,), F32)])[None]
            q, k, v, qk, vbt, ob, gt, gc, gr = _inproj(
                xcat, ms, g_norm1[i][None], w_in, cos, sin,
                jnp.tile(g_qnorm[e], A_HEADS)[None], jnp.tile(g_knorm[e], A_KV_HEADS)[None], gmat, bg,
                rt=rt, ta=seq + n_ctx, n_lat_tiles=nlt)
            oa = _attention(q, k, v, rt=rt, n_tiles=n_upd, n_lat_tiles=nlt, seq=seq)
            qct = _short_conv(qk, w_conv[e], seq=seq, keys=False)
            kc = _short_conv(qk, w_conv[e], seq=seq, keys=True)
            gates_row = jnp.swapaxes(jnp.concatenate([gt[:, :, :N_GATES], gc[:, :, :N_GATES]], axis=2), 1, 2)
            hf, hb = _mlstm(qct, kc, vbt, gates_row, gr, seq=seq, blk=rt)
            xcat = _outproj(xcat, ms, oa, hf, hb, ob, g_hnorm[e][None], w_out_even[e].astype(BF16),
                            rt=rt, n_tiles=n_upd, n_lat_tiles=nlt)
        else:
            o = i // 2
            w_in = w_in_odd[o].astype(BF16)
            wgrp = w_pool_grp[o].astype(BF16)
            rt_lat = 2 * rt if nlt % 2 == 0 else rt
            u = _mod_matmul(xcat, ms, g_norm1[i][None], w_in, rt=rt_lat, tile0=0, n_tiles=seq // rt_lat, sel=0)
            if ctx_next:
                uc = _mod_matmul(xcat, ms, g_norm1[i][None], w_in, rt=rt, tile0=nlt, n_tiles=nt - nlt, sel=1)
            xcat = _pool(u, xcat, ms, wgrp, s_pool[o][None], n=seq, row_block=0, sel=0)
            if ctx_next:
                xcat = _pool(uc, xcat, ms, wgrp, s_pool[o][None], n=n_ctx, row_block=seq // n_ctx, sel=1)
        xcat = _ec_moe(xcat, ms, g_norm2[i][None], wr[i], upper, wg, wu, wd, i,
                       rt=rt, seq=seq, ctx=n_ctx, with_ctx=ctx_next,
                       final_gain=g_final[None] if i == depth - 1 else None)
    return xcat
```

```python
import functools

import jax
import jax.numpy as jnp
from jax import lax
from jax.experimental import pallas as pl
from jax.experimental.pallas import tpu as pltpu

D_MODEL = 1024
GRID_W = 64
A_HEADS = 8
A_KV_HEADS = 2
A_HEAD_DIM = 64
ROPE_THETA = 10000.0
B_HEADS = 4
B_HEAD_DIM = 128
B_CHUNK = 128
POOL_WINDOWS = (2, 4, 8, 16)
POOL_GROUP = D_MODEL // 4
N_EXPERTS = 16
EC_FACTOR = 2
NORM_EPS = 1e-6

A_Q = A_HEADS * A_HEAD_DIM
A_KV = A_KV_HEADS * A_HEAD_DIM
B_W = B_HEADS * B_HEAD_DIM
N_GATES = 2 * 2 * B_HEADS
GATE_PAD = 128
C_QA, C_KA, C_VA = 0, A_Q, A_Q + A_KV
C_QK = A_Q + 2 * A_KV
C_VB = C_QK + 2 * B_W
C_OB = C_VB + B_W
C_GT = C_OB + B_W
EVEN_COLS = C_GT + GATE_PAD
POOL_HALO = 16

F32 = jnp.float32
BF16 = jnp.bfloat16
HIGHEST = lax.Precision.HIGHEST
NT_DIMS = (((1,), (1,)), ((), ()))
TN_DIMS = (((0,), (0,)), ((), ()))
VMEM_LIMIT = 56 * 1024 * 1024


def _params(*sem):
    return pltpu.CompilerParams(dimension_semantics=sem, vmem_limit_bytes=VMEM_LIMIT)


def _modulate(x, g, shift, scale):
    y = x * lax.rsqrt(jnp.mean(x * x, axis=-1, keepdims=True) + NORM_EPS)
    return (y * g) * (1.0 + scale) + shift


def _silu(x):
    return x * jax.nn.sigmoid(x)


def _adaln_kernel(c_ref, w_ref, b_ref, o_ref):
    s = _silu(c_ref[...])
    o_ref[0] = jnp.dot(s, w_ref[0], precision=HIGHEST, preferred_element_type=F32) + b_ref[0]


def _adaln(cond, w_mod, b_mod):
    depth, dm, n6 = w_mod.shape
    rows = cond.shape[0]
    tn = 1536
    return pl.pallas_call(
        _adaln_kernel,
        grid=(depth, n6 // tn),
        in_specs=[pl.BlockSpec((rows, dm), lambda l, j: (0, 0)),
                  pl.BlockSpec((1, dm, tn), lambda l, j: (l, 0, j)),
                  pl.BlockSpec((1, 1, tn), lambda l, j: (l, 0, j))],
        out_specs=pl.BlockSpec((1, rows, tn), lambda l, j: (l, 0, j)),
        out_shape=jax.ShapeDtypeStruct((depth, rows, n6), F32),
        compiler_params=_params("arbitrary", "arbitrary"),
        name="adaln",
    )(cond, w_mod, b_mod.reshape(depth, 1, n6))


def _group_mean_sq(x, gmat, width):
    s = jnp.dot((x * x).astype(BF16), gmat, preferred_element_type=F32)
    return s * (1.0 / width)


def _dot_exact_lhs(mat, x):
    x1 = x.astype(BF16)
    r1 = x - x1.astype(F32)
    x2 = r1.astype(BF16)
    x3 = (r1 - x2.astype(F32)).astype(BF16)
    return (jnp.dot(mat, x1, preferred_element_type=F32) + jnp.dot(mat, x2, preferred_element_type=F32)
            + jnp.dot(mat, x3, preferred_element_type=F32))


def _rope(x, cos, sin):
    w = x.shape[1]
    lane = lax.broadcasted_iota(jnp.int32, x.shape, 1)
    first = (lane % 32) < 16
    partner = jnp.where(first, pltpu.roll(x, w - 16, 1), pltpu.roll(x, 16, 1))
    return x * cos + partner * sin


def _inproj_kernel(x_ref, m_ref, g1_ref, w_ref, cos_ref, sin_ref, gq_ref, gk_ref, gm_ref, bg_ref, lp_ref,
                   q_ref, k_ref, v_ref, qk_ref, vb_ref, ob_ref, gt_ref, gc_ref, gr_ref):
    m = m_ref[0, 0]
    h = _modulate(x_ref[0], g1_ref[...], m[0:1], m[1:2]).astype(BF16)

    def mm(lo, hi):
        return jnp.dot(h, w_ref[:, lo:hi], preferred_element_type=F32)

    cos = cos_ref[...]
    sin = sin_ref[...]
    gm = gm_ref[...]
    qa = mm(C_QA, C_KA)
    qa = qa * lax.rsqrt(_group_mean_sq(qa, gm, A_HEAD_DIM) + NORM_EPS) * gq_ref[...]
    q_ref[0] = (_rope(qa, cos, sin) * (A_HEAD_DIM ** -0.5)).astype(BF16)
    ka = mm(C_KA, C_VA)
    ka = ka * lax.rsqrt(_group_mean_sq(ka, gm[:A_KV, :A_KV], A_HEAD_DIM) + NORM_EPS) * gk_ref[...]
    k_ref[0] = _rope(ka, cos[:, :A_KV], sin[:, :A_KV]).astype(BF16)
    va = mm(C_VA, C_QK).astype(BF16)
    ones = jnp.ones((va.shape[0], A_HEAD_DIM), BF16)
    v_ref[0] = jnp.concatenate(
        [piece for g in range(A_KV_HEADS) for piece in (va[:, g * A_HEAD_DIM:(g + 1) * A_HEAD_DIM], ones)], axis=1)
    qk_ref[0] = mm(C_QK, C_VB).astype(BF16)
    vb_ref[0] = mm(C_VB, C_OB).T.astype(BF16)
    ob_ref[0] = mm(C_OB, C_GT).astype(BF16)
    gt = mm(C_GT, EVEN_COLS) + bg_ref[...]
    lane = lax.broadcasted_iota(jnp.int32, gt.shape, 1)
    log_sig = jnp.minimum(gt, 0.0) - jnp.log1p(jnp.exp(-jnp.abs(gt)))
    gt = jnp.where((lane % (2 * B_HEADS)) >= B_HEADS, log_sig, gt)
    gt_ref[0] = gt
    prefix = _dot_exact_lhs(lp_ref[...], gt)
    row = lax.broadcasted_iota(jnp.int32, gt.shape, 0)
    total = prefix[gt.shape[0] - 1:gt.shape[0]]
    for c in range(gt.shape[0] // B_CHUNK - 2, -1, -1):
        total = jnp.where(row < (c + 1) * B_CHUNK, prefix[(c + 1) * B_CHUNK - 1:(c + 1) * B_CHUNK], total)
    suffix = total - prefix + gt
    cum = jnp.where((lane % (4 * B_HEADS)) < 2 * B_HEADS, prefix, suffix)
    gc_ref[0] = cum
    gr_ref[0] = gt - pltpu.roll(cum, GATE_PAD - B_HEADS, 1)


def _inproj(xcat, modsel, g1, w, cos, sin, gq, gk, gmat, bgate, *, rt, ta, n_lat_tiles):
    b, _, dm = xcat.shape
    nt = ta // rt
    row = lambda bb, r: (bb, r, 0)
    full = lambda bb, r: (0, 0)
    out_widths = (A_Q, A_KV, 2 * A_KV, 2 * B_W, B_W, B_W, GATE_PAD, GATE_PAD, GATE_PAD)
    out_dtypes = (BF16, BF16, BF16, BF16, BF16, BF16, F32, F32, F32)
    out_specs = [pl.BlockSpec((1, rt, wd), row) for wd in out_widths]
    out_shapes = [jax.ShapeDtypeStruct((b, ta, wd), dt) for wd, dt in zip(out_widths, out_dtypes)]
    vb_slot = 4
    out_specs[vb_slot] = pl.BlockSpec((1, B_W, rt), lambda bb, r: (bb, 0, r))
    out_shapes[vb_slot] = jax.ShapeDtypeStruct((b, B_W, ta), BF16)
    idx = jnp.arange(rt)
    same_chunk = idx[:, None] // B_CHUNK == idx[None, :] // B_CHUNK
    lower = jnp.logical_and(same_chunk, idx[None, :] <= idx[:, None]).astype(BF16)
    return pl.pallas_call(
        _inproj_kernel,
        grid=(b, nt),
        in_specs=[pl.BlockSpec((1, rt, dm), row),
                  pl.BlockSpec((1, 1, 6, dm), lambda bb, r: (bb, (r >= n_lat_tiles).astype(jnp.int32), 0, 0)),
                  pl.BlockSpec((1, dm), full),
                  pl.BlockSpec((dm, EVEN_COLS), full),
                  pl.BlockSpec((rt, A_Q), lambda bb, r: (r, 0)),
                  pl.BlockSpec((rt, A_Q), lambda bb, r: (r, 0)),
                  pl.BlockSpec((1, A_Q), full),
                  pl.BlockSpec((1, A_KV), full),
                  pl.BlockSpec((A_Q, A_Q), full),
                  pl.BlockSpec((1, GATE_PAD), full),
                  pl.BlockSpec((rt, rt), full)],
        out_specs=out_specs,
        out_shape=out_shapes,
        compiler_params=_params("parallel", "arbitrary"),
        name="even_inproj",
    )(xcat, modsel, g1, w, cos, sin, gq, gk, gmat, bgate, lower)


def _attn_kernel(q_ref, k_ref, v_ref, o_ref, *, n_lat_tiles, n_tiles, seq):
    group = A_HEADS // A_KV_HEADS

    def attend(key_lo):
        for hd in range(A_HEADS):
            g = hd // group
            qh = q_ref[0, :, hd * A_HEAD_DIM:(hd + 1) * A_HEAD_DIM]
            kg = k_ref[0, key_lo:, g * A_HEAD_DIM:(g + 1) * A_HEAD_DIM]
            vg = v_ref[0, key_lo:, 2 * g * A_HEAD_DIM:2 * (g + 1) * A_HEAD_DIM]
            s = lax.dot_general(qh, kg, NT_DIMS, preferred_element_type=F32)
            p = jnp.exp(s - jnp.max(s, axis=-1, keepdims=True))
            o = jnp.dot(p.astype(BF16), vg, preferred_element_type=F32)
            o = o[:, :A_HEAD_DIM] / o[:, A_HEAD_DIM:]
            o_ref[0, :, hd * A_HEAD_DIM:(hd + 1) * A_HEAD_DIM] = o.astype(BF16)

    if n_tiles == n_lat_tiles:
        attend(0)
    else:
        r = pl.program_id(1)
        pl.when(r < n_lat_tiles)(lambda: attend(0))
        pl.when(r >= n_lat_tiles)(lambda: attend(seq))


def _attention(q, k, v, *, rt, n_tiles, n_lat_tiles, seq):
    b, ta, _ = q.shape
    return pl.pallas_call(
        functools.partial(_attn_kernel, n_lat_tiles=n_lat_tiles, n_tiles=n_tiles, seq=seq),
        grid=(b, n_tiles),
        in_specs=[pl.BlockSpec((1, rt, A_Q), lambda bb, r: (bb, r, 0)),
                  pl.BlockSpec((1, ta, A_KV), lambda bb, r: (bb, 0, 0)),
                  pl.BlockSpec((1, ta, 2 * A_KV), lambda bb, r: (bb, 0, 0))],
        out_specs=pl.BlockSpec((1, rt, A_Q), lambda bb, r: (bb, r, 0)),
        out_shape=jax.ShapeDtypeStruct((b, n_tiles * rt, A_Q), BF16),
        compiler_params=_params("parallel", "arbitrary"),
        name="gqa_attention",
    )(q, k, v)


def _conv_kernel(x_ref, w_ref, o_ref, *, seq, keys):
    x = x_ref[0].astype(F32)
    ta = x.shape[0]
    w = w_ref[...]
    row = lax.broadcasted_iota(jnp.int32, (ta, 1), 0)
    prev = jnp.where(jnp.logical_or(row == 0, row == seq), 0.0, pltpu.roll(x, 1, 0))
    nxt = jnp.where(jnp.logical_or(row == seq - 1, row == ta - 1), 0.0, pltpu.roll(x, ta - 1, 0))
    y = _silu(prev * w[0:1] + x * w[1:2] + nxt * w[2:3])
    if keys:
        o_ref[0] = (y * (B_HEAD_DIM ** -0.5)).astype(BF16)
    else:
        o_ref[0] = y.T.astype(BF16)


def _short_conv(qk, w_conv, *, seq, keys):
    b, ta, _ = qk.shape
    ct = 256
    c0 = B_W // ct if keys else 0
    if keys:
        out_spec = pl.BlockSpec((1, ta, ct), lambda bb, j: (bb, 0, j))
        out_shape = jax.ShapeDtypeStruct((b, ta, B_W), BF16)
    else:
        out_spec = pl.BlockSpec((1, ct, ta), lambda bb, j: (bb, j, 0))
        out_shape = jax.ShapeDtypeStruct((b, B_W, ta), BF16)
    return pl.pallas_call(
        functools.partial(_conv_kernel, seq=seq, keys=keys),
        grid=(b, B_W // ct),
        in_specs=[pl.BlockSpec((1, ta, ct), lambda bb, j: (bb, 0, c0 + j)),
                  pl.BlockSpec((w_conv.shape[0], ct), lambda bb, j: (0, c0 + j))],
        out_specs=out_spec,
        out_shape=out_shape,
        compiler_params=_params("parallel", "arbitrary"),
        name="mlstm_conv_k" if keys else "mlstm_conv_q",
    )(qk, w_conv)


def _mlstm_chain(qt, k, vat, ig, b_row, r_col, total, ca, m_old, tri):
    log_d = jnp.where(tri, b_row + r_col, -jnp.inf)
    log_inter = m_old + b_row
    m_t = jnp.maximum(log_inter, jnp.max(log_d, axis=0, keepdims=True))
    w_intra = jnp.exp(log_d - m_t) * jnp.dot(k, qt, preferred_element_type=F32)
    w_inter = jnp.exp(log_inter - m_t)
    nd = (w_inter * jnp.dot(ca.astype(BF16), qt, preferred_element_type=F32)
          + jnp.dot(vat, w_intra.astype(BF16), preferred_element_type=F32))
    dv = nd.shape[0] // 2
    h = nd[:dv] / jnp.maximum(jnp.abs(nd[dv:]), jnp.exp(-m_t))
    log_w = total - b_row + ig
    m_new = jnp.maximum(m_old + total, jnp.max(log_w, axis=-1, keepdims=True))
    w_s = jnp.exp(log_w - m_new)
    decay = jnp.exp(m_old + total - m_new)
    ca_new = decay * ca + jnp.dot((vat * w_s).astype(BF16), k, preferred_element_type=F32)
    return h, ca_new, m_new


def _mlstm_kernel(qf_ref, kf_ref, vf_ref, grf_ref, gcf_ref, qb_ref, kb_ref, vb_ref, grb_ref, gcb_ref,
                  hf_ref, hb_ref, c_ref, m_ref):
    @pl.when(pl.program_id(1) == 0)
    def _():
        c_ref[...] = jnp.zeros(c_ref.shape, F32)
        m_ref[...] = jnp.full(m_ref.shape, -1e30, F32)

    t = B_CHUNK
    n_chunks = kf_ref.shape[1] // t
    src = lax.broadcasted_iota(jnp.int32, (t, t), 0)
    tgt = lax.broadcasted_iota(jnp.int32, (t, t), 1)
    ones = jnp.ones((B_HEAD_DIM, t), BF16)
    dirs = ((qf_ref, kf_ref, vf_ref, grf_ref, gcf_ref, hf_ref, src <= tgt, t - 1, range(n_chunks)),
            (qb_ref, kb_ref, vb_ref, grb_ref, gcb_ref, hb_ref, src >= tgt, 0, range(n_chunks - 1, -1, -1)))
    outputs = []
    states = []
    for d, (q_ref, k_ref, v_ref, gr_ref, gc_ref, h_ref, tri, last, order) in enumerate(dirs):
        for hd in range(B_HEADS):
            sl = slice(hd * B_HEAD_DIM, (hd + 1) * B_HEAD_DIM)
            gi = d * 2 * B_HEADS + hd
            fi = gi + B_HEADS
            si = d * B_HEADS + hd
            ca = c_ref[si]
            m = m_ref[si][:, 0:1]
            for ci in order:
                rows = slice(ci * t, (ci + 1) * t)
                ig = gr_ref[0, gi:gi + 1, rows]
                b_row = gr_ref[0, N_GATES + fi:N_GATES + fi + 1, rows]
                r_col = gc_ref[0, rows, gi:gi + 1]
                vat = jnp.concatenate([v_ref[0, sl, rows], ones], axis=0)
                h, ca, m = _mlstm_chain(q_ref[0, sl, rows], k_ref[0, rows, sl], vat, ig, b_row, r_col,
                                        b_row[:, last:last + 1], ca, m, tri)
                outputs.append((h_ref, rows, sl, h))
            states.append((si, ca, m))
    for h_ref, rows, sl, h in outputs:
        h_ref[0, sl, rows] = h.astype(BF16)
    for si, ca, m in states:
        c_ref[si] = ca
        m_ref[si] = jnp.broadcast_to(m, (1, 128))


def _mlstm(qct, kc, vbt, gates_row, gates_col, *, seq, blk):
    b, ta, _ = kc.shape
    nc = ta // blk
    ncl = seq // blk
    ncc = nc - ncl
    fwd = lambda j: jnp.where(j < ncc, ncl + j, j - ncc)
    bwd = lambda j: jnp.where(j < ncc, nc - 1 - j, ncl - 1 - (j - ncc))

    def specs(order):
        feature_major = pl.BlockSpec((1, B_W, blk), lambda bb, j: (bb, 0, order(j)))
        return [feature_major,
                pl.BlockSpec((1, blk, B_W), lambda bb, j: (bb, order(j), 0)),
                feature_major,
                pl.BlockSpec((1, 2 * N_GATES, blk), lambda bb, j: (bb, 0, order(j))),
                pl.BlockSpec((1, blk, GATE_PAD), lambda bb, j: (bb, order(j), 0))]

    n_state = 2 * B_HEADS
    ops = (qct, kc, vbt, gates_row, gates_col)
    return pl.pallas_call(
        _mlstm_kernel,
        grid=(b, nc),
        in_specs=specs(fwd) + specs(bwd),
        out_specs=[pl.BlockSpec((1, B_W, blk), lambda bb, j: (bb, 0, fwd(j))),
                   pl.BlockSpec((1, B_W, blk), lambda bb, j: (bb, 0, bwd(j)))],
        out_shape=[jax.ShapeDtypeStruct((b, B_W, ta), BF16)] * 2,
        scratch_shapes=[pltpu.VMEM((n_state, 2 * B_HEAD_DIM, B_HEAD_DIM), F32),
                        pltpu.VMEM((n_state, 1, 128), F32)],
        compiler_params=_params("parallel", "arbitrary"),
        name="mlstm_scan",
    )(*ops, *ops)


def _outproj_kernel(x_ref, m_ref, oa_ref, hf_ref, hb_ref, ob_ref, ghn_ref, w_ref, o_ref):
    hm = (hf_ref[0].astype(F32) + hb_ref[0].astype(F32)).T
    ob = ob_ref[0].astype(F32)
    ghn = ghn_ref[...]
    y = jnp.dot(oa_ref[0], w_ref[0:A_Q, :], preferred_element_type=F32)
    for hd in range(B_HEADS):
        sl = slice(hd * B_HEAD_DIM, (hd + 1) * B_HEAD_DIM)
        z = hm[:, sl]
        z = z * lax.rsqrt(jnp.mean(z * z, axis=-1, keepdims=True) + NORM_EPS) * ghn[:, sl]
        z = (z * jax.nn.sigmoid(ob[:, sl])).astype(BF16)
        y = y + jnp.dot(z, w_ref[A_Q + hd * B_HEAD_DIM:A_Q + (hd + 1) * B_HEAD_DIM, :],
                        preferred_element_type=F32)
    o_ref[0] = x_ref[0] + m_ref[0, 0][2:3] * y


def _outproj(xcat, modsel, oa, hf, hb, ob, ghn, w, *, rt, n_tiles, n_lat_tiles):
    b, ta, dm = xcat.shape
    row = lambda bb, r: (bb, r, 0)
    full = lambda bb, r: (0, 0)
    return pl.pallas_call(
        _outproj_kernel,
        grid=(b, n_tiles),
        in_specs=[pl.BlockSpec((1, rt, dm), row),
                  pl.BlockSpec((1, 1, 6, dm), lambda bb, r: (bb, (r >= n_lat_tiles).astype(jnp.int32), 0, 0)),
                  pl.BlockSpec((1, rt, A_Q), row),
                  pl.BlockSpec((1, B_W, rt), lambda bb, r: (bb, 0, r)),
                  pl.BlockSpec((1, B_W, rt), lambda bb, r: (bb, 0, r)),
                  pl.BlockSpec((1, rt, B_W), row),
                  pl.BlockSpec((1, B_W), full),
                  pl.BlockSpec((A_Q + B_W, dm), full)],
        out_specs=pl.BlockSpec((1, rt, dm), row),
        out_shape=jax.ShapeDtypeStruct(xcat.shape, F32),
        input_output_aliases={0: 0},
        compiler_params=_params("parallel", "arbitrary"),
        name="even_outproj",
    )(xcat, modsel, oa, hf, hb, ob, ghn, w)


def _modmm_kernel(x_ref, m_ref, g_ref, w_ref, o_ref):
    m = m_ref[0, 0]
    h = _modulate(x_ref[0], g_ref[...], m[0:1], m[1:2]).astype(BF16)
    o_ref[0] = jnp.dot(h, w_ref[...], preferred_element_type=F32)


def _mod_matmul(xcat, modsel, g, w, *, rt, tile0, n_tiles, sel):
    b, _, dm = xcat.shape
    n_out = w.shape[1]
    return pl.pallas_call(
        _modmm_kernel,
        grid=(b, n_tiles),
        in_specs=[pl.BlockSpec((1, rt, dm), lambda bb, r: (bb, tile0 + r, 0)),
                  pl.BlockSpec((1, 1, 6, dm), lambda bb, r: (bb, sel, 0, 0)),
                  pl.BlockSpec((1, dm), lambda bb, r: (0, 0)),
                  pl.BlockSpec((dm, n_out), lambda bb, r: (0, 0))],
        out_specs=pl.BlockSpec((1, rt, n_out), lambda bb, r: (bb, r, 0)),
        out_shape=jax.ShapeDtypeStruct((b, n_tiles * rt, n_out), F32),
        compiler_params=_params("parallel", "arbitrary"),
        name="odd_inproj",
    )(xcat, modsel, g, w)


def _pool_kernel(u_ref, x_ref, m_ref, wg_ref, sp_ref, o_ref, pad_ref, sa_ref, sb_ref, *, n, rc):
    g = pl.program_id(1)
    halo = POOL_HALO
    ext = n + 2 * halo
    zeros = jnp.zeros((halo, pad_ref.shape[1]), F32)
    pad_ref[0:halo, :] = zeros
    pad_ref[halo + n:ext, :] = zeros
    pad_ref[halo:halo + n, :] = u_ref[0]
    for ref in (pad_ref, sa_ref, sb_ref):
        ref[ext:ext + halo, :] = zeros
    gate = m_ref[0, 0][2:3]
    chunks = [(r0, min(rc, ext - r0)) for r0 in range(0, ext, rc)]

    def pair_sums(src, dst, shift):
        for r0, size in chunks:
            dst[r0:r0 + size, :] = src[r0:r0 + size, :] + src[r0 + shift:r0 + shift + size, :]

    for gi, win in enumerate(POOL_WINDOWS):
        @pl.when(g == gi)
        def _(win=win):
            lo = win // 2
            hi = win - 1 - lo
            sums, width = pad_ref, 1
            for dst in (sa_ref, sb_ref, sa_ref, sb_ref):
                if width == win:
                    break
                pair_sums(sums, dst, width)
                sums, width = dst, 2 * width
            assert width == win
            for r0 in range(0, n, rc):
                base = halo + r0
                acc = sums[base - lo:base - lo + rc, :]
                t = r0 + lax.broadcasted_iota(jnp.int32, (rc, 1), 0)
                cnt = jnp.minimum(t + hi, n - 1) - jnp.maximum(t - lo, 0) + 1
                p = acc / cnt.astype(F32) - pad_ref[base:base + rc, :]
                y = jnp.dot(p.astype(BF16), wg_ref[0], preferred_element_type=F32) * sp_ref[...]
                o_ref[0, r0:r0 + rc, :] = x_ref[0, r0:r0 + rc, :] + gate * y


def _pool(u, xcat, modsel, wgrp, spool, *, n, row_block, sel):
    b, _, dm = xcat.shape
    pg = POOL_GROUP
    rc = min(256, n)
    blk = lambda bb, g: (bb, row_block, g)
    return pl.pallas_call(
        functools.partial(_pool_kernel, n=n, rc=rc),
        grid=(b, dm // pg),
        in_specs=[pl.BlockSpec((1, n, pg), lambda bb, g: (bb, 0, g)),
                  pl.BlockSpec((1, n, pg), blk),
                  pl.BlockSpec((1, 1, 6, pg), lambda bb, g: (bb, sel, 0, g)),
                  pl.BlockSpec((1, pg, pg), lambda bb, g: (g, 0, 0)),
                  pl.BlockSpec((1, pg), lambda bb, g: (0, g))],
        out_specs=pl.BlockSpec((1, n, pg), blk),
        out_shape=jax.ShapeDtypeStruct(xcat.shape, F32),
        scratch_shapes=[pltpu.VMEM((n + 3 * POOL_HALO, pg), F32)] * 3,
        input_output_aliases={1: 0},
        compiler_params=_params("parallel", "arbitrary"),
        name="pool_mixer",
    )(u, xcat, modsel, wgrp, spool)


def _lane_cumsum(x01, upper, blk):
    n = x01.shape[1]
    out = []
    carry = jnp.zeros((x01.shape[0], 1), F32)
    for j in range(0, n, blk):
        cs = jnp.dot(x01[:, j:j + blk].astype(BF16), upper, preferred_element_type=F32) + carry
        carry = cs[:, blk - 1:blk]
        out.append(cs)
    return jnp.concatenate(out, axis=1) if len(out) > 1 else out[0]


def _router_kernel(x_ref, m_ref, g_ref, wr_ref, h_ref, lg_ref):
    m = m_ref[0, 0]
    h = _modulate(x_ref[0], g_ref[...], m[3:4], m[4:5])
    h_hi = h.astype(BF16)
    h_ref[0] = h_hi
    h_lo = (h - h_hi.astype(F32)).astype(BF16)
    wr = wr_ref[...]
    w_hi = wr.astype(BF16)
    w_lo = (wr - w_hi.astype(F32)).astype(BF16)
    lg = (jnp.dot(h_hi, w_hi, preferred_element_type=F32)
          + (jnp.dot(h_hi, w_lo, preferred_element_type=F32) + jnp.dot(h_lo, w_hi, preferred_element_type=F32)))
    lg_ref[0] = lg.T[:lg_ref.shape[1]]


def _topk_kernel(lg_ref, up_ref, aff_ref, pos_ref, *, cap):
    lg = lg_ref[...]
    nb, ne, n = lg.shape
    e = jnp.exp(lg - jnp.max(lg, axis=1, keepdims=True))
    aff3 = e / jnp.sum(e, axis=1, keepdims=True)
    aff_ref[...] = aff3
    aff = aff3.reshape(nb * ne, n)

    def bit_step(i, thr):
        cand = thr | lax.shift_left(jnp.int32(1), 30 - i)
        cnt = jnp.sum(jnp.where(aff >= pltpu.bitcast(cand, F32), 1.0, 0.0), axis=-1, keepdims=True)
        return jnp.where(cnt >= cap, cand, thr)

    thr = lax.fori_loop(0, 31, bit_step, jnp.zeros((aff.shape[0], 1), jnp.int32))
    thr = pltpu.bitcast(thr, F32)
    above = aff > thr
    tied = aff == thr
    room = cap - jnp.sum(jnp.where(above, 1.0, 0.0), axis=-1, keepdims=True)
    upper = up_ref[...]
    blk = upper.shape[0]
    tie_rank = _lane_cumsum(jnp.where(tied, 1.0, 0.0), upper, blk)
    sel = jnp.logical_or(above, jnp.logical_and(tied, tie_rank <= room))
    slot = _lane_cumsum(jnp.where(sel, 1.0, 0.0), upper, blk) - 1.0
    pos_ref[...] = jnp.where(sel, slot, -1.0).astype(jnp.int32).reshape(nb, ne, n)


def _router(xcat, modsel, g2, wr, upper, *, rt, n, row_tile0, sel):
    b, ta, dm = xcat.shape
    n_tiles = n // rt
    cap = max(1, EC_FACTOR * n // N_EXPERTS)
    ne = N_EXPERTS
    h, logits = pl.pallas_call(
        _router_kernel,
        grid=(b, n_tiles),
        in_specs=[pl.BlockSpec((1, rt, dm), lambda bb, r: (bb, row_tile0 + r, 0)),
                  pl.BlockSpec((1, 1, 6, dm), lambda bb, r: (bb, sel, 0, 0)),
                  pl.BlockSpec((1, dm), lambda bb, r: (0, 0)),
                  pl.BlockSpec(wr.shape, lambda bb, r: (0, 0))],
        out_specs=[pl.BlockSpec((1, rt, dm), lambda bb, r: (bb, r, 0)),
                   pl.BlockSpec((1, ne, rt), lambda bb, r: (bb, 0, r))],
        out_shape=[jax.ShapeDtypeStruct((b, n, dm), BF16),
                   jax.ShapeDtypeStruct((b, ne, n), F32)],
        compiler_params=_params("parallel", "arbitrary"),
        name="moe_router",
    )(xcat, modsel, g2, wr)
    whole = lambda i: (0, 0, 0)
    aff, pos = pl.pallas_call(
        functools.partial(_topk_kernel, cap=cap),
        grid=(1,),
        in_specs=[pl.BlockSpec((b, ne, n), whole),
                  pl.BlockSpec(upper.shape, lambda i: (0, 0))],
        out_specs=[pl.BlockSpec((b, ne, n), whole),
                   pl.BlockSpec((b, ne, n), whole)],
        out_shape=[jax.ShapeDtypeStruct((b, ne, n), F32),
                   jax.ShapeDtypeStruct((b, ne, n), jnp.int32)],
        compiler_params=_params("arbitrary"),
        name="moe_topk",
    )(logits, upper)
    return h, aff, pos


MOE_EXPERT_BLOCK = 4
MOE_TOKEN_BLOCK = 256
MOE_WINDOW = 64


def _slot_onehot(pos_ref, first, count, cap):
    hits = []
    for i in range(count):
        pos = pos_ref[0, pl.ds(first + i, 1), :]
        hits.append(lax.broadcasted_iota(jnp.int32, (cap, pos.shape[1]), 0) == pos)
    pick = jnp.concatenate([jnp.where(hit, 1.0, 0.0).astype(BF16) for hit in hits], axis=0)
    return pick, hits


def _moe_plan(pos, *, cap, tb, w):
    nb, ne, n = pos.shape
    nk = n // tb
    assert w % 16 == 0 and (cap - w) % 16 == 0 and n % tb == 0
    cnt = jnp.sum((pos.reshape(nb, ne, nk, tb) >= 0).astype(jnp.int32), axis=-1)
    first = jnp.cumsum(cnt, axis=-1) - cnt
    start = jnp.minimum((first // 16) * 16, cap - w)
    fits = first - start + cnt <= w
    dense_group = jnp.logical_not(jnp.all(fits.reshape(nb, ne // MOE_EXPERT_BLOCK, -1), axis=-1))
    dense_block = jnp.logical_not(jnp.all(fits, axis=1))
    flat = lambda a: a.reshape(-1).astype(jnp.int32)
    return flat(start), flat(dense_group), flat(dense_block)


def _moe_gather_kernel(start_ref, dense_ref, pos_ref, aff_ref, h_ref, xs_ref, gate_ref, *, cap, tb, w):
    b = pl.program_id(0)
    g = pl.program_id(1)
    eb, n = xs_ref.shape[1], pos_ref.shape[2]
    nk = n // tb
    dense = dense_ref[b * pl.num_programs(1) + g]
    expert_row = lambda i: pl.ds(g * eb + i, 1)

    @pl.when(dense != 0)
    def _():
        pick, hits = _slot_onehot(pos_ref, g * eb, eb, cap)
        xs = jnp.dot(pick, h_ref[0], preferred_element_type=F32).astype(BF16)
        xs_ref[0] = xs.reshape(xs_ref.shape[1:])
        for i, hit in enumerate(hits):
            gate = jnp.sum(jnp.where(hit, aff_ref[0, expert_row(i), :], 0.0), axis=-1, keepdims=True)
            gate_ref[0, i] = jnp.broadcast_to(gate, gate_ref.shape[2:])

    @pl.when(dense == 0)
    def _():
        xs_ref[...] = jnp.zeros(xs_ref.shape, BF16)
        gate_ref[...] = jnp.zeros(gate_ref.shape, F32)
        slot = lax.broadcasted_iota(jnp.int32, (w, tb), 0)
        for k in range(nk):
            cols = slice(k * tb, (k + 1) * tb)
            starts = [pl.multiple_of(start_ref[((b * pl.num_programs(1) + g) * eb + i) * nk + k], 16)
                      for i in range(eb)]
            hits = [(slot + starts[i]) == pos_ref[0, expert_row(i), cols] for i in range(eb)]
            pick = jnp.concatenate([jnp.where(hit, 1.0, 0.0).astype(BF16) for hit in hits], axis=0)
            part = jnp.dot(pick, h_ref[0, cols, :], preferred_element_type=F32).astype(BF16)
            for i in range(eb):
                rows = pl.ds(starts[i], w)
                xs_ref[0, i, rows, :] = xs_ref[0, i, rows, :] + part[i * w:(i + 1) * w]
                gate = jnp.sum(jnp.where(hits[i], aff_ref[0, expert_row(i), cols], 0.0), axis=-1, keepdims=True)
                gate_ref[0, i, rows, :] = gate_ref[0, i, rows, :] + gate


def _moe_gather(start, dense, pos, aff, h, *, cap, tb, w):
    nb, ne, n = pos.shape
    dm = h.shape[2]
    eb = MOE_EXPERT_BLOCK
    grid_spec = pltpu.PrefetchScalarGridSpec(
        num_scalar_prefetch=2,
        grid=(nb, ne // eb),
        in_specs=[pl.BlockSpec((1, ne, n), lambda b, e, *_: (b, 0, 0)),
                  pl.BlockSpec((1, ne, n), lambda b, e, *_: (b, 0, 0)),
                  pl.BlockSpec((1, n, dm), lambda b, e, *_: (b, 0, 0), pipeline_mode=pl.Buffered(1))],
        out_specs=[pl.BlockSpec((1, eb, cap, dm), lambda b, e, *_: (b, e, 0, 0)),
                   pl.BlockSpec((1, eb, cap, 128), lambda b, e, *_: (b, e, 0, 0))])
    return pl.pallas_call(
        functools.partial(_moe_gather_kernel, cap=cap, tb=tb, w=w),
        grid_spec=grid_spec,
        out_shape=[jax.ShapeDtypeStruct((nb, ne, cap, dm), BF16),
                   jax.ShapeDtypeStruct((nb, ne, cap, 128), F32)],
        compiler_params=_params("parallel", "arbitrary"),
        name="moe_gather",
    )(start, dense, pos, aff, h)


def _moe_ffn_kernel(*refs, n_sets):
    sets = [refs[3 * i:3 * i + 3] for i in range(n_sets)]
    wg_ref, wu_ref, wd_ref = refs[3 * n_sets:3 * n_sets + 3]
    outs = refs[3 * n_sets + 3:4 * n_sets + 3]
    wgb_ref, wub_ref, wdb_ref = refs[4 * n_sets + 3:]
    c = pl.program_id(1)

    @pl.when(c == 0)
    def _():
        rows = 256
        for src, dst in ((wg_ref, wgb_ref), (wu_ref, wub_ref), (wd_ref, wdb_ref)):
            for r0 in range(0, dst.shape[0], rows):
                dst[r0:r0 + rows, :] = src[0, 0, r0:r0 + rows, :].astype(BF16)

    def run(xs_ref, gate_ref, m_ref, y_ref):
        nb, _, cap, dm = xs_ref.shape
        xs = xs_ref[...].reshape(nb * cap, dm)
        a = jnp.dot(xs, wgb_ref[...], preferred_element_type=F32)
        u = jnp.dot(xs, wub_ref[...], preferred_element_type=F32)
        act = (_silu(a) * u).astype(BF16)
        out = jnp.dot(act, wdb_ref[...], preferred_element_type=F32)
        for i in range(nb):
            y = out[i * cap:(i + 1) * cap] * gate_ref[i, 0][:, 0:1]
            y_ref[i, 0] = (y * m_ref[i, 0][5:6]).astype(BF16)

    if n_sets == 1:
        run(*sets[0], outs[0])
    else:
        last = pl.num_programs(1) - 1
        pl.when(c < last)(lambda: run(*sets[0], outs[0]))
        pl.when(c == last)(lambda: run(*sets[1], outs[1]))


def _moe_ffn(slot_sets, modsel, wg, wu, wd, layer):
    xs, gate, _ = slot_sets[0]
    nb_all, ne, cap, dm = xs.shape
    df = wg.shape[3]
    nb = 2 if nb_all % 2 == 0 else 1
    n_lat = nb_all // nb
    lat = lambda e, c: (jnp.minimum(c, n_lat - 1), e, 0, 0)
    in_specs = [pl.BlockSpec((nb, 1, cap, dm), lat),
                pl.BlockSpec((nb, 1, cap, 128), lat),
                pl.BlockSpec((nb, 1, 6, dm), lambda e, c: (jnp.minimum(c, n_lat - 1), slot_sets[0][2], 0, 0))]
    out_specs = [pl.BlockSpec((nb, 1, cap, dm), lat)]
    out_shape = [jax.ShapeDtypeStruct(xs.shape, BF16)]
    operands = [xs, gate, modsel]
    if len(slot_sets) == 2:
        xs2, gate2, sel2 = slot_sets[1]
        nb2, _, cap2, _ = xs2.shape
        fixed = lambda e, c: (0, e, 0, 0)
        in_specs += [pl.BlockSpec((nb2, 1, cap2, dm), fixed),
                     pl.BlockSpec((nb2, 1, cap2, 128), fixed),
                     pl.BlockSpec((nb2, 1, 6, dm), lambda e, c: (0, sel2, 0, 0))]
        out_specs.append(pl.BlockSpec((nb2, 1, cap2, dm), fixed))
        out_shape.append(jax.ShapeDtypeStruct(xs2.shape, BF16))
        operands += [xs2, gate2, modsel]
    weight = lambda e, c: (layer, e, 0, 0)
    in_specs += [pl.BlockSpec((1, 1, dm, df), weight), pl.BlockSpec((1, 1, dm, df), weight),
                 pl.BlockSpec((1, 1, df, dm), weight)]
    return pl.pallas_call(
        functools.partial(_moe_ffn_kernel, n_sets=len(slot_sets)),
        grid=(ne, n_lat + len(slot_sets) - 1),
        in_specs=in_specs,
        out_specs=out_specs,
        out_shape=out_shape,
        scratch_shapes=[pltpu.VMEM((dm, df), BF16), pltpu.VMEM((dm, df), BF16), pltpu.VMEM((df, dm), BF16)],
        compiler_params=_params("arbitrary", "arbitrary"),
        name="moe_ffn",
    )(*operands, wg, wu, wd)


def _moe_scatter_kernel(start_ref, dense_ref, posc_ref, y_ref, x_ref, *rest, cap, w):
    b = pl.program_id(0)
    k = pl.program_id(1)
    nk = pl.num_programs(1)
    ne = y_ref.shape[1]
    tb = posc_ref.shape[1]
    dense = dense_ref[b * nk + k]
    o_ref = rest[-1]

    def emit(update):
        total = x_ref[0] + update
        if len(rest) == 2:
            total = total * lax.rsqrt(jnp.mean(total * total, axis=-1, keepdims=True) + NORM_EPS) * rest[0][...]
        o_ref[0] = total

    @pl.when(dense != 0)
    def _():
        slot = lax.broadcasted_iota(jnp.int32, (tb, cap), 1)
        pick = jnp.concatenate([jnp.where(slot == posc_ref[0, :, i:i + 1], 1.0, 0.0).astype(BF16)
                                for i in range(ne)], axis=1)
        y = y_ref[0].reshape(ne * cap, y_ref.shape[3])
        emit(jnp.dot(pick, y, preferred_element_type=F32))

    @pl.when(dense == 0)
    def _():
        per = 128 // w
        lane = lax.broadcasted_iota(jnp.int32, (tb, 128), 1)
        tiles = []
        wins = []
        for j in range(ne // per):
            col = None
            for u in reversed(range(per)):
                i = j * per + u
                s = pl.multiple_of(start_ref[(b * ne + i) * nk + k], 16)
                rel = posc_ref[0, :, i:i + 1] - s
                target = jnp.where(jnp.logical_and(rel >= 0, rel < w), rel + u * w, -1)
                col = target if col is None else jnp.where(lane < (u + 1) * w, target, col)
            tiles.append(jnp.where(lane == col, 1.0, 0.0).astype(BF16))
            for u in range(per):
                i = j * per + u
                s = pl.multiple_of(start_ref[(b * ne + i) * nk + k], 16)
                wins.append(y_ref[0, i, pl.ds(s, w), :])
        pick = jnp.concatenate(tiles, axis=1)
        emit(jnp.dot(pick, jnp.concatenate(wins, axis=0), preferred_element_type=F32))


def _moe_scatter(start, dense, pos, y, xcat, *, bb, row0, tb, w, final_gain=None):
    nb, ne, cap, dm = y.shape
    n = pos.shape[2]
    assert 128 % w == 0 and ne % (128 // w) == 0
    posc = jnp.transpose(pos, (0, 2, 1))
    x_index = (lambda b, k, *_: (b, k, 0)) if bb == 1 else (lambda b, k, *_: (k, row0 // tb, 0))
    in_specs = [pl.BlockSpec((1, tb, ne), lambda b, k, *_: (b, k, 0)),
                pl.BlockSpec((1, ne, cap, dm), lambda b, k, *_: (b, 0, 0, 0)),
                pl.BlockSpec((1, tb, dm), x_index)]
    operands = (start, dense, posc, y, xcat)
    if final_gain is None:
        out_shape, aliases = jax.ShapeDtypeStruct(xcat.shape, F32), {4: 0}
    else:
        assert bb == 1
        in_specs.append(pl.BlockSpec((1, dm), lambda b, k, *_: (0, 0)))
        operands += (final_gain,)
        out_shape, aliases = jax.ShapeDtypeStruct((nb, n, dm), F32), {}
    grid_spec = pltpu.PrefetchScalarGridSpec(
        num_scalar_prefetch=2,
        grid=(nb, n // tb),
        in_specs=in_specs,
        out_specs=pl.BlockSpec((1, tb, dm), x_index))
    return pl.pallas_call(
        functools.partial(_moe_scatter_kernel, cap=cap, w=w),
        grid_spec=grid_spec,
        out_shape=out_shape,
        input_output_aliases=aliases,
        compiler_params=_params("parallel", "arbitrary"),
        name="moe_scatter",
    )(*operands)


def _ec_moe(xcat, modsel, g2, wr, upper, wg, wu, wd, layer, *, rt, seq, ctx, with_ctx, final_gain=None):
    b = xcat.shape[0]
    ne = N_EXPERTS
    rt_lat = 2 * rt if seq % (2 * rt) == 0 else rt
    h, aff, pos = _router(xcat, modsel, g2, wr, upper, rt=rt_lat, n=seq, row_tile0=0, sel=0)
    cap = max(1, EC_FACTOR * seq // N_EXPERTS)
    tb = min(MOE_TOKEN_BLOCK, ctx)
    win = dict(tb=tb, w=min(MOE_WINDOW, cap))
    start, dense_group, dense_block = _moe_plan(pos, cap=cap, **win)
    xs, gate = _moe_gather(start, dense_group, pos, aff, h, cap=cap, **win)
    if not with_ctx:
        (y,) = _moe_ffn([(xs, gate, 0)], modsel, wg, wu, wd, layer)
        return _moe_scatter(start, dense_block, pos, y, xcat, bb=1, row0=0, final_gain=final_gain, **win)
    hc, affc, posc = _router(xcat, modsel, g2, wr, upper, rt=rt, n=ctx, row_tile0=seq // rt, sel=1)
    capc = max(1, EC_FACTOR * ctx // N_EXPERTS)
    offs = (jnp.arange(b, dtype=jnp.int32) * capc)[:, None, None]
    posc = jnp.where(posc >= 0, posc + offs, -1)
    posc = jnp.transpose(posc, (1, 0, 2)).reshape(1, ne, b * ctx)
    affc = jnp.transpose(affc, (1, 0, 2)).reshape(1, ne, b * ctx)
    winc = dict(tb=tb, w=min(MOE_WINDOW, b * capc))
    assert tb == ctx
    startc, dense_groupc, dense_blockc = _moe_plan(posc, cap=b * capc, **winc)
    xsc, gatec = _moe_gather(startc, dense_groupc, posc, affc, hc.reshape(1, b * ctx, -1), cap=b * capc, **winc)
    y, yc = _moe_ffn([(xs, gate, 0), (xsc, gatec, 1)], modsel, wg, wu, wd, layer)
    xcat = _moe_scatter(start, dense_block, pos, y, xcat, bb=1, row0=0, final_gain=final_gain, **win)
    return _moe_scatter(startc, dense_blockc, posc, yc, xcat, bb=b, row0=seq, **winc)


def _rope_tables(seq, ctx):
    rows = seq // GRID_W
    row_ids = jnp.repeat(jnp.arange(rows, dtype=F32), GRID_W)
    col_ids = jnp.tile(jnp.arange(GRID_W, dtype=F32), rows)
    half = A_HEAD_DIM // 2
    inv = ROPE_THETA ** (-jnp.arange(0, half, 2, dtype=F32) / half)
    ang_r = row_ids[:, None] * inv
    ang_c = col_ids[:, None] * inv
    cos = jnp.concatenate([jnp.cos(ang_r)] * 2 + [jnp.cos(ang_c)] * 2, axis=1)
    sin = jnp.concatenate([-jnp.sin(ang_r), jnp.sin(ang_r), -jnp.sin(ang_c), jnp.sin(ang_c)], axis=1)
    cos = jnp.tile(cos, (1, A_HEADS))
    sin = jnp.tile(sin, (1, A_HEADS))
    cos = jnp.concatenate([cos, jnp.ones((ctx, A_Q), F32)], axis=0)
    sin = jnp.concatenate([sin, jnp.zeros((ctx, A_Q), F32)], axis=0)
    return cos, sin


def kernel(x, c, ctx, c_ctx, w_mod, b_mod, g_norm1, g_norm2, w_in_even, w_out_even, g_qnorm, g_knorm, w_conv,
           b_gate, g_hnorm, w_in_odd, w_pool_grp, s_pool, w_router, w_exp_gate, w_exp_up, w_exp_down, g_final):
    b, seq, dm = x.shape
    n_ctx = ctx.shape[1]
    depth = w_mod.shape[0]
    rt = 256 if (seq % 256 == 0 and n_ctx % 256 == 0) else 128
    assert seq % rt == 0 and n_ctx % rt == 0 and seq % n_ctx == 0 and seq % GRID_W == 0
    nlt = seq // rt
    nt = (seq + n_ctx) // rt

    xcat = jnp.concatenate([x, ctx], axis=1)

    rows = -(-(b + 1) // 8) * 8
    cond = jnp.zeros((rows, dm), F32).at[:b].set(c).at[b].set(c_ctx)
    mods = _adaln(cond, w_mod, b_mod)
    lat = mods[:, :b].reshape(depth, b, 1, 6, dm)
    cx = jnp.broadcast_to(mods[:, b].reshape(depth, 1, 1, 6, dm), (depth, b, 1, 6, dm))
    modsel = jnp.concatenate([lat, cx], axis=2)

    cos, sin = _rope_tables(seq, n_ctx)
    lane = jnp.arange(A_Q)
    gmat = (lane[:, None] // A_HEAD_DIM == lane[None, :] // A_HEAD_DIM).astype(BF16)
    blk = min(256, n_ctx)
    tri = jnp.arange(blk)
    upper = (tri[:, None] <= tri[None, :]).astype(BF16)

    wg, wu, wd = w_exp_gate, w_exp_up, w_exp_down
    wr = jnp.pad(w_router, ((0, 0), (0, 0), (0, GATE_PAD - N_EXPERTS)))

    for i in range(depth):
        ctx_next = any(j % 2 == 0 for j in range(i + 1, depth))
        n_upd = nt if ctx_next else nlt
        ms = modsel[i]
        if i % 2 == 0:
            e = i // 2
            w_in = w_in_even[e]
            pad = jnp.zeros((dm, GATE_PAD - N_GATES), F32)
            w_in = jnp.concatenate([w_in, pad], axis=1).astype(BF16)
            bg = jnp.concatenate([b_gate[e], jnp.zeros((GATE_PAD - N_GATES,), F32)])[None]
            q, k, v, qk, vbt, ob, gt, gc, gr = _inproj(
                xcat, ms, g_norm1[i][None], w_in, cos, sin,
                jnp.tile(g_qnorm[e], A_HEADS)[None], jnp.tile(g_knorm[e], A_KV_HEADS)[None], gmat, bg,
                rt=rt, ta=seq + n_ctx, n_lat_tiles=nlt)
            oa = _attention(q, k, v, rt=rt, n_tiles=n_upd, n_lat_tiles=nlt, seq=seq)
            qct = _short_conv(qk, w_conv[e], seq=seq, keys=False)
            kc = _short_conv(qk, w_conv[e], seq=seq, keys=True)
            gates_row = jnp.swapaxes(jnp.concatenate([gt[:, :, :N_GATES], gc[:, :, :N_GATES]], axis=2), 1, 2)
            hf, hb = _mlstm(qct, kc, vbt, gates_row, gr, seq=seq, blk=rt)
            xcat = _outproj(xcat, ms, oa, hf, hb, ob, g_hnorm[e][None], w_out_even[e].astype(BF16),
                            rt=rt, n_tiles=n_upd, n_lat_tiles=nlt)
        else:
            o = i // 2
            w_in = w_in_odd[o].astype(BF16)
            wgrp = w_pool_grp[o].astype(BF16)
            rt_lat = 2 * rt if nlt % 2 == 0 else rt
            u = _mod_matmul(xcat, ms, g_norm1[i][None], w_in, rt=rt_lat, tile0=0, n_tiles=seq // rt_lat, sel=0)
            if ctx_next:
                uc = _mod_matmul(xcat, ms, g_norm1[i][None], w_in, rt=rt, tile0=nlt, n_tiles=nt - nlt, sel=1)
            xcat = _pool(u, xcat, ms, wgrp, s_pool[o][None], n=seq, row_block=0, sel=0)
            if ctx_next:
                xcat = _pool(uc, xcat, ms, wgrp, s_pool[o][None], n=n_ctx, row_block=seq // n_ctx, sel=1)
        xcat = _ec_moe(xcat, ms, g_norm2[i][None], wr[i], upper, wg, wu, wd, i,
                       rt=rt, seq=seq, ctx=n_ctx, with_ctx=ctx_next,
                       final_gain=g_final[None] if i == depth - 1 else None)
    return xcat
```
